```python
import math
import jax, jax.numpy as jnp
from jax import lax
import numpy as np

D_MODEL = 1024
BATCH = 8
SEQ = 2048
DEPTH = 2
DEC_BATCH = 128
DEC_SEQ = 8
PAST_LEN = 2048
PAGE_SIZE = 128

HEAD_DIM = 64
NORM_EPS = 1e-6
CONV_W = 4
SSM_HEADS = 8
SSM_WIDTH = SSM_HEADS * HEAD_DIM
SSM_GROUPS = 2
SSM_STATE = 64
SSM_CONV_CH = SSM_WIDTH + 2 * SSM_GROUPS * SSM_STATE
SSM_CHUNK = 64
RWKV_HEADS = 8
RWKV_WIDTH = RWKV_HEADS * HEAD_DIM
RWKV_W_RANK = 64
RWKV_A_RANK = 64
RWKV_SHIFT_CH = 3 * RWKV_WIDTH + RWKV_W_RANK + RWKV_A_RANK
RWKV_LN_EPS = 64e-5
MLSTM_HEADS = 8
MLSTM_WIDTH = MLSTM_HEADS * HEAD_DIM
MLSTM_CHUNK = 64
ATTN_GROUPS = ((128, 1), (512, 4), (2048, 16))
ATTN_N_GROUPS = 3
ATTN_HEADS = 4
ATTN_QKV_WIDTH = ATTN_N_GROUPS * ATTN_HEADS * HEAD_DIM
ATTN_OUT_WIDTH = ATTN_HEADS * HEAD_DIM
ATTN_QBLOCK = 128
N_BRANCH = 4
BRANCH_WIDTHS = (SSM_WIDTH, RWKV_WIDTH, MLSTM_WIDTH, ATTN_OUT_WIDTH)
BRANCH_TOTAL = SSM_WIDTH + RWKV_WIDTH + MLSTM_WIDTH + ATTN_OUT_WIDTH
IN_SEGMENTS = (
    ("ssm_z", SSM_WIDTH), ("ssm_xbc", SSM_CONV_CH), ("ssm_dt", SSM_HEADS),
    ("rwkv_z", RWKV_WIDTH), ("rwkv_shift", RWKV_SHIFT_CH),
    ("mlstm_z", MLSTM_WIDTH), ("mlstm_qk", 2 * MLSTM_WIDTH), ("mlstm_v", MLSTM_WIDTH),
    ("mlstm_o", MLSTM_WIDTH), ("mlstm_if", 2 * MLSTM_HEADS),
    ("attn_z", ATTN_OUT_WIDTH), ("attn_qkv", 3 * ATTN_QKV_WIDTH),
    ("merge", N_BRANCH * D_MODEL),
)
IN_COLS = sum(size for _, size in IN_SEGMENTS)
STATE_NAMES = ("ssm", "ssm_conv", "rwkv", "rwkv_shift", "mlstm_c", "mlstm_n", "mlstm_m",
               "mlstm_conv", "kv_0", "kv_1", "kv_2")

kernel_name = "hybrid_ssd_rwkv7_mlstm_dilattn_decode_step"


def _rms(x):
    xf = x.astype(jnp.float32)
    return xf * lax.rsqrt(jnp.mean(xf * xf, axis=-1, keepdims=True) + NORM_EPS)


def _split_in(proj):
    out, start = {}, 0
    for name, size in IN_SEGMENTS:
        out[name] = proj[..., start:start + size]
        start += size
    return out


def _causal_conv(u, buf, w, b):
    K, L = w.shape[0], u.shape[1]
    ext = jnp.concatenate([buf.astype(u.dtype), u], axis=1)
    y = b
    for i in range(K):
        y = y + ext[:, i:i + L] * w[i]
    return y, ext[:, L:]


def _token_shift(u, prev, mu):
    ext = jnp.concatenate([prev[:, None].astype(u.dtype), u[:, :-1]], axis=1)
    return u + (ext - u) * mu, u[:, -1]


def _ssd_scan(x, dt, a, bm, cm, h0):
    Bsz, L, H, P = x.shape
    G, N = bm.shape[2], bm.shape[3]
    R = H // G
    c = math.gcd(L, SSM_CHUNK)
    nc = L // c

    def chunks(t):
        return jnp.moveaxis(t.reshape((Bsz, nc, c) + t.shape[2:]), 1, 0)

    causal = jnp.tril(jnp.ones((c, c), bool))[None, :, :, None, None]
    a_gr = a.reshape(G, R)

    def step(hs, inp):
        xc, dtc, bc, cc = inp
        cs = jnp.cumsum(dtc * a_gr, axis=1)
        seg = jnp.where(causal, cs[:, :, None] - cs[:, None], -jnp.inf)
        lmat = jnp.einsum("btgn,bsgn->btsg", cc, bc)[..., None] * jnp.exp(seg) * dtc[:, None]
        y = jnp.einsum("btsgr,bsgrp->btgrp", lmat, xc)
        y = y + jnp.einsum("btgn,bgrpn->btgrp", cc, hs) * jnp.exp(cs)[..., None]
        wl = jnp.exp(cs[:, -1:] - cs) * dtc
        hs = hs * jnp.exp(cs[:, -1])[..., None, None] + jnp.einsum("bsgr,bsgrp,bsgn->bgrpn", wl, xc, bc)
        return hs, y

    xs = (chunks(x.reshape(Bsz, L, G, R, P)), chunks(dt.reshape(Bsz, L, G, R)), chunks(bm), chunks(cm))
    hT, ys = lax.scan(step, h0.reshape(Bsz, G, R, P, N), xs)
    return jnp.moveaxis(ys, 0, 1).reshape(Bsz, L, H, P), hT.reshape(Bsz, H, P, N)


def _rwkv7_scan(r, w, k, v, a, b, s0):
    def step(S, inp):
        rt, wt, kt, vt, at, bt = inp
        sa = jnp.einsum("bhvk,bhk->bhv", S, at)
        S = S * wt[:, :, None, :] + sa[..., None] * bt[:, :, None, :] + vt[..., None] * kt[:, :, None, :]
        return S, jnp.einsum("bhvk,bhk->bhv", S, rt)

    sT, ys = lax.scan(step, s0, tuple(jnp.moveaxis(t, 1, 0) for t in (r, w, k, v, a, b)))
    return jnp.moveaxis(ys, 0, 1), sT


def _mlstm_scan(q, k, v, ig, lf, c0, n0, m0):
    B, L, H, E = q.shape
    c = math.gcd(L, MLSTM_CHUNK)
    nc = L // c

    def chunks(t):
        return jnp.moveaxis(t.reshape((B, nc, c) + t.shape[2:]), 1, 0)

    causal = jnp.tril(jnp.ones((c, c), bool))[None, :, :, None]

    def step(carry, inp):
        cs, ns, ms = carry
        qc, kc, vc, ic, fc = inp
        b = jnp.cumsum(fc, axis=1)
        dmat = jnp.where(causal, b[:, :, None] - b[:, None] + ic[:, None], -jnp.inf)
        g = b + ms[:, None]
        mt = jnp.maximum(g, jnp.max(dmat, axis=2))
        sw = jnp.exp(dmat - mt[:, :, None]) * jnp.einsum("bthe,bshe->btsh", qc, kc)
        wi = jnp.exp(g - mt)
        num = jnp.einsum("btsh,bshe->bthe", sw, vc) + wi[..., None] * jnp.einsum("bthd,bhde->bthe", qc, cs)
        den = jnp.sum(sw, axis=2) + wi * jnp.einsum("bthd,bhd->bth", qc, ns)
        h = num / jnp.maximum(jnp.abs(den), jnp.exp(-mt))[..., None]
        m_new = mt[:, -1]
        kw = kc * jnp.exp(b[:, -1:] - b + ic - m_new[:, None])[..., None]
        dec = jnp.exp(b[:, -1] + ms - m_new)
        cs = dec[..., None, None] * cs + jnp.einsum("bshd,bshe->bhde", kw, vc)
        ns = dec[..., None] * ns + jnp.sum(kw, axis=1)
        return (cs, ns, m_new), h

    (cT, nT, mT), hs = lax.scan(step, (c0, n0, m0), tuple(chunks(t) for t in (q, k, v, ig, lf)))
    return jnp.moveaxis(hs, 0, 1).reshape(B, L, H, E), cT, nT, mT


def _dilated_attn_prompt(q, k, v, window, dil):
    B, S, H, E = q.shape
    J = window // dil
    Lr = S // dil
    bq = math.gcd(Lr, ATTN_QBLOCK)
    nb = Lr // bq

    def res(t):
        return jnp.moveaxis(t.reshape(B, Lr, dil, H, E), 2, 1)

    qr = res(q).reshape(B, dil, nb, bq, H, E)
    pad = ((0, 0), (0, 0), (J, 0), (0, 0), (0, 0))
    kp, vp = jnp.pad(res(k), pad), jnp.pad(res(v), pad)
    idx = jnp.arange(nb)[:, None] * bq + jnp.arange(bq + J)[None, :]
    kb, vb = kp[:, :, idx], vp[:, :, idx]
    s = jnp.einsum("brnqhe,brnkhe->brnhqk", qr, kb).astype(jnp.float32) * E ** -0.5
    qi = jnp.arange(bq)[:, None]
    ki = jnp.arange(bq + J)[None, :]
    dist = qi + J - ki
    key_u = jnp.arange(nb)[:, None, None] * bq + ki[None] - J
    valid = (dist >= 0) & (dist <= J) & (key_u >= 0)
    s = jnp.where(valid[None, None, :, None], s, -jnp.inf)
    m = jnp.max(s, axis=-1, keepdims=True)
    p = jnp.exp(s - m)
    den = jnp.sum(p, axis=-1, keepdims=True)
    o = jnp.einsum("brnhqk,brnkhe->brnqhe", p / den, vb)
    lse = (m + jnp.log(den))[..., 0]
    o = jnp.moveaxis(o.reshape(B, dil, Lr, H, E), 1, 2).reshape(B, S, H, E)
    lse = jnp.moveaxis(jnp.moveaxis(lse, 3, 4).reshape(B, dil, Lr, H), 1, 2).reshape(B, S, H)
    return o, lse


def _dilated_attn_step(q, k_new, v_new, k_past, v_past, window, dil):
    B, L, H, E = q.shape
    W = k_past.shape[1]
    J = window // dil
    k_all = jnp.concatenate([k_past.astype(k_new.dtype), k_new], axis=1)
    v_all = jnp.concatenate([v_past.astype(v_new.dtype), v_new], axis=1)
    idx = W + jnp.arange(L)[:, None] - dil * jnp.arange(J + 1)[None, :]
    valid = idx >= 0
    kg = jnp.take(k_all, jnp.maximum(idx, 0), axis=1)
    vg = jnp.take(v_all, jnp.maximum(idx, 0), axis=1)
    s = jnp.einsum("blhe,bljhe->blhj", q, kg).astype(jnp.float32) * E ** -0.5
    s = jnp.where(valid[None, :, None, :], s, -jnp.inf)
    m = jnp.max(s, axis=-1, keepdims=True)
    p = jnp.exp(s - m)
    den = jnp.sum(p, axis=-1, keepdims=True)
    o = jnp.einsum("blhj,bljhe->blhe", p / den, vg)
    return o, (m + jnp.log(den))[..., 0]


def _layer(x, c, st, kv_past, p):
    f32 = jnp.float32
    B, L, _ = x.shape
    st = {name: val.astype(f32) for name, val in st.items()}
    mod = jax.nn.silu(c.astype(f32)) @ p["ada_w"] + p["ada_b"]
    shift, scale, gate = jnp.split(mod, 3, axis=-1)
    h = _rms(x) * p["norm_w"] * (1.0 + scale[:, None]) + shift[:, None]
    seg = _split_in(h @ p["w_in"])

    xbc, ssm_conv_new = _causal_conv(seg["ssm_xbc"], st["ssm_conv"], p["ssm_conv_w"], p["ssm_conv_b"])
    xbc = jax.nn.silu(xbc)
    xs = xbc[..., :SSM_WIDTH].reshape(B, L, SSM_HEADS, HEAD_DIM)
    bm = xbc[..., SSM_WIDTH:SSM_WIDTH + SSM_GROUPS * SSM_STATE].reshape(B, L, SSM_GROUPS, SSM_STATE)
    cm = xbc[..., SSM_WIDTH + SSM_GROUPS * SSM_STATE:].reshape(B, L, SSM_GROUPS, SSM_STATE)
    dt = jax.nn.softplus(seg["ssm_dt"] + p["ssm_dt_bias"])
    a_ssm = -jnp.exp(p["ssm_a_log"].astype(f32))
    ys, ssm_new = _ssd_scan(xs, dt, a_ssm, bm, cm, st["ssm"])
    ys = (ys + xs * p["ssm_d"][:, None]).reshape(B, L, SSM_WIDTH)
    gy = (ys * jax.nn.silu(seg["ssm_z"])).reshape(B, L, SSM_GROUPS, SSM_WIDTH // SSM_GROUPS)
    out_a = _rms(gy).reshape(B, L, SSM_WIDTH) * p["ssm_norm_w"]

    sh, rwkv_shift_new = _token_shift(seg["rwkv_shift"], st["rwkv_shift"], p["rwkv_mu"])
    r, k, v, w_lo, a_lo = jnp.split(
        sh, [RWKV_WIDTH, 2 * RWKV_WIDTH, 3 * RWKV_WIDTH, 3 * RWKV_WIDTH + RWKV_W_RANK], axis=-1)
    w_log = -jax.nn.softplus(-(p["rwkv_w0"] + jnp.tanh(w_lo) @ p["rwkv_w2"])) - 0.5
    decay = jnp.exp(-jnp.exp(w_log))
    a_in = jax.nn.sigmoid(p["rwkv_a0"] + a_lo @ p["rwkv_a2"])

    def rh(t):
        return t.reshape(B, L, RWKV_HEADS, HEAD_DIM)

    kk = rh(k * p["rwkv_k_k"])
    kk = kk / jnp.maximum(jnp.sqrt(jnp.sum(kk * kk, axis=-1, keepdims=True)), 1e-12)
    k = k * (1.0 + (a_in - 1.0) * p["rwkv_k_a"])
    r_h, k_h, v_h, a_h = rh(r), rh(k), rh(v), rh(a_in)
    yr, rwkv_new = _rwkv7_scan(r_h, rh(decay), k_h, v_h, -kk, kk * a_h, st["rwkv"])
    mu = jnp.mean(yr, axis=-1, keepdims=True)
    var = jnp.mean(jnp.square(yr - mu), axis=-1, keepdims=True)
    yr = ((yr - mu) * lax.rsqrt(var + RWKV_LN_EPS)).reshape(B, L, RWKV_WIDTH) * p["rwkv_ln_w"] + p["rwkv_ln_b"]
    yr = yr + (jnp.sum(r_h * k_h * p["rwkv_r_k"], axis=-1, keepdims=True) * v_h).reshape(B, L, RWKV_WIDTH)
    out_b = yr * jax.nn.silu(seg["rwkv_z"])

    qk, mlstm_conv_new = _causal_conv(seg["mlstm_qk"], st["mlstm_conv"], p["mlstm_conv_w"], p["mlstm_conv_b"])
    qk = jax.nn.silu(qk)

    def mh(t):
        return t.reshape(B, L, MLSTM_HEADS, HEAD_DIM)

    qm = mh(qk[..., :MLSTM_WIDTH])
    km = mh(qk[..., MLSTM_WIDTH:]) * HEAD_DIM ** -0.5
    vm = mh(seg["mlstm_v"])
    gates = seg["mlstm_if"] + p["mlstm_gate_b"]
    ig, lf = gates[..., :MLSTM_HEADS], jax.nn.log_sigmoid(gates[..., MLSTM_HEADS:])
    hm, mlstm_c_new, mlstm_n_new, mlstm_m_new = _mlstm_scan(
        qm, km, vm, ig, lf, st["mlstm_c"], st["mlstm_n"], st["mlstm_m"])
    hm = jax.nn.sigmoid(seg["mlstm_o"]) * hm.reshape(B, L, MLSTM_WIDTH)
    out_c = _rms(mh(hm)).reshape(B, L, MLSTM_WIDTH) * p["mlstm_norm_w"] * jax.nn.silu(seg["mlstm_z"])

    qkv = seg["attn_qkv"].reshape(B, L, 3, ATTN_N_GROUPS, ATTN_HEADS, HEAD_DIM)
    qa = _rms(qkv[:, :, 0]) * p["attn_q_norm"]
    ka = _rms(qkv[:, :, 1]) * p["attn_k_norm"]
    va = qkv[:, :, 2]
    outs, lses, kv_new = [], [], []
    for gi, (window, dil) in enumerate(ATTN_GROUPS):
        qg, kg, vg = qa[:, :, gi], ka[:, :, gi], va[:, :, gi]
        if kv_past is None:
            o, lse = _dilated_attn_prompt(qg, kg, vg, window, dil)
            rows = jnp.stack([kg, vg], axis=2)[:, L - min(window, L):]
        else:
            past = kv_past[gi]
            o, lse = _dilated_attn_step(qg, kg, vg, past[:, :, 0], past[:, :, 1], window, dil)
            rows = jnp.stack([kg, vg], axis=2)
        outs.append(o)
        lses.append(lse)
        kv_new.append(rows)
    wgt = jax.nn.softmax(jnp.stack(lses, axis=2), axis=2)
    oa = jnp.sum(jnp.stack(outs, axis=2) * wgt[..., None], axis=2)
    out_d = oa.reshape(B, L, ATTN_OUT_WIDTH) * jax.nn.silu(seg["attn_z"])

    mg = jax.nn.sigmoid(seg["merge"]).reshape(B, L, N_BRANCH, D_MODEL)
    merged = None
    start = 0
    for bi, (ob, width) in enumerate(zip((out_a, out_b, out_c, out_d), BRANCH_WIDTHS)):
        term = mg[:, :, bi] * (ob @ p["w_branch"][start:start + width])
        merged = term if merged is None else merged + term
        start += width
    y = merged @ p["w_out"]
    x_new = (x + gate[:, None] * y).astype(x.dtype)
    new_state = {"ssm": ssm_new, "ssm_conv": ssm_conv_new, "rwkv": rwkv_new, "rwkv_shift": rwkv_shift_new,
                 "mlstm_c": mlstm_c_new, "mlstm_n": mlstm_n_new, "mlstm_m": mlstm_m_new,
                 "mlstm_conv": mlstm_conv_new, "kv_0": kv_new[0], "kv_1": kv_new[1], "kv_2": kv_new[2]}
    return x_new, new_state


def setup_inputs(seed: int = 0) -> dict:
    key = jax.random.key(seed)
    ks = jax.random.split(key, 64)
    counter = [0]

    def nxt():
        kk = ks[counter[0]]
        counter[0] += 1
        return kk

    def nrm(shape, scale=1.0):
        return scale * jax.random.normal(nxt(), shape, jnp.float32)

    def uni(shape, lo, hi):
        return jax.random.uniform(nxt(), shape, jnp.float32, lo, hi)

    dt0 = jnp.exp(uni((DEPTH, SSM_HEADS), math.log(1e-3), math.log(1e-1)))
    return {
        "x_prompt": nrm((BATCH, SEQ, D_MODEL)),
        "x_sample": nrm((DEC_BATCH, DEC_SEQ, D_MODEL)),
        "c_prompt": nrm((BATCH, D_MODEL)),
        "c_sample": nrm((DEC_BATCH, D_MODEL)),
        "state_ssm": nrm((DEPTH, DEC_BATCH, SSM_HEADS, HEAD_DIM, SSM_STATE), 0.5),
        "state_ssm_conv": nrm((DEPTH, DEC_BATCH, CONV_W - 1, SSM_CONV_CH)),
        "state_rwkv": nrm((DEPTH, DEC_BATCH, RWKV_HEADS, HEAD_DIM, HEAD_DIM), 0.5),
        "state_rwkv_shift": nrm((DEPTH, DEC_BATCH, RWKV_SHIFT_CH)),
        "state_mlstm_c": nrm((DEPTH, DEC_BATCH, MLSTM_HEADS, HEAD_DIM, HEAD_DIM), 0.5),
        "state_mlstm_n": nrm((DEPTH, DEC_BATCH, MLSTM_HEADS, HEAD_DIM), 0.5),
        "state_mlstm_m": nrm((DEPTH, DEC_BATCH, MLSTM_HEADS)),
        "state_mlstm_conv": nrm((DEPTH, DEC_BATCH, CONV_W - 1, 2 * MLSTM_WIDTH)),
        "cache_kv_w128": nrm((DEPTH, DEC_BATCH, min(128, PAST_LEN), 2, ATTN_HEADS, HEAD_DIM)),
        "cache_kv_w512": nrm((DEPTH, DEC_BATCH, min(512, PAST_LEN), 2, ATTN_HEADS, HEAD_DIM)),
        "cache_kv_w2048": nrm((DEPTH, DEC_BATCH, min(2048, PAST_LEN), 2, ATTN_HEADS, HEAD_DIM)),
        "norm_w": 1.0 + nrm((DEPTH, D_MODEL), 0.02),
        "ada_w": nrm((DEPTH, D_MODEL, 3 * D_MODEL), 0.5 * D_MODEL ** -0.5),
        "ada_b": nrm((DEPTH, 3 * D_MODEL), 0.1),
        "w_in": nrm((DEPTH, D_MODEL, IN_COLS), D_MODEL ** -0.5),
        "w_branch": nrm((DEPTH, BRANCH_TOTAL, D_MODEL), SSM_WIDTH ** -0.5),
        "w_out": nrm((DEPTH, D_MODEL, D_MODEL), D_MODEL ** -0.5),
        "ssm_conv_w": nrm((DEPTH, CONV_W, SSM_CONV_CH), CONV_W ** -0.5),
        "ssm_conv_b": nrm((DEPTH, SSM_CONV_CH), 0.02),
        "ssm_dt_bias": dt0 + jnp.log(-jnp.expm1(-dt0)),
        "ssm_a_log": jnp.log(uni((DEPTH, SSM_HEADS), 1.0, 16.0)),
        "ssm_d": 1.0 + nrm((DEPTH, SSM_HEADS), 0.1),
        "ssm_norm_w": 1.0 + nrm((DEPTH, SSM_WIDTH), 0.02),
        "rwkv_mu": uni((DEPTH, RWKV_SHIFT_CH), 0.0, 1.0),
        "rwkv_w0": uni((DEPTH, RWKV_WIDTH), -6.0, 1.0),
        "rwkv_w2": nrm((DEPTH, RWKV_W_RANK, RWKV_WIDTH), 0.5 * RWKV_W_RANK ** -0.5),
        "rwkv_a0": nrm((DEPTH, RWKV_WIDTH), 0.1),
        "rwkv_a2": nrm((DEPTH, RWKV_A_RANK, RWKV_WIDTH), 0.5 * RWKV_A_RANK ** -0.5),
        "rwkv_k_k": 0.85 + nrm((DEPTH, RWKV_WIDTH), 0.05),
        "rwkv_k_a": 1.0 + nrm((DEPTH, RWKV_WIDTH), 0.05),
        "rwkv_r_k": nrm((DEPTH, RWKV_HEADS, HEAD_DIM), 0.1),
        "rwkv_ln_w": 1.0 + nrm((DEPTH, RWKV_WIDTH), 0.02),
        "rwkv_ln_b": nrm((DEPTH, RWKV_WIDTH), 0.02),
        "mlstm_conv_w": nrm((DEPTH, CONV_W, 2 * MLSTM_WIDTH), CONV_W ** -0.5),
        "mlstm_conv_b": nrm((DEPTH, 2 * MLSTM_WIDTH), 0.02),
        "mlstm_gate_b": jnp.concatenate([nrm((DEPTH, MLSTM_HEADS), 0.1),
                                         uni((DEPTH, MLSTM_HEADS), 3.0, 6.0)], axis=-1),
        "mlstm_norm_w": 1.0 + nrm((DEPTH, MLSTM_WIDTH), 0.02),
        "attn_q_norm": 1.0 + nrm((DEPTH, HEAD_DIM), 0.02),
        "attn_k_norm": 1.0 + nrm((DEPTH, HEAD_DIM), 0.02),
    }


def reference(x_prompt, x_sample, c_prompt, c_sample, state_ssm, state_ssm_conv, state_rwkv,
              state_rwkv_shift, state_mlstm_c, state_mlstm_n, state_mlstm_m, state_mlstm_conv,
              cache_kv_w128, cache_kv_w512, cache_kv_w2048, norm_w, ada_w, ada_b, w_in, w_branch,
              w_out, ssm_conv_w, ssm_conv_b, ssm_dt_bias, ssm_a_log, ssm_d, ssm_norm_w, rwkv_mu,
              rwkv_w0, rwkv_w2, rwkv_a0, rwkv_a2, rwkv_k_k, rwkv_k_a, rwkv_r_k, rwkv_ln_w, rwkv_ln_b,
              mlstm_conv_w, mlstm_conv_b, mlstm_gate_b, mlstm_norm_w, attn_q_norm, attn_k_norm):
    f32 = jnp.float32
    weights = {"norm_w": norm_w, "ada_w": ada_w, "ada_b": ada_b, "w_in": w_in, "w_branch": w_branch,
               "w_out": w_out, "ssm_conv_w": ssm_conv_w, "ssm_conv_b": ssm_conv_b,
               "ssm_dt_bias": ssm_dt_bias, "ssm_a_log": ssm_a_log, "ssm_d": ssm_d,
               "ssm_norm_w": ssm_norm_w, "rwkv_mu": rwkv_mu, "rwkv_w0": rwkv_w0, "rwkv_w2": rwkv_w2,
               "rwkv_a0": rwkv_a0, "rwkv_a2": rwkv_a2, "rwkv_k_k": rwkv_k_k, "rwkv_k_a": rwkv_k_a,
               "rwkv_r_k": rwkv_r_k, "rwkv_ln_w": rwkv_ln_w, "rwkv_ln_b": rwkv_ln_b,
               "mlstm_conv_w": mlstm_conv_w, "mlstm_conv_b": mlstm_conv_b,
               "mlstm_gate_b": mlstm_gate_b, "mlstm_norm_w": mlstm_norm_w,
               "attn_q_norm": attn_q_norm, "attn_k_norm": attn_k_norm}
    bp = x_prompt.shape[0]
    fresh = {"ssm": jnp.zeros((bp, SSM_HEADS, HEAD_DIM, SSM_STATE), f32),
             "ssm_conv": jnp.zeros((bp, CONV_W - 1, SSM_CONV_CH), f32),
             "rwkv": jnp.zeros((bp, RWKV_HEADS, HEAD_DIM, HEAD_DIM), f32),
             "rwkv_shift": jnp.zeros((bp, RWKV_SHIFT_CH), f32),
             "mlstm_c": jnp.zeros((bp, MLSTM_HEADS, HEAD_DIM, HEAD_DIM), f32),
             "mlstm_n": jnp.zeros((bp, MLSTM_HEADS, HEAD_DIM), f32),
             "mlstm_m": jnp.zeros((bp, MLSTM_HEADS), f32),
             "mlstm_conv": jnp.zeros((bp, CONV_W - 1, 2 * MLSTM_WIDTH), f32)}
    y_prompt, y_sample = x_prompt, x_sample
    prompt_states, sample_states = [], []
    for l in range(DEPTH):
        p = {name: arr[l] for name, arr in weights.items()}
        y_prompt, sp = _layer(y_prompt, c_prompt, fresh, None, p)
        st = {"ssm": state_ssm[l], "ssm_conv": state_ssm_conv[l], "rwkv": state_rwkv[l],
              "rwkv_shift": state_rwkv_shift[l], "mlstm_c": state_mlstm_c[l],
              "mlstm_n": state_mlstm_n[l], "mlstm_m": state_mlstm_m[l],
              "mlstm_conv": state_mlstm_conv[l]}
        y_sample, ss = _layer(y_sample, c_sample, st,
                              (cache_kv_w128[l], cache_kv_w512[l], cache_kv_w2048[l]), p)
        prompt_states.append(sp)
        sample_states.append(ss)

    def sp_(name):
        return jnp.stack([s[name] for s in prompt_states])

    def ss_(name):
        return jnp.stack([s[name] for s in sample_states])

    return (y_prompt, y_sample,
            sp_("ssm"), sp_("ssm_conv"), sp_("rwkv"), sp_("rwkv_shift"), sp_("mlstm_c"), sp_("mlstm_n"),
            sp_("mlstm_m"), sp_("mlstm_conv"), sp_("kv_0"), sp_("kv_1"), sp_("kv_2"),
            ss_("ssm"), ss_("ssm_conv"), ss_("rwkv"), ss_("rwkv_shift"), ss_("mlstm_c"), ss_("mlstm_n"),
            ss_("mlstm_m"), ss_("mlstm_conv"), ss_("kv_0"), ss_("kv_1"), ss_("kv_2"))
```

```python
import functools
import math

import jax
import jax.numpy as jnp
from jax import lax
from jax.experimental import pallas as pl
from jax.experimental.pallas import tpu as pltpu

f32 = jnp.float32
bf16 = jnp.bfloat16
HI = lax.Precision.HIGHEST

D_MODEL = 1024
DEPTH = 2
HEAD_DIM = 64
NORM_EPS = 1e-6
CONV_W = 4
N_HEADS = 8
WIDTH = N_HEADS * HEAD_DIM
SSM_GROUPS = 2
SSM_STATE = 64
SSM_CONV_CH = WIDTH + 2 * SSM_GROUPS * SSM_STATE
RWKV_RANK = 64
RWKV_SHIFT_CH = 3 * WIDTH + 2 * RWKV_RANK
RWKV_LN_EPS = 64e-5
ATTN_GROUPS = ((128, 1), (512, 4), (2048, 16))
ATTN_HEADS = 4
ATTN_W = ATTN_HEADS * HEAD_DIM
ATTN_J = 128
N_BRANCH = 4
LANES = 128
VMEM_LIMIT = 52 * 1024 * 1024

IN_SEGMENTS = (
    ("ssm_z", WIDTH), ("ssm_xbc", SSM_CONV_CH), ("ssm_dt", N_HEADS),
    ("rwkv_z", WIDTH), ("rwkv_shift", RWKV_SHIFT_CH),
    ("mlstm_z", WIDTH), ("mlstm_qk", 2 * WIDTH), ("mlstm_v", WIDTH),
    ("mlstm_o", WIDTH), ("mlstm_if", 2 * N_HEADS),
    ("attn_z", ATTN_W), ("attn_qkv", 9 * ATTN_W),
    ("merge", N_BRANCH * D_MODEL),
)

NT = (((1,), (1,)), ((), ()))
TN = (((0,), (0,)), ((), ()))


def _dot(a, b, prec=None):
    return jnp.dot(a, b, preferred_element_type=f32, precision=prec)


def _dot_nt(a, b, prec=None):
    return lax.dot_general(a, b, NT, preferred_element_type=f32, precision=prec)


def _dot_tn(a, b, prec=None):
    return lax.dot_general(a, b, TN, preferred_element_type=f32, precision=prec)


def _silu(x):
    return x * jax.nn.sigmoid(x)


def _softplus(x):
    return jnp.maximum(x, 0.0) + jnp.log1p(jnp.exp(-jnp.abs(x)))


def _params(*sem):
    return pltpu.CompilerParams(dimension_semantics=sem, vmem_limit_bytes=VMEM_LIMIT)


def _iota2(shape, dim):
    return lax.broadcasted_iota(jnp.int32, shape, dim)


def _ada_kernel(c_ref, w_ref, b_ref, o_ref):
    o_ref[...] = _dot(_silu(c_ref[...]), w_ref[...], HI) + b_ref[...]


def _ada(c, w, b):
    n = c.shape[0]
    return pl.pallas_call(
        _ada_kernel,
        grid=(3,),
        in_specs=[pl.BlockSpec((n, D_MODEL), lambda j: (0, 0)),
                  pl.BlockSpec((D_MODEL, D_MODEL), lambda j: (0, j)),
                  pl.BlockSpec((1, D_MODEL), lambda j: (0, j))],
        out_specs=pl.BlockSpec((n, D_MODEL), lambda j: (0, j)),
        out_shape=jax.ShapeDtypeStruct((n, 3 * D_MODEL), f32),
        compiler_params=_params("parallel"),
        name="ada_mod",
    )(c, w, b.reshape(1, -1))


def _norm_kernel(x_ref, nw_ref, sc_ref, sh_ref, o_ref):
    x = x_ref[0]
    r = x * lax.rsqrt(jnp.mean(x * x, axis=-1, keepdims=True) + NORM_EPS)
    o_ref[0] = (r * nw_ref[...] * (1.0 + sc_ref[0]) + sh_ref[0]).astype(o_ref.dtype)


def _norm(x3, nw, sc3, sh3, tl):
    b, l, d = x3.shape
    per_row = sc3.shape[1] != 1
    mod_spec = pl.BlockSpec((1, tl if per_row else 1, d), (lambda i, j: (i, j, 0)) if per_row else (lambda i, j: (i, 0, 0)))
    return pl.pallas_call(
        _norm_kernel,
        grid=(b, l // tl),
        in_specs=[pl.BlockSpec((1, tl, d), lambda i, j: (i, j, 0)),
                  pl.BlockSpec((1, d), lambda i, j: (0, 0)),
                  mod_spec, mod_spec],
        out_specs=pl.BlockSpec((1, tl, d), lambda i, j: (i, j, 0)),
        out_shape=jax.ShapeDtypeStruct((b, l, d), bf16),
        compiler_params=_params("parallel", "parallel"),
        name="mod_rmsnorm",
    )(x3, nw.reshape(1, d), sc3, sh3)


def _mm_kernel(x_ref, w_ref, o_ref):
    o_ref[...] = jnp.dot(x_ref[...], w_ref[...], preferred_element_type=f32).astype(o_ref.dtype)


def _matmul(x, w, tm, tn):
    n, k = x.shape
    m = w.shape[1]
    return pl.pallas_call(
        _mm_kernel,
        grid=(m // tn, n // tm),
        in_specs=[pl.BlockSpec((tm, k), lambda j, i: (i, 0)),
                  pl.BlockSpec((k, tn), lambda j, i: (0, j))],
        out_specs=pl.BlockSpec((tm, tn), lambda j, i: (i, j)),
        out_shape=jax.ShapeDtypeStruct((n, m), f32),
        compiler_params=_params("parallel", "parallel"),
        name="in_proj",
    )(x, w)


def _causal_conv(ext_scr, u, c, cw_ref, cb_ref):
    ext_scr[8:8 + c, :] = u
    acc = cb_ref[...]
    for i in range(CONV_W):
        acc = acc + ext_scr[5 + i:5 + i + c, :] * cw_ref[i:i + 1, :]
    tail = ext_scr[c + 5:c + 8, :]
    ext_scr[5:8, :] = tail
    return acc, tail


def _ssm_kernel(proj_ref, conv0_ref, h0_ref, cw_ref, cb_ref, dtb_ref, alog_ref, dsk_ref, nw_ref,
                out_ref, hn_ref, convn_ref, ext_scr, y_scr, *, c):
    @pl.when(pl.program_id(1) == 0)
    def _():
        ext_scr[5:8, :] = conv0_ref[0]
        hn_ref[0] = h0_ref[0]

    u = proj_ref[0, :, WIDTH:WIDTH + SSM_CONV_CH]
    acc, tail = _causal_conv(ext_scr, u, c, cw_ref, cb_ref)
    convn_ref[0] = tail
    xbc = _silu(acc)
    xs = xbc[:, :WIDTH]
    bm = xbc[:, WIDTH:WIDTH + LANES]
    cm = xbc[:, WIDTH + LANES:]
    dt = _softplus(proj_ref[0, :, WIDTH + SSM_CONV_CH:] + dtb_ref[...])
    da = dt * (-jnp.exp(alog_ref[...]))
    row = _iota2((c, c), 0)
    col = _iota2((c, c), 1)
    causal = row >= col
    cs = _dot(causal.astype(f32), da, HI)
    cs_t = cs.T
    dt_t = dt.T
    cs_end = cs[c - 1:c, :]
    for g in range(SSM_GROUPS):
        bg = bm[:, g * SSM_STATE:(g + 1) * SSM_STATE]
        cg = cm[:, g * SSM_STATE:(g + 1) * SSM_STATE]
        gmat = _dot_nt(cg, bg, HI)
        for r in range(N_HEADS // SSM_GROUPS):
            h = g * (N_HEADS // SSM_GROUPS) + r
            sl = slice(h * HEAD_DIM, (h + 1) * HEAD_DIM)
            cs_col = cs[:, h:h + 1]
            seg = jnp.where(causal, cs_col - cs_t[h:h + 1, :], -jnp.inf)
            lmat = gmat * jnp.exp(seg) * dt_t[h:h + 1, :]
            xh = xs[:, sl]
            hprev = hn_ref[0, h]
            y = _dot(lmat, xh, HI) + _dot_nt(cg, hprev, HI) * jnp.exp(cs_col)
            y_scr[:, sl] = y + xh * dsk_ref[:, h:h + 1]
            ce = cs_end[:, h:h + 1]
            wl = jnp.exp(ce - cs_col) * dt[:, h:h + 1]
            hn_ref[0, h] = hprev * jnp.exp(ce) + _dot_tn(xh * wl, bg, HI)
    gy = y_scr[...] * _silu(proj_ref[0, :, :WIDTH])
    gw = WIDTH // SSM_GROUPS
    for g in range(SSM_GROUPS):
        part = gy[:, g * gw:(g + 1) * gw]
        ms = jnp.mean(part * part, axis=-1, keepdims=True)
        out_ref[0, :, g * gw:(g + 1) * gw] = part * lax.rsqrt(ms + NORM_EPS) * nw_ref[:, g * gw:(g + 1) * gw]


def _pad_lanes(v):
    v = v.reshape(1, -1)
    return jnp.pad(v, ((0, 0), (0, LANES - v.shape[1])))


def _ssm(proj, conv0, h0, p, c):
    b, l, wp = proj.shape
    kern = functools.partial(_ssm_kernel, c=c)
    full = lambda shape: pl.BlockSpec(shape, lambda i, j: (0,) * len(shape))
    return pl.pallas_call(
        kern,
        grid=(b, l // c),
        in_specs=[pl.BlockSpec((1, c, wp), lambda i, j: (i, j, 0)),
                  pl.BlockSpec((1, CONV_W - 1, SSM_CONV_CH), lambda i, j: (i, 0, 0)),
                  pl.BlockSpec((1, N_HEADS, HEAD_DIM, SSM_STATE), lambda i, j: (i, 0, 0, 0)),
                  full((CONV_W, SSM_CONV_CH)), full((1, SSM_CONV_CH)),
                  full((1, LANES)), full((1, LANES)), full((1, LANES)), full((1, WIDTH))],
        out_specs=[pl.BlockSpec((1, c, WIDTH), lambda i, j: (i, j, 0)),
                   pl.BlockSpec((1, N_HEADS, HEAD_DIM, SSM_STATE), lambda i, j: (i, 0, 0, 0)),
                   pl.BlockSpec((1, CONV_W - 1, SSM_CONV_CH), lambda i, j: (i, 0, 0))],
        out_shape=[jax.ShapeDtypeStruct((b, l, WIDTH), f32),
                   jax.ShapeDtypeStruct((b, N_HEADS, HEAD_DIM, SSM_STATE), f32),
                   jax.ShapeDtypeStruct((b, CONV_W - 1, SSM_CONV_CH), f32)],
        scratch_shapes=[pltpu.VMEM((c + 8, SSM_CONV_CH), f32), pltpu.VMEM((c, WIDTH), f32)],
        compiler_params=_params("parallel", "arbitrary"),
        name="ssd_branch",
    )(proj, conv0, h0, p["ssm_conv_w"], p["ssm_conv_b"].reshape(1, -1), _pad_lanes(p["ssm_dt_bias"]),
      _pad_lanes(p["ssm_a_log"]), _pad_lanes(p["ssm_d"]), p["ssm_norm_w"].reshape(1, -1))


def _rwkv_kernel(proj_ref, shift0_ref, s0_ref, mu_ref, w0_ref, w2_ref, a0_ref, a2_ref, kk_ref, ka_ref, rk_ref,
                 lnw_ref, lnb_ref, out_ref, sn_ref, shiftn_ref,
                 ext_scr, r_scr, w_scr, k_scr, v_scr, a_scr, b_scr, y_scr, *, c):
    @pl.when(pl.program_id(1) == 0)
    def _():
        ext_scr[7:8, :] = shift0_ref[0]
        sn_ref[0] = s0_ref[0]

    u = proj_ref[0, :, WIDTH:]
    ext_scr[8:8 + c, :] = u
    sh = u + (ext_scr[7:7 + c, :] - u) * mu_ref[...]
    last = ext_scr[c + 7:c + 8, :]
    ext_scr[7:8, :] = last
    shiftn_ref[0] = last
    r = sh[:, :WIDTH]
    k = sh[:, WIDTH:2 * WIDTH]
    v = sh[:, 2 * WIDTH:3 * WIDTH]
    w_lo = sh[:, 3 * WIDTH:3 * WIDTH + RWKV_RANK]
    a_lo = sh[:, 3 * WIDTH + RWKV_RANK:]
    w_log = -_softplus(-(w0_ref[...] + _dot(jnp.tanh(w_lo), w2_ref[...], HI))) - 0.5
    decay = jnp.exp(-jnp.exp(w_log))
    a_in = jax.nn.sigmoid(a0_ref[...] + _dot(a_lo, a2_ref[...], HI))
    kk = k * kk_ref[...]
    k2 = k * (1.0 + (a_in - 1.0) * ka_ref[...])
    for h in range(N_HEADS):
        sl = slice(h * HEAD_DIM, (h + 1) * HEAD_DIM)
        kkh = kk[:, sl]
        kkn = kkh / jnp.maximum(jnp.sqrt(jnp.sum(kkh * kkh, axis=-1, keepdims=True)), 1e-12)
        a_scr[h] = -kkn
        b_scr[h] = kkn * a_in[:, sl]
        r_scr[h] = r[:, sl]
        w_scr[h] = decay[:, sl]
        k_scr[h] = k2[:, sl]
        v_scr[h] = v[:, sl]

    eye = (_iota2((HEAD_DIM, HEAD_DIM), 0) == _iota2((HEAD_DIM, HEAD_DIM), 1)).astype(f32)

    def step(t, carry):
        row = pl.ds(t, 1)
        for h in range(N_HEADS):
            s = sn_ref[0, h]
            sa = jnp.sum(s * a_scr[h, row, :], axis=-1, keepdims=True)
            vcol = jnp.sum(eye * v_scr[h, row, :], axis=-1, keepdims=True)
            s = s * w_scr[h, row, :] + sa * b_scr[h, row, :] + vcol * k_scr[h, row, :]
            sn_ref[0, h] = s
            ycol = jnp.sum(s * r_scr[h, row, :], axis=-1, keepdims=True)
            y_scr[h, row, :] = jnp.sum(ycol * eye, axis=0, keepdims=True)
        return carry

    lax.fori_loop(0, c, step, 0)

    for h in range(N_HEADS):
        sl = slice(h * HEAD_DIM, (h + 1) * HEAD_DIM)
        yh = y_scr[h]
        mu = jnp.mean(yh, axis=-1, keepdims=True)
        var = jnp.mean(jnp.square(yh - mu), axis=-1, keepdims=True)
        ln = (yh - mu) * lax.rsqrt(var + RWKV_LN_EPS) * lnw_ref[:, sl] + lnb_ref[:, sl]
        bonus = jnp.sum(r_scr[h] * k_scr[h] * rk_ref[:, sl], axis=-1, keepdims=True) * v_scr[h]
        out_ref[0, :, sl] = (ln + bonus) * _silu(proj_ref[0, :, sl])


def _rwkv(proj, shift0, s0, p, c):
    b, l, wp = proj.shape
    kern = functools.partial(_rwkv_kernel, c=c)
    full = lambda shape: pl.BlockSpec(shape, lambda i, j: (0,) * len(shape))
    row = lambda name: p[name].reshape(1, -1)
    head_scr = pltpu.VMEM((N_HEADS, c, HEAD_DIM), f32)
    return pl.pallas_call(
        kern,
        grid=(b, l // c),
        in_specs=[pl.BlockSpec((1, c, wp), lambda i, j: (i, j, 0)),
                  pl.BlockSpec((1, 1, RWKV_SHIFT_CH), lambda i, j: (i, 0, 0)),
                  pl.BlockSpec((1, N_HEADS, HEAD_DIM, HEAD_DIM), lambda i, j: (i, 0, 0, 0)),
                  full((1, RWKV_SHIFT_CH)), full((1, WIDTH)), full((RWKV_RANK, WIDTH)), full((1, WIDTH)),
                  full((RWKV_RANK, WIDTH)), full((1, WIDTH)), full((1, WIDTH)), full((1, WIDTH)),
                  full((1, WIDTH)), full((1, WIDTH))],
        out_specs=[pl.BlockSpec((1, c, WIDTH), lambda i, j: (i, j, 0)),
                   pl.BlockSpec((1, N_HEADS, HEAD_DIM, HEAD_DIM), lambda i, j: (i, 0, 0, 0)),
                   pl.BlockSpec((1, 1, RWKV_SHIFT_CH), lambda i, j: (i, 0, 0))],
        out_shape=[jax.ShapeDtypeStruct((b, l, WIDTH), f32),
                   jax.ShapeDtypeStruct((b, N_HEADS, HEAD_DIM, HEAD_DIM), f32),
                   jax.ShapeDtypeStruct((b, 1, RWKV_SHIFT_CH), f32)],
        scratch_shapes=[pltpu.VMEM((c + 8, RWKV_SHIFT_CH), f32)] + [head_scr] * 7,
        compiler_params=_params("parallel", "arbitrary"),
        name="rwkv7_branch",
    )(proj, shift0, s0, row("rwkv_mu"), row("rwkv_w0"), p["rwkv_w2"], row("rwkv_a0"), p["rwkv_a2"],
      row("rwkv_k_k"), row("rwkv_k_a"), row("rwkv_r_k"), row("rwkv_ln_w"), row("rwkv_ln_b"))


def _mlstm_kernel(proj_ref, conv0_ref, c0_ref, n0_ref, m0_ref, cw_ref, cb_ref, gb_ref, nw_ref,
                  out_ref, cn_ref, nn_ref, mn_ref, convn_ref, ext_scr, h_scr, *, c):
    @pl.when(pl.program_id(1) == 0)
    def _():
        ext_scr[5:8, :] = conv0_ref[0]
        cn_ref[0] = c0_ref[0]
        nn_ref[0] = n0_ref[0]
        mn_ref[0] = m0_ref[0]

    u = proj_ref[0, :, WIDTH:3 * WIDTH]
    acc, tail = _causal_conv(ext_scr, u, c, cw_ref, cb_ref)
    convn_ref[0] = tail
    qk = _silu(acc)
    gates = proj_ref[0, :, 5 * WIDTH:] + gb_ref[...]
    lf = -_softplus(-gates)
    row = _iota2((c, c), 0)
    col = _iota2((c, c), 1)
    causal = row >= col
    bc = _dot(causal.astype(f32), lf, HI)
    bc_t = bc.T
    ig_t = gates.T
    m_prev = mn_ref[0]
    lane = _iota2((1, LANES), 1)
    m_out = jnp.zeros((1, LANES), f32)
    for h in range(N_HEADS):
        sl = slice(h * HEAD_DIM, (h + 1) * HEAD_DIM)
        b_col = bc[:, N_HEADS + h:N_HEADS + h + 1]
        dmat = jnp.where(causal, b_col - bc_t[N_HEADS + h:N_HEADS + h + 1, :] + ig_t[h:h + 1, :], -jnp.inf)
        m0 = m_prev[:, h:h + 1]
        gcol = b_col + m0
        mt = jnp.maximum(gcol, jnp.max(dmat, axis=-1, keepdims=True))
        qh = qk[:, sl]
        kh = qk[:, WIDTH + h * HEAD_DIM:WIDTH + (h + 1) * HEAD_DIM] * HEAD_DIM ** -0.5
        vh = proj_ref[0, :, 3 * WIDTH + h * HEAD_DIM:3 * WIDTH + (h + 1) * HEAD_DIM]
        sw = jnp.exp(dmat - mt) * _dot_nt(qh, kh, HI)
        wi = jnp.exp(gcol - mt)
        cprev = cn_ref[0, h]
        nprev = nn_ref[0, h:h + 1, :]
        num = _dot(sw, vh, HI) + wi * _dot(qh, cprev, HI)
        den = jnp.sum(sw, axis=-1, keepdims=True) + wi * jnp.sum(qh * nprev, axis=-1, keepdims=True)
        h_scr[:, sl] = num / jnp.maximum(jnp.abs(den), jnp.exp(-mt))
        m_new = mt[c - 1:c, :]
        b_end = b_col[c - 1:c, :]
        kw = kh * jnp.exp(b_end - b_col + gates[:, h:h + 1] - m_new)
        dec = jnp.exp(b_end + m0 - m_new)
        cn_ref[0, h] = dec * cprev + _dot_tn(kw, vh, HI)
        nn_ref[0, h:h + 1, :] = dec * nprev + jnp.sum(kw, axis=0, keepdims=True)
        m_out = jnp.where(lane == h, m_new, m_out)
    mn_ref[0] = m_out
    for h in range(N_HEADS):
        sl = slice(h * HEAD_DIM, (h + 1) * HEAD_DIM)
        hm = jax.nn.sigmoid(proj_ref[0, :, 4 * WIDTH + h * HEAD_DIM:4 * WIDTH + (h + 1) * HEAD_DIM]) * h_scr[:, sl]
        ms = jnp.mean(hm * hm, axis=-1, keepdims=True)
        out_ref[0, :, sl] = hm * lax.rsqrt(ms + NORM_EPS) * nw_ref[:, sl] * _silu(proj_ref[0, :, sl])


def _mlstm(proj, conv0, c0, n0, m0, p, c):
    b, l, wp = proj.shape
    kern = functools.partial(_mlstm_kernel, c=c)
    full = lambda shape: pl.BlockSpec(shape, lambda i, j: (0,) * len(shape))
    m0p = jnp.pad(m0, ((0, 0), (0, LANES - N_HEADS))).reshape(b, 1, LANES)
    outs = pl.pallas_call(
        kern,
        grid=(b, l // c),
        in_specs=[pl.BlockSpec((1, c, wp), lambda i, j: (i, j, 0)),
                  pl.BlockSpec((1, CONV_W - 1, 2 * WIDTH), lambda i, j: (i, 0, 0)),
                  pl.BlockSpec((1, N_HEADS, HEAD_DIM, HEAD_DIM), lambda i, j: (i, 0, 0, 0)),
                  pl.BlockSpec((1, N_HEADS, HEAD_DIM), lambda i, j: (i, 0, 0)),
                  pl.BlockSpec((1, 1, LANES), lambda i, j: (i, 0, 0)),
                  full((CONV_W, 2 * WIDTH)), full((1, 2 * WIDTH)), full((1, LANES)), full((1, WIDTH))],
        out_specs=[pl.BlockSpec((1, c, WIDTH), lambda i, j: (i, j, 0)),
                   pl.BlockSpec((1, N_HEADS, HEAD_DIM, HEAD_DIM), lambda i, j: (i, 0, 0, 0)),
                   pl.BlockSpec((1, N_HEADS, HEAD_DIM), lambda i, j: (i, 0, 0)),
                   pl.BlockSpec((1, 1, LANES), lambda i, j: (i, 0, 0)),
                   pl.BlockSpec((1, CONV_W - 1, 2 * WIDTH), lambda i, j: (i, 0, 0))],
        out_shape=[jax.ShapeDtypeStruct((b, l, WIDTH), f32),
                   jax.ShapeDtypeStruct((b, N_HEADS, HEAD_DIM, HEAD_DIM), f32),
                   jax.ShapeDtypeStruct((b, N_HEADS, HEAD_DIM), f32),
                   jax.ShapeDtypeStruct((b, 1, LANES), f32),
                   jax.ShapeDtypeStruct((b, CONV_W - 1, 2 * WIDTH), f32)],
        scratch_shapes=[pltpu.VMEM((c + 8, 2 * WIDTH), f32), pltpu.VMEM((c, WIDTH), f32)],
        compiler_params=_params("parallel", "arbitrary"),
        name="mlstm_branch",
    )(proj, conv0, c0, n0, m0p, p["mlstm_conv_w"], p["mlstm_conv_b"].reshape(1, -1),
      _pad_lanes(p["mlstm_gate_b"]), p["mlstm_norm_w"].reshape(1, -1))
    out, cn, nn, mn, convn = outs
    return out, cn, nn, mn[:, 0, :N_HEADS], convn


def _head_mean_matrix():
    hid = jnp.arange(ATTN_W) // HEAD_DIM
    return (hid[:, None] == hid[None, :]).astype(f32) / HEAD_DIM


def _qk_norm(x, hm_ref, w_ref):
    return x * lax.rsqrt(_dot(x * x, hm_ref[...], HI) + NORM_EPS) * w_ref[...]


def _attn_prompt_kernel(q_ref, k_ref, v_ref, z_ref, hm_ref, qw_ref, kw_ref, out_ref, kv_ref,
                        qn_scr, kn_scr, vv_scr, acc_scr, m_scr, d_scr, to_scr, tl_scr, *, l):
    g = pl.program_id(1)
    rb = 256
    n_pairs = ATTN_W // LANES

    def norm_body(i, carry):
        rows = pl.ds(pl.multiple_of(i * rb, rb), rb)
        qn = _qk_norm(q_ref[0, rows, :], hm_ref, qw_ref)
        kn = _qk_norm(k_ref[0, rows, :], hm_ref, kw_ref)
        v = v_ref[0, rows, :]
        kv_ref[0, 0, rows, 0:ATTN_W] = kn
        kv_ref[0, 0, rows, ATTN_W:2 * ATTN_W] = v
        for p in range(n_pairs):
            qn_scr[p, rows, :] = qn[:, p * LANES:(p + 1) * LANES]
            kn_scr[p, rows, :] = kn[:, p * LANES:(p + 1) * LANES]
            vv_scr[p, rows, :] = v[:, p * LANES:(p + 1) * LANES]
        return carry

    lax.fori_loop(0, l // rb, norm_body, 0)

    qb = ATTN_J
    qi = _iota2((qb, qb), 0)
    ki = _iota2((qb, qb), 1)
    scale = HEAD_DIM ** -0.5

    def run_group(dil, first):
        n_iter = l // qb

        def body(i, carry):
            r = i % dil
            blk = i // dil
            u0 = blk * qb
            rows_c = pl.ds(u0 * dil + r, qb, stride=dil)
            rows_p = pl.ds(jnp.maximum(u0 - qb, 0) * dil + r, qb, stride=dil)
            mask_c = ki <= qi
            mask_p = (ki >= qi) & (blk > 0)
            for p in range(n_pairs):
                qv = qn_scr[p, rows_c, :]
                kc = kn_scr[p, rows_c, :]
                vc = vv_scr[p, rows_c, :]
                kp = kn_scr[p, rows_p, :]
                vp = vv_scr[p, rows_p, :]
                for hh in range(LANES // HEAD_DIM):
                    sl = slice(hh * HEAD_DIM, (hh + 1) * HEAD_DIM)
                    sc = jnp.where(mask_c, _dot_nt(qv[:, sl], kc[:, sl]) * scale, -jnp.inf)
                    sp = jnp.where(mask_p, _dot_nt(qv[:, sl], kp[:, sl]) * scale, -jnp.inf)
                    m = jnp.maximum(jnp.max(sc, axis=-1, keepdims=True), jnp.max(sp, axis=-1, keepdims=True))
                    pc = jnp.exp(sc - m)
                    pp = jnp.exp(sp - m)
                    den = jnp.sum(pc, axis=-1, keepdims=True) + jnp.sum(pp, axis=-1, keepdims=True)
                    to_scr[:, sl] = (_dot(pc, vc[:, sl]) + _dot(pp, vp[:, sl])) / den
                    tl_scr[:, sl] = jnp.broadcast_to(m + jnp.log(den), (qb, HEAD_DIM))
                o = to_scr[...]
                lse = tl_scr[...]
                if first:
                    acc_scr[p, rows_c, :] = o
                    m_scr[p, rows_c, :] = lse
                    d_scr[p, rows_c, :] = jnp.ones_like(lse)
                else:
                    m_old = m_scr[p, rows_c, :]
                    m_new = jnp.maximum(m_old, lse)
                    a_old = jnp.exp(m_old - m_new)
                    a_new = jnp.exp(lse - m_new)
                    acc_scr[p, rows_c, :] = acc_scr[p, rows_c, :] * a_old + o * a_new
                    d_scr[p, rows_c, :] = d_scr[p, rows_c, :] * a_old + a_new
                    m_scr[p, rows_c, :] = m_new
            return carry

        lax.fori_loop(0, n_iter, body, 0)

    for gi, (_, dil) in enumerate(ATTN_GROUPS):
        pl.when(g == gi)(functools.partial(run_group, dil, gi == 0))

    @pl.when(g == len(ATTN_GROUPS) - 1)
    def _():
        def fin_body(i, carry):
            rows = pl.ds(pl.multiple_of(i * rb, rb), rb)
            for p in range(n_pairs):
                lanes = slice(p * LANES, (p + 1) * LANES)
                out_ref[0, rows, lanes] = acc_scr[p, rows, :] / d_scr[p, rows, :] * _silu(z_ref[0, rows, lanes])
            return carry

        lax.fori_loop(0, l // rb, fin_body, 0)


def _attn_prompt(proj, p):
    b, l, _ = proj.shape
    ng = len(ATTN_GROUPS)
    kern = functools.partial(_attn_prompt_kernel, l=l)
    col = lambda base: pl.BlockSpec((1, l, ATTN_W), lambda i, g: (i, 0, base + g))
    full = lambda shape: pl.BlockSpec(shape, lambda i, g: (0,) * len(shape))
    pair_scr = pltpu.VMEM((ATTN_W // LANES, l, LANES), f32)
    tile = lambda name: jnp.tile(p[name], ATTN_HEADS).reshape(1, ATTN_W)
    return pl.pallas_call(
        kern,
        grid=(b, ng),
        in_specs=[col(1), col(1 + ng), col(1 + 2 * ng),
                  pl.BlockSpec((1, l, ATTN_W), lambda i, g: (i, 0, 0)),
                  full((ATTN_W, ATTN_W)), full((1, ATTN_W)), full((1, ATTN_W))],
        out_specs=[pl.BlockSpec((1, l, ATTN_W), lambda i, g: (i, 0, 0)),
                   pl.BlockSpec((1, 1, l, 2 * ATTN_W), lambda i, g: (i, g, 0, 0))],
        out_shape=[jax.ShapeDtypeStruct((b, l, ATTN_W), f32),
                   jax.ShapeDtypeStruct((b, ng, l, 2 * ATTN_W), f32)],
        scratch_shapes=[pair_scr] * 6 + [pltpu.VMEM((ATTN_J, LANES), f32)] * 2,
        compiler_params=_params("parallel", "arbitrary"),
        name="dilated_attn_prompt",
    )(proj, proj, proj, proj, _head_mean_matrix(), tile("attn_q_norm"), tile("attn_k_norm"))


def _attn_step_kernel(x_ref, c0_ref, c1_ref, c2_ref, hm_ref, qw_ref, kw_ref, out_ref, kv_ref, *, l):
    x = x_ref[0]
    ng = len(ATTN_GROUPS)
    scale = HEAD_DIM ** -0.5
    outs, lses = [], []
    for gi, ((window, dil), c_ref) in enumerate(zip(ATTN_GROUPS, (c0_ref, c1_ref, c2_ref))):
        w = c_ref.shape[1]
        jmax = window // dil
        shift = dil.bit_length() - 1
        qn = _qk_norm(x[:, (1 + gi) * ATTN_W:(2 + gi) * ATTN_W], hm_ref, qw_ref)
        kn = _qk_norm(x[:, (1 + ng + gi) * ATTN_W:(2 + ng + gi) * ATTN_W], hm_ref, kw_ref)
        v = x[:, (1 + 2 * ng + gi) * ATTN_W:(2 + 2 * ng + gi) * ATTN_W]
        kv_ref[0, gi, :, 0:ATTN_W] = kn
        kv_ref[0, gi, :, ATTN_W:2 * ATTN_W] = v
        dist_p = w + _iota2((l, w), 0) - _iota2((l, w), 1)
        mask_p = ((dist_p & (dil - 1)) == 0) & ((dist_p >> shift) <= jmax)
        dist_n = _iota2((l, l), 0) - _iota2((l, l), 1)
        mask_n = (dist_n >= 0) & ((dist_n & (dil - 1)) == 0) & ((dist_n >> shift) <= jmax)
        og, lg = [], []
        for h in range(ATTN_HEADS):
            sl = slice(h * HEAD_DIM, (h + 1) * HEAD_DIM)
            kc = c_ref[0, :, h * HEAD_DIM:(h + 1) * HEAD_DIM]
            vc = c_ref[0, :, ATTN_W + h * HEAD_DIM:ATTN_W + (h + 1) * HEAD_DIM]
            sp = jnp.where(mask_p, _dot_nt(qn[:, sl], kc) * scale, -jnp.inf)
            sn = jnp.where(mask_n, _dot_nt(qn[:, sl], kn[:, sl]) * scale, -jnp.inf)
            m = jnp.maximum(jnp.max(sp, axis=-1, keepdims=True), jnp.max(sn, axis=-1, keepdims=True))
            pp = jnp.exp(sp - m)
            pn = jnp.exp(sn - m)
            den = jnp.sum(pp, axis=-1, keepdims=True) + jnp.sum(pn, axis=-1, keepdims=True)
            og.append((_dot(pp, vc) + _dot(pn, v[:, sl])) / den)
            lg.append(m + jnp.log(den))
        outs.append(og)
        lses.append(lg)
    for h in range(ATTN_HEADS):
        sl = slice(h * HEAD_DIM, (h + 1) * HEAD_DIM)
        mx = functools.reduce(jnp.maximum, [lses[gi][h] for gi in range(ng)])
        ws = [jnp.exp(lses[gi][h] - mx) for gi in range(ng)]
        tot = functools.reduce(lambda a, b: a + b, ws)
        o = functools.reduce(lambda a, b: a + b, [outs[gi][h] * (ws[gi] / tot) for gi in range(ng)])
        out_ref[0, :, sl] = o * _silu(x[:, sl])


def _attn_step(proj, caches, p):
    b, l, wp = proj.shape
    ng = len(ATTN_GROUPS)
    kern = functools.partial(_attn_step_kernel, l=l)
    full = lambda shape: pl.BlockSpec(shape, lambda i: (0,) * len(shape))
    tile = lambda name: jnp.tile(p[name], ATTN_HEADS).reshape(1, ATTN_W)
    cache_specs = [pl.BlockSpec((1, c.shape[1], 2 * ATTN_W), lambda i: (i, 0, 0)) for c in caches]
    return pl.pallas_call(
        kern,
        grid=(b,),
        in_specs=[pl.BlockSpec((1, l, wp), lambda i: (i, 0, 0))] + cache_specs
                 + [full((ATTN_W, ATTN_W)), full((1, ATTN_W)), full((1, ATTN_W))],
        out_specs=[pl.BlockSpec((1, l, ATTN_W), lambda i: (i, 0, 0)),
                   pl.BlockSpec((1, ng, l, 2 * ATTN_W), lambda i: (i, 0, 0, 0))],
        out_shape=[jax.ShapeDtypeStruct((b, l, ATTN_W), f32),
                   jax.ShapeDtypeStruct((b, ng, l, 2 * ATTN_W), f32)],
        compiler_params=_params("parallel"),
        name="dilated_attn_step",
    )(proj, *caches, _head_mean_matrix(), tile("attn_q_norm"), tile("attn_k_norm"))


def _merge_kernel(oa_ref, ob_ref, oc_ref, od_ref, mg_ref, x_ref, gate_ref, wb_ref, wo_ref, o_ref):
    merged = None
    start = 0
    for bi, ref in enumerate((oa_ref, ob_ref, oc_ref, od_ref)):
        width = ref.shape[-1]
        t = jnp.dot(ref[0].astype(bf16), wb_ref[start:start + width, :], preferred_element_type=f32)
        term = jax.nn.sigmoid(mg_ref[0, :, bi * D_MODEL:(bi + 1) * D_MODEL]) * t
        merged = term if merged is None else merged + term
        start += width
    y = jnp.dot(merged.astype(bf16), wo_ref[...], preferred_element_type=f32)
    o_ref[0] = x_ref[0] + gate_ref[0] * y


def _merge(branches, mg3, x3, gate3, wb, wo, tl):
    b, l, d = x3.shape
    per_row = gate3.shape[1] != 1
    rows = lambda w: pl.BlockSpec((1, tl, w), lambda i, j: (i, j, 0))
    gate_spec = pl.BlockSpec((1, tl if per_row else 1, d), (lambda i, j: (i, j, 0)) if per_row else (lambda i, j: (i, 0, 0)))
    full = lambda shape: pl.BlockSpec(shape, lambda i, j: (0,) * len(shape))
    return pl.pallas_call(
        _merge_kernel,
        grid=(b, l // tl),
        in_specs=[rows(o.shape[-1]) for o in branches] + [rows(N_BRANCH * d), rows(d), gate_spec,
                                                          full(wb.shape), full(wo.shape)],
        out_specs=rows(d),
        out_shape=jax.ShapeDtypeStruct((b, l, d), f32),
        compiler_params=_params("parallel", "parallel"),
        name="merge_out",
    )(*branches, mg3, x3, gate3, wb, wo)


def _split_w_in(w_in):
    cols, start = {}, 0
    for name, size in IN_SEGMENTS:
        cols[name] = w_in[:, start:start + size]
        start += size
    zeros = lambda n: jnp.zeros((w_in.shape[0], n), w_in.dtype)
    groups = {
        "ssm": [cols["ssm_z"], cols["ssm_xbc"], cols["ssm_dt"], zeros(LANES - N_HEADS)],
        "rwkv": [cols["rwkv_z"], cols["rwkv_shift"]],
        "mlstm": [cols["mlstm_z"], cols["mlstm_qk"], cols["mlstm_v"], cols["mlstm_o"], cols["mlstm_if"],
                  zeros(LANES - 2 * N_HEADS)],
        "attn": [cols["attn_z"], cols["attn_qkv"]],
        "merge": [cols["merge"]],
    }
    return {k: jnp.concatenate(v, axis=1).astype(bf16) for k, v in groups.items()}


_PROJ_TN = {"ssm": 1408, "rwkv": 2176, "mlstm": 896, "attn": 1280, "merge": 1024}


def _layer(x, mod, st, kv_past, p, w_groups, wb, wo):
    b, l, d = x.shape
    prompt = kv_past is None
    shift, scale, gate = mod[:, :d], mod[:, d:2 * d], mod[:, 2 * d:]
    if prompt:
        x3, sc3, sh3, g3 = x, scale[:, None], shift[:, None], gate[:, None]
        tl, chunk = 512, 128
        tl_merge = 256
    else:
        rep = lambda t: jnp.repeat(t, l, axis=0)[None]
        x3, sc3, sh3, g3 = x.reshape(1, b * l, d), rep(scale), rep(shift), rep(gate)
        tl, chunk = min(256, b * l), l
        tl_merge = tl
    h2 = _norm(x3, p["norm_w"], sc3, sh3, tl).reshape(b * l, d)
    proj = {k: _matmul(h2, w, min(512, b * l), _PROJ_TN[k]).reshape(b, l, -1) for k, w in w_groups.items()}

    out_a, ssm_new, ssm_conv_new = _ssm(proj["ssm"], st["ssm_conv"], st["ssm"], p, chunk)
    out_b, rwkv_new, shift_new = _rwkv(proj["rwkv"], st["rwkv_shift"][:, None], st["rwkv"], p, chunk)
    out_c, c_new, n_new, m_new, mconv_new = _mlstm(proj["mlstm"], st["mlstm_conv"], st["mlstm_c"], st["mlstm_n"],
                                                   st["mlstm_m"], p, chunk)
    if prompt:
        out_d, kv = _attn_prompt(proj["attn"], p)
        kv_new = [kv[:, gi, l - min(w, l):].reshape(b, min(w, l), 2, ATTN_HEADS, HEAD_DIM)
                  for gi, (w, _) in enumerate(ATTN_GROUPS)]
    else:
        caches = [c.reshape(b, c.shape[1], 2 * ATTN_W) for c in kv_past]
        out_d, kv = _attn_step(proj["attn"], caches, p)
        kv_new = [kv[:, gi].reshape(b, l, 2, ATTN_HEADS, HEAD_DIM) for gi in range(len(ATTN_GROUPS))]

    branches = [o.reshape(x3.shape[0], x3.shape[1], -1) for o in (out_a, out_b, out_c, out_d)]
    mg3 = proj["merge"].reshape(x3.shape[0], x3.shape[1], -1)
    x_new = _merge(branches, mg3, x3, g3, wb, wo, tl_merge).reshape(b, l, d)
    new_state = {"ssm": ssm_new, "ssm_conv": ssm_conv_new, "rwkv": rwkv_new, "rwkv_shift": shift_new[:, 0],
                 "mlstm_c": c_new, "mlstm_n": n_new, "mlstm_m": m_new, "mlstm_conv": mconv_new,
                 "kv_0": kv_new[0], "kv_1": kv_new[1], "kv_2": kv_new[2]}
    return x_new, new_state


_STATE_NAMES = ("ssm", "ssm_conv", "rwkv", "rwkv_shift", "mlstm_c", "mlstm_n", "mlstm_m", "mlstm_conv",
                "kv_0", "kv_1", "kv_2")


def kernel(x_prompt, x_sample, c_prompt, c_sample, state_ssm, state_ssm_conv, state_rwkv, state_rwkv_shift, state_mlstm_c, state_mlstm_n, state_mlstm_m, state_mlstm_conv, cache_kv_w128, cache_kv_w512, cache_kv_w2048, norm_w, ada_w, ada_b, w_in, w_branch, w_out, ssm_conv_w, ssm_conv_b, ssm_dt_bias, ssm_a_log, ssm_d, ssm_norm_w, rwkv_mu, rwkv_w0, rwkv_w2, rwkv_a0, rwkv_a2, rwkv_k_k, rwkv_k_a, rwkv_r_k, rwkv_ln_w, rwkv_ln_b, mlstm_conv_w, mlstm_conv_b, mlstm_gate_b, mlstm_norm_w, attn_q_norm, attn_k_norm):
    weights = {"norm_w": norm_w, "ssm_conv_w": ssm_conv_w, "ssm_conv_b": ssm_conv_b,
               "ssm_dt_bias": ssm_dt_bias, "ssm_a_log": ssm_a_log, "ssm_d": ssm_d,
               "ssm_norm_w": ssm_norm_w, "rwkv_mu": rwkv_mu, "rwkv_w0": rwkv_w0, "rwkv_w2": rwkv_w2,
               "rwkv_a0": rwkv_a0, "rwkv_a2": rwkv_a2, "rwkv_k_k": rwkv_k_k, "rwkv_k_a": rwkv_k_a,
               "rwkv_r_k": rwkv_r_k, "rwkv_ln_w": rwkv_ln_w, "rwkv_ln_b": rwkv_ln_b,
               "mlstm_conv_w": mlstm_conv_w, "mlstm_conv_b": mlstm_conv_b,
               "mlstm_gate_b": mlstm_gate_b, "mlstm_norm_w": mlstm_norm_w,
               "attn_q_norm": attn_q_norm, "attn_k_norm": attn_k_norm}
    bp = x_prompt.shape[0]
    fresh = {"ssm": jnp.zeros((bp, N_HEADS, HEAD_DIM, SSM_STATE), f32),
             "ssm_conv": jnp.zeros((bp, CONV_W - 1, SSM_CONV_CH), f32),
             "rwkv": jnp.zeros((bp, N_HEADS, HEAD_DIM, HEAD_DIM), f32),
             "rwkv_shift": jnp.zeros((bp, RWKV_SHIFT_CH), f32),
             "mlstm_c": jnp.zeros((bp, N_HEADS, HEAD_DIM, HEAD_DIM), f32),
             "mlstm_n": jnp.zeros((bp, N_HEADS, HEAD_DIM), f32),
             "mlstm_m": jnp.zeros((bp, N_HEADS), f32),
             "mlstm_conv": jnp.zeros((bp, CONV_W - 1, 2 * WIDTH), f32)}
    c_all = jnp.concatenate([c_prompt, c_sample], axis=0)
    y_prompt, y_sample = x_prompt, x_sample
    prompt_states, sample_states = [], []
    for layer in range(DEPTH):
        p = {name: arr[layer] for name, arr in weights.items()}
        mod = _ada(c_all, ada_w[layer], ada_b[layer])
        w_groups = _split_w_in(w_in[layer])
        wb = w_branch[layer].astype(bf16)
        wo = w_out[layer].astype(bf16)
        y_prompt, sp = _layer(y_prompt, mod[:bp], fresh, None, p, w_groups, wb, wo)
        st = {"ssm": state_ssm[layer], "ssm_conv": state_ssm_conv[layer], "rwkv": state_rwkv[layer],
              "rwkv_shift": state_rwkv_shift[layer], "mlstm_c": state_mlstm_c[layer],
              "mlstm_n": state_mlstm_n[layer], "mlstm_m": state_mlstm_m[layer],
              "mlstm_conv": state_mlstm_conv[layer]}
        y_sample, ss = _layer(y_sample, mod[bp:], st,
                              (cache_kv_w128[layer], cache_kv_w512[layer], cache_kv_w2048[layer]),
                              p, w_groups, wb, wo)
        prompt_states.append(sp)
        sample_states.append(ss)
    stack = lambda states, name: jnp.stack([s[name] for s in states])
    return ((y_prompt, y_sample)
            + tuple(stack(prompt_states, n) for n in _STATE_NAMES)
            + tuple(stack(sample_states, n) for n in _STATE_NAMES))
```

```python
import functools
import math

import jax
import jax.numpy as jnp
from jax import lax
from jax.experimental import pallas as pl
from jax.experimental.pallas import tpu as pltpu

f32 = jnp.float32
bf16 = jnp.bfloat16
HI = lax.Precision.HIGHEST

D_MODEL = 1024
DEPTH = 2
HEAD_DIM = 64
NORM_EPS = 1e-6
CONV_W = 4
N_HEADS = 8
WIDTH = N_HEADS * HEAD_DIM
SSM_GROUPS = 2
SSM_STATE = 64
SSM_CONV_CH = WIDTH + 2 * SSM_GROUPS * SSM_STATE
RWKV_RANK = 64
RWKV_SHIFT_CH = 3 * WIDTH + 2 * RWKV_RANK
RWKV_LN_EPS = 64e-5
RWKV_CHUNK = 64
ATTN_GROUPS = ((128, 1), (512, 4), (2048, 16))
ATTN_HEADS = 4
ATTN_W = ATTN_HEADS * HEAD_DIM
ATTN_J = 128
N_BRANCH = 4
LANES = 128
VMEM_LIMIT = 52 * 1024 * 1024

IN_SEGMENTS = (
    ("ssm_z", WIDTH), ("ssm_xbc", SSM_CONV_CH), ("ssm_dt", N_HEADS),
    ("rwkv_z", WIDTH), ("rwkv_shift", RWKV_SHIFT_CH),
    ("mlstm_z", WIDTH), ("mlstm_qk", 2 * WIDTH), ("mlstm_v", WIDTH),
    ("mlstm_o", WIDTH), ("mlstm_if", 2 * N_HEADS),
    ("attn_z", ATTN_W), ("attn_qkv", 9 * ATTN_W),
    ("merge", N_BRANCH * D_MODEL),
)

NT = (((1,), (1,)), ((), ()))
TN = (((0,), (0,)), ((), ()))


def _dot(a, b, prec=None):
    return jnp.dot(a, b, preferred_element_type=f32, precision=prec)


def _dot_nt(a, b, prec=None):
    return lax.dot_general(a, b, NT, preferred_element_type=f32, precision=prec)


def _dot_tn(a, b, prec=None):
    return lax.dot_general(a, b, TN, preferred_element_type=f32, precision=prec)


def _silu(x):
    return x * jax.nn.sigmoid(x)


def _softplus(x):
    return jnp.maximum(x, 0.0) + jnp.log1p(jnp.exp(-jnp.abs(x)))


def _params(*sem):
    return pltpu.CompilerParams(dimension_semantics=sem, vmem_limit_bytes=VMEM_LIMIT)


def _iota2(shape, dim):
    return lax.broadcasted_iota(jnp.int32, shape, dim)


def _ada_kernel(c_ref, w_ref, b_ref, o_ref):
    o_ref[...] = _dot(_silu(c_ref[...]), w_ref[...], HI) + b_ref[...]


def _ada(c, w, b):
    n = c.shape[0]
    return pl.pallas_call(
        _ada_kernel,
        grid=(3,),
        in_specs=[pl.BlockSpec((n, D_MODEL), lambda j: (0, 0)),
                  pl.BlockSpec((D_MODEL, D_MODEL), lambda j: (0, j)),
                  pl.BlockSpec((1, D_MODEL), lambda j: (0, j))],
        out_specs=pl.BlockSpec((n, D_MODEL), lambda j: (0, j)),
        out_shape=jax.ShapeDtypeStruct((n, 3 * D_MODEL), f32),
        compiler_params=_params("parallel"),
        name="ada_mod",
    )(c, w, b.reshape(1, -1))


def _norm_kernel(x_ref, nw_ref, sc_ref, sh_ref, o_ref):
    x = x_ref[0]
    r = x * lax.rsqrt(jnp.mean(x * x, axis=-1, keepdims=True) + NORM_EPS)
    o_ref[0] = (r * nw_ref[...] * (1.0 + sc_ref[0]) + sh_ref[0]).astype(o_ref.dtype)


def _norm(x3, nw, sc3, sh3, tl):
    b, l, d = x3.shape
    per_row = sc3.shape[1] != 1
    mod_spec = pl.BlockSpec((1, tl if per_row else 1, d), (lambda i, j: (i, j, 0)) if per_row else (lambda i, j: (i, 0, 0)))
    return pl.pallas_call(
        _norm_kernel,
        grid=(b, l // tl),
        in_specs=[pl.BlockSpec((1, tl, d), lambda i, j: (i, j, 0)),
                  pl.BlockSpec((1, d), lambda i, j: (0, 0)),
                  mod_spec, mod_spec],
        out_specs=pl.BlockSpec((1, tl, d), lambda i, j: (i, j, 0)),
        out_shape=jax.ShapeDtypeStruct((b, l, d), bf16),
        compiler_params=_params("parallel", "parallel"),
        name="mod_rmsnorm",
    )(x3, nw.reshape(1, d), sc3, sh3)


def _mm_kernel(x_ref, w_ref, o_ref):
    o_ref[...] = jnp.dot(x_ref[...], w_ref[...], preferred_element_type=f32).astype(o_ref.dtype)


def _matmul(x, w, tm, tn):
    n, k = x.shape
    m = w.shape[1]
    return pl.pallas_call(
        _mm_kernel,
        grid=(m // tn, n // tm),
        in_specs=[pl.BlockSpec((tm, k), lambda j, i: (i, 0)),
                  pl.BlockSpec((k, tn), lambda j, i: (0, j))],
        out_specs=pl.BlockSpec((tm, tn), lambda j, i: (i, j)),
        out_shape=jax.ShapeDtypeStruct((n, m), f32),
        compiler_params=_params("parallel", "parallel"),
        name="in_proj",
    )(x, w)


def _causal_conv(ext_scr, u, c, cw_ref, cb_ref):
    ext_scr[8:8 + c, :] = u
    acc = cb_ref[...]
    for i in range(CONV_W):
        acc = acc + ext_scr[5 + i:5 + i + c, :] * cw_ref[i:i + 1, :]
    tail = ext_scr[c + 5:c + 8, :]
    ext_scr[5:8, :] = tail
    return acc, tail


def _ssm_kernel(proj_ref, conv0_ref, h0_ref, cw_ref, cb_ref, dtb_ref, alog_ref, dsk_ref, nw_ref,
                out_ref, hn_ref, convn_ref, ext_scr, y_scr, *, c):
    @pl.when(pl.program_id(1) == 0)
    def _():
        ext_scr[5:8, :] = conv0_ref[0]
        hn_ref[0] = h0_ref[0]

    u = proj_ref[0, :, WIDTH:WIDTH + SSM_CONV_CH]
    acc, tail = _causal_conv(ext_scr, u, c, cw_ref, cb_ref)
    convn_ref[0] = tail
    xbc = _silu(acc)
    xs = xbc[:, :WIDTH]
    bm = xbc[:, WIDTH:WIDTH + LANES]
    cm = xbc[:, WIDTH + LANES:]
    dt = _softplus(proj_ref[0, :, WIDTH + SSM_CONV_CH:] + dtb_ref[...])
    da = dt * (-jnp.exp(alog_ref[...]))
    row = _iota2((c, c), 0)
    col = _iota2((c, c), 1)
    causal = row >= col
    cs = _dot(causal.astype(f32), da, HI)
    cs_t = cs.T
    dt_t = dt.T
    cs_end = cs[c - 1:c, :]
    for g in range(SSM_GROUPS):
        bg = bm[:, g * SSM_STATE:(g + 1) * SSM_STATE]
        cg = cm[:, g * SSM_STATE:(g + 1) * SSM_STATE]
        gmat = _dot_nt(cg, bg, HI)
        for r in range(N_HEADS // SSM_GROUPS):
            h = g * (N_HEADS // SSM_GROUPS) + r
            sl = slice(h * HEAD_DIM, (h + 1) * HEAD_DIM)
            cs_col = cs[:, h:h + 1]
            seg = jnp.where(causal, cs_col - cs_t[h:h + 1, :], -jnp.inf)
            lmat = gmat * jnp.exp(seg) * dt_t[h:h + 1, :]
            xh = xs[:, sl]
            hprev = hn_ref[0, h]
            y = _dot(lmat, xh, HI) + _dot_nt(cg, hprev, HI) * jnp.exp(cs_col)
            y_scr[:, sl] = y + xh * dsk_ref[:, h:h + 1]
            ce = cs_end[:, h:h + 1]
            wl = jnp.exp(ce - cs_col) * dt[:, h:h + 1]
            hn_ref[0, h] = hprev * jnp.exp(ce) + _dot_tn(xh * wl, bg, HI)
    gy = y_scr[...] * _silu(proj_ref[0, :, :WIDTH])
    gw = WIDTH // SSM_GROUPS
    for g in range(SSM_GROUPS):
        part = gy[:, g * gw:(g + 1) * gw]
        ms = jnp.mean(part * part, axis=-1, keepdims=True)
        out_ref[0, :, g * gw:(g + 1) * gw] = part * lax.rsqrt(ms + NORM_EPS) * nw_ref[:, g * gw:(g + 1) * gw]


def _pad_lanes(v):
    v = v.reshape(1, -1)
    return jnp.pad(v, ((0, 0), (0, LANES - v.shape[1])))


def _ssm(proj, conv0, h0, p, c):
    b, l, wp = proj.shape
    kern = functools.partial(_ssm_kernel, c=c)
    full = lambda shape: pl.BlockSpec(shape, lambda i, j: (0,) * len(shape))
    return pl.pallas_call(
        kern,
        grid=(b, l // c),
        in_specs=[pl.BlockSpec((1, c, wp), lambda i, j: (i, j, 0)),
                  pl.BlockSpec((1, CONV_W - 1, SSM_CONV_CH), lambda i, j: (i, 0, 0)),
                  pl.BlockSpec((1, N_HEADS, HEAD_DIM, SSM_STATE), lambda i, j: (i, 0, 0, 0)),
                  full((CONV_W, SSM_CONV_CH)), full((1, SSM_CONV_CH)),
                  full((1, LANES)), full((1, LANES)), full((1, LANES)), full((1, WIDTH))],
        out_specs=[pl.BlockSpec((1, c, WIDTH), lambda i, j: (i, j, 0)),
                   pl.BlockSpec((1, N_HEADS, HEAD_DIM, SSM_STATE), lambda i, j: (i, 0, 0, 0)),
                   pl.BlockSpec((1, CONV_W - 1, SSM_CONV_CH), lambda i, j: (i, 0, 0))],
        out_shape=[jax.ShapeDtypeStruct((b, l, WIDTH), f32),
                   jax.ShapeDtypeStruct((b, N_HEADS, HEAD_DIM, SSM_STATE), f32),
                   jax.ShapeDtypeStruct((b, CONV_W - 1, SSM_CONV_CH), f32)],
        scratch_shapes=[pltpu.VMEM((c + 8, SSM_CONV_CH), f32), pltpu.VMEM((c, WIDTH), f32)],
        compiler_params=_params("parallel", "arbitrary"),
        name="ssd_branch",
    )(proj, conv0, h0, p["ssm_conv_w"], p["ssm_conv_b"].reshape(1, -1), _pad_lanes(p["ssm_dt_bias"]),
      _pad_lanes(p["ssm_a_log"]), _pad_lanes(p["ssm_d"]), p["ssm_norm_w"].reshape(1, -1))


def _rwkv_kernel(proj_ref, shift0_ref, s0_ref, mu_ref, w0_ref, w2_ref, a0_ref, a2_ref, kk_ref, ka_ref, rk_ref,
                 lnw_ref, lnb_ref, out_ref, sn_ref, shiftn_ref,
                 ext_scr, r_scr, w_scr, k_scr, v_scr, a_scr, b_scr, y_scr, *, c):
    @pl.when(pl.program_id(1) == 0)
    def _():
        ext_scr[7:8, :] = shift0_ref[0]
        sn_ref[0] = s0_ref[0]

    u = proj_ref[0, :, WIDTH:]
    ext_scr[8:8 + c, :] = u
    sh = u + (ext_scr[7:7 + c, :] - u) * mu_ref[...]
    last = ext_scr[c + 7:c + 8, :]
    ext_scr[7:8, :] = last
    shiftn_ref[0] = last
    r = sh[:, :WIDTH]
    k = sh[:, WIDTH:2 * WIDTH]
    v = sh[:, 2 * WIDTH:3 * WIDTH]
    w_lo = sh[:, 3 * WIDTH:3 * WIDTH + RWKV_RANK]
    a_lo = sh[:, 3 * WIDTH + RWKV_RANK:]
    w_log = -_softplus(-(w0_ref[...] + _dot(jnp.tanh(w_lo), w2_ref[...], HI))) - 0.5
    decay = jnp.exp(-jnp.exp(w_log))
    a_in = jax.nn.sigmoid(a0_ref[...] + _dot(a_lo, a2_ref[...], HI))
    kk = k * kk_ref[...]
    k2 = k * (1.0 + (a_in - 1.0) * ka_ref[...])
    for h in range(N_HEADS):
        sl = slice(h * HEAD_DIM, (h + 1) * HEAD_DIM)
        kkh = kk[:, sl]
        kkn = kkh / jnp.maximum(jnp.sqrt(jnp.sum(kkh * kkh, axis=-1, keepdims=True)), 1e-12)
        a_scr[h] = -kkn
        b_scr[h] = kkn * a_in[:, sl]
        r_scr[h] = r[:, sl]
        w_scr[h] = decay[:, sl]
        k_scr[h] = k2[:, sl]
        v_scr[h] = v[:, sl]

    eye = (_iota2((HEAD_DIM, HEAD_DIM), 0) == _iota2((HEAD_DIM, HEAD_DIM), 1)).astype(f32)

    def step(t, carry):
        row = pl.ds(t, 1)
        for h in range(N_HEADS):
            s = sn_ref[0, h]
            sa = jnp.sum(s * a_scr[h, row, :], axis=-1, keepdims=True)
            vcol = jnp.sum(eye * v_scr[h, row, :], axis=-1, keepdims=True)
            s = s * w_scr[h, row, :] + sa * b_scr[h, row, :] + vcol * k_scr[h, row, :]
            sn_ref[0, h] = s
            ycol = jnp.sum(s * r_scr[h, row, :], axis=-1, keepdims=True)
            y_scr[h, row, :] = jnp.sum(ycol * eye, axis=0, keepdims=True)
        return carry

    lax.fori_loop(0, c, step, 0)

    for h in range(N_HEADS):
        sl = slice(h * HEAD_DIM, (h + 1) * HEAD_DIM)
        yh = y_scr[h]
        mu = jnp.mean(yh, axis=-1, keepdims=True)
        var = jnp.mean(jnp.square(yh - mu), axis=-1, keepdims=True)
        ln = (yh - mu) * lax.rsqrt(var + RWKV_LN_EPS) * lnw_ref[:, sl] + lnb_ref[:, sl]
        bonus = jnp.sum(r_scr[h] * k_scr[h] * rk_ref[:, sl], axis=-1, keepdims=True) * v_scr[h]
        out_ref[0, :, sl] = (ln + bonus) * _silu(proj_ref[0, :, sl])


def _rwkv(proj, shift0, s0, p, c):
    b, l, wp = proj.shape
    kern = functools.partial(_rwkv_kernel, c=c)
    full = lambda shape: pl.BlockSpec(shape, lambda i, j: (0,) * len(shape))
    row = lambda name: p[name].reshape(1, -1)
    head_scr = pltpu.VMEM((N_HEADS, c, HEAD_DIM), f32)
    return pl.pallas_call(
        kern,
        grid=(b, l // c),
        in_specs=[pl.BlockSpec((1, c, wp), lambda i, j: (i, j, 0)),
                  pl.BlockSpec((1, 1, RWKV_SHIFT_CH), lambda i, j: (i, 0, 0)),
                  pl.BlockSpec((1, N_HEADS, HEAD_DIM, HEAD_DIM), lambda i, j: (i, 0, 0, 0)),
                  full((1, RWKV_SHIFT_CH)), full((1, WIDTH)), full((RWKV_RANK, WIDTH)), full((1, WIDTH)),
                  full((RWKV_RANK, WIDTH)), full((1, WIDTH)), full((1, WIDTH)), full((1, WIDTH)),
                  full((1, WIDTH)), full((1, WIDTH))],
        out_specs=[pl.BlockSpec((1, c, WIDTH), lambda i, j: (i, j, 0)),
                   pl.BlockSpec((1, N_HEADS, HEAD_DIM, HEAD_DIM), lambda i, j: (i, 0, 0, 0)),
                   pl.BlockSpec((1, 1, RWKV_SHIFT_CH), lambda i, j: (i, 0, 0))],
        out_shape=[jax.ShapeDtypeStruct((b, l, WIDTH), f32),
                   jax.ShapeDtypeStruct((b, N_HEADS, HEAD_DIM, HEAD_DIM), f32),
                   jax.ShapeDtypeStruct((b, 1, RWKV_SHIFT_CH), f32)],
        scratch_shapes=[pltpu.VMEM((c + 8, RWKV_SHIFT_CH), f32)] + [head_scr] * 7,
        compiler_params=_params("parallel", "arbitrary"),
        name="rwkv7_branch",
    )(proj, shift0, s0, row("rwkv_mu"), row("rwkv_w0"), p["rwkv_w2"], row("rwkv_a0"), p["rwkv_a2"],
      row("rwkv_k_k"), row("rwkv_k_a"), row("rwkv_r_k"), row("rwkv_ln_w"), row("rwkv_ln_b"))


_RWKV_MM = "bf16"


def _rmm(a, b, dims):
    if _RWKV_MM == "hi":
        return lax.dot_general(a, b, dims, preferred_element_type=f32, precision=HI)
    ah = a.astype(bf16)
    bh = b.astype(bf16)
    out = lax.dot_general(ah, bh, dims, preferred_element_type=f32)
    if _RWKV_MM == "x3":
        al = (a - ah.astype(f32)).astype(bf16)
        bl = (b - bh.astype(f32)).astype(bf16)
        out = out + lax.dot_general(ah, bl, dims, preferred_element_type=f32)
        out = out + lax.dot_general(al, bh, dims, preferred_element_type=f32)
    return out


NN = (((1,), (0,)), ((), ()))


def _rwkv_chunk_kernel(proj_ref, shift0_ref, s0_ref, mu_ref, w0_ref, w2_ref, a0_ref, a2_ref, kk_ref, ka_ref, rk_ref,
                       lnw_ref, lnb_ref, out_ref, sn_ref, shiftn_ref, ext_scr, *, c):
    @pl.when(pl.program_id(1) == 0)
    def _():
        ext_scr[7:8, :] = shift0_ref[0]
        sn_ref[0] = s0_ref[0]

    u = proj_ref[0, :, WIDTH:]
    ext_scr[8:8 + c, :] = u
    sh = u + (ext_scr[7:7 + c, :] - u) * mu_ref[...]
    last = ext_scr[c + 7:c + 8, :]
    ext_scr[7:8, :] = last
    shiftn_ref[0] = last
    r = sh[:, :WIDTH]
    k = sh[:, WIDTH:2 * WIDTH]
    v = sh[:, 2 * WIDTH:3 * WIDTH]
    w_lo = sh[:, 3 * WIDTH:3 * WIDTH + RWKV_RANK]
    a_lo = sh[:, 3 * WIDTH + RWKV_RANK:]
    w_log = -_softplus(-(w0_ref[...] + _dot(jnp.tanh(w_lo), w2_ref[...], HI))) - 0.5
    lw = -jnp.exp(w_log)
    a_in = jax.nn.sigmoid(a0_ref[...] + _dot(a_lo, a2_ref[...], HI))
    kk = k * kk_ref[...]
    k2 = k * (1.0 + (a_in - 1.0) * ka_ref[...])

    row = _iota2((c, c), 0)
    col = _iota2((c, c), 1)
    cl = _dot((row >= col).astype(f32), lw, HI)
    cl_end = cl[c - 1:c, :]
    g_fwd = jnp.exp(cl)
    g_inv = jnp.exp(-cl)
    g_rem = jnp.exp(cl_end - cl)
    g_prev = jnp.exp(cl - lw)
    g_end = jnp.exp(cl_end)

    row2 = _iota2((c, 2 * c), 0)
    col2 = _iota2((c, 2 * c), 1)
    col2 = jnp.where(col2 >= c, col2 - c, col2)
    strict2 = row2 > col2
    incl2 = row2 >= col2
    n_double = (c - 1).bit_length()

    heads = range(N_HEADS)
    sls = [slice(h * HEAD_DIM, (h + 1) * HEAD_DIM) for h in heads]
    s0s = [sn_ref[0, h] for h in heads]
    bhs, ars, bks = [], [], []
    for sl in sls:
        kkh = kk[:, sl]
        kkn = kkh / jnp.maximum(jnp.sqrt(jnp.sum(kkh * kkh, axis=-1, keepdims=True)), 1e-12)
        bhs.append(kkn * a_in[:, sl])
        ars.append(jnp.concatenate([-kkn * g_prev[:, sl], r[:, sl] * g_fwd[:, sl]], axis=0))
        bks.append(jnp.concatenate([bhs[-1] * g_inv[:, sl], k2[:, sl] * g_inv[:, sl]], axis=0))
    m4s = [_rmm(ars[h], bks[h], NT) for h in heads]
    ahs = [_rmm(ars[h], s0s[h], NT) for h in heads]
    tops = [jnp.where(strict2, m4[:c, :], 0.0) for m4 in m4s]
    bots = [jnp.where(incl2, m4[c:, :], 0.0) for m4 in m4s]
    amats = [top[:, :c] for top in tops]
    xs = [ahs[h][:c] + _rmm(tops[h][:, c:], v[:, sls[h]], NN) for h in heads]
    for i in range(n_double):
        xs = [xs[h] + _rmm(amats[h], xs[h], NN) for h in heads]
        if i + 1 < n_double:
            amats = [_rmm(amats[h], amats[h], NN) for h in heads]
    pvs = [jnp.concatenate([xs[h], v[:, sls[h]]], axis=0) for h in heads]
    ys = [ahs[h][c:] + _rmm(bots[h], pvs[h], NN) for h in heads]
    for h in heads:
        sl = sls[h]
        bk_end = jnp.concatenate([bhs[h] * g_rem[:, sl], k2[:, sl] * g_rem[:, sl]], axis=0)
        sn_ref[0, h] = s0s[h] * g_end[:, sl] + _rmm(pvs[h], bk_end, TN)

    for h in heads:
        sl = sls[h]
        y = ys[h]
        vh = v[:, sl]
        mu = jnp.mean(y, axis=-1, keepdims=True)
        var = jnp.mean(jnp.square(y - mu), axis=-1, keepdims=True)
        ln = (y - mu) * lax.rsqrt(var + RWKV_LN_EPS) * lnw_ref[:, sl] + lnb_ref[:, sl]
        bonus = jnp.sum(r[:, sl] * k2[:, sl] * rk_ref[:, sl], axis=-1, keepdims=True) * vh
        out_ref[0, :, sl] = (ln + bonus) * _silu(proj_ref[0, :, sl])


def _rwkv_chunked(proj, shift0, s0, p, c):
    b, l, wp = proj.shape
    kern = functools.partial(_rwkv_chunk_kernel, c=c)
    full = lambda shape: pl.BlockSpec(shape, lambda i, j: (0,) * len(shape))
    row = lambda name: p[name].reshape(1, -1)
    return pl.pallas_call(
        kern,
        grid=(b, l // c),
        in_specs=[pl.BlockSpec((1, c, wp), lambda i, j: (i, j, 0)),
                  pl.BlockSpec((1, 1, RWKV_SHIFT_CH), lambda i, j: (i, 0, 0)),
                  pl.BlockSpec((1, N_HEADS, HEAD_DIM, HEAD_DIM), lambda i, j: (i, 0, 0, 0)),
                  full((1, RWKV_SHIFT_CH)), full((1, WIDTH)), full((RWKV_RANK, WIDTH)), full((1, WIDTH)),
                  full((RWKV_RANK, WIDTH)), full((1, WIDTH)), full((1, WIDTH)), full((1, WIDTH)),
                  full((1, WIDTH)), full((1, WIDTH))],
        out_specs=[pl.BlockSpec((1, c, WIDTH), lambda i, j: (i, j, 0)),
                   pl.BlockSpec((1, N_HEADS, HEAD_DIM, HEAD_DIM), lambda i, j: (i, 0, 0, 0)),
                   pl.BlockSpec((1, 1, RWKV_SHIFT_CH), lambda i, j: (i, 0, 0))],
        out_shape=[jax.ShapeDtypeStruct((b, l, WIDTH), f32),
                   jax.ShapeDtypeStruct((b, N_HEADS, HEAD_DIM, HEAD_DIM), f32),
                   jax.ShapeDtypeStruct((b, 1, RWKV_SHIFT_CH), f32)],
        scratch_shapes=[pltpu.VMEM((c + 8, RWKV_SHIFT_CH), f32)],
        compiler_params=_params("parallel", "arbitrary"),
        name="rwkv7_chunked",
    )(proj, shift0, s0, row("rwkv_mu"), row("rwkv_w0"), p["rwkv_w2"], row("rwkv_a0"), p["rwkv_a2"],
      row("rwkv_k_k"), row("rwkv_k_a"), row("rwkv_r_k"), row("rwkv_ln_w"), row("rwkv_ln_b"))


def _mlstm_kernel(proj_ref, conv0_ref, c0_ref, n0_ref, m0_ref, cw_ref, cb_ref, gb_ref, nw_ref,
                  out_ref, cn_ref, nn_ref, mn_ref, convn_ref, ext_scr, h_scr, *, c):
    @pl.when(pl.program_id(1) == 0)
    def _():
        ext_scr[5:8, :] = conv0_ref[0]
        cn_ref[0] = c0_ref[0]
        nn_ref[0] = n0_ref[0]
        mn_ref[0] = m0_ref[0]

    u = proj_ref[0, :, WIDTH:3 * WIDTH]
    acc, tail = _causal_conv(ext_scr, u, c, cw_ref, cb_ref)
    convn_ref[0] = tail
    qk = _silu(acc)
    gates = proj_ref[0, :, 5 * WIDTH:] + gb_ref[...]
    lf = -_softplus(-gates)
    row = _iota2((c, c), 0)
    col = _iota2((c, c), 1)
    causal = row >= col
    bc = _dot(causal.astype(f32), lf, HI)
    bc_t = bc.T
    ig_t = gates.T
    m_prev = mn_ref[0]
    lane = _iota2((1, LANES), 1)
    m_out = jnp.zeros((1, LANES), f32)
    for h in range(N_HEADS):
        sl = slice(h * HEAD_DIM, (h + 1) * HEAD_DIM)
        b_col = bc[:, N_HEADS + h:N_HEADS + h + 1]
        dmat = jnp.where(causal, b_col - bc_t[N_HEADS + h:N_HEADS + h + 1, :] + ig_t[h:h + 1, :], -jnp.inf)
        m0 = m_prev[:, h:h + 1]
        gcol = b_col + m0
        mt = jnp.maximum(gcol, jnp.max(dmat, axis=-1, keepdims=True))
        qh = qk[:, sl]
        kh = qk[:, WIDTH + h * HEAD_DIM:WIDTH + (h + 1) * HEAD_DIM] * HEAD_DIM ** -0.5
        vh = proj_ref[0, :, 3 * WIDTH + h * HEAD_DIM:3 * WIDTH + (h + 1) * HEAD_DIM]
        sw = jnp.exp(dmat - mt) * _dot_nt(qh, kh, HI)
        wi = jnp.exp(gcol - mt)
        cprev = cn_ref[0, h]
        nprev = nn_ref[0, h:h + 1, :]
        num = _dot(sw, vh, HI) + wi * _dot(qh, cprev, HI)
        den = jnp.sum(sw, axis=-1, keepdims=True) + wi * jnp.sum(qh * nprev, axis=-1, keepdims=True)
        h_scr[:, sl] = num / jnp.maximum(jnp.abs(den), jnp.exp(-mt))
        m_new = mt[c - 1:c, :]
        b_end = b_col[c - 1:c, :]
        kw = kh * jnp.exp(b_end - b_col + gates[:, h:h + 1] - m_new)
        dec = jnp.exp(b_end + m0 - m_new)
        cn_ref[0, h] = dec * cprev + _dot_tn(kw, vh, HI)
        nn_ref[0, h:h + 1, :] = dec * nprev + jnp.sum(kw, axis=0, keepdims=True)
        m_out = jnp.where(lane == h, m_new, m_out)
    mn_ref[0] = m_out
    for h in range(N_HEADS):
        sl = slice(h * HEAD_DIM, (h + 1) * HEAD_DIM)
        hm = jax.nn.sigmoid(proj_ref[0, :, 4 * WIDTH + h * HEAD_DIM:4 * WIDTH + (h + 1) * HEAD_DIM]) * h_scr[:, sl]
        ms = jnp.mean(hm * hm, axis=-1, keepdims=True)
        out_ref[0, :, sl] = hm * lax.rsqrt(ms + NORM_EPS) * nw_ref[:, sl] * _silu(proj_ref[0, :, sl])


def _mlstm(proj, conv0, c0, n0, m0, p, c):
    b, l, wp = proj.shape
    kern = functools.partial(_mlstm_kernel, c=c)
    full = lambda shape: pl.BlockSpec(shape, lambda i, j: (0,) * len(shape))
    m0p = jnp.pad(m0, ((0, 0), (0, LANES - N_HEADS))).reshape(b, 1, LANES)
    outs = pl.pallas_call(
        kern,
        grid=(b, l // c),
        in_specs=[pl.BlockSpec((1, c, wp), lambda i, j: (i, j, 0)),
                  pl.BlockSpec((1, CONV_W - 1, 2 * WIDTH), lambda i, j: (i, 0, 0)),
                  pl.BlockSpec((1, N_HEADS, HEAD_DIM, HEAD_DIM), lambda i, j: (i, 0, 0, 0)),
                  pl.BlockSpec((1, N_HEADS, HEAD_DIM), lambda i, j: (i, 0, 0)),
                  pl.BlockSpec((1, 1, LANES), lambda i, j: (i, 0, 0)),
                  full((CONV_W, 2 * WIDTH)), full((1, 2 * WIDTH)), full((1, LANES)), full((1, WIDTH))],
        out_specs=[pl.BlockSpec((1, c, WIDTH), lambda i, j: (i, j, 0)),
                   pl.BlockSpec((1, N_HEADS, HEAD_DIM, HEAD_DIM), lambda i, j: (i, 0, 0, 0)),
                   pl.BlockSpec((1, N_HEADS, HEAD_DIM), lambda i, j: (i, 0, 0)),
                   pl.BlockSpec((1, 1, LANES), lambda i, j: (i, 0, 0)),
                   pl.BlockSpec((1, CONV_W - 1, 2 * WIDTH), lambda i, j: (i, 0, 0))],
        out_shape=[jax.ShapeDtypeStruct((b, l, WIDTH), f32),
                   jax.ShapeDtypeStruct((b, N_HEADS, HEAD_DIM, HEAD_DIM), f32),
                   jax.ShapeDtypeStruct((b, N_HEADS, HEAD_DIM), f32),
                   jax.ShapeDtypeStruct((b, 1, LANES), f32),
                   jax.ShapeDtypeStruct((b, CONV_W - 1, 2 * WIDTH), f32)],
        scratch_shapes=[pltpu.VMEM((c + 8, 2 * WIDTH), f32), pltpu.VMEM((c, WIDTH), f32)],
        compiler_params=_params("parallel", "arbitrary"),
        name="mlstm_branch",
    )(proj, conv0, c0, n0, m0p, p["mlstm_conv_w"], p["mlstm_conv_b"].reshape(1, -1),
      _pad_lanes(p["mlstm_gate_b"]), p["mlstm_norm_w"].reshape(1, -1))
    out, cn, nn, mn, convn = outs
    return out, cn, nn, mn[:, 0, :N_HEADS], convn


def _head_mean_matrix():
    hid = jnp.arange(ATTN_W) // HEAD_DIM
    return (hid[:, None] == hid[None, :]).astype(f32) / HEAD_DIM


def _qk_norm(x, hm_ref, w_ref):
    return x * lax.rsqrt(_dot(x * x, hm_ref[...], HI) + NORM_EPS) * w_ref[...]


def _attn_prompt_kernel(q_ref, k_ref, v_ref, z_ref, hm_ref, qw_ref, kw_ref, out_ref, kv_ref,
                        qn_scr, kn_scr, vv_scr, acc_scr, m_scr, d_scr, to_scr, tl_scr, *, l):
    g = pl.program_id(1)
    rb = 256
    n_pairs = ATTN_W // LANES

    def norm_body(i, carry):
        rows = pl.ds(pl.multiple_of(i * rb, rb), rb)
        qn = _qk_norm(q_ref[0, rows, :], hm_ref, qw_ref)
        kn = _qk_norm(k_ref[0, rows, :], hm_ref, kw_ref)
        v = v_ref[0, rows, :]
        kv_ref[0, 0, rows, 0:ATTN_W] = kn
        kv_ref[0, 0, rows, ATTN_W:2 * ATTN_W] = v
        for p in range(n_pairs):
            qn_scr[p, rows, :] = qn[:, p * LANES:(p + 1) * LANES]
            kn_scr[p, rows, :] = kn[:, p * LANES:(p + 1) * LANES]
            vv_scr[p, rows, :] = v[:, p * LANES:(p + 1) * LANES]
        return carry

    lax.fori_loop(0, l // rb, norm_body, 0)

    qb = ATTN_J
    qi = _iota2((qb, qb), 0)
    ki = _iota2((qb, qb), 1)
    scale = HEAD_DIM ** -0.5

    def run_group(dil, first):
        n_iter = l // qb

        def body(i, carry):
            r = i % dil
            blk = i // dil
            u0 = blk * qb
            rows_c = pl.ds(u0 * dil + r, qb, stride=dil)
            rows_p = pl.ds(jnp.maximum(u0 - qb, 0) * dil + r, qb, stride=dil)
            mask_c = ki <= qi
            mask_p = (ki >= qi) & (blk > 0)
            for p in range(n_pairs):
                qv = qn_scr[p, rows_c, :]
                kc = kn_scr[p, rows_c, :]
                vc = vv_scr[p, rows_c, :]
                kp = kn_scr[p, rows_p, :]
                vp = vv_scr[p, rows_p, :]
                for hh in range(LANES // HEAD_DIM):
                    sl = slice(hh * HEAD_DIM, (hh + 1) * HEAD_DIM)
                    sc = jnp.where(mask_c, _dot_nt(qv[:, sl], kc[:, sl]) * scale, -jnp.inf)
                    sp = jnp.where(mask_p, _dot_nt(qv[:, sl], kp[:, sl]) * scale, -jnp.inf)
                    m = jnp.maximum(jnp.max(sc, axis=-1, keepdims=True), jnp.max(sp, axis=-1, keepdims=True))
                    pc = jnp.exp(sc - m)
                    pp = jnp.exp(sp - m)
                    den = jnp.sum(pc, axis=-1, keepdims=True) + jnp.sum(pp, axis=-1, keepdims=True)
                    to_scr[:, sl] = (_dot(pc, vc[:, sl]) + _dot(pp, vp[:, sl])) / den
                    tl_scr[:, sl] = jnp.broadcast_to(m + jnp.log(den), (qb, HEAD_DIM))
                o = to_scr[...]
                lse = tl_scr[...]
                if first:
                    acc_scr[p, rows_c, :] = o
                    m_scr[p, rows_c, :] = lse
                    d_scr[p, rows_c, :] = jnp.ones_like(lse)
                else:
                    m_old = m_scr[p, rows_c, :]
                    m_new = jnp.maximum(m_old, lse)
                    a_old = jnp.exp(m_old - m_new)
                    a_new = jnp.exp(lse - m_new)
                    acc_scr[p, rows_c, :] = acc_scr[p, rows_c, :] * a_old + o * a_new
                    d_scr[p, rows_c, :] = d_scr[p, rows_c, :] * a_old + a_new
                    m_scr[p, rows_c, :] = m_new
            return carry

        lax.fori_loop(0, n_iter, body, 0)

    for gi, (_, dil) in enumerate(ATTN_GROUPS):
        pl.when(g == gi)(functools.partial(run_group, dil, gi == 0))

    @pl.when(g == len(ATTN_GROUPS) - 1)
    def _():
        def fin_body(i, carry):
            rows = pl.ds(pl.multiple_of(i * rb, rb), rb)
            for p in range(n_pairs):
                lanes = slice(p * LANES, (p + 1) * LANES)
                out_ref[0, rows, lanes] = acc_scr[p, rows, :] / d_scr[p, rows, :] * _silu(z_ref[0, rows, lanes])
            return carry

        lax.fori_loop(0, l // rb, fin_body, 0)


def _attn_prompt(proj, p):
    b, l, _ = proj.shape
    ng = len(ATTN_GROUPS)
    kern = functools.partial(_attn_prompt_kernel, l=l)
    col = lambda base: pl.BlockSpec((1, l, ATTN_W), lambda i, g: (i, 0, base + g))
    full = lambda shape: pl.BlockSpec(shape, lambda i, g: (0,) * len(shape))
    pair_scr = pltpu.VMEM((ATTN_W // LANES, l, LANES), f32)
    tile = lambda name: jnp.tile(p[name], ATTN_HEADS).reshape(1, ATTN_W)
    return pl.pallas_call(
        kern,
        grid=(b, ng),
        in_specs=[col(1), col(1 + ng), col(1 + 2 * ng),
                  pl.BlockSpec((1, l, ATTN_W), lambda i, g: (i, 0, 0)),
                  full((ATTN_W, ATTN_W)), full((1, ATTN_W)), full((1, ATTN_W))],
        out_specs=[pl.BlockSpec((1, l, ATTN_W), lambda i, g: (i, 0, 0)),
                   pl.BlockSpec((1, 1, l, 2 * ATTN_W), lambda i, g: (i, g, 0, 0))],
        out_shape=[jax.ShapeDtypeStruct((b, l, ATTN_W), f32),
                   jax.ShapeDtypeStruct((b, ng, l, 2 * ATTN_W), f32)],
        scratch_shapes=[pair_scr] * 6 + [pltpu.VMEM((ATTN_J, LANES), f32)] * 2,
        compiler_params=_params("parallel", "arbitrary"),
        name="dilated_attn_prompt",
    )(proj, proj, proj, proj, _head_mean_matrix(), tile("attn_q_norm"), tile("attn_k_norm"))


def _attn_step_kernel(x_ref, c0_ref, c1_ref, c2_ref, hm_ref, qw_ref, kw_ref, out_ref, kv_ref, *, l):
    x = x_ref[0]
    ng = len(ATTN_GROUPS)
    scale = HEAD_DIM ** -0.5
    outs, lses = [], []
    for gi, ((window, dil), c_ref) in enumerate(zip(ATTN_GROUPS, (c0_ref, c1_ref, c2_ref))):
        w = c_ref.shape[-1]
        jmax = window // dil
        shift = dil.bit_length() - 1
        qn = _qk_norm(x[:, (1 + gi) * ATTN_W:(2 + gi) * ATTN_W], hm_ref, qw_ref)
        kn = _qk_norm(x[:, (1 + ng + gi) * ATTN_W:(2 + ng + gi) * ATTN_W], hm_ref, kw_ref)
        v = x[:, (1 + 2 * ng + gi) * ATTN_W:(2 + 2 * ng + gi) * ATTN_W]
        kv_ref[0, gi, :, 0:ATTN_W] = kn
        kv_ref[0, gi, :, ATTN_W:2 * ATTN_W] = v
        dist_p = w + _iota2((l, w), 0) - _iota2((l, w), 1)
        mask_p = ((dist_p & (dil - 1)) == 0) & ((dist_p >> shift) <= jmax)
        dist_n = _iota2((l, l), 0) - _iota2((l, l), 1)
        mask_n = (dist_n >= 0) & ((dist_n & (dil - 1)) == 0) & ((dist_n >> shift) <= jmax)
        og, lg = [], []
        for h in range(ATTN_HEADS):
            sl = slice(h * HEAD_DIM, (h + 1) * HEAD_DIM)
            kc_t = c_ref[0, 0, 0, h]
            vc_t = c_ref[0, 0, 1, h]
            sp = jnp.where(mask_p, _dot(qn[:, sl], kc_t) * scale, -jnp.inf)
            sn = jnp.where(mask_n, _dot_nt(qn[:, sl], kn[:, sl]) * scale, -jnp.inf)
            m = jnp.maximum(jnp.max(sp, axis=-1, keepdims=True), jnp.max(sn, axis=-1, keepdims=True))
            pp = jnp.exp(sp - m)
            pn = jnp.exp(sn - m)
            den = jnp.sum(pp, axis=-1, keepdims=True) + jnp.sum(pn, axis=-1, keepdims=True)
            og.append((_dot_nt(pp, vc_t) + _dot(pn, v[:, sl])) / den)
            lg.append(m + jnp.log(den))
        outs.append(og)
        lses.append(lg)
    for h in range(ATTN_HEADS):
        sl = slice(h * HEAD_DIM, (h + 1) * HEAD_DIM)
        mx = functools.reduce(jnp.maximum, [lses[gi][h] for gi in range(ng)])
        ws = [jnp.exp(lses[gi][h] - mx) for gi in range(ng)]
        tot = functools.reduce(lambda a, b: a + b, ws)
        o = functools.reduce(lambda a, b: a + b, [outs[gi][h] * (ws[gi] / tot) for gi in range(ng)])
        out_ref[0, :, sl] = o * _silu(x[:, sl])


def _attn_step(proj, caches, layer, p):
    b, l, wp = proj.shape
    ng = len(ATTN_GROUPS)
    kern = functools.partial(_attn_step_kernel, l=l)
    full = lambda shape: pl.BlockSpec(shape, lambda i: (0,) * len(shape))
    tile = lambda name: jnp.tile(p[name], ATTN_HEADS).reshape(1, ATTN_W)
    cache_specs = [pl.BlockSpec((1, 1) + c.shape[2:], lambda i: (layer, i, 0, 0, 0, 0)) for c in caches]
    return pl.pallas_call(
        kern,
        grid=(b,),
        in_specs=[pl.BlockSpec((1, l, wp), lambda i: (i, 0, 0))] + cache_specs
                 + [full((ATTN_W, ATTN_W)), full((1, ATTN_W)), full((1, ATTN_W))],
        out_specs=[pl.BlockSpec((1, l, ATTN_W), lambda i: (i, 0, 0)),
                   pl.BlockSpec((1, ng, l, 2 * ATTN_W), lambda i: (i, 0, 0, 0))],
        out_shape=[jax.ShapeDtypeStruct((b, l, ATTN_W), f32),
                   jax.ShapeDtypeStruct((b, ng, l, 2 * ATTN_W), f32)],
        compiler_params=_params("parallel"),
        name="dilated_attn_step",
    )(proj, *caches, _head_mean_matrix(), tile("attn_q_norm"), tile("attn_k_norm"))


def _merge_kernel(oa_ref, ob_ref, oc_ref, od_ref, mg_ref, x_ref, gate_ref, wb_ref, wo_ref, o_ref):
    merged = None
    start = 0
    for bi, ref in enumerate((oa_ref, ob_ref, oc_ref, od_ref)):
        width = ref.shape[-1]
        t = jnp.dot(ref[0].astype(bf16), wb_ref[start:start + width, :], preferred_element_type=f32)
        term = jax.nn.sigmoid(mg_ref[0, :, bi * D_MODEL:(bi + 1) * D_MODEL]) * t
        merged = term if merged is None else merged + term
        start += width
    y = jnp.dot(merged.astype(bf16), wo_ref[...], preferred_element_type=f32)
    o_ref[0] = x_ref[0] + gate_ref[0] * y


def _merge(branches, mg3, x3, gate3, wb, wo, tl):
    b, l, d = x3.shape
    per_row = gate3.shape[1] != 1
    rows = lambda w: pl.BlockSpec((1, tl, w), lambda i, j: (i, j, 0))
    gate_spec = pl.BlockSpec((1, tl if per_row else 1, d), (lambda i, j: (i, j, 0)) if per_row else (lambda i, j: (i, 0, 0)))
    full = lambda shape: pl.BlockSpec(shape, lambda i, j: (0,) * len(shape))
    return pl.pallas_call(
        _merge_kernel,
        grid=(b, l // tl),
        in_specs=[rows(o.shape[-1]) for o in branches] + [rows(N_BRANCH * d), rows(d), gate_spec,
                                                          full(wb.shape), full(wo.shape)],
        out_specs=rows(d),
        out_shape=jax.ShapeDtypeStruct((b, l, d), f32),
        compiler_params=_params("parallel", "parallel"),
        name="merge_out",
    )(*branches, mg3, x3, gate3, wb, wo)


def _split_w_in(w_in):
    cols, start = {}, 0
    for name, size in IN_SEGMENTS:
        cols[name] = w_in[:, start:start + size]
        start += size
    zeros = lambda n: jnp.zeros((w_in.shape[0], n), w_in.dtype)
    groups = {
        "ssm": [cols["ssm_z"], cols["ssm_xbc"], cols["ssm_dt"], zeros(LANES - N_HEADS)],
        "rwkv": [cols["rwkv_z"], cols["rwkv_shift"]],
        "mlstm": [cols["mlstm_z"], cols["mlstm_qk"], cols["mlstm_v"], cols["mlstm_o"], cols["mlstm_if"],
                  zeros(LANES - 2 * N_HEADS)],
        "attn": [cols["attn_z"], cols["attn_qkv"]],
        "merge": [cols["merge"]],
    }
    return {k: jnp.concatenate(v, axis=1).astype(bf16) for k, v in groups.items()}


_PROJ_TN = {"ssm": 1408, "rwkv": 2176, "mlstm": 896, "attn": 1280, "merge": 1024}


def _layer(x, mod, st, kv_past, p, w_groups, wb, wo):
    b, l, d = x.shape
    prompt = kv_past is None
    shift, scale, gate = mod[:, :d], mod[:, d:2 * d], mod[:, 2 * d:]
    if prompt:
        x3, sc3, sh3, g3 = x, scale[:, None], shift[:, None], gate[:, None]
        tl, chunk = 512, 128
        tl_merge = 256
    else:
        rep = lambda t: jnp.repeat(t, l, axis=0)[None]
        x3, sc3, sh3, g3 = x.reshape(1, b * l, d), rep(scale), rep(shift), rep(gate)
        tl, chunk = min(256, b * l), l
        tl_merge = tl
    h2 = _norm(x3, p["norm_w"], sc3, sh3, tl).reshape(b * l, d)
    proj = {k: _matmul(h2, w, min(512, b * l), _PROJ_TN[k]).reshape(b, l, -1) for k, w in w_groups.items()}

    out_a, ssm_new, ssm_conv_new = _ssm(proj["ssm"], st["ssm_conv"], st["ssm"], p, chunk)
    if prompt:
        out_b, rwkv_new, shift_new = _rwkv_chunked(proj["rwkv"], st["rwkv_shift"][:, None], st["rwkv"], p, RWKV_CHUNK)
    else:
        out_b, rwkv_new, shift_new = _rwkv(proj["rwkv"], st["rwkv_shift"][:, None], st["rwkv"], p, chunk)
    out_c, c_new, n_new, m_new, mconv_new = _mlstm(proj["mlstm"], st["mlstm_conv"], st["mlstm_c"], st["mlstm_n"],
                                                   st["mlstm_m"], p, chunk)
    if prompt:
        out_d, kv = _attn_prompt(proj["attn"], p)
        kv_new = [kv[:, gi, l - min(w, l):].reshape(b, min(w, l), 2, ATTN_HEADS, HEAD_DIM)
                  for gi, (w, _) in enumerate(ATTN_GROUPS)]
    else:
        caches, layer = kv_past
        out_d, kv = _attn_step(proj["attn"], caches, layer, p)
        kv_new = [kv[:, gi].reshape(b, l, 2, ATTN_HEADS, HEAD_DIM) for gi in range(len(ATTN_GROUPS))]

    branches = [o.reshape(x3.shape[0], x3.shape[1], -1) for o in (out_a, out_b, out_c, out_d)]
    mg3 = proj["merge"].reshape(x3.shape[0], x3.shape[1], -1)
    x_new = _merge(branches, mg3, x3, g3, wb, wo, tl_merge).reshape(b, l, d)
    new_state = {"ssm": ssm_new, "ssm_conv": ssm_conv_new, "rwkv": rwkv_new, "rwkv_shift": shift_new[:, 0],
                 "mlstm_c": c_new, "mlstm_n": n_new, "mlstm_m": m_new, "mlstm_conv": mconv_new,
                 "kv_0": kv_new[0], "kv_1": kv_new[1], "kv_2": kv_new[2]}
    return x_new, new_state


_STATE_NAMES = ("ssm", "ssm_conv", "rwkv", "rwkv_shift", "mlstm_c", "mlstm_n", "mlstm_m", "mlstm_conv",
                "kv_0", "kv_1", "kv_2")


def kernel(x_prompt, x_sample, c_prompt, c_sample, state_ssm, state_ssm_conv, state_rwkv, state_rwkv_shift, state_mlstm_c, state_mlstm_n, state_mlstm_m, state_mlstm_conv, cache_kv_w128, cache_kv_w512, cache_kv_w2048, norm_w, ada_w, ada_b, w_in, w_branch, w_out, ssm_conv_w, ssm_conv_b, ssm_dt_bias, ssm_a_log, ssm_d, ssm_norm_w, rwkv_mu, rwkv_w0, rwkv_w2, rwkv_a0, rwkv_a2, rwkv_k_k, rwkv_k_a, rwkv_r_k, rwkv_ln_w, rwkv_ln_b, mlstm_conv_w, mlstm_conv_b, mlstm_gate_b, mlstm_norm_w, attn_q_norm, attn_k_norm):
    weights = {"norm_w": norm_w, "ssm_conv_w": ssm_conv_w, "ssm_conv_b": ssm_conv_b,
               "ssm_dt_bias": ssm_dt_bias, "ssm_a_log": ssm_a_log, "ssm_d": ssm_d,
               "ssm_norm_w": ssm_norm_w, "rwkv_mu": rwkv_mu, "rwkv_w0": rwkv_w0, "rwkv_w2": rwkv_w2,
               "rwkv_a0": rwkv_a0, "rwkv_a2": rwkv_a2, "rwkv_k_k": rwkv_k_k, "rwkv_k_a": rwkv_k_a,
               "rwkv_r_k": rwkv_r_k, "rwkv_ln_w": rwkv_ln_w, "rwkv_ln_b": rwkv_ln_b,
               "mlstm_conv_w": mlstm_conv_w, "mlstm_conv_b": mlstm_conv_b,
               "mlstm_gate_b": mlstm_gate_b, "mlstm_norm_w": mlstm_norm_w,
               "attn_q_norm": attn_q_norm, "attn_k_norm": attn_k_norm}
    bp = x_prompt.shape[0]
    fresh = {"ssm": jnp.zeros((bp, N_HEADS, HEAD_DIM, SSM_STATE), f32),
             "ssm_conv": jnp.zeros((bp, CONV_W - 1, SSM_CONV_CH), f32),
             "rwkv": jnp.zeros((bp, N_HEADS, HEAD_DIM, HEAD_DIM), f32),
             "rwkv_shift": jnp.zeros((bp, RWKV_SHIFT_CH), f32),
             "mlstm_c": jnp.zeros((bp, N_HEADS, HEAD_DIM, HEAD_DIM), f32),
             "mlstm_n": jnp.zeros((bp, N_HEADS, HEAD_DIM), f32),
             "mlstm_m": jnp.zeros((bp, N_HEADS), f32),
             "mlstm_conv": jnp.zeros((bp, CONV_W - 1, 2 * WIDTH), f32)}
    c_all = jnp.concatenate([c_prompt, c_sample], axis=0)
    caches_t = [jnp.transpose(c, (0, 1, 3, 4, 5, 2)) for c in (cache_kv_w128, cache_kv_w512, cache_kv_w2048)]
    y_prompt, y_sample = x_prompt, x_sample
    prompt_states, sample_states = [], []
    for layer in range(DEPTH):
        p = {name: arr[layer] for name, arr in weights.items()}
        mod = _ada(c_all, ada_w[layer], ada_b[layer])
        w_groups = _split_w_in(w_in[layer])
        wb = w_branch[layer].astype(bf16)
        wo = w_out[layer].astype(bf16)
        y_prompt, sp = _layer(y_prompt, mod[:bp], fresh, None, p, w_groups, wb, wo)
        st = {"ssm": state_ssm[layer], "ssm_conv": state_ssm_conv[layer], "rwkv": state_rwkv[layer],
              "rwkv_shift": state_rwkv_shift[layer], "mlstm_c": state_mlstm_c[layer],
              "mlstm_n": state_mlstm_n[layer], "mlstm_m": state_mlstm_m[layer],
              "mlstm_conv": state_mlstm_conv[layer]}
        y_sample, ss = _layer(y_sample, mod[bp:], st, (caches_t, layer), p, w_groups, wb, wo)
        prompt_states.append(sp)
        sample_states.append(ss)
    stack = lambda states, name: jnp.stack([s[name] for s in states])
    return ((y_prompt, y_sample)
            + tuple(stack(prompt_states, n) for n in _STATE_NAMES)
            + tuple(stack(sample_states, n) for n in _STATE_NAMES))
```

```python
import functools
import math

import jax
import jax.numpy as jnp
from jax import lax
from jax.experimental import pallas as pl
from jax.experimental.pallas import tpu as pltpu

f32 = jnp.float32
bf16 = jnp.bfloat16
HI = lax.Precision.HIGHEST

D_MODEL = 1024
DEPTH = 2
HEAD_DIM = 64
NORM_EPS = 1e-6
CONV_W = 4
N_HEADS = 8
WIDTH = N_HEADS * HEAD_DIM
SSM_GROUPS = 2
SSM_STATE = 64
SSM_CONV_CH = WIDTH + 2 * SSM_GROUPS * SSM_STATE
RWKV_RANK = 64
RWKV_SHIFT_CH = 3 * WIDTH + 2 * RWKV_RANK
RWKV_LN_EPS = 64e-5
RWKV_CHUNK = 64
ATTN_GROUPS = ((128, 1), (512, 4), (2048, 16))
ATTN_HEADS = 4
ATTN_W = ATTN_HEADS * HEAD_DIM
ATTN_J = 128
N_BRANCH = 4
LANES = 128
VMEM_LIMIT = 52 * 1024 * 1024

IN_SEGMENTS = (
    ("ssm_z", WIDTH), ("ssm_xbc", SSM_CONV_CH), ("ssm_dt", N_HEADS),
    ("rwkv_z", WIDTH), ("rwkv_shift", RWKV_SHIFT_CH),
    ("mlstm_z", WIDTH), ("mlstm_qk", 2 * WIDTH), ("mlstm_v", WIDTH),
    ("mlstm_o", WIDTH), ("mlstm_if", 2 * N_HEADS),
    ("attn_z", ATTN_W), ("attn_qkv", 9 * ATTN_W),
    ("merge", N_BRANCH * D_MODEL),
)

NN = (((1,), (0,)), ((), ()))
NT = (((1,), (1,)), ((), ()))
TN = (((0,), (0,)), ((), ()))


def _mm(a, b, dims, prec):
    if prec is None:
        a, b = a.astype(bf16), b.astype(bf16)
    return lax.dot_general(a, b, dims, preferred_element_type=f32, precision=prec)


def _dot(a, b, prec=None):
    return _mm(a, b, NN, prec)


def _dot_nt(a, b, prec=None):
    return _mm(a, b, NT, prec)


def _dot_tn(a, b, prec=None):
    return _mm(a, b, TN, prec)


def _silu(x):
    return x * jax.nn.sigmoid(x)


def _softplus(x):
    return jnp.maximum(x, 0.0) + jnp.log1p(jnp.exp(-jnp.abs(x)))


def _params(*sem):
    return pltpu.CompilerParams(dimension_semantics=sem, vmem_limit_bytes=VMEM_LIMIT)


def _iota2(shape, dim):
    return lax.broadcasted_iota(jnp.int32, shape, dim)


def _ada_kernel(c_ref, w_ref, b_ref, o_ref):
    o_ref[...] = _dot(_silu(c_ref[...]), w_ref[...], HI) + b_ref[...]


def _ada(c, w, b):
    n = c.shape[0]
    return pl.pallas_call(
        _ada_kernel,
        grid=(3,),
        in_specs=[pl.BlockSpec((n, D_MODEL), lambda j: (0, 0)),
                  pl.BlockSpec((D_MODEL, D_MODEL), lambda j: (0, j)),
                  pl.BlockSpec((1, D_MODEL), lambda j: (0, j))],
        out_specs=pl.BlockSpec((n, D_MODEL), lambda j: (0, j)),
        out_shape=jax.ShapeDtypeStruct((n, 3 * D_MODEL), f32),
        compiler_params=_params("parallel"),
        name="ada_mod",
    )(c, w, b.reshape(1, -1))


def _norm_kernel(x_ref, nw_ref, sc_ref, sh_ref, o_ref):
    x = x_ref[0]
    r = x * lax.rsqrt(jnp.mean(x * x, axis=-1, keepdims=True) + NORM_EPS)
    o_ref[0] = (r * nw_ref[...] * (1.0 + sc_ref[0]) + sh_ref[0]).astype(o_ref.dtype)


def _norm(x3, nw, sc3, sh3, tl):
    b, l, d = x3.shape
    per_row = sc3.shape[1] != 1
    mod_spec = pl.BlockSpec((1, tl if per_row else 1, d), (lambda i, j: (i, j, 0)) if per_row else (lambda i, j: (i, 0, 0)))
    return pl.pallas_call(
        _norm_kernel,
        grid=(b, l // tl),
        in_specs=[pl.BlockSpec((1, tl, d), lambda i, j: (i, j, 0)),
                  pl.BlockSpec((1, d), lambda i, j: (0, 0)),
                  mod_spec, mod_spec],
        out_specs=pl.BlockSpec((1, tl, d), lambda i, j: (i, j, 0)),
        out_shape=jax.ShapeDtypeStruct((b, l, d), bf16),
        compiler_params=_params("parallel", "parallel"),
        name="mod_rmsnorm",
    )(x3, nw.reshape(1, d), sc3, sh3)


def _mm_kernel(x_ref, w_ref, o_ref):
    o_ref[...] = jnp.dot(x_ref[...], w_ref[...], preferred_element_type=f32).astype(o_ref.dtype)


def _matmul(x, w, tm, tn):
    n, k = x.shape
    m = w.shape[1]
    return pl.pallas_call(
        _mm_kernel,
        grid=(m // tn, n // tm),
        in_specs=[pl.BlockSpec((tm, k), lambda j, i: (i, 0)),
                  pl.BlockSpec((k, tn), lambda j, i: (0, j))],
        out_specs=pl.BlockSpec((tm, tn), lambda j, i: (i, j)),
        out_shape=jax.ShapeDtypeStruct((n, m), f32),
        compiler_params=_params("parallel", "parallel"),
        name="in_proj",
    )(x, w)


def _causal_conv(ext_scr, u, c, cw_ref, cb_ref):
    ext_scr[8:8 + c, :] = u
    acc = cb_ref[...]
    for i in range(CONV_W):
        acc = acc + ext_scr[5 + i:5 + i + c, :] * cw_ref[i:i + 1, :]
    tail = ext_scr[c + 5:c + 8, :]
    ext_scr[5:8, :] = tail
    return acc, tail


def _ssm_kernel(proj_ref, conv0_ref, h0_ref, cw_ref, cb_ref, dtb_ref, alog_ref, dsk_ref, nw_ref,
                out_ref, hn_ref, convn_ref, ext_scr, y_scr, *, c):
    @pl.when(pl.program_id(1) == 0)
    def _():
        ext_scr[5:8, :] = conv0_ref[0]
        hn_ref[0] = h0_ref[0]

    u = proj_ref[0, :, WIDTH:WIDTH + SSM_CONV_CH]
    acc, tail = _causal_conv(ext_scr, u, c, cw_ref, cb_ref)
    convn_ref[0] = tail
    xbc = _silu(acc)
    xs = xbc[:, :WIDTH]
    bm = xbc[:, WIDTH:WIDTH + LANES]
    cm = xbc[:, WIDTH + LANES:]
    dt = _softplus(proj_ref[0, :, WIDTH + SSM_CONV_CH:] + dtb_ref[...])
    da = dt * (-jnp.exp(alog_ref[...]))
    row = _iota2((c, c), 0)
    col = _iota2((c, c), 1)
    causal = row >= col
    cs = _dot(causal.astype(f32), da, HI)
    cs_t = cs.T
    dt_t = dt.T
    cs_end = cs[c - 1:c, :]
    heads = range(N_HEADS)
    per_group = N_HEADS // SSM_GROUPS
    bgs = [bm[:, g * SSM_STATE:(g + 1) * SSM_STATE] for g in range(SSM_GROUPS)]
    cgs = [cm[:, g * SSM_STATE:(g + 1) * SSM_STATE] for g in range(SSM_GROUPS)]
    gmats = [_dot_nt(cgs[g], bgs[g]) for g in range(SSM_GROUPS)]
    sls = [slice(h * HEAD_DIM, (h + 1) * HEAD_DIM) for h in heads]
    cs_cols = [cs[:, h:h + 1] for h in heads]
    lmats = [gmats[h // per_group] * jnp.exp(jnp.where(causal, cs_cols[h] - cs_t[h:h + 1, :], -jnp.inf))
             * dt_t[h:h + 1, :] for h in heads]
    xhs = [xs[:, sl] for sl in sls]
    hprevs = [hn_ref[0, h] for h in heads]
    y_in = [_dot(lmats[h], xhs[h]) for h in heads]
    y_st = [_dot_nt(cgs[h // per_group], hprevs[h]) for h in heads]
    for h in heads:
        y_scr[:, sls[h]] = y_in[h] + y_st[h] * jnp.exp(cs_cols[h]) + xhs[h] * dsk_ref[:, h:h + 1]
    ces = [cs_end[:, h:h + 1] for h in heads]
    upd = [_dot_tn(xhs[h] * (jnp.exp(ces[h] - cs_cols[h]) * dt[:, h:h + 1]), bgs[h // per_group]) for h in heads]
    for h in heads:
        hn_ref[0, h] = hprevs[h] * jnp.exp(ces[h]) + upd[h]
    gy = y_scr[...] * _silu(proj_ref[0, :, :WIDTH])
    gw = WIDTH // SSM_GROUPS
    for g in range(SSM_GROUPS):
        part = gy[:, g * gw:(g + 1) * gw]
        ms = jnp.mean(part * part, axis=-1, keepdims=True)
        out_ref[0, :, g * gw:(g + 1) * gw] = part * lax.rsqrt(ms + NORM_EPS) * nw_ref[:, g * gw:(g + 1) * gw]


def _pad_lanes(v):
    v = v.reshape(1, -1)
    return jnp.pad(v, ((0, 0), (0, LANES - v.shape[1])))


def _ssm(proj, conv0, h0, p, c):
    b, l, wp = proj.shape
    kern = functools.partial(_ssm_kernel, c=c)
    full = lambda shape: pl.BlockSpec(shape, lambda i, j: (0,) * len(shape))
    return pl.pallas_call(
        kern,
        grid=(b, l // c),
        in_specs=[pl.BlockSpec((1, c, wp), lambda i, j: (i, j, 0)),
                  pl.BlockSpec((1, CONV_W - 1, SSM_CONV_CH), lambda i, j: (i, 0, 0)),
                  pl.BlockSpec((1, N_HEADS, HEAD_DIM, SSM_STATE), lambda i, j: (i, 0, 0, 0)),
                  full((CONV_W, SSM_CONV_CH)), full((1, SSM_CONV_CH)),
                  full((1, LANES)), full((1, LANES)), full((1, LANES)), full((1, WIDTH))],
        out_specs=[pl.BlockSpec((1, c, WIDTH), lambda i, j: (i, j, 0)),
                   pl.BlockSpec((1, N_HEADS, HEAD_DIM, SSM_STATE), lambda i, j: (i, 0, 0, 0)),
                   pl.BlockSpec((1, CONV_W - 1, SSM_CONV_CH), lambda i, j: (i, 0, 0))],
        out_shape=[jax.ShapeDtypeStruct((b, l, WIDTH), f32),
                   jax.ShapeDtypeStruct((b, N_HEADS, HEAD_DIM, SSM_STATE), f32),
                   jax.ShapeDtypeStruct((b, CONV_W - 1, SSM_CONV_CH), f32)],
        scratch_shapes=[pltpu.VMEM((c + 8, SSM_CONV_CH), f32), pltpu.VMEM((c, WIDTH), f32)],
        compiler_params=_params("parallel", "arbitrary"),
        name="ssd_branch",
    )(proj, conv0, h0, p["ssm_conv_w"], p["ssm_conv_b"].reshape(1, -1), _pad_lanes(p["ssm_dt_bias"]),
      _pad_lanes(p["ssm_a_log"]), _pad_lanes(p["ssm_d"]), p["ssm_norm_w"].reshape(1, -1))


def _rmm(a, b, dims):
    return _mm(a, b, dims, None)


def _rwkv_chunk_kernel(proj_ref, shift0_ref, s0_ref, mu_ref, w0_ref, w2_ref, a0_ref, a2_ref, kk_ref, ka_ref, rk_ref,
                       lnw_ref, lnb_ref, out_ref, sn_ref, shiftn_ref, ext_scr, *, c):
    @pl.when(pl.program_id(1) == 0)
    def _():
        ext_scr[7:8, :] = shift0_ref[0]
        sn_ref[0] = s0_ref[0]

    u = proj_ref[0, :, WIDTH:]
    ext_scr[8:8 + c, :] = u
    sh = u + (ext_scr[7:7 + c, :] - u) * mu_ref[...]
    last = ext_scr[c + 7:c + 8, :]
    ext_scr[7:8, :] = last
    shiftn_ref[0] = last
    r = sh[:, :WIDTH]
    k = sh[:, WIDTH:2 * WIDTH]
    v = sh[:, 2 * WIDTH:3 * WIDTH]
    w_lo = sh[:, 3 * WIDTH:3 * WIDTH + RWKV_RANK]
    a_lo = sh[:, 3 * WIDTH + RWKV_RANK:]
    w_log = -_softplus(-(w0_ref[...] + _dot(jnp.tanh(w_lo), w2_ref[...], HI))) - 0.5
    lw = -jnp.exp(w_log)
    a_in = jax.nn.sigmoid(a0_ref[...] + _dot(a_lo, a2_ref[...], HI))
    kk = k * kk_ref[...]
    k2 = k * (1.0 + (a_in - 1.0) * ka_ref[...])

    row = _iota2((c, c), 0)
    col = _iota2((c, c), 1)
    cl = _dot((row >= col).astype(f32), lw, HI)
    cl_end = cl[c - 1:c, :]
    g_fwd = jnp.exp(cl)
    g_inv = jnp.exp(-cl)
    g_rem = jnp.exp(cl_end - cl)
    g_prev = jnp.exp(cl - lw)
    g_end = jnp.exp(cl_end)

    row2 = _iota2((c, 2 * c), 0)
    col2 = _iota2((c, 2 * c), 1)
    col2 = jnp.where(col2 >= c, col2 - c, col2)
    strict2 = row2 > col2
    incl2 = row2 >= col2
    n_double = (c - 1).bit_length()

    heads = range(N_HEADS)
    sls = [slice(h * HEAD_DIM, (h + 1) * HEAD_DIM) for h in heads]
    s0s = [sn_ref[0, h] for h in heads]
    bhs, ars, bks = [], [], []
    for sl in sls:
        kkh = kk[:, sl]
        kkn = kkh / jnp.maximum(jnp.sqrt(jnp.sum(kkh * kkh, axis=-1, keepdims=True)), 1e-12)
        bhs.append(kkn * a_in[:, sl])
        ars.append(jnp.concatenate([-kkn * g_prev[:, sl], r[:, sl] * g_fwd[:, sl]], axis=0))
        bks.append(jnp.concatenate([bhs[-1] * g_inv[:, sl], k2[:, sl] * g_inv[:, sl]], axis=0))
    m4s = [_rmm(ars[h], bks[h], NT) for h in heads]
    ahs = [_rmm(ars[h], s0s[h], NT) for h in heads]
    tops = [jnp.where(strict2, m4[:c, :], 0.0) for m4 in m4s]
    bots = [jnp.where(incl2, m4[c:, :], 0.0) for m4 in m4s]
    amats = [top[:, :c] for top in tops]
    xs = [ahs[h][:c] + _rmm(tops[h][:, c:], v[:, sls[h]], NN) for h in heads]
    for i in range(n_double):
        xs = [xs[h] + _rmm(amats[h], xs[h], NN) for h in heads]
        if i + 1 < n_double:
            amats = [_rmm(amats[h], amats[h], NN) for h in heads]
    pvs = [jnp.concatenate([xs[h], v[:, sls[h]]], axis=0) for h in heads]
    ys = [ahs[h][c:] + _rmm(bots[h], pvs[h], NN) for h in heads]
    for h in heads:
        sl = sls[h]
        bk_end = jnp.concatenate([bhs[h] * g_rem[:, sl], k2[:, sl] * g_rem[:, sl]], axis=0)
        sn_ref[0, h] = s0s[h] * g_end[:, sl] + _rmm(pvs[h], bk_end, TN)

    for h in heads:
        sl = sls[h]
        y = ys[h]
        vh = v[:, sl]
        mu = jnp.mean(y, axis=-1, keepdims=True)
        var = jnp.mean(jnp.square(y - mu), axis=-1, keepdims=True)
        ln = (y - mu) * lax.rsqrt(var + RWKV_LN_EPS) * lnw_ref[:, sl] + lnb_ref[:, sl]
        bonus = jnp.sum(r[:, sl] * k2[:, sl] * rk_ref[:, sl], axis=-1, keepdims=True) * vh
        out_ref[0, :, sl] = (ln + bonus) * _silu(proj_ref[0, :, sl])


def _rwkv_chunked(proj, shift0, s0, p, c):
    b, l, wp = proj.shape
    kern = functools.partial(_rwkv_chunk_kernel, c=c)
    full = lambda shape: pl.BlockSpec(shape, lambda i, j: (0,) * len(shape))
    row = lambda name: p[name].reshape(1, -1)
    return pl.pallas_call(
        kern,
        grid=(b, l // c),
        in_specs=[pl.BlockSpec((1, c, wp), lambda i, j: (i, j, 0)),
                  pl.BlockSpec((1, 1, RWKV_SHIFT_CH), lambda i, j: (i, 0, 0)),
                  pl.BlockSpec((1, N_HEADS, HEAD_DIM, HEAD_DIM), lambda i, j: (i, 0, 0, 0)),
                  full((1, RWKV_SHIFT_CH)), full((1, WIDTH)), full((RWKV_RANK, WIDTH)), full((1, WIDTH)),
                  full((RWKV_RANK, WIDTH)), full((1, WIDTH)), full((1, WIDTH)), full((1, WIDTH)),
                  full((1, WIDTH)), full((1, WIDTH))],
        out_specs=[pl.BlockSpec((1, c, WIDTH), lambda i, j: (i, j, 0)),
                   pl.BlockSpec((1, N_HEADS, HEAD_DIM, HEAD_DIM), lambda i, j: (i, 0, 0, 0)),
                   pl.BlockSpec((1, 1, RWKV_SHIFT_CH), lambda i, j: (i, 0, 0))],
        out_shape=[jax.ShapeDtypeStruct((b, l, WIDTH), f32),
                   jax.ShapeDtypeStruct((b, N_HEADS, HEAD_DIM, HEAD_DIM), f32),
                   jax.ShapeDtypeStruct((b, 1, RWKV_SHIFT_CH), f32)],
        scratch_shapes=[pltpu.VMEM((c + 8, RWKV_SHIFT_CH), f32)],
        compiler_params=_params("parallel", "arbitrary"),
        name="rwkv7_chunked",
    )(proj, shift0, s0, row("rwkv_mu"), row("rwkv_w0"), p["rwkv_w2"], row("rwkv_a0"), p["rwkv_a2"],
      row("rwkv_k_k"), row("rwkv_k_a"), row("rwkv_r_k"), row("rwkv_ln_w"), row("rwkv_ln_b"))


def _mlstm_kernel(proj_ref, conv0_ref, c0_ref, n0_ref, m0_ref, cw_ref, cb_ref, gb_ref, nw_ref,
                  out_ref, cn_ref, nn_ref, mn_ref, convn_ref, ext_scr, h_scr, *, c):
    @pl.when(pl.program_id(1) == 0)
    def _():
        ext_scr[5:8, :] = conv0_ref[0]
        cn_ref[0] = c0_ref[0]
        nn_ref[0] = n0_ref[0]
        mn_ref[0] = m0_ref[0]

    u = proj_ref[0, :, WIDTH:3 * WIDTH]
    acc, tail = _causal_conv(ext_scr, u, c, cw_ref, cb_ref)
    convn_ref[0] = tail
    qk = _silu(acc)
    gates = proj_ref[0, :, 5 * WIDTH:] + gb_ref[...]
    lf = -_softplus(-gates)
    row = _iota2((c, c), 0)
    col = _iota2((c, c), 1)
    causal = row >= col
    bc = _dot(causal.astype(f32), lf, HI)
    bc_t = bc.T
    ig_t = gates.T
    m_prev = mn_ref[0]
    lane = _iota2((1, LANES), 1)
    m_out = jnp.zeros((1, LANES), f32)
    heads = range(N_HEADS)
    sls = [slice(h * HEAD_DIM, (h + 1) * HEAD_DIM) for h in heads]
    b_cols = [bc[:, N_HEADS + h:N_HEADS + h + 1] for h in heads]
    dmats = [jnp.where(causal, b_cols[h] - bc_t[N_HEADS + h:N_HEADS + h + 1, :] + ig_t[h:h + 1, :], -jnp.inf)
             for h in heads]
    m0s = [m_prev[:, h:h + 1] for h in heads]
    gcols = [b_cols[h] + m0s[h] for h in heads]
    mts = [jnp.maximum(gcols[h], jnp.max(dmats[h], axis=-1, keepdims=True)) for h in heads]
    qhs = [qk[:, sl] for sl in sls]
    khs = [qk[:, WIDTH + h * HEAD_DIM:WIDTH + (h + 1) * HEAD_DIM] * HEAD_DIM ** -0.5 for h in heads]
    vhs = [proj_ref[0, :, 3 * WIDTH + h * HEAD_DIM:3 * WIDTH + (h + 1) * HEAD_DIM] for h in heads]
    cprevs = [cn_ref[0, h] for h in heads]
    nprevs = [nn_ref[0, h:h + 1, :] for h in heads]
    sws = [jnp.exp(dmats[h] - mts[h]) * _dot_nt(qhs[h], khs[h]) for h in heads]
    wis = [jnp.exp(gcols[h] - mts[h]) for h in heads]
    qcs = [_dot(qhs[h], cprevs[h]) for h in heads]
    nums = [_dot(sws[h], vhs[h]) + wis[h] * qcs[h] for h in heads]
    for h in heads:
        den = jnp.sum(sws[h], axis=-1, keepdims=True) + wis[h] * jnp.sum(qhs[h] * nprevs[h], axis=-1, keepdims=True)
        h_scr[:, sls[h]] = nums[h] / jnp.maximum(jnp.abs(den), jnp.exp(-mts[h]))
    m_news = [mts[h][c - 1:c, :] for h in heads]
    b_ends = [b_cols[h][c - 1:c, :] for h in heads]
    kws = [khs[h] * jnp.exp(b_ends[h] - b_cols[h] + gates[:, h:h + 1] - m_news[h]) for h in heads]
    kvs = [_dot_tn(kws[h], vhs[h]) for h in heads]
    for h in heads:
        dec = jnp.exp(b_ends[h] + m0s[h] - m_news[h])
        cn_ref[0, h] = dec * cprevs[h] + kvs[h]
        nn_ref[0, h:h + 1, :] = dec * nprevs[h] + jnp.sum(kws[h], axis=0, keepdims=True)
        m_out = jnp.where(lane == h, m_news[h], m_out)
    mn_ref[0] = m_out
    for h in range(N_HEADS):
        sl = slice(h * HEAD_DIM, (h + 1) * HEAD_DIM)
        hm = jax.nn.sigmoid(proj_ref[0, :, 4 * WIDTH + h * HEAD_DIM:4 * WIDTH + (h + 1) * HEAD_DIM]) * h_scr[:, sl]
        ms = jnp.mean(hm * hm, axis=-1, keepdims=True)
        out_ref[0, :, sl] = hm * lax.rsqrt(ms + NORM_EPS) * nw_ref[:, sl] * _silu(proj_ref[0, :, sl])


def _mlstm(proj, conv0, c0, n0, m0, p, c):
    b, l, wp = proj.shape
    kern = functools.partial(_mlstm_kernel, c=c)
    full = lambda shape: pl.BlockSpec(shape, lambda i, j: (0,) * len(shape))
    m0p = jnp.pad(m0, ((0, 0), (0, LANES - N_HEADS))).reshape(b, 1, LANES)
    outs = pl.pallas_call(
        kern,
        grid=(b, l // c),
        in_specs=[pl.BlockSpec((1, c, wp), lambda i, j: (i, j, 0)),
                  pl.BlockSpec((1, CONV_W - 1, 2 * WIDTH), lambda i, j: (i, 0, 0)),
                  pl.BlockSpec((1, N_HEADS, HEAD_DIM, HEAD_DIM), lambda i, j: (i, 0, 0, 0)),
                  pl.BlockSpec((1, N_HEADS, HEAD_DIM), lambda i, j: (i, 0, 0)),
                  pl.BlockSpec((1, 1, LANES), lambda i, j: (i, 0, 0)),
                  full((CONV_W, 2 * WIDTH)), full((1, 2 * WIDTH)), full((1, LANES)), full((1, WIDTH))],
        out_specs=[pl.BlockSpec((1, c, WIDTH), lambda i, j: (i, j, 0)),
                   pl.BlockSpec((1, N_HEADS, HEAD_DIM, HEAD_DIM), lambda i, j: (i, 0, 0, 0)),
                   pl.BlockSpec((1, N_HEADS, HEAD_DIM), lambda i, j: (i, 0, 0)),
                   pl.BlockSpec((1, 1, LANES), lambda i, j: (i, 0, 0)),
                   pl.BlockSpec((1, CONV_W - 1, 2 * WIDTH), lambda i, j: (i, 0, 0))],
        out_shape=[jax.ShapeDtypeStruct((b, l, WIDTH), f32),
                   jax.ShapeDtypeStruct((b, N_HEADS, HEAD_DIM, HEAD_DIM), f32),
                   jax.ShapeDtypeStruct((b, N_HEADS, HEAD_DIM), f32),
                   jax.ShapeDtypeStruct((b, 1, LANES), f32),
                   jax.ShapeDtypeStruct((b, CONV_W - 1, 2 * WIDTH), f32)],
        scratch_shapes=[pltpu.VMEM((c + 8, 2 * WIDTH), f32), pltpu.VMEM((c, WIDTH), f32)],
        compiler_params=_params("parallel", "arbitrary"),
        name="mlstm_branch",
    )(proj, conv0, c0, n0, m0p, p["mlstm_conv_w"], p["mlstm_conv_b"].reshape(1, -1),
      _pad_lanes(p["mlstm_gate_b"]), p["mlstm_norm_w"].reshape(1, -1))
    out, cn, nn, mn, convn = outs
    return out, cn, nn, mn[:, 0, :N_HEADS], convn


def _head_mean_matrix():
    hid = jnp.arange(ATTN_W) // HEAD_DIM
    return ((hid[:, None] == hid[None, :]).astype(f32) / HEAD_DIM).astype(bf16)


def _qk_norm(x, hm_ref, w_ref):
    sq = x * x
    hi = sq.astype(bf16)
    lo = (sq - hi.astype(f32)).astype(bf16)
    ms = _dot(hi, hm_ref[...]) + _dot(lo, hm_ref[...])
    return x * lax.rsqrt(ms + NORM_EPS) * w_ref[...]


def _attn_prompt_kernel(q_ref, k_ref, v_ref, z_ref, hm_ref, qw_ref, kw_ref, out_ref, kv_ref,
                        qn_scr, kn_scr, vv_scr, acc_scr, m_scr, d_scr, to_scr, tl_scr, *, l):
    g = pl.program_id(1)
    rb = 256
    n_pairs = ATTN_W // LANES

    def norm_body(i, carry):
        rows = pl.ds(pl.multiple_of(i * rb, rb), rb)
        qn = _qk_norm(q_ref[0, rows, :], hm_ref, qw_ref)
        kn = _qk_norm(k_ref[0, rows, :], hm_ref, kw_ref)
        v = v_ref[0, rows, :]
        kv_ref[0, 0, rows, 0:ATTN_W] = kn
        kv_ref[0, 0, rows, ATTN_W:2 * ATTN_W] = v
        for p in range(n_pairs):
            qn_scr[p, rows, :] = qn[:, p * LANES:(p + 1) * LANES]
            kn_scr[p, rows, :] = kn[:, p * LANES:(p + 1) * LANES]
            vv_scr[p, rows, :] = v[:, p * LANES:(p + 1) * LANES]
        return carry

    lax.fori_loop(0, l // rb, norm_body, 0)

    qb = ATTN_J
    qi = _iota2((qb, qb), 0)
    ki = _iota2((qb, qb), 1)
    scale = HEAD_DIM ** -0.5

    def run_group(dil, first):
        n_iter = l // qb

        def body(i, carry):
            r = i % dil
            blk = i // dil
            u0 = blk * qb
            rows_c = pl.ds(u0 * dil + r, qb, stride=dil)
            rows_p = pl.ds(jnp.maximum(u0 - qb, 0) * dil + r, qb, stride=dil)
            mask_c = ki <= qi
            mask_p = (ki >= qi) & (blk > 0)
            hsl = [slice(hh * HEAD_DIM, (hh + 1) * HEAD_DIM) for hh in range(LANES // HEAD_DIM)]
            heads = [(p, sl) for p in range(n_pairs) for sl in hsl]
            qv = [qn_scr[p, rows_c, :] for p in range(n_pairs)]
            kc = [kn_scr[p, rows_c, :] for p in range(n_pairs)]
            vc = [vv_scr[p, rows_c, :] for p in range(n_pairs)]
            kp = [kn_scr[p, rows_p, :] for p in range(n_pairs)]
            vp = [vv_scr[p, rows_p, :] for p in range(n_pairs)]
            scs = [jnp.where(mask_c, _dot_nt(qv[p][:, sl], kc[p][:, sl]) * scale, -jnp.inf) for p, sl in heads]
            sps = [jnp.where(mask_p, _dot_nt(qv[p][:, sl], kp[p][:, sl]) * scale, -jnp.inf) for p, sl in heads]
            ms = [jnp.maximum(jnp.max(sc, axis=-1, keepdims=True), jnp.max(sp, axis=-1, keepdims=True))
                  for sc, sp in zip(scs, sps)]
            pcs = [jnp.exp(sc - m) for sc, m in zip(scs, ms)]
            pps = [jnp.exp(sp - m) for sp, m in zip(sps, ms)]
            dens = [jnp.sum(pc, axis=-1, keepdims=True) + jnp.sum(pp, axis=-1, keepdims=True)
                    for pc, pp in zip(pcs, pps)]
            for (p, sl), pc, pp, m, den in zip(heads, pcs, pps, ms, dens):
                to_scr[p, :, sl] = (_dot(pc, vc[p][:, sl]) + _dot(pp, vp[p][:, sl])) / den
                tl_scr[p, :, sl] = jnp.broadcast_to(m + jnp.log(den), (qb, HEAD_DIM))
            for p in range(n_pairs):
                o = to_scr[p]
                lse = tl_scr[p]
                if first:
                    acc_scr[p, rows_c, :] = o
                    m_scr[p, rows_c, :] = lse
                    d_scr[p, rows_c, :] = jnp.ones_like(lse)
                else:
                    m_old = m_scr[p, rows_c, :]
                    m_new = jnp.maximum(m_old, lse)
                    a_old = jnp.exp(m_old - m_new)
                    a_new = jnp.exp(lse - m_new)
                    acc_scr[p, rows_c, :] = acc_scr[p, rows_c, :] * a_old + o * a_new
                    d_scr[p, rows_c, :] = d_scr[p, rows_c, :] * a_old + a_new
                    m_scr[p, rows_c, :] = m_new
            return carry

        lax.fori_loop(0, n_iter, body, 0)

    for gi, (_, dil) in enumerate(ATTN_GROUPS):
        pl.when(g == gi)(functools.partial(run_group, dil, gi == 0))

    @pl.when(g == len(ATTN_GROUPS) - 1)
    def _():
        def fin_body(i, carry):
            rows = pl.ds(pl.multiple_of(i * rb, rb), rb)
            for p in range(n_pairs):
                lanes = slice(p * LANES, (p + 1) * LANES)
                out_ref[0, rows, lanes] = acc_scr[p, rows, :] / d_scr[p, rows, :] * _silu(z_ref[0, rows, lanes])
            return carry

        lax.fori_loop(0, l // rb, fin_body, 0)


def _attn_prompt(proj, p):
    b, l, _ = proj.shape
    ng = len(ATTN_GROUPS)
    kern = functools.partial(_attn_prompt_kernel, l=l)
    col = lambda base: pl.BlockSpec((1, l, ATTN_W), lambda i, g: (i, 0, base + g))
    full = lambda shape: pl.BlockSpec(shape, lambda i, g: (0,) * len(shape))
    pair_scr = pltpu.VMEM((ATTN_W // LANES, l, LANES), f32)
    tile = lambda name: jnp.tile(p[name], ATTN_HEADS).reshape(1, ATTN_W)
    return pl.pallas_call(
        kern,
        grid=(b, ng),
        in_specs=[col(1), col(1 + ng), col(1 + 2 * ng),
                  pl.BlockSpec((1, l, ATTN_W), lambda i, g: (i, 0, 0)),
                  full((ATTN_W, ATTN_W)), full((1, ATTN_W)), full((1, ATTN_W))],
        out_specs=[pl.BlockSpec((1, l, ATTN_W), lambda i, g: (i, 0, 0)),
                   pl.BlockSpec((1, 1, l, 2 * ATTN_W), lambda i, g: (i, g, 0, 0))],
        out_shape=[jax.ShapeDtypeStruct((b, l, ATTN_W), f32),
                   jax.ShapeDtypeStruct((b, ng, l, 2 * ATTN_W), f32)],
        scratch_shapes=[pair_scr] * 6 + [pltpu.VMEM((ATTN_W // LANES, ATTN_J, LANES), f32)] * 2,
        compiler_params=_params("parallel", "arbitrary"),
        name="dilated_attn_prompt",
    )(proj, proj, proj, proj, _head_mean_matrix(), tile("attn_q_norm"), tile("attn_k_norm"))


def _attn_step_kernel(x_ref, c0_ref, c1_ref, c2_ref, hm_ref, qw_ref, kw_ref, out_ref, kv_ref, *, l):
    x = x_ref[0]
    ng = len(ATTN_GROUPS)
    scale = HEAD_DIM ** -0.5
    c_refs = (c0_ref, c1_ref, c2_ref)
    hsl = [slice(h * HEAD_DIM, (h + 1) * HEAD_DIM) for h in range(ATTN_HEADS)]
    qns, kns, vs, masks_p, masks_n = [], [], [], [], []
    for gi, (window, dil) in enumerate(ATTN_GROUPS):
        w = c_refs[gi].shape[-1]
        jmax = window // dil
        shift = dil.bit_length() - 1
        qns.append(_qk_norm(x[:, (1 + gi) * ATTN_W:(2 + gi) * ATTN_W], hm_ref, qw_ref))
        kns.append(_qk_norm(x[:, (1 + ng + gi) * ATTN_W:(2 + ng + gi) * ATTN_W], hm_ref, kw_ref))
        vs.append(x[:, (1 + 2 * ng + gi) * ATTN_W:(2 + 2 * ng + gi) * ATTN_W])
        kv_ref[0, gi, :, 0:ATTN_W] = kns[gi]
        kv_ref[0, gi, :, ATTN_W:2 * ATTN_W] = vs[gi]
        dist_p = w + _iota2((l, w), 0) - _iota2((l, w), 1)
        masks_p.append(((dist_p & (dil - 1)) == 0) & ((dist_p >> shift) <= jmax))
        dist_n = _iota2((l, l), 0) - _iota2((l, l), 1)
        masks_n.append((dist_n >= 0) & ((dist_n & (dil - 1)) == 0) & ((dist_n >> shift) <= jmax))
    gh = [(gi, h) for gi in range(ng) for h in range(ATTN_HEADS)]
    sps = [jnp.where(masks_p[gi], _dot(qns[gi][:, hsl[h]], c_refs[gi][0, 0, 0, h]) * scale, -jnp.inf) for gi, h in gh]
    sns = [jnp.where(masks_n[gi], _dot_nt(qns[gi][:, hsl[h]], kns[gi][:, hsl[h]]) * scale, -jnp.inf) for gi, h in gh]
    ms = [jnp.maximum(jnp.max(sp, axis=-1, keepdims=True), jnp.max(sn, axis=-1, keepdims=True))
          for sp, sn in zip(sps, sns)]
    pps = [jnp.exp(sp - m) for sp, m in zip(sps, ms)]
    pns = [jnp.exp(sn - m) for sn, m in zip(sns, ms)]
    dens = [jnp.sum(pp, axis=-1, keepdims=True) + jnp.sum(pn, axis=-1, keepdims=True) for pp, pn in zip(pps, pns)]
    og = [(_dot_nt(pp, c_refs[gi][0, 0, 1, h]) + _dot(pn, vs[gi][:, hsl[h]])) / den
          for (gi, h), pp, pn, den in zip(gh, pps, pns, dens)]
    lg = [m + jnp.log(den) for m, den in zip(ms, dens)]
    outs = [og[gi * ATTN_HEADS:(gi + 1) * ATTN_HEADS] for gi in range(ng)]
    lses = [lg[gi * ATTN_HEADS:(gi + 1) * ATTN_HEADS] for gi in range(ng)]
    for h in range(ATTN_HEADS):
        sl = slice(h * HEAD_DIM, (h + 1) * HEAD_DIM)
        mx = functools.reduce(jnp.maximum, [lses[gi][h] for gi in range(ng)])
        ws = [jnp.exp(lses[gi][h] - mx) for gi in range(ng)]
        tot = functools.reduce(lambda a, b: a + b, ws)
        o = functools.reduce(lambda a, b: a + b, [outs[gi][h] * (ws[gi] / tot) for gi in range(ng)])
        out_ref[0, :, sl] = o * _silu(x[:, sl])


def _attn_step(proj, caches, layer, p):
    b, l, wp = proj.shape
    ng = len(ATTN_GROUPS)
    kern = functools.partial(_attn_step_kernel, l=l)
    full = lambda shape: pl.BlockSpec(shape, lambda i: (0,) * len(shape))
    tile = lambda name: jnp.tile(p[name], ATTN_HEADS).reshape(1, ATTN_W)
    cache_specs = [pl.BlockSpec((1, 1) + c.shape[2:], lambda i: (layer, i, 0, 0, 0, 0)) for c in caches]
    return pl.pallas_call(
        kern,
        grid=(b,),
        in_specs=[pl.BlockSpec((1, l, wp), lambda i: (i, 0, 0))] + cache_specs
                 + [full((ATTN_W, ATTN_W)), full((1, ATTN_W)), full((1, ATTN_W))],
        out_specs=[pl.BlockSpec((1, l, ATTN_W), lambda i: (i, 0, 0)),
                   pl.BlockSpec((1, ng, l, 2 * ATTN_W), lambda i: (i, 0, 0, 0))],
        out_shape=[jax.ShapeDtypeStruct((b, l, ATTN_W), f32),
                   jax.ShapeDtypeStruct((b, ng, l, 2 * ATTN_W), f32)],
        compiler_params=_params("parallel"),
        name="dilated_attn_step",
    )(proj, *caches, _head_mean_matrix(), tile("attn_q_norm"), tile("attn_k_norm"))


def _merge_kernel(oa_ref, ob_ref, oc_ref, od_ref, mg_ref, x_ref, gate_ref, wb_ref, wo_ref, o_ref):
    merged = None
    start = 0
    for bi, ref in enumerate((oa_ref, ob_ref, oc_ref, od_ref)):
        width = ref.shape[-1]
        t = jnp.dot(ref[0].astype(bf16), wb_ref[start:start + width, :], preferred_element_type=f32)
        term = jax.nn.sigmoid(mg_ref[0, :, bi * D_MODEL:(bi + 1) * D_MODEL]) * t
        merged = term if merged is None else merged + term
        start += width
    y = jnp.dot(merged.astype(bf16), wo_ref[...], preferred_element_type=f32)
    o_ref[0] = x_ref[0] + gate_ref[0] * y


def _merge(branches, mg3, x3, gate3, wb, wo, tl):
    b, l, d = x3.shape
    per_row = gate3.shape[1] != 1
    rows = lambda w: pl.BlockSpec((1, tl, w), lambda i, j: (i, j, 0))
    gate_spec = pl.BlockSpec((1, tl if per_row else 1, d), (lambda i, j: (i, j, 0)) if per_row else (lambda i, j: (i, 0, 0)))
    full = lambda shape: pl.BlockSpec(shape, lambda i, j: (0,) * len(shape))
    return pl.pallas_call(
        _merge_kernel,
        grid=(b, l // tl),
        in_specs=[rows(o.shape[-1]) for o in branches] + [rows(N_BRANCH * d), rows(d), gate_spec,
                                                          full(wb.shape), full(wo.shape)],
        out_specs=rows(d),
        out_shape=jax.ShapeDtypeStruct((b, l, d), f32),
        compiler_params=_params("parallel", "parallel"),
        name="merge_out",
    )(*branches, mg3, x3, gate3, wb, wo)


def _split_w_in(w_in):
    cols, start = {}, 0
    for name, size in IN_SEGMENTS:
        cols[name] = w_in[:, start:start + size]
        start += size
    zeros = lambda n: jnp.zeros((w_in.shape[0], n), w_in.dtype)
    groups = {
        "ssm": [cols["ssm_z"], cols["ssm_xbc"], cols["ssm_dt"], zeros(LANES - N_HEADS)],
        "rwkv": [cols["rwkv_z"], cols["rwkv_shift"]],
        "mlstm": [cols["mlstm_z"], cols["mlstm_qk"], cols["mlstm_v"], cols["mlstm_o"], cols["mlstm_if"],
                  zeros(LANES - 2 * N_HEADS)],
        "attn": [cols["attn_z"], cols["attn_qkv"]],
        "merge": [cols["merge"]],
    }
    return {k: jnp.concatenate(v, axis=1).astype(bf16) for k, v in groups.items()}


_PROJ_TN = {"ssm": 1408, "rwkv": 2176, "mlstm": 896, "attn": 1280, "merge": 1024}


def _layer(x, mod, st, kv_past, p, w_groups, wb, wo):
    b, l, d = x.shape
    prompt = kv_past is None
    shift, scale, gate = mod[:, :d], mod[:, d:2 * d], mod[:, 2 * d:]
    if prompt:
        x3, sc3, sh3, g3 = x, scale[:, None], shift[:, None], gate[:, None]
        tl, chunk = 512, 128
        tl_merge = 256
    else:
        rep = lambda t: jnp.repeat(t, l, axis=0)[None]
        x3, sc3, sh3, g3 = x.reshape(1, b * l, d), rep(scale), rep(shift), rep(gate)
        tl, chunk = min(256, b * l), l
        tl_merge = tl
    h2 = _norm(x3, p["norm_w"], sc3, sh3, tl).reshape(b * l, d)
    proj = {k: _matmul(h2, w, min(512, b * l), _PROJ_TN[k]).reshape(b, l, -1) for k, w in w_groups.items()}

    out_a, ssm_new, ssm_conv_new = _ssm(proj["ssm"], st["ssm_conv"], st["ssm"], p, chunk)
    out_b, rwkv_new, shift_new = _rwkv_chunked(proj["rwkv"], st["rwkv_shift"][:, None], st["rwkv"], p,
                                               min(RWKV_CHUNK, l))
    out_c, c_new, n_new, m_new, mconv_new = _mlstm(proj["mlstm"], st["mlstm_conv"], st["mlstm_c"], st["mlstm_n"],
                                                   st["mlstm_m"], p, chunk)
    if prompt:
        out_d, kv = _attn_prompt(proj["attn"], p)
        kv_new = [kv[:, gi, l - min(w, l):].reshape(b, min(w, l), 2, ATTN_HEADS, HEAD_DIM)
                  for gi, (w, _) in enumerate(ATTN_GROUPS)]
    else:
        caches, layer = kv_past
        out_d, kv = _attn_step(proj["attn"], caches, layer, p)
        kv_new = [kv[:, gi].reshape(b, l, 2, ATTN_HEADS, HEAD_DIM) for gi in range(len(ATTN_GROUPS))]

    branches = [o.reshape(x3.shape[0], x3.shape[1], -1) for o in (out_a, out_b, out_c, out_d)]
    mg3 = proj["merge"].reshape(x3.shape[0], x3.shape[1], -1)
    x_new = _merge(branches, mg3, x3, g3, wb, wo, tl_merge).reshape(b, l, d)
    new_state = {"ssm": ssm_new, "ssm_conv": ssm_conv_new, "rwkv": rwkv_new, "rwkv_shift": shift_new[:, 0],
                 "mlstm_c": c_new, "mlstm_n": n_new, "mlstm_m": m_new, "mlstm_conv": mconv_new,
                 "kv_0": kv_new[0], "kv_1": kv_new[1], "kv_2": kv_new[2]}
    return x_new, new_state


_STATE_NAMES = ("ssm", "ssm_conv", "rwkv", "rwkv_shift", "mlstm_c", "mlstm_n", "mlstm_m", "mlstm_conv",
                "kv_0", "kv_1", "kv_2")


def kernel(x_prompt, x_sample, c_prompt, c_sample, state_ssm, state_ssm_conv, state_rwkv, state_rwkv_shift, state_mlstm_c, state_mlstm_n, state_mlstm_m, state_mlstm_conv, cache_kv_w128, cache_kv_w512, cache_kv_w2048, norm_w, ada_w, ada_b, w_in, w_branch, w_out, ssm_conv_w, ssm_conv_b, ssm_dt_bias, ssm_a_log, ssm_d, ssm_norm_w, rwkv_mu, rwkv_w0, rwkv_w2, rwkv_a0, rwkv_a2, rwkv_k_k, rwkv_k_a, rwkv_r_k, rwkv_ln_w, rwkv_ln_b, mlstm_conv_w, mlstm_conv_b, mlstm_gate_b, mlstm_norm_w, attn_q_norm, attn_k_norm):
    weights = {"norm_w": norm_w, "ssm_conv_w": ssm_conv_w, "ssm_conv_b": ssm_conv_b,
               "ssm_dt_bias": ssm_dt_bias, "ssm_a_log": ssm_a_log, "ssm_d": ssm_d,
               "ssm_norm_w": ssm_norm_w, "rwkv_mu": rwkv_mu, "rwkv_w0": rwkv_w0, "rwkv_w2": rwkv_w2,
               "rwkv_a0": rwkv_a0, "rwkv_a2": rwkv_a2, "rwkv_k_k": rwkv_k_k, "rwkv_k_a": rwkv_k_a,
               "rwkv_r_k": rwkv_r_k, "rwkv_ln_w": rwkv_ln_w, "rwkv_ln_b": rwkv_ln_b,
               "mlstm_conv_w": mlstm_conv_w, "mlstm_conv_b": mlstm_conv_b,
               "mlstm_gate_b": mlstm_gate_b, "mlstm_norm_w": mlstm_norm_w,
               "attn_q_norm": attn_q_norm, "attn_k_norm": attn_k_norm}
    bp = x_prompt.shape[0]
    fresh = {"ssm": jnp.zeros((bp, N_HEADS, HEAD_DIM, SSM_STATE), f32),
             "ssm_conv": jnp.zeros((bp, CONV_W - 1, SSM_CONV_CH), f32),
             "rwkv": jnp.zeros((bp, N_HEADS, HEAD_DIM, HEAD_DIM), f32),
             "rwkv_shift": jnp.zeros((bp, RWKV_SHIFT_CH), f32),
             "mlstm_c": jnp.zeros((bp, N_HEADS, HEAD_DIM, HEAD_DIM), f32),
             "mlstm_n": jnp.zeros((bp, N_HEADS, HEAD_DIM), f32),
             "mlstm_m": jnp.zeros((bp, N_HEADS), f32),
             "mlstm_conv": jnp.zeros((bp, CONV_W - 1, 2 * WIDTH), f32)}
    c_all = jnp.concatenate([c_prompt, c_sample], axis=0)
    caches_t = [jnp.transpose(c, (0, 1, 3, 4, 5, 2)) for c in (cache_kv_w128, cache_kv_w512, cache_kv_w2048)]
    y_prompt, y_sample = x_prompt, x_sample
    prompt_states, sample_states = [], []
    for layer in range(DEPTH):
        p = {name: arr[layer] for name, arr in weights.items()}
        mod = _ada(c_all, ada_w[layer], ada_b[layer])
        w_groups = _split_w_in(w_in[layer])
        wb = w_branch[layer].astype(bf16)
        wo = w_out[layer].astype(bf16)
        y_prompt, sp = _layer(y_prompt, mod[:bp], fresh, None, p, w_groups, wb, wo)
        st = {"ssm": state_ssm[layer], "ssm_conv": state_ssm_conv[layer], "rwkv": state_rwkv[layer],
              "rwkv_shift": state_rwkv_shift[layer], "mlstm_c": state_mlstm_c[layer],
              "mlstm_n": state_mlstm_n[layer], "mlstm_m": state_mlstm_m[layer],
              "mlstm_conv": state_mlstm_conv[layer]}
        y_sample, ss = _layer(y_sample, mod[bp:], st, (caches_t, layer), p, w_groups, wb, wo)
        prompt_states.append(sp)
        sample_states.append(ss)
    stack = lambda states, name: jnp.stack([s[name] for s in states])
    return ((y_prompt, y_sample)
            + tuple(stack(prompt_states, n) for n in _STATE_NAMES)
            + tuple(stack(sample_states, n) for n in _STATE_NAMES))
```

```python
import functools

import jax
import jax.numpy as jnp
from jax import lax
from jax.experimental import pallas as pl
from jax.experimental.pallas import tpu as pltpu

f32 = jnp.float32
bf16 = jnp.bfloat16
HI = lax.Precision.HIGHEST

D_MODEL = 1024
DEPTH = 2
HEAD_DIM = 64
NORM_EPS = 1e-6
CONV_W = 4
N_HEADS = 8
WIDTH = N_HEADS * HEAD_DIM
SSM_GROUPS = 2
SSM_STATE = 64
SSM_CONV_CH = WIDTH + 2 * SSM_GROUPS * SSM_STATE
RWKV_RANK = 64
RWKV_SHIFT_CH = 3 * WIDTH + 2 * RWKV_RANK
RWKV_LN_EPS = 64e-5
RWKV_CHUNK = 64
ATTN_GROUPS = ((128, 1), (512, 4), (2048, 16))
ATTN_HEADS = 4
ATTN_W = ATTN_HEADS * HEAD_DIM
ATTN_J = 128
N_BRANCH = 4
LANES = 128
VMEM_LIMIT = 52 * 1024 * 1024

IN_SEGMENTS = (
    ("ssm_z", WIDTH), ("ssm_xbc", SSM_CONV_CH), ("ssm_dt", N_HEADS),
    ("rwkv_z", WIDTH), ("rwkv_shift", RWKV_SHIFT_CH),
    ("mlstm_z", WIDTH), ("mlstm_qk", 2 * WIDTH), ("mlstm_v", WIDTH),
    ("mlstm_o", WIDTH), ("mlstm_if", 2 * N_HEADS),
    ("attn_z", ATTN_W), ("attn_qkv", 9 * ATTN_W),
    ("merge", N_BRANCH * D_MODEL),
)

NN = (((1,), (0,)), ((), ()))
NT = (((1,), (1,)), ((), ()))
TN = (((0,), (0,)), ((), ()))


def _mm(a, b, dims, prec):
    if prec is None:
        a, b = a.astype(bf16), b.astype(bf16)
    return lax.dot_general(a, b, dims, preferred_element_type=f32, precision=prec)


def _dot(a, b, prec=None):
    return _mm(a, b, NN, prec)


def _dot_nt(a, b, prec=None):
    return _mm(a, b, NT, prec)


def _dot_tn(a, b, prec=None):
    return _mm(a, b, TN, prec)


def _silu(x):
    return x * jax.nn.sigmoid(x)


def _softplus(x):
    return jnp.maximum(x, 0.0) + jnp.log1p(jnp.exp(-jnp.abs(x)))


def _params(*sem):
    return pltpu.CompilerParams(dimension_semantics=sem, vmem_limit_bytes=VMEM_LIMIT)


def _iota2(shape, dim):
    return lax.broadcasted_iota(jnp.int32, shape, dim)


def _running_sum(x, c):
    tri = (_iota2((c, c), 0) >= _iota2((c, c), 1)).astype(f32)
    return _dot(tri, x, HI)


def _vec_layout(fields):
    out, off = {}, 0
    for name, size in fields:
        padded = -(-size // LANES) * LANES
        out[name] = (off, padded)
        off += padded
    return out, off


def _pack_vecs(layout, total, vecs):
    parts = []
    for name, (_, padded) in layout.items():
        v = vecs[name].reshape(DEPTH, -1)
        parts.append(jnp.pad(v, ((0, 0), (0, padded - v.shape[1]))))
    packed = jnp.concatenate(parts, axis=1)
    assert packed.shape[1] == total
    return packed[:, None, :]


def _vec_reader(ref, layout):
    return lambda name: ref[0, :, layout[name][0]:layout[name][0] + layout[name][1]]


SSM_VEC, SSM_VEC_N = _vec_layout((("conv_b", SSM_CONV_CH), ("dt_bias", N_HEADS), ("a_log", N_HEADS),
                                  ("d", N_HEADS), ("norm_w", WIDTH)))
RWKV_VEC, RWKV_VEC_N = _vec_layout((("mu", RWKV_SHIFT_CH), ("w0", WIDTH), ("a0", WIDTH), ("k_k", WIDTH),
                                    ("k_a", WIDTH), ("r_k", WIDTH), ("ln_w", WIDTH), ("ln_b", WIDTH)))
MLSTM_VEC, MLSTM_VEC_N = _vec_layout((("conv_b", 2 * WIDTH), ("gate_b", 2 * N_HEADS), ("norm_w", WIDTH)))
ATTN_VEC, ATTN_VEC_N = _vec_layout((("q_norm", ATTN_W), ("k_norm", ATTN_W)))


def _layer_spec(shape, layer, grid_rank):
    zeros = (0,) * (len(shape) - 1)
    if grid_rank == 1:
        return pl.BlockSpec((1,) + tuple(shape[1:]), lambda i: (layer,) + zeros)
    return pl.BlockSpec((1,) + tuple(shape[1:]), lambda i, j: (layer,) + zeros)


def _ada_kernel(c_ref, w_ref, b_ref, o_ref):
    o_ref[...] = _dot(_silu(c_ref[...]), w_ref[0], HI) + b_ref[0]


def _ada(c, w, b3, layer):
    n = c.shape[0]
    return pl.pallas_call(
        _ada_kernel,
        grid=(3,),
        in_specs=[pl.BlockSpec((n, D_MODEL), lambda j: (0, 0)),
                  pl.BlockSpec((1, D_MODEL, D_MODEL), lambda j: (layer, 0, j)),
                  pl.BlockSpec((1, 1, D_MODEL), lambda j: (layer, 0, j))],
        out_specs=pl.BlockSpec((n, D_MODEL), lambda j: (0, j)),
        out_shape=jax.ShapeDtypeStruct((n, 3 * D_MODEL), f32),
        compiler_params=_params("parallel"),
        name="ada_mod",
    )(c, w, b3)


def _norm_kernel(x_ref, nw_ref, sc_ref, sh_ref, o_ref):
    x = x_ref[0]
    r = x * lax.rsqrt(jnp.mean(x * x, axis=-1, keepdims=True) + NORM_EPS)
    o_ref[0] = (r * nw_ref[0] * (1.0 + sc_ref[0]) + sh_ref[0]).astype(o_ref.dtype)


def _norm(x3, nw3, layer, sc3, sh3, tl):
    b, l, d = x3.shape
    per_row = sc3.shape[1] != 1
    mod_spec = pl.BlockSpec((1, tl if per_row else 1, d), (lambda i, j: (i, j, 0)) if per_row else (lambda i, j: (i, 0, 0)))
    return pl.pallas_call(
        _norm_kernel,
        grid=(b, l // tl),
        in_specs=[pl.BlockSpec((1, tl, d), lambda i, j: (i, j, 0)),
                  _layer_spec(nw3.shape, layer, 2), mod_spec, mod_spec],
        out_specs=pl.BlockSpec((1, tl, d), lambda i, j: (i, j, 0)),
        out_shape=jax.ShapeDtypeStruct((b, l, d), bf16),
        compiler_params=_params("parallel", "parallel"),
        name="mod_rmsnorm",
    )(x3, nw3, sc3, sh3)


def _mm_kernel(x_ref, w_ref, o_ref):
    o_ref[...] = jnp.dot(x_ref[...], w_ref[0], preferred_element_type=f32).astype(o_ref.dtype)


def _matmul(x, w3, layer, tm, tn, out_dtype):
    n, k = x.shape
    m = w3.shape[2]
    return pl.pallas_call(
        _mm_kernel,
        grid=(m // tn, n // tm),
        in_specs=[pl.BlockSpec((tm, k), lambda j, i: (i, 0)),
                  pl.BlockSpec((1, k, tn), lambda j, i: (layer, 0, j))],
        out_specs=pl.BlockSpec((tm, tn), lambda j, i: (i, j)),
        out_shape=jax.ShapeDtypeStruct((n, m), out_dtype),
        compiler_params=_params("parallel", "parallel"),
        name="in_proj",
    )(x, w3)


def _causal_conv(ext_scr, u, c, cw_ref, bias):
    ext_scr[8:8 + c, :] = u
    acc = bias
    for i in range(CONV_W):
        acc = acc + ext_scr[5 + i:5 + i + c, :] * cw_ref[0, i:i + 1, :]
    tail = ext_scr[c + 5:c + 8, :]
    ext_scr[5:8, :] = tail
    return acc, tail


def _ssm_kernel(proj_ref, conv0_ref, h0_ref, cw_ref, vec_ref, out_ref, hn_ref, convn_ref, ext_scr, y_scr, *, c):
    vec = _vec_reader(vec_ref, SSM_VEC)

    @pl.when(pl.program_id(1) == 0)
    def _():
        ext_scr[5:8, :] = conv0_ref[0, 0]
        hn_ref[0] = h0_ref[0, 0]

    u = proj_ref[0, :, WIDTH:WIDTH + SSM_CONV_CH].astype(f32)
    acc, tail = _causal_conv(ext_scr, u, c, cw_ref, vec("conv_b"))
    convn_ref[0] = tail
    xbc = _silu(acc)
    xs = xbc[:, :WIDTH]
    bm = xbc[:, WIDTH:WIDTH + LANES]
    cm = xbc[:, WIDTH + LANES:]
    dt = _softplus(proj_ref[0, :, WIDTH + SSM_CONV_CH:].astype(f32) + vec("dt_bias"))
    da = dt * (-jnp.exp(vec("a_log")))
    causal = _iota2((c, c), 0) >= _iota2((c, c), 1)
    cs = _running_sum(da, c)
    cs_t = cs.T
    dt_t = dt.T
    cs_end = cs[c - 1:c, :]
    dsk = vec("d")
    heads = range(N_HEADS)
    per_group = N_HEADS // SSM_GROUPS
    bgs = [bm[:, g * SSM_STATE:(g + 1) * SSM_STATE] for g in range(SSM_GROUPS)]
    cgs = [cm[:, g * SSM_STATE:(g + 1) * SSM_STATE] for g in range(SSM_GROUPS)]
    gmats = [_dot_nt(cgs[g], bgs[g]) for g in range(SSM_GROUPS)]
    sls = [slice(h * HEAD_DIM, (h + 1) * HEAD_DIM) for h in heads]
    cs_cols = [cs[:, h:h + 1] for h in heads]
    lmats = [gmats[h // per_group] * jnp.exp(jnp.where(causal, cs_cols[h] - cs_t[h:h + 1, :], -jnp.inf))
             * dt_t[h:h + 1, :] for h in heads]
    xhs = [xs[:, sl] for sl in sls]
    hprevs = [hn_ref[0, h] for h in heads]
    y_in = [_dot(lmats[h], xhs[h]) for h in heads]
    y_st = [_dot_nt(cgs[h // per_group], hprevs[h]) for h in heads]
    for h in heads:
        y_scr[:, sls[h]] = y_in[h] + y_st[h] * jnp.exp(cs_cols[h]) + xhs[h] * dsk[:, h:h + 1]
    ces = [cs_end[:, h:h + 1] for h in heads]
    upd = [_dot_tn(xhs[h] * (jnp.exp(ces[h] - cs_cols[h]) * dt[:, h:h + 1]), bgs[h // per_group]) for h in heads]
    for h in heads:
        hn_ref[0, h] = hprevs[h] * jnp.exp(ces[h]) + upd[h]
    gy = y_scr[...] * _silu(proj_ref[0, :, :WIDTH].astype(f32))
    nw = vec("norm_w")
    gw = WIDTH // SSM_GROUPS
    for g in range(SSM_GROUPS):
        part = gy[:, g * gw:(g + 1) * gw]
        ms = jnp.mean(part * part, axis=-1, keepdims=True)
        out_ref[0, :, g * gw:(g + 1) * gw] = part * lax.rsqrt(ms + NORM_EPS) * nw[:, g * gw:(g + 1) * gw]


def _state_spec(shape, sidx):
    zeros = (0,) * (len(shape) - 2)
    return pl.BlockSpec((1, 1) + tuple(shape[2:]), lambda i, j: (sidx, i) + zeros)


def _ssm(proj, conv0, h0, sidx, P, layer, c):
    b, l, wp = proj.shape
    return pl.pallas_call(
        functools.partial(_ssm_kernel, c=c),
        grid=(b, l // c),
        in_specs=[pl.BlockSpec((1, c, wp), lambda i, j: (i, j, 0)),
                  _state_spec(conv0.shape, sidx), _state_spec(h0.shape, sidx),
                  _layer_spec(P["ssm_conv_w"].shape, layer, 2), _layer_spec(P["ssm_vec"].shape, layer, 2)],
        out_specs=[pl.BlockSpec((1, c, WIDTH), lambda i, j: (i, j, 0)),
                   pl.BlockSpec((1, N_HEADS, HEAD_DIM, SSM_STATE), lambda i, j: (i, 0, 0, 0)),
                   pl.BlockSpec((1, CONV_W - 1, SSM_CONV_CH), lambda i, j: (i, 0, 0))],
        out_shape=[jax.ShapeDtypeStruct((b, l, WIDTH), f32),
                   jax.ShapeDtypeStruct((b, N_HEADS, HEAD_DIM, SSM_STATE), f32),
                   jax.ShapeDtypeStruct((b, CONV_W - 1, SSM_CONV_CH), f32)],
        scratch_shapes=[pltpu.VMEM((c + 8, SSM_CONV_CH), f32), pltpu.VMEM((c, WIDTH), f32)],
        compiler_params=_params("parallel", "arbitrary"),
        name="ssd_branch",
    )(proj, conv0, h0, P["ssm_conv_w"], P["ssm_vec"])


def _rwkv_chunk_kernel(proj_ref, shift0_ref, s0_ref, vec_ref, w2_ref, a2_ref,
                       out_ref, sn_ref, shiftn_ref, ext_scr, *, c):
    vec = _vec_reader(vec_ref, RWKV_VEC)

    @pl.when(pl.program_id(1) == 0)
    def _():
        ext_scr[7:8, :] = shift0_ref[0, 0]
        sn_ref[0] = s0_ref[0, 0]

    u = proj_ref[0, :, WIDTH:].astype(f32)
    ext_scr[8:8 + c, :] = u
    sh = u + (ext_scr[7:7 + c, :] - u) * vec("mu")
    last = ext_scr[c + 7:c + 8, :]
    ext_scr[7:8, :] = last
    shiftn_ref[0] = last
    r = sh[:, :WIDTH]
    k = sh[:, WIDTH:2 * WIDTH]
    v = sh[:, 2 * WIDTH:3 * WIDTH]
    w_lo = sh[:, 3 * WIDTH:3 * WIDTH + RWKV_RANK]
    a_lo = sh[:, 3 * WIDTH + RWKV_RANK:]
    w_log = -_softplus(-(vec("w0") + _dot(jnp.tanh(w_lo), w2_ref[0], HI))) - 0.5
    lw = -jnp.exp(w_log)
    a_in = jax.nn.sigmoid(vec("a0") + _dot(a_lo, a2_ref[0], HI))
    kk = k * vec("k_k")
    k2 = k * (1.0 + (a_in - 1.0) * vec("k_a"))
    rk, lnw, lnb = vec("r_k"), vec("ln_w"), vec("ln_b")

    cl = _running_sum(lw, c)
    cl_end = cl[c - 1:c, :]
    g_fwd = jnp.exp(cl)
    g_inv = jnp.exp(-cl)
    g_rem = jnp.exp(cl_end - cl)
    g_prev = jnp.exp(cl - lw)
    g_end = jnp.exp(cl_end)

    row2 = _iota2((c, 2 * c), 0)
    col2 = _iota2((c, 2 * c), 1)
    col2 = jnp.where(col2 >= c, col2 - c, col2)
    strict2 = row2 > col2
    incl2 = row2 >= col2
    n_double = (c - 1).bit_length()

    heads = range(N_HEADS)
    sls = [slice(h * HEAD_DIM, (h + 1) * HEAD_DIM) for h in heads]
    s0s = [sn_ref[0, h] for h in heads]
    bhs, ars, bks = [], [], []
    for sl in sls:
        kkh = kk[:, sl]
        kkn = kkh / jnp.maximum(jnp.sqrt(jnp.sum(kkh * kkh, axis=-1, keepdims=True)), 1e-12)
        bhs.append(kkn * a_in[:, sl])
        ars.append(jnp.concatenate([-kkn * g_prev[:, sl], r[:, sl] * g_fwd[:, sl]], axis=0))
        bks.append(jnp.concatenate([bhs[-1] * g_inv[:, sl], k2[:, sl] * g_inv[:, sl]], axis=0))
    m4s = [_dot_nt(ars[h], bks[h]) for h in heads]
    ahs = [_dot_nt(ars[h], s0s[h]) for h in heads]
    tops = [jnp.where(strict2, m4[:c, :], 0.0) for m4 in m4s]
    bots = [jnp.where(incl2, m4[c:, :], 0.0) for m4 in m4s]
    amats = [top[:, :c] for top in tops]
    xs = [ahs[h][:c] + _dot(tops[h][:, c:], v[:, sls[h]]) for h in heads]
    for i in range(n_double):
        xs = [xs[h] + _dot(amats[h], xs[h]) for h in heads]
        if i + 1 < n_double:
            amats = [_dot(amats[h], amats[h]) for h in heads]
    pvs = [jnp.concatenate([xs[h], v[:, sls[h]]], axis=0) for h in heads]
    ys = [ahs[h][c:] + _dot(bots[h], pvs[h]) for h in heads]
    for h in heads:
        sl = sls[h]
        bk_end = jnp.concatenate([bhs[h] * g_rem[:, sl], k2[:, sl] * g_rem[:, sl]], axis=0)
        sn_ref[0, h] = s0s[h] * g_end[:, sl] + _dot_tn(pvs[h], bk_end)

    for h in heads:
        sl = sls[h]
        y = ys[h]
        vh = v[:, sl]
        mu = jnp.mean(y, axis=-1, keepdims=True)
        var = jnp.mean(jnp.square(y - mu), axis=-1, keepdims=True)
        ln = (y - mu) * lax.rsqrt(var + RWKV_LN_EPS) * lnw[:, sl] + lnb[:, sl]
        bonus = jnp.sum(r[:, sl] * k2[:, sl] * rk[:, sl], axis=-1, keepdims=True) * vh
        out_ref[0, :, sl] = (ln + bonus) * _silu(proj_ref[0, :, sl].astype(f32))


def _rwkv(proj, shift0, s0, sidx, P, layer, c):
    b, l, wp = proj.shape
    return pl.pallas_call(
        functools.partial(_rwkv_chunk_kernel, c=c),
        grid=(b, l // c),
        in_specs=[pl.BlockSpec((1, c, wp), lambda i, j: (i, j, 0)),
                  _state_spec(shift0.shape, sidx), _state_spec(s0.shape, sidx),
                  _layer_spec(P["rwkv_vec"].shape, layer, 2),
                  _layer_spec(P["rwkv_w2"].shape, layer, 2), _layer_spec(P["rwkv_a2"].shape, layer, 2)],
        out_specs=[pl.BlockSpec((1, c, WIDTH), lambda i, j: (i, j, 0)),
                   pl.BlockSpec((1, N_HEADS, HEAD_DIM, HEAD_DIM), lambda i, j: (i, 0, 0, 0)),
                   pl.BlockSpec((1, 1, RWKV_SHIFT_CH), lambda i, j: (i, 0, 0))],
        out_shape=[jax.ShapeDtypeStruct((b, l, WIDTH), f32),
                   jax.ShapeDtypeStruct((b, N_HEADS, HEAD_DIM, HEAD_DIM), f32),
                   jax.ShapeDtypeStruct((b, 1, RWKV_SHIFT_CH), f32)],
        scratch_shapes=[pltpu.VMEM((c + 8, RWKV_SHIFT_CH), f32)],
        compiler_params=_params("parallel", "arbitrary"),
        name="rwkv7_chunked",
    )(proj, shift0, s0, P["rwkv_vec"], P["rwkv_w2"], P["rwkv_a2"])


def _mlstm_kernel(proj_ref, conv0_ref, c0_ref, n0_ref, m0_ref, cw_ref, vec_ref,
                  out_ref, cn_ref, nn_ref, mn_ref, convn_ref, ext_scr, h_scr, *, c):
    vec = _vec_reader(vec_ref, MLSTM_VEC)

    @pl.when(pl.program_id(1) == 0)
    def _():
        ext_scr[5:8, :] = conv0_ref[0, 0]
        cn_ref[0] = c0_ref[0, 0]
        nn_ref[0] = n0_ref[0, 0]
        mn_ref[0] = m0_ref[0, 0]

    u = proj_ref[0, :, WIDTH:3 * WIDTH].astype(f32)
    acc, tail = _causal_conv(ext_scr, u, c, cw_ref, vec("conv_b"))
    convn_ref[0] = tail
    qk = _silu(acc)
    gates = proj_ref[0, :, 5 * WIDTH:].astype(f32) + vec("gate_b")
    lf = -_softplus(-gates)
    causal = _iota2((c, c), 0) >= _iota2((c, c), 1)
    bc = _running_sum(lf, c)
    bc_t = bc.T
    ig_t = gates.T
    m_prev = mn_ref[0]
    lane = _iota2((1, LANES), 1)
    m_out = jnp.zeros((1, LANES), f32)
    heads = range(N_HEADS)
    sls = [slice(h * HEAD_DIM, (h + 1) * HEAD_DIM) for h in heads]
    b_cols = [bc[:, N_HEADS + h:N_HEADS + h + 1] for h in heads]
    dmats = [jnp.where(causal, b_cols[h] - bc_t[N_HEADS + h:N_HEADS + h + 1, :] + ig_t[h:h + 1, :], -jnp.inf)
             for h in heads]
    m0s = [m_prev[:, h:h + 1] for h in heads]
    gcols = [b_cols[h] + m0s[h] for h in heads]
    mts = [jnp.maximum(gcols[h], jnp.max(dmats[h], axis=-1, keepdims=True)) for h in heads]
    qhs = [qk[:, sl] for sl in sls]
    khs = [qk[:, WIDTH + h * HEAD_DIM:WIDTH + (h + 1) * HEAD_DIM] * HEAD_DIM ** -0.5 for h in heads]
    vhs = [proj_ref[0, :, 3 * WIDTH + h * HEAD_DIM:3 * WIDTH + (h + 1) * HEAD_DIM].astype(f32) for h in heads]
    cprevs = [cn_ref[0, h] for h in heads]
    nprevs = [nn_ref[0, h:h + 1, :] for h in heads]
    sws = [jnp.exp(dmats[h] - mts[h]) * _dot_nt(qhs[h], khs[h]) for h in heads]
    wis = [jnp.exp(gcols[h] - mts[h]) for h in heads]
    qcs = [_dot(qhs[h], cprevs[h]) for h in heads]
    nums = [_dot(sws[h], vhs[h]) + wis[h] * qcs[h] for h in heads]
    for h in heads:
        den = jnp.sum(sws[h], axis=-1, keepdims=True) + wis[h] * jnp.sum(qhs[h] * nprevs[h], axis=-1, keepdims=True)
        h_scr[:, sls[h]] = nums[h] / jnp.maximum(jnp.abs(den), jnp.exp(-mts[h]))
    m_news = [mts[h][c - 1:c, :] for h in heads]
    b_ends = [b_cols[h][c - 1:c, :] for h in heads]
    kws = [khs[h] * jnp.exp(b_ends[h] - b_cols[h] + gates[:, h:h + 1] - m_news[h]) for h in heads]
    kvs = [_dot_tn(kws[h], vhs[h]) for h in heads]
    for h in heads:
        dec = jnp.exp(b_ends[h] + m0s[h] - m_news[h])
        cn_ref[0, h] = dec * cprevs[h] + kvs[h]
        nn_ref[0, h:h + 1, :] = dec * nprevs[h] + jnp.sum(kws[h], axis=0, keepdims=True)
        m_out = jnp.where(lane == h, m_news[h], m_out)
    mn_ref[0] = m_out
    nw = vec("norm_w")
    for h in heads:
        sl = sls[h]
        o_gate = proj_ref[0, :, 4 * WIDTH + h * HEAD_DIM:4 * WIDTH + (h + 1) * HEAD_DIM].astype(f32)
        hm = jax.nn.sigmoid(o_gate) * h_scr[:, sl]
        ms = jnp.mean(hm * hm, axis=-1, keepdims=True)
        out_ref[0, :, sl] = hm * lax.rsqrt(ms + NORM_EPS) * nw[:, sl] * _silu(proj_ref[0, :, sl].astype(f32))


def _mlstm(proj, conv0, c0, n0, m0, sidx, P, layer, c):
    b, l, wp = proj.shape
    outs = pl.pallas_call(
        functools.partial(_mlstm_kernel, c=c),
        grid=(b, l // c),
        in_specs=[pl.BlockSpec((1, c, wp), lambda i, j: (i, j, 0)),
                  _state_spec(conv0.shape, sidx), _state_spec(c0.shape, sidx), _state_spec(n0.shape, sidx),
                  _state_spec(m0.shape, sidx),
                  _layer_spec(P["mlstm_conv_w"].shape, layer, 2), _layer_spec(P["mlstm_vec"].shape, layer, 2)],
        out_specs=[pl.BlockSpec((1, c, WIDTH), lambda i, j: (i, j, 0)),
                   pl.BlockSpec((1, N_HEADS, HEAD_DIM, HEAD_DIM), lambda i, j: (i, 0, 0, 0)),
                   pl.BlockSpec((1, N_HEADS, HEAD_DIM), lambda i, j: (i, 0, 0)),
                   pl.BlockSpec((1, 1, LANES), lambda i, j: (i, 0, 0)),
                   pl.BlockSpec((1, CONV_W - 1, 2 * WIDTH), lambda i, j: (i, 0, 0))],
        out_shape=[jax.ShapeDtypeStruct((b, l, WIDTH), f32),
                   jax.ShapeDtypeStruct((b, N_HEADS, HEAD_DIM, HEAD_DIM), f32),
                   jax.ShapeDtypeStruct((b, N_HEADS, HEAD_DIM), f32),
                   jax.ShapeDtypeStruct((b, 1, LANES), f32),
                   jax.ShapeDtypeStruct((b, CONV_W - 1, 2 * WIDTH), f32)],
        scratch_shapes=[pltpu.VMEM((c + 8, 2 * WIDTH), f32), pltpu.VMEM((c, WIDTH), f32)],
        compiler_params=_params("parallel", "arbitrary"),
        name="mlstm_branch",
    )(proj, conv0, c0, n0, m0, P["mlstm_conv_w"], P["mlstm_vec"])
    out, cn, nn, mn, convn = outs
    return out, cn, nn, mn[:, 0, :N_HEADS], convn


def _head_mean_matrix():
    hid = jnp.arange(ATTN_W) // HEAD_DIM
    return ((hid[:, None] == hid[None, :]).astype(f32) / HEAD_DIM).astype(bf16)


def _qk_norm(x, hm_ref, w):
    sq = x * x
    hi = sq.astype(bf16)
    lo = (sq - hi.astype(f32)).astype(bf16)
    ms = _dot(hi, hm_ref[...]) + _dot(lo, hm_ref[...])
    return x * lax.rsqrt(ms + NORM_EPS) * w


def _attn_prompt_kernel(q_ref, k_ref, v_ref, z_ref, hm_ref, vec_ref, out_ref, kv_ref,
                        qn_scr, kn_scr, vv_scr, acc_scr, m_scr, d_scr, to_scr, tl_scr, *, l):
    vec = _vec_reader(vec_ref, ATTN_VEC)
    g = pl.program_id(1)
    rb = 256
    n_pairs = ATTN_W // LANES

    def norm_body(i, carry):
        rows = pl.ds(pl.multiple_of(i * rb, rb), rb)
        qn = _qk_norm(q_ref[0, rows, :].astype(f32), hm_ref, vec("q_norm"))
        kn = _qk_norm(k_ref[0, rows, :].astype(f32), hm_ref, vec("k_norm"))
        v = v_ref[0, rows, :].astype(f32)
        kv_ref[0, 0, rows, 0:ATTN_W] = kn
        kv_ref[0, 0, rows, ATTN_W:2 * ATTN_W] = v
        for p in range(n_pairs):
            qn_scr[p, rows, :] = qn[:, p * LANES:(p + 1) * LANES]
            kn_scr[p, rows, :] = kn[:, p * LANES:(p + 1) * LANES]
            vv_scr[p, rows, :] = v[:, p * LANES:(p + 1) * LANES]
        return carry

    lax.fori_loop(0, l // rb, norm_body, 0)

    qb = ATTN_J
    qi = _iota2((qb, qb), 0)
    ki = _iota2((qb, qb), 1)
    scale = HEAD_DIM ** -0.5

    def run_group(dil, first):
        n_iter = l // qb

        def body(i, carry):
            r = i % dil
            blk = i // dil
            u0 = blk * qb
            rows_c = pl.ds(u0 * dil + r, qb, stride=dil)
            rows_p = pl.ds(jnp.maximum(u0 - qb, 0) * dil + r, qb, stride=dil)
            mask_c = ki <= qi
            mask_p = (ki >= qi) & (blk > 0)
            hsl = [slice(hh * HEAD_DIM, (hh + 1) * HEAD_DIM) for hh in range(LANES // HEAD_DIM)]
            heads = [(p, sl) for p in range(n_pairs) for sl in hsl]
            qv = [qn_scr[p, rows_c, :] for p in range(n_pairs)]
            kc = [kn_scr[p, rows_c, :] for p in range(n_pairs)]
            vc = [vv_scr[p, rows_c, :] for p in range(n_pairs)]
            kp = [kn_scr[p, rows_p, :] for p in range(n_pairs)]
            vp = [vv_scr[p, rows_p, :] for p in range(n_pairs)]
            scs = [jnp.where(mask_c, _dot_nt(qv[p][:, sl], kc[p][:, sl]) * scale, -jnp.inf) for p, sl in heads]
            sps = [jnp.where(mask_p, _dot_nt(qv[p][:, sl], kp[p][:, sl]) * scale, -jnp.inf) for p, sl in heads]
            ms = [jnp.maximum(jnp.max(sc, axis=-1, keepdims=True), jnp.max(sp, axis=-1, keepdims=True))
                  for sc, sp in zip(scs, sps)]
            pcs = [jnp.exp(sc - m) for sc, m in zip(scs, ms)]
            pps = [jnp.exp(sp - m) for sp, m in zip(sps, ms)]
            dens = [jnp.sum(pc, axis=-1, keepdims=True) + jnp.sum(pp, axis=-1, keepdims=True)
                    for pc, pp in zip(pcs, pps)]
            for (p, sl), pc, pp, m, den in zip(heads, pcs, pps, ms, dens):
                to_scr[p, :, sl] = (_dot(pc, vc[p][:, sl]) + _dot(pp, vp[p][:, sl])) / den
                tl_scr[p, :, sl] = jnp.broadcast_to(m + jnp.log(den), (qb, HEAD_DIM))
            for p in range(n_pairs):
                o = to_scr[p]
                lse = tl_scr[p]
                if first:
                    acc_scr[p, rows_c, :] = o
                    m_scr[p, rows_c, :] = lse
                    d_scr[p, rows_c, :] = jnp.ones_like(lse)
                else:
                    m_old = m_scr[p, rows_c, :]
                    m_new = jnp.maximum(m_old, lse)
                    a_old = jnp.exp(m_old - m_new)
                    a_new = jnp.exp(lse - m_new)
                    acc_scr[p, rows_c, :] = acc_scr[p, rows_c, :] * a_old + o * a_new
                    d_scr[p, rows_c, :] = d_scr[p, rows_c, :] * a_old + a_new
                    m_scr[p, rows_c, :] = m_new
            return carry

        lax.fori_loop(0, n_iter, body, 0)

    for gi, (_, dil) in enumerate(ATTN_GROUPS):
        pl.when(g == gi)(functools.partial(run_group, dil, gi == 0))

    @pl.when(g == len(ATTN_GROUPS) - 1)
    def _():
        def fin_body(i, carry):
            rows = pl.ds(pl.multiple_of(i * rb, rb), rb)
            for p in range(n_pairs):
                lanes = slice(p * LANES, (p + 1) * LANES)
                z = z_ref[0, rows, lanes].astype(f32)
                out_ref[0, rows, lanes] = acc_scr[p, rows, :] / d_scr[p, rows, :] * _silu(z)
            return carry

        lax.fori_loop(0, l // rb, fin_body, 0)


def _attn_prompt(proj, P, layer):
    b, l, _ = proj.shape
    ng = len(ATTN_GROUPS)
    col = lambda base: pl.BlockSpec((1, l, ATTN_W), lambda i, g: (i, 0, base + g))
    pair_scr = pltpu.VMEM((ATTN_W // LANES, l, LANES), f32)
    return pl.pallas_call(
        functools.partial(_attn_prompt_kernel, l=l),
        grid=(b, ng),
        in_specs=[col(1), col(1 + ng), col(1 + 2 * ng),
                  pl.BlockSpec((1, l, ATTN_W), lambda i, g: (i, 0, 0)),
                  pl.BlockSpec((ATTN_W, ATTN_W), lambda i, g: (0, 0)),
                  _layer_spec(P["attn_vec"].shape, layer, 2)],
        out_specs=[pl.BlockSpec((1, l, ATTN_W), lambda i, g: (i, 0, 0)),
                   pl.BlockSpec((1, 1, l, 2 * ATTN_W), lambda i, g: (i, g, 0, 0))],
        out_shape=[jax.ShapeDtypeStruct((b, l, ATTN_W), f32),
                   jax.ShapeDtypeStruct((b, ng, l, 2 * ATTN_W), f32)],
        scratch_shapes=[pair_scr] * 6 + [pltpu.VMEM((ATTN_W // LANES, ATTN_J, LANES), f32)] * 2,
        compiler_params=_params("parallel", "arbitrary"),
        name="dilated_attn_prompt",
    )(proj, proj, proj, proj, P["head_mean"], P["attn_vec"])


def _attn_step_kernel(x_ref, c0_ref, c1_ref, c2_ref, hm_ref, vec_ref, out_ref, kv_ref, *, l):
    vec = _vec_reader(vec_ref, ATTN_VEC)
    x = x_ref[0]
    ng = len(ATTN_GROUPS)
    scale = HEAD_DIM ** -0.5
    c_refs = (c0_ref, c1_ref, c2_ref)
    hsl = [slice(h * HEAD_DIM, (h + 1) * HEAD_DIM) for h in range(ATTN_HEADS)]
    qns, kns, vs, masks_p, masks_n = [], [], [], [], []
    for gi, (window, dil) in enumerate(ATTN_GROUPS):
        w = c_refs[gi].shape[-1]
        jmax = window // dil
        shift = dil.bit_length() - 1
        qns.append(_qk_norm(x[:, (1 + gi) * ATTN_W:(2 + gi) * ATTN_W], hm_ref, vec("q_norm")))
        kns.append(_qk_norm(x[:, (1 + ng + gi) * ATTN_W:(2 + ng + gi) * ATTN_W], hm_ref, vec("k_norm")))
        vs.append(x[:, (1 + 2 * ng + gi) * ATTN_W:(2 + 2 * ng + gi) * ATTN_W])
        kv_ref[0, gi, :, 0:ATTN_W] = kns[gi]
        kv_ref[0, gi, :, ATTN_W:2 * ATTN_W] = vs[gi]
        dist_p = w + _iota2((l, w), 0) - _iota2((l, w), 1)
        masks_p.append(((dist_p & (dil - 1)) == 0) & ((dist_p >> shift) <= jmax))
        dist_n = _iota2((l, l), 0) - _iota2((l, l), 1)
        masks_n.append((dist_n >= 0) & ((dist_n & (dil - 1)) == 0) & ((dist_n >> shift) <= jmax))
    gh = [(gi, h) for gi in range(ng) for h in range(ATTN_HEADS)]
    sps = [jnp.where(masks_p[gi], _dot(qns[gi][:, hsl[h]], c_refs[gi][0, 0, 0, h]) * scale, -jnp.inf) for gi, h in gh]
    sns = [jnp.where(masks_n[gi], _dot_nt(qns[gi][:, hsl[h]], kns[gi][:, hsl[h]]) * scale, -jnp.inf) for gi, h in gh]
    ms = [jnp.maximum(jnp.max(sp, axis=-1, keepdims=True), jnp.max(sn, axis=-1, keepdims=True))
          for sp, sn in zip(sps, sns)]
    pps = [jnp.exp(sp - m) for sp, m in zip(sps, ms)]
    pns = [jnp.exp(sn - m) for sn, m in zip(sns, ms)]
    dens = [jnp.sum(pp, axis=-1, keepdims=True) + jnp.sum(pn, axis=-1, keepdims=True) for pp, pn in zip(pps, pns)]
    og = [(_dot_nt(pp, c_refs[gi][0, 0, 1, h]) + _dot(pn, vs[gi][:, hsl[h]])) / den
          for (gi, h), pp, pn, den in zip(gh, pps, pns, dens)]
    lg = [m + jnp.log(den) for m, den in zip(ms, dens)]
    outs = [og[gi * ATTN_HEADS:(gi + 1) * ATTN_HEADS] for gi in range(ng)]
    lses = [lg[gi * ATTN_HEADS:(gi + 1) * ATTN_HEADS] for gi in range(ng)]
    for h in range(ATTN_HEADS):
        sl = hsl[h]
        mx = functools.reduce(jnp.maximum, [lses[gi][h] for gi in range(ng)])
        ws = [jnp.exp(lses[gi][h] - mx) for gi in range(ng)]
        tot = functools.reduce(lambda a, b: a + b, ws)
        o = functools.reduce(lambda a, b: a + b, [outs[gi][h] * (ws[gi] / tot) for gi in range(ng)])
        out_ref[0, :, sl] = o * _silu(x[:, sl])


def _attn_step(proj, caches, P, layer):
    b, l, wp = proj.shape
    ng = len(ATTN_GROUPS)
    cache_specs = [pl.BlockSpec((1, 1) + c.shape[2:], lambda i: (layer, i, 0, 0, 0, 0)) for c in caches]
    return pl.pallas_call(
        functools.partial(_attn_step_kernel, l=l),
        grid=(b,),
        in_specs=[pl.BlockSpec((1, l, wp), lambda i: (i, 0, 0))] + cache_specs
                 + [pl.BlockSpec((ATTN_W, ATTN_W), lambda i: (0, 0)), _layer_spec(P["attn_vec"].shape, layer, 1)],
        out_specs=[pl.BlockSpec((1, l, ATTN_W), lambda i: (i, 0, 0)),
                   pl.BlockSpec((1, ng, l, 2 * ATTN_W), lambda i: (i, 0, 0, 0))],
        out_shape=[jax.ShapeDtypeStruct((b, l, ATTN_W), f32),
                   jax.ShapeDtypeStruct((b, ng, l, 2 * ATTN_W), f32)],
        compiler_params=_params("parallel"),
        name="dilated_attn_step",
    )(proj, *caches, P["head_mean"], P["attn_vec"])


def _merge_kernel(oa_ref, ob_ref, oc_ref, od_ref, mg_ref, x_ref, gate_ref, wb_ref, wo_ref, o_ref):
    merged = None
    start = 0
    for bi, ref in enumerate((oa_ref, ob_ref, oc_ref, od_ref)):
        width = ref.shape[-1]
        t = jnp.dot(ref[0].astype(bf16), wb_ref[0, start:start + width, :], preferred_element_type=f32)
        term = jax.nn.sigmoid(mg_ref[0, :, bi * D_MODEL:(bi + 1) * D_MODEL].astype(f32)) * t
        merged = term if merged is None else merged + term
        start += width
    y = jnp.dot(merged.astype(bf16), wo_ref[0], preferred_element_type=f32)
    o_ref[0] = x_ref[0] + gate_ref[0] * y


def _merge(branches, mg3, x3, gate3, P, layer, tl):
    b, l, d = x3.shape
    per_row = gate3.shape[1] != 1
    rows = lambda w: pl.BlockSpec((1, tl, w), lambda i, j: (i, j, 0))
    gate_spec = pl.BlockSpec((1, tl if per_row else 1, d), (lambda i, j: (i, j, 0)) if per_row else (lambda i, j: (i, 0, 0)))
    return pl.pallas_call(
        _merge_kernel,
        grid=(b, l // tl),
        in_specs=[rows(o.shape[-1]) for o in branches] + [rows(N_BRANCH * d), rows(d), gate_spec,
                                                          _layer_spec(P["w_branch"].shape, layer, 2),
                                                          _layer_spec(P["w_out"].shape, layer, 2)],
        out_specs=rows(d),
        out_shape=jax.ShapeDtypeStruct((b, l, d), f32),
        compiler_params=_params("parallel", "parallel"),
        name="merge_out",
    )(*branches, mg3, x3, gate3, P["w_branch"], P["w_out"])


def _split_w_in(w_in):
    cols, start = {}, 0
    for name, size in IN_SEGMENTS:
        cols[name] = w_in[:, :, start:start + size]
        start += size
    zeros = lambda n: jnp.zeros(w_in.shape[:2] + (n,), w_in.dtype)
    groups = {
        "ssm": [cols["ssm_z"], cols["ssm_xbc"], cols["ssm_dt"], zeros(LANES - N_HEADS)],
        "rwkv": [cols["rwkv_z"], cols["rwkv_shift"]],
        "mlstm": [cols["mlstm_z"], cols["mlstm_qk"], cols["mlstm_v"], cols["mlstm_o"], cols["mlstm_if"],
                  zeros(LANES - 2 * N_HEADS)],
        "attn": [cols["attn_z"], cols["attn_qkv"]],
        "merge": [cols["merge"]],
    }
    return {k: jnp.concatenate(v, axis=2).astype(bf16) for k, v in groups.items()}


_PROJ_TN = {"ssm": 1408, "rwkv": 2176, "mlstm": 896, "attn": 1280, "merge": 1024}


def _layer(x, mod, st, sidx, caches, P, layer):
    b, l, d = x.shape
    prompt = caches is None
    shift, scale, gate = mod[:, :d], mod[:, d:2 * d], mod[:, 2 * d:]
    if prompt:
        x3, sc3, sh3, g3 = x, scale[:, None], shift[:, None], gate[:, None]
        tl, chunk, tl_merge, tm, proj_dtype = 512, 128, 256, 1024, bf16
    else:
        rep = lambda t: jnp.repeat(t, l, axis=0)[None]
        x3, sc3, sh3, g3 = x.reshape(1, b * l, d), rep(scale), rep(shift), rep(gate)
        tl = min(256, b * l)
        chunk, tl_merge, tm, proj_dtype = l, tl, min(512, b * l), f32
    h2 = _norm(x3, P["norm_w"], layer, sc3, sh3, tl).reshape(b * l, d)
    proj = {k: _matmul(h2, P["w_in_" + k], layer, tm, _PROJ_TN[k], proj_dtype).reshape(b, l, -1) for k in _PROJ_TN}

    out_a, ssm_new, ssm_conv_new = _ssm(proj["ssm"], st["ssm_conv"], st["ssm"], sidx, P, layer, chunk)
    out_b, rwkv_new, shift_new = _rwkv(proj["rwkv"], st["rwkv_shift"], st["rwkv"], sidx, P, layer, min(RWKV_CHUNK, l))
    out_c, c_new, n_new, m_new, mconv_new = _mlstm(proj["mlstm"], st["mlstm_conv"], st["mlstm_c"], st["mlstm_n"],
                                                   st["mlstm_m"], sidx, P, layer, chunk)
    if prompt:
        out_d, kv = _attn_prompt(proj["attn"], P, layer)
        kv_new = [kv[:, gi, l - min(w, l):].reshape(b, min(w, l), 2, ATTN_HEADS, HEAD_DIM)
                  for gi, (w, _) in enumerate(ATTN_GROUPS)]
    else:
        out_d, kv = _attn_step(proj["attn"], caches, P, layer)
        kv_new = [kv[:, gi].reshape(b, l, 2, ATTN_HEADS, HEAD_DIM) for gi in range(len(ATTN_GROUPS))]

    branches = [o.reshape(x3.shape[0], x3.shape[1], -1) for o in (out_a, out_b, out_c, out_d)]
    mg3 = proj["merge"].reshape(x3.shape[0], x3.shape[1], -1)
    x_new = _merge(branches, mg3, x3, g3, P, layer, tl_merge).reshape(b, l, d)
    new_state = {"ssm": ssm_new, "ssm_conv": ssm_conv_new, "rwkv": rwkv_new, "rwkv_shift": shift_new[:, 0],
                 "mlstm_c": c_new, "mlstm_n": n_new, "mlstm_m": m_new, "mlstm_conv": mconv_new,
                 "kv_0": kv_new[0], "kv_1": kv_new[1], "kv_2": kv_new[2]}
    return x_new, new_state


_STATE_NAMES = ("ssm", "ssm_conv", "rwkv", "rwkv_shift", "mlstm_c", "mlstm_n", "mlstm_m", "mlstm_conv",
                "kv_0", "kv_1", "kv_2")


def _prepare_params(w):
    P = {"norm_w": w["norm_w"][:, None, :], "ada_w": w["ada_w"], "ada_b": w["ada_b"][:, None, :],
         "w_branch": w["w_branch"].astype(bf16), "w_out": w["w_out"].astype(bf16),
         "ssm_conv_w": w["ssm_conv_w"], "mlstm_conv_w": w["mlstm_conv_w"],
         "rwkv_w2": w["rwkv_w2"], "rwkv_a2": w["rwkv_a2"], "head_mean": _head_mean_matrix(),
         "ssm_vec": _pack_vecs(SSM_VEC, SSM_VEC_N, {"conv_b": w["ssm_conv_b"], "dt_bias": w["ssm_dt_bias"],
                                                    "a_log": w["ssm_a_log"], "d": w["ssm_d"],
                                                    "norm_w": w["ssm_norm_w"]}),
         "rwkv_vec": _pack_vecs(RWKV_VEC, RWKV_VEC_N, {"mu": w["rwkv_mu"], "w0": w["rwkv_w0"], "a0": w["rwkv_a0"],
                                                       "k_k": w["rwkv_k_k"], "k_a": w["rwkv_k_a"],
                                                       "r_k": w["rwkv_r_k"], "ln_w": w["rwkv_ln_w"],
                                                       "ln_b": w["rwkv_ln_b"]}),
         "mlstm_vec": _pack_vecs(MLSTM_VEC, MLSTM_VEC_N, {"conv_b": w["mlstm_conv_b"], "gate_b": w["mlstm_gate_b"],
                                                          "norm_w": w["mlstm_norm_w"]}),
         "attn_vec": _pack_vecs(ATTN_VEC, ATTN_VEC_N, {"q_norm": jnp.tile(w["attn_q_norm"], (1, ATTN_HEADS)),
                                                       "k_norm": jnp.tile(w["attn_k_norm"], (1, ATTN_HEADS))})}
    for name, mat in _split_w_in(w["w_in"]).items():
        P["w_in_" + name] = mat
    return P


def _pad_heads(m):
    return jnp.pad(m, [(0, 0)] * (m.ndim - 1) + [(0, LANES - N_HEADS)])[..., None, :]


def kernel(x_prompt, x_sample, c_prompt, c_sample, state_ssm, state_ssm_conv, state_rwkv, state_rwkv_shift, state_mlstm_c, state_mlstm_n, state_mlstm_m, state_mlstm_conv, cache_kv_w128, cache_kv_w512, cache_kv_w2048, norm_w, ada_w, ada_b, w_in, w_branch, w_out, ssm_conv_w, ssm_conv_b, ssm_dt_bias, ssm_a_log, ssm_d, ssm_norm_w, rwkv_mu, rwkv_w0, rwkv_w2, rwkv_a0, rwkv_a2, rwkv_k_k, rwkv_k_a, rwkv_r_k, rwkv_ln_w, rwkv_ln_b, mlstm_conv_w, mlstm_conv_b, mlstm_gate_b, mlstm_norm_w, attn_q_norm, attn_k_norm):
    P = _prepare_params(dict(
        norm_w=norm_w, ada_w=ada_w, ada_b=ada_b, w_in=w_in, w_branch=w_branch, w_out=w_out, ssm_conv_w=ssm_conv_w,
        ssm_conv_b=ssm_conv_b, ssm_dt_bias=ssm_dt_bias, ssm_a_log=ssm_a_log, ssm_d=ssm_d, ssm_norm_w=ssm_norm_w,
        rwkv_mu=rwkv_mu, rwkv_w0=rwkv_w0, rwkv_w2=rwkv_w2, rwkv_a0=rwkv_a0, rwkv_a2=rwkv_a2, rwkv_k_k=rwkv_k_k,
        rwkv_k_a=rwkv_k_a, rwkv_r_k=rwkv_r_k, rwkv_ln_w=rwkv_ln_w, rwkv_ln_b=rwkv_ln_b, mlstm_conv_w=mlstm_conv_w,
        mlstm_conv_b=mlstm_conv_b, mlstm_gate_b=mlstm_gate_b, mlstm_norm_w=mlstm_norm_w, attn_q_norm=attn_q_norm,
        attn_k_norm=attn_k_norm))
    bp = x_prompt.shape[0]
    fresh = {"ssm": jnp.zeros((1, bp, N_HEADS, HEAD_DIM, SSM_STATE), f32),
             "ssm_conv": jnp.zeros((1, bp, CONV_W - 1, SSM_CONV_CH), f32),
             "rwkv": jnp.zeros((1, bp, N_HEADS, HEAD_DIM, HEAD_DIM), f32),
             "rwkv_shift": jnp.zeros((1, bp, 1, RWKV_SHIFT_CH), f32),
             "mlstm_c": jnp.zeros((1, bp, N_HEADS, HEAD_DIM, HEAD_DIM), f32),
             "mlstm_n": jnp.zeros((1, bp, N_HEADS, HEAD_DIM), f32),
             "mlstm_m": jnp.zeros((1, bp, 1, LANES), f32),
             "mlstm_conv": jnp.zeros((1, bp, CONV_W - 1, 2 * WIDTH), f32)}
    carried = {"ssm": state_ssm, "ssm_conv": state_ssm_conv, "rwkv": state_rwkv,
               "rwkv_shift": state_rwkv_shift[:, :, None, :], "mlstm_c": state_mlstm_c, "mlstm_n": state_mlstm_n,
               "mlstm_m": _pad_heads(state_mlstm_m), "mlstm_conv": state_mlstm_conv}
    c_all = jnp.concatenate([c_prompt, c_sample], axis=0)
    caches_t = [jnp.transpose(c, (0, 1, 3, 4, 5, 2)) for c in (cache_kv_w128, cache_kv_w512, cache_kv_w2048)]
    y_prompt, y_sample = x_prompt, x_sample
    prompt_states, sample_states = [], []
    for layer in range(DEPTH):
        mod = _ada(c_all, P["ada_w"], P["ada_b"], layer)
        y_prompt, sp = _layer(y_prompt, mod[:bp], fresh, 0, None, P, layer)
        y_sample, ss = _layer(y_sample, mod[bp:], carried, layer, caches_t, P, layer)
        prompt_states.append(sp)
        sample_states.append(ss)
    stack = lambda states, name: jnp.stack([s[name] for s in states])
    return ((y_prompt, y_sample)
            + tuple(stack(prompt_states, n) for n in _STATE_NAMES)
            + tuple(stack(sample_states, n) for n in _STATE_NAMES))
```

```python
import functools
import math

import jax
import jax.numpy as jnp
from jax import lax
from jax.experimental import pallas as pl
from jax.experimental.pallas import tpu as pltpu

f32 = jnp.float32
bf16 = jnp.bfloat16
HI = lax.Precision.HIGHEST

D_MODEL = 1024
DEPTH = 2
HEAD_DIM = 64
NORM_EPS = 1e-6
CONV_W = 4
N_HEADS = 8
WIDTH = N_HEADS * HEAD_DIM
SSM_GROUPS = 2
SSM_STATE = 64
SSM_CONV_CH = WIDTH + 2 * SSM_GROUPS * SSM_STATE
RWKV_RANK = 64
RWKV_SHIFT_CH = 3 * WIDTH + 2 * RWKV_RANK
RWKV_LN_EPS = 64e-5
RWKV_CHUNK = 64
ATTN_GROUPS = ((128, 1), (512, 4), (2048, 16))
ATTN_HEADS = 4
ATTN_W = ATTN_HEADS * HEAD_DIM
ATTN_J = 128
N_BRANCH = 4
LANES = 128
VMEM_LIMIT = 52 * 1024 * 1024

IN_SEGMENTS = (
    ("ssm_z", WIDTH), ("ssm_xbc", SSM_CONV_CH), ("ssm_dt", N_HEADS),
    ("rwkv_z", WIDTH), ("rwkv_shift", RWKV_SHIFT_CH),
    ("mlstm_z", WIDTH), ("mlstm_qk", 2 * WIDTH), ("mlstm_v", WIDTH),
    ("mlstm_o", WIDTH), ("mlstm_if", 2 * N_HEADS),
    ("attn_z", ATTN_W), ("attn_qkv", 9 * ATTN_W),
    ("merge", N_BRANCH * D_MODEL),
)

NN = (((1,), (0,)), ((), ()))
NT = (((1,), (1,)), ((), ()))
TN = (((0,), (0,)), ((), ()))


def _mm(a, b, dims, prec):
    if prec is None:
        a, b = a.astype(bf16), b.astype(bf16)
    return lax.dot_general(a, b, dims, preferred_element_type=f32, precision=prec)


def _dot(a, b, prec=None):
    return _mm(a, b, NN, prec)


def _dot_nt(a, b, prec=None):
    return _mm(a, b, NT, prec)


def _dot_tn(a, b, transpose_on_mxu=False):
    if not transpose_on_mxu:
        return _mm(a, b, TN, None)
    n = a.shape[1]
    eye = (_iota2((n, n), 0) == _iota2((n, n), 1)).astype(bf16)
    a_t = lax.dot_general(eye, a.astype(bf16), NT, preferred_element_type=f32).astype(bf16)
    return lax.dot_general(a_t, b.astype(bf16), NN, preferred_element_type=f32)


def _silu(x):
    return x * jax.nn.sigmoid(x)


def _softplus(x):
    return jnp.maximum(x, 0.0) + jnp.log1p(jnp.exp(-jnp.abs(x)))


def _params(*sem):
    return pltpu.CompilerParams(dimension_semantics=sem, vmem_limit_bytes=VMEM_LIMIT)


def _iota2(shape, dim):
    return lax.broadcasted_iota(jnp.int32, shape, dim)


def _running_sum(x, c):
    tri = (_iota2((c, c), 0) >= _iota2((c, c), 1)).astype(f32)
    return _dot(tri, x, HI)


def _vec_layout(fields):
    out, off = {}, 0
    for name, size in fields:
        padded = -(-size // LANES) * LANES
        out[name] = (off, padded)
        off += padded
    return out, off


def _pack_vecs(layout, total, vecs):
    parts = []
    for name, (_, padded) in layout.items():
        v = vecs[name].reshape(DEPTH, -1)
        parts.append(jnp.pad(v, ((0, 0), (0, padded - v.shape[1]))))
    packed = jnp.concatenate(parts, axis=1)
    assert packed.shape[1] == total
    return packed[:, None, :]


def _vec_reader(ref, layout):
    return lambda name: ref[0, :, layout[name][0]:layout[name][0] + layout[name][1]]


SSM_VEC, SSM_VEC_N = _vec_layout((("conv_b", SSM_CONV_CH), ("dt_bias", N_HEADS), ("a_log", N_HEADS),
                                  ("d", N_HEADS), ("norm_w", WIDTH)))
RWKV_VEC, RWKV_VEC_N = _vec_layout((("mu", RWKV_SHIFT_CH), ("w0", WIDTH), ("a0", WIDTH), ("k_k", WIDTH),
                                    ("k_a", WIDTH), ("r_k", WIDTH), ("ln_w", WIDTH), ("ln_b", WIDTH)))
MLSTM_VEC, MLSTM_VEC_N = _vec_layout((("conv_b", 2 * WIDTH), ("gate_b", 2 * N_HEADS), ("norm_w", WIDTH)))
ATTN_VEC, ATTN_VEC_N = _vec_layout((("q_norm", ATTN_W), ("k_norm", ATTN_W)))


def _layer_spec(shape, layer, grid_rank):
    zeros = (0,) * (len(shape) - 1)
    if grid_rank == 1:
        return pl.BlockSpec((1,) + tuple(shape[1:]), lambda i: (layer,) + zeros)
    return pl.BlockSpec((1,) + tuple(shape[1:]), lambda i, j: (layer,) + zeros)


def _ada_kernel(c_ref, w_ref, b_ref, o_ref):
    o_ref[...] = _dot(_silu(c_ref[...]), w_ref[0], HI) + b_ref[0]


def _ada(c, w, b3, layer):
    n = c.shape[0]
    return pl.pallas_call(
        _ada_kernel,
        grid=(3,),
        in_specs=[pl.BlockSpec((n, D_MODEL), lambda j: (0, 0)),
                  pl.BlockSpec((1, D_MODEL, D_MODEL), lambda j: (layer, 0, j)),
                  pl.BlockSpec((1, 1, D_MODEL), lambda j: (layer, 0, j))],
        out_specs=pl.BlockSpec((n, D_MODEL), lambda j: (0, j)),
        out_shape=jax.ShapeDtypeStruct((n, 3 * D_MODEL), f32),
        compiler_params=_params("parallel"),
        name="ada_mod",
    )(c, w, b3)


def _norm_kernel(x_ref, nw_ref, sc_ref, sh_ref, o_ref):
    x = x_ref[0]
    r = x * lax.rsqrt(jnp.mean(x * x, axis=-1, keepdims=True) + NORM_EPS)
    o_ref[0] = (r * nw_ref[0] * (1.0 + sc_ref[0]) + sh_ref[0]).astype(o_ref.dtype)


def _norm(x3, nw3, layer, sc3, sh3, tl):
    b, l, d = x3.shape
    per_row = sc3.shape[1] != 1
    mod_spec = pl.BlockSpec((1, tl if per_row else 1, d), (lambda i, j: (i, j, 0)) if per_row else (lambda i, j: (i, 0, 0)))
    return pl.pallas_call(
        _norm_kernel,
        grid=(b, l // tl),
        in_specs=[pl.BlockSpec((1, tl, d), lambda i, j: (i, j, 0)),
                  _layer_spec(nw3.shape, layer, 2), mod_spec, mod_spec],
        out_specs=pl.BlockSpec((1, tl, d), lambda i, j: (i, j, 0)),
        out_shape=jax.ShapeDtypeStruct((b, l, d), bf16),
        compiler_params=_params("parallel", "parallel"),
        name="mod_rmsnorm",
    )(x3, nw3, sc3, sh3)


def _mm_kernel(x_ref, w_ref, o_ref):
    o_ref[...] = jnp.dot(x_ref[...], w_ref[0], preferred_element_type=f32).astype(o_ref.dtype)


def _matmul(x, w3, layer, tm, tn, out_dtype):
    n, k = x.shape
    m = w3.shape[2]
    return pl.pallas_call(
        _mm_kernel,
        grid=(m // tn, n // tm),
        in_specs=[pl.BlockSpec((tm, k), lambda j, i: (i, 0)),
                  pl.BlockSpec((1, k, tn), lambda j, i: (layer, 0, j))],
        out_specs=pl.BlockSpec((tm, tn), lambda j, i: (i, j)),
        out_shape=jax.ShapeDtypeStruct((n, m), out_dtype),
        compiler_params=_params("parallel", "parallel"),
        name="in_proj",
    )(x, w3)


def _causal_conv(ext_scr, u, c, cw_ref, bias):
    ext_scr[8:8 + c, :] = u
    acc = bias
    for i in range(CONV_W):
        acc = acc + ext_scr[5 + i:5 + i + c, :] * cw_ref[0, i:i + 1, :]
    tail = ext_scr[c + 5:c + 8, :]
    ext_scr[5:8, :] = tail
    return acc, tail


def _interleave(gens):
    gens = list(gens)
    while gens:
        alive = []
        for g in gens:
            try:
                next(g)
                alive.append(g)
            except StopIteration:
                pass
        gens = alive


def _ssm_kernel(proj_ref, conv0_ref, h0_ref, cw_ref, vec_ref, out_ref, hn_ref, convn_ref, ext_scr, y_scr, *, c, nb):
    vec = _vec_reader(vec_ref, SSM_VEC)

    @pl.when(pl.program_id(1) == 0)
    def _():
        for bi in range(nb):
            ext_scr[bi, 5:8, :] = conv0_ref[0, bi]
            hn_ref[bi] = h0_ref[0, bi]

    causal = _iota2((c, c), 0) >= _iota2((c, c), 1)
    heads = range(N_HEADS)
    per_group = N_HEADS // SSM_GROUPS
    sls = [slice(h * HEAD_DIM, (h + 1) * HEAD_DIM) for h in heads]

    def elem(bi):
        proj, out, hn, y_acc = proj_ref.at[bi], out_ref.at[bi], hn_ref.at[bi], y_scr.at[bi]
        u = proj[:, WIDTH:WIDTH + SSM_CONV_CH].astype(f32)
        acc, tail = _causal_conv(ext_scr.at[bi], u, c, cw_ref, vec("conv_b"))
        convn_ref[bi] = tail
        xbc = _silu(acc)
        xs = xbc[:, :WIDTH]
        bm = xbc[:, WIDTH:WIDTH + LANES]
        cm = xbc[:, WIDTH + LANES:]
        dt = _softplus(proj[:, WIDTH + SSM_CONV_CH:].astype(f32) + vec("dt_bias"))
        da = dt * (-jnp.exp(vec("a_log")))
        yield
        cs = _running_sum(da, c)
        cs_t = cs.T
        dt_t = dt.T
        cs_end = cs[c - 1:c, :]
        dsk = vec("d")
        bgs = [bm[:, g * SSM_STATE:(g + 1) * SSM_STATE] for g in range(SSM_GROUPS)]
        cgs = [cm[:, g * SSM_STATE:(g + 1) * SSM_STATE] for g in range(SSM_GROUPS)]
        yield
        gmats = [_dot_nt(cgs[g], bgs[g]) for g in range(SSM_GROUPS)]
        cs_cols = [cs[:, h:h + 1] for h in heads]
        yield
        lmats = [gmats[h // per_group] * jnp.exp(jnp.where(causal, cs_cols[h] - cs_t[h:h + 1, :], -jnp.inf))
                 * dt_t[h:h + 1, :] for h in heads]
        xhs = [xs[:, sl] for sl in sls]
        hprevs = [hn[h] for h in heads]
        yield
        y_in = [_dot(lmats[h], xhs[h]) for h in heads]
        y_st = [_dot_nt(cgs[h // per_group], hprevs[h]) for h in heads]
        yield
        for h in heads:
            y_acc[:, sls[h]] = y_in[h] + y_st[h] * jnp.exp(cs_cols[h]) + xhs[h] * dsk[:, h:h + 1]
        ces = [cs_end[:, h:h + 1] for h in heads]
        upd = [_dot_tn(xhs[h] * (jnp.exp(ces[h] - cs_cols[h]) * dt[:, h:h + 1]), bgs[h // per_group],
                       transpose_on_mxu=True) for h in heads]
        yield
        for h in heads:
            hn[h] = hprevs[h] * jnp.exp(ces[h]) + upd[h]
        gy = y_acc[...] * _silu(proj[:, :WIDTH].astype(f32))
        nw = vec("norm_w")
        gw = WIDTH // SSM_GROUPS
        for g in range(SSM_GROUPS):
            part = gy[:, g * gw:(g + 1) * gw]
            ms = jnp.mean(part * part, axis=-1, keepdims=True)
            out[:, g * gw:(g + 1) * gw] = part * lax.rsqrt(ms + NORM_EPS) * nw[:, g * gw:(g + 1) * gw]

    _interleave(elem(bi) for bi in range(nb))


def _state_spec(shape, sidx, nb):
    zeros = (0,) * (len(shape) - 2)
    return pl.BlockSpec((1, nb) + tuple(shape[2:]), lambda i, j: (sidx, i) + zeros)


def _batch_spec(nb, *dims):
    zeros = (0,) * len(dims)
    return pl.BlockSpec((nb,) + dims, lambda i, j: (i,) + zeros)


def _ssm(proj, conv0, h0, sidx, P, layer, c, nb):
    b, l, wp = proj.shape
    return pl.pallas_call(
        functools.partial(_ssm_kernel, c=c, nb=nb),
        grid=(b // nb, l // c),
        in_specs=[pl.BlockSpec((nb, c, wp), lambda i, j: (i, j, 0)),
                  _state_spec(conv0.shape, sidx, nb), _state_spec(h0.shape, sidx, nb),
                  _layer_spec(P["ssm_conv_w"].shape, layer, 2), _layer_spec(P["ssm_vec"].shape, layer, 2)],
        out_specs=[pl.BlockSpec((nb, c, WIDTH), lambda i, j: (i, j, 0)),
                   _batch_spec(nb, N_HEADS, HEAD_DIM, SSM_STATE), _batch_spec(nb, CONV_W - 1, SSM_CONV_CH)],
        out_shape=[jax.ShapeDtypeStruct((b, l, WIDTH), f32),
                   jax.ShapeDtypeStruct((b, N_HEADS, HEAD_DIM, SSM_STATE), f32),
                   jax.ShapeDtypeStruct((b, CONV_W - 1, SSM_CONV_CH), f32)],
        scratch_shapes=[pltpu.VMEM((nb, c + 8, SSM_CONV_CH), f32), pltpu.VMEM((nb, c, WIDTH), f32)],
        compiler_params=_params("parallel", "arbitrary"),
        name="ssd_branch",
    )(proj, conv0, h0, P["ssm_conv_w"], P["ssm_vec"])


def _rwkv_chunk_kernel(proj_ref, shift0_ref, s0_ref, vec_ref, w2_ref, a2_ref,
                       out_ref, sn_ref, shiftn_ref, ext_scr, *, c, rows, nb):
    vec = _vec_reader(vec_ref, RWKV_VEC)

    @pl.when(pl.program_id(1) == 0)
    def _():
        for bi in range(nb):
            ext_scr[bi, 7:8, :] = shift0_ref[0, bi]
            sn_ref[bi] = s0_ref[0, bi]

    row2 = _iota2((c, 2 * c), 0)
    col2 = _iota2((c, 2 * c), 1)
    col2 = jnp.where(col2 >= c, col2 - c, col2)
    strict2 = row2 > col2
    incl2 = row2 >= col2
    n_double = (c - 1).bit_length()
    heads = range(N_HEADS)
    sls = [slice(h * HEAD_DIM, (h + 1) * HEAD_DIM) for h in heads]

    def elem(bi):
        proj, out, sn, ext = proj_ref.at[bi], out_ref.at[bi], sn_ref.at[bi], ext_scr.at[bi]
        u = proj[:, WIDTH:].astype(f32)
        ext[8:8 + rows, :] = u
        sh = u + (ext[7:7 + rows, :] - u) * vec("mu")
        last = ext[rows + 7:rows + 8, :]
        ext[7:8, :] = last
        shiftn_ref[bi] = last
        r_all = sh[:, :WIDTH]
        k = sh[:, WIDTH:2 * WIDTH]
        v_all = sh[:, 2 * WIDTH:3 * WIDTH]
        w_lo = sh[:, 3 * WIDTH:3 * WIDTH + RWKV_RANK]
        a_lo = sh[:, 3 * WIDTH + RWKV_RANK:]
        yield
        w_log = -_softplus(-(vec("w0") + _dot(jnp.tanh(w_lo), w2_ref[0]))) - 0.5
        lw_all = -jnp.exp(w_log)
        a_in_all = jax.nn.sigmoid(vec("a0") + _dot(a_lo, a2_ref[0]))
        kk_all = k * vec("k_k")
        k2_all = k * (1.0 + (a_in_all - 1.0) * vec("k_a"))
        rk, lnw, lnb = vec("r_k"), vec("ln_w"), vec("ln_b")
        yield
        for c0 in range(0, rows, c):
            cr = slice(c0, c0 + c)
            r, v, lw, a_in, kk, k2 = r_all[cr], v_all[cr], lw_all[cr], a_in_all[cr], kk_all[cr], k2_all[cr]
            cl = _running_sum(lw, c)
            cl_end = cl[c - 1:c, :]
            g_fwd = jnp.exp(cl)
            g_inv = jnp.exp(-cl)
            g_rem = jnp.exp(cl_end - cl)
            g_prev = jnp.exp(cl - lw)
            g_end = jnp.exp(cl_end)
            yield
            s0s = [sn[h] for h in heads]
            bhs, ars, bks = [], [], []
            for sl in sls:
                kkh = kk[:, sl]
                kkn = kkh / jnp.maximum(jnp.sqrt(jnp.sum(kkh * kkh, axis=-1, keepdims=True)), 1e-12)
                bhs.append(kkn * a_in[:, sl])
                ars.append(jnp.concatenate([-kkn * g_prev[:, sl], r[:, sl] * g_fwd[:, sl]], axis=0))
                bks.append(jnp.concatenate([bhs[-1] * g_inv[:, sl], k2[:, sl] * g_inv[:, sl]], axis=0))
            yield
            m4s = [_dot_nt(ars[h], bks[h]) for h in heads]
            ahs = [_dot_nt(ars[h], s0s[h]) for h in heads]
            yield
            tops = [jnp.where(strict2, m4[:c, :], 0.0) for m4 in m4s]
            bots = [jnp.where(incl2, m4[c:, :], 0.0) for m4 in m4s]
            amats = [top[:, :c] for top in tops]
            xs = [ahs[h][:c] + _dot(tops[h][:, c:], v[:, sls[h]]) for h in heads]
            yield
            for i in range(n_double):
                xs = [xs[h] + _dot(amats[h], xs[h]) for h in heads]
                if i + 1 < n_double:
                    amats = [_dot(amats[h], amats[h]) for h in heads]
                yield
            pvs = [jnp.concatenate([xs[h], v[:, sls[h]]], axis=0) for h in heads]
            ys = [ahs[h][c:] + _dot(bots[h], pvs[h]) for h in heads]
            upd = [_dot_tn(pvs[h], jnp.concatenate([bhs[h] * g_rem[:, sls[h]], k2[:, sls[h]] * g_rem[:, sls[h]]],
                                                   axis=0)) for h in heads]
            yield
            for h in heads:
                sl = sls[h]
                sn[h] = s0s[h] * g_end[:, sl] + upd[h]
                y = ys[h]
                mu = jnp.mean(y, axis=-1, keepdims=True)
                var = jnp.mean(jnp.square(y - mu), axis=-1, keepdims=True)
                ln = (y - mu) * lax.rsqrt(var + RWKV_LN_EPS) * lnw[:, sl] + lnb[:, sl]
                bonus = jnp.sum(r[:, sl] * k2[:, sl] * rk[:, sl], axis=-1, keepdims=True) * v[:, sl]
                out[cr, sl] = (ln + bonus) * _silu(proj[cr, sl].astype(f32))
            yield

    _interleave(elem(bi) for bi in range(nb))


def _rwkv(proj, shift0, s0, sidx, P, layer, c, rows, nb):
    b, l, wp = proj.shape
    return pl.pallas_call(
        functools.partial(_rwkv_chunk_kernel, c=c, rows=rows, nb=nb),
        grid=(b // nb, l // rows),
        in_specs=[pl.BlockSpec((nb, rows, wp), lambda i, j: (i, j, 0)),
                  _state_spec(shift0.shape, sidx, nb), _state_spec(s0.shape, sidx, nb),
                  _layer_spec(P["rwkv_vec"].shape, layer, 2),
                  _layer_spec(P["rwkv_w2"].shape, layer, 2), _layer_spec(P["rwkv_a2"].shape, layer, 2)],
        out_specs=[pl.BlockSpec((nb, rows, WIDTH), lambda i, j: (i, j, 0)),
                   _batch_spec(nb, N_HEADS, HEAD_DIM, HEAD_DIM), _batch_spec(nb, 1, RWKV_SHIFT_CH)],
        out_shape=[jax.ShapeDtypeStruct((b, l, WIDTH), f32),
                   jax.ShapeDtypeStruct((b, N_HEADS, HEAD_DIM, HEAD_DIM), f32),
                   jax.ShapeDtypeStruct((b, 1, RWKV_SHIFT_CH), f32)],
        scratch_shapes=[pltpu.VMEM((nb, rows + 8, RWKV_SHIFT_CH), f32)],
        compiler_params=_params("parallel", "arbitrary"),
        name="rwkv7_chunked",
    )(proj, shift0, s0, P["rwkv_vec"], P["rwkv_w2"], P["rwkv_a2"])


def _mlstm_kernel(proj_ref, conv0_ref, c0_ref, n0_ref, m0_ref, cw_ref, vec_ref,
                  out_ref, cn_ref, nn_ref, mn_ref, convn_ref, ext_scr, h_scr, *, c, nb):
    vec = _vec_reader(vec_ref, MLSTM_VEC)

    @pl.when(pl.program_id(1) == 0)
    def _():
        for bi in range(nb):
            ext_scr[bi, 5:8, :] = conv0_ref[0, bi]
            cn_ref[bi] = c0_ref[0, bi]
            nn_ref[bi] = n0_ref[0, bi]
            mn_ref[bi] = m0_ref[0, bi]

    causal = _iota2((c, c), 0) >= _iota2((c, c), 1)
    lane = _iota2((1, LANES), 1)
    heads = range(N_HEADS)
    sls = [slice(h * HEAD_DIM, (h + 1) * HEAD_DIM) for h in heads]

    def elem(bi):
        proj, out, cn, nn, h_acc = proj_ref.at[bi], out_ref.at[bi], cn_ref.at[bi], nn_ref.at[bi], h_scr.at[bi]
        u = proj[:, WIDTH:3 * WIDTH].astype(f32)
        acc, tail = _causal_conv(ext_scr.at[bi], u, c, cw_ref, vec("conv_b"))
        convn_ref[bi] = tail
        qk = _silu(acc)
        gates = proj[:, 5 * WIDTH:].astype(f32) + vec("gate_b")
        lf = -_softplus(-gates)
        yield
        bc = _running_sum(lf, c)
        bc_t = bc.T
        ig_t = gates.T
        m_prev = mn_ref[bi]
        yield
        b_cols = [bc[:, N_HEADS + h:N_HEADS + h + 1] for h in heads]
        dmats = [jnp.where(causal, b_cols[h] - bc_t[N_HEADS + h:N_HEADS + h + 1, :] + ig_t[h:h + 1, :], -jnp.inf)
                 for h in heads]
        m0s = [m_prev[:, h:h + 1] for h in heads]
        gcols = [b_cols[h] + m0s[h] for h in heads]
        mts = [jnp.maximum(gcols[h], jnp.max(dmats[h], axis=-1, keepdims=True)) for h in heads]
        yield
        qhs = [qk[:, sl] for sl in sls]
        khs = [qk[:, WIDTH + h * HEAD_DIM:WIDTH + (h + 1) * HEAD_DIM] * HEAD_DIM ** -0.5 for h in heads]
        vhs = [proj[:, 3 * WIDTH + h * HEAD_DIM:3 * WIDTH + (h + 1) * HEAD_DIM].astype(f32) for h in heads]
        cprevs = [cn[h] for h in heads]
        nprevs = [nn[h:h + 1, :] for h in heads]
        yield
        sws = [jnp.exp(dmats[h] - mts[h]) * _dot_nt(qhs[h], khs[h]) for h in heads]
        wis = [jnp.exp(gcols[h] - mts[h]) for h in heads]
        qcs = [_dot(qhs[h], cprevs[h]) for h in heads]
        yield
        nums = [_dot(sws[h], vhs[h]) + wis[h] * qcs[h] for h in heads]
        yield
        for h in heads:
            den = (jnp.sum(sws[h], axis=-1, keepdims=True)
                   + wis[h] * jnp.sum(qhs[h] * nprevs[h], axis=-1, keepdims=True))
            h_acc[:, sls[h]] = nums[h] / jnp.maximum(jnp.abs(den), jnp.exp(-mts[h]))
        m_news = [mts[h][c - 1:c, :] for h in heads]
        b_ends = [b_cols[h][c - 1:c, :] for h in heads]
        kws = [khs[h] * jnp.exp(b_ends[h] - b_cols[h] + gates[:, h:h + 1] - m_news[h]) for h in heads]
        yield
        kvs = [_dot_tn(kws[h], vhs[h]) for h in heads]
        yield
        m_out = jnp.zeros((1, LANES), f32)
        for h in heads:
            dec = jnp.exp(b_ends[h] + m0s[h] - m_news[h])
            cn[h] = dec * cprevs[h] + kvs[h]
            nn[h:h + 1, :] = dec * nprevs[h] + jnp.sum(kws[h], axis=0, keepdims=True)
            m_out = jnp.where(lane == h, m_news[h], m_out)
        mn_ref[bi] = m_out
        nw = vec("norm_w")
        for h in heads:
            sl = sls[h]
            o_gate = proj[:, 4 * WIDTH + h * HEAD_DIM:4 * WIDTH + (h + 1) * HEAD_DIM].astype(f32)
            hm = jax.nn.sigmoid(o_gate) * h_acc[:, sl]
            ms = jnp.mean(hm * hm, axis=-1, keepdims=True)
            out[:, sl] = hm * lax.rsqrt(ms + NORM_EPS) * nw[:, sl] * _silu(proj[:, sl].astype(f32))

    _interleave(elem(bi) for bi in range(nb))


def _mlstm(proj, conv0, c0, n0, m0, sidx, P, layer, c, nb):
    b, l, wp = proj.shape
    outs = pl.pallas_call(
        functools.partial(_mlstm_kernel, c=c, nb=nb),
        grid=(b // nb, l // c),
        in_specs=[pl.BlockSpec((nb, c, wp), lambda i, j: (i, j, 0)),
                  _state_spec(conv0.shape, sidx, nb), _state_spec(c0.shape, sidx, nb),
                  _state_spec(n0.shape, sidx, nb), _state_spec(m0.shape, sidx, nb),
                  _layer_spec(P["mlstm_conv_w"].shape, layer, 2), _layer_spec(P["mlstm_vec"].shape, layer, 2)],
        out_specs=[pl.BlockSpec((nb, c, WIDTH), lambda i, j: (i, j, 0)),
                   _batch_spec(nb, N_HEADS, HEAD_DIM, HEAD_DIM), _batch_spec(nb, N_HEADS, HEAD_DIM),
                   _batch_spec(nb, 1, LANES), _batch_spec(nb, CONV_W - 1, 2 * WIDTH)],
        out_shape=[jax.ShapeDtypeStruct((b, l, WIDTH), f32),
                   jax.ShapeDtypeStruct((b, N_HEADS, HEAD_DIM, HEAD_DIM), f32),
                   jax.ShapeDtypeStruct((b, N_HEADS, HEAD_DIM), f32),
                   jax.ShapeDtypeStruct((b, 1, LANES), f32),
                   jax.ShapeDtypeStruct((b, CONV_W - 1, 2 * WIDTH), f32)],
        scratch_shapes=[pltpu.VMEM((nb, c + 8, 2 * WIDTH), f32), pltpu.VMEM((nb, c, WIDTH), f32)],
        compiler_params=_params("parallel", "arbitrary"),
        name="mlstm_branch",
    )(proj, conv0, c0, n0, m0, P["mlstm_conv_w"], P["mlstm_vec"])
    out, cn, nn, mn, convn = outs
    return out, cn, nn, mn[:, 0, :N_HEADS], convn


def _head_mean_matrix():
    hid = jnp.arange(ATTN_W) // HEAD_DIM
    return ((hid[:, None] == hid[None, :]).astype(f32) / HEAD_DIM).astype(bf16)


def _qk_norm(x, hm_ref, w):
    sq = x * x
    hi = sq.astype(bf16)
    lo = (sq - hi.astype(f32)).astype(bf16)
    ms = _dot(hi, hm_ref[...]) + _dot(lo, hm_ref[...])
    return x * lax.rsqrt(ms + NORM_EPS) * w


def _attn_prompt_kernel(q_ref, k_ref, v_ref, z_ref, hm_ref, vec_ref, out_ref, kv_ref,
                        qn_scr, kn_scr, vv_scr, acc_scr, m_scr, d_scr, to_scr, tl_scr, *, l):
    vec = _vec_reader(vec_ref, ATTN_VEC)
    g = pl.program_id(1)
    rb = 256
    n_pairs = ATTN_W // LANES

    def norm_body(i, carry):
        rows = pl.ds(pl.multiple_of(i * rb, rb), rb)
        qn = _qk_norm(q_ref[0, rows, :].astype(f32), hm_ref, vec("q_norm"))
        kn = _qk_norm(k_ref[0, rows, :].astype(f32), hm_ref, vec("k_norm"))
        v = v_ref[0, rows, :].astype(f32)
        kv_ref[0, 0, rows, 0:ATTN_W] = kn
        kv_ref[0, 0, rows, ATTN_W:2 * ATTN_W] = v
        for p in range(n_pairs):
            qn_scr[p, rows, :] = qn[:, p * LANES:(p + 1) * LANES]
            kn_scr[p, rows, :] = kn[:, p * LANES:(p + 1) * LANES]
            vv_scr[p, rows, :] = v[:, p * LANES:(p + 1) * LANES]
        return carry

    lax.fori_loop(0, l // rb, norm_body, 0)

    qb = ATTN_J
    qi = _iota2((qb, qb), 0)
    ki = _iota2((qb, qb), 1)
    scale = HEAD_DIM ** -0.5

    def run_group(dil, first):
        n_iter = l // qb

        def body(i, carry):
            r = i % dil
            blk = i // dil
            u0 = blk * qb
            rows_c = pl.ds(u0 * dil + r, qb, stride=dil)
            rows_p = pl.ds(jnp.maximum(u0 - qb, 0) * dil + r, qb, stride=dil)
            mask_c = ki <= qi
            mask_p = (ki >= qi) & (blk > 0)
            hsl = [slice(hh * HEAD_DIM, (hh + 1) * HEAD_DIM) for hh in range(LANES // HEAD_DIM)]
            heads = [(p, sl) for p in range(n_pairs) for sl in hsl]
            qv = [qn_scr[p, rows_c, :] for p in range(n_pairs)]
            kc = [kn_scr[p, rows_c, :] for p in range(n_pairs)]
            vc = [vv_scr[p, rows_c, :] for p in range(n_pairs)]
            kp = [kn_scr[p, rows_p, :] for p in range(n_pairs)]
            vp = [vv_scr[p, rows_p, :] for p in range(n_pairs)]
            scs = [jnp.where(mask_c, _dot_nt(qv[p][:, sl], kc[p][:, sl]) * scale, -jnp.inf) for p, sl in heads]
            sps = [jnp.where(mask_p, _dot_nt(qv[p][:, sl], kp[p][:, sl]) * scale, -jnp.inf) for p, sl in heads]
            ms = [jnp.maximum(jnp.max(sc, axis=-1, keepdims=True), jnp.max(sp, axis=-1, keepdims=True))
                  for sc, sp in zip(scs, sps)]
            pcs = [jnp.exp(sc - m) for sc, m in zip(scs, ms)]
            pps = [jnp.exp(sp - m) for sp, m in zip(sps, ms)]
            dens = [jnp.sum(pc, axis=-1, keepdims=True) + jnp.sum(pp, axis=-1, keepdims=True)
                    for pc, pp in zip(pcs, pps)]
            for (p, sl), pc, pp, m, den in zip(heads, pcs, pps, ms, dens):
                to_scr[p, :, sl] = (_dot(pc, vc[p][:, sl]) + _dot(pp, vp[p][:, sl])) / den
                tl_scr[p, :, sl] = jnp.broadcast_to(m + jnp.log(den), (qb, HEAD_DIM))
            for p in range(n_pairs):
                o = to_scr[p]
                lse = tl_scr[p]
                if first:
                    acc_scr[p, rows_c, :] = o
                    m_scr[p, rows_c, :] = lse
                    d_scr[p, rows_c, :] = jnp.ones_like(lse)
                else:
                    m_old = m_scr[p, rows_c, :]
                    m_new = jnp.maximum(m_old, lse)
                    a_old = jnp.exp(m_old - m_new)
                    a_new = jnp.exp(lse - m_new)
                    acc_scr[p, rows_c, :] = acc_scr[p, rows_c, :] * a_old + o * a_new
                    d_scr[p, rows_c, :] = d_scr[p, rows_c, :] * a_old + a_new
                    m_scr[p, rows_c, :] = m_new
            return carry

        lax.fori_loop(0, n_iter, body, 0)

    for gi, (_, dil) in enumerate(ATTN_GROUPS):
        pl.when(g == gi)(functools.partial(run_group, dil, gi == 0))

    @pl.when(g == len(ATTN_GROUPS) - 1)
    def _():
        def fin_body(i, carry):
            rows = pl.ds(pl.multiple_of(i * rb, rb), rb)
            for p in range(n_pairs):
                lanes = slice(p * LANES, (p + 1) * LANES)
                z = z_ref[0, rows, lanes].astype(f32)
                out_ref[0, rows, lanes] = acc_scr[p, rows, :] / d_scr[p, rows, :] * _silu(z)
            return carry

        lax.fori_loop(0, l // rb, fin_body, 0)


def _attn_prompt(proj, P, layer):
    b, l, _ = proj.shape
    ng = len(ATTN_GROUPS)
    col = lambda base: pl.BlockSpec((1, l, ATTN_W), lambda i, g: (i, 0, base + g))
    pair_scr = pltpu.VMEM((ATTN_W // LANES, l, LANES), f32)
    return pl.pallas_call(
        functools.partial(_attn_prompt_kernel, l=l),
        grid=(b, ng),
        in_specs=[col(1), col(1 + ng), col(1 + 2 * ng),
                  pl.BlockSpec((1, l, ATTN_W), lambda i, g: (i, 0, 0)),
                  pl.BlockSpec((ATTN_W, ATTN_W), lambda i, g: (0, 0)),
                  _layer_spec(P["attn_vec"].shape, layer, 2)],
        out_specs=[pl.BlockSpec((1, l, ATTN_W), lambda i, g: (i, 0, 0)),
                   pl.BlockSpec((1, 1, l, 2 * ATTN_W), lambda i, g: (i, g, 0, 0))],
        out_shape=[jax.ShapeDtypeStruct((b, l, ATTN_W), f32),
                   jax.ShapeDtypeStruct((b, ng, l, 2 * ATTN_W), f32)],
        scratch_shapes=[pair_scr] * 6 + [pltpu.VMEM((ATTN_W // LANES, ATTN_J, LANES), f32)] * 2,
        compiler_params=_params("parallel", "arbitrary"),
        name="dilated_attn_prompt",
    )(proj, proj, proj, proj, P["head_mean"], P["attn_vec"])


def _attn_step_kernel(x_ref, c0_ref, c1_ref, c2_ref, hm_ref, vec_ref, out_ref, kv_ref, *, l, nb):
    vec = _vec_reader(vec_ref, ATTN_VEC)
    ng = len(ATTN_GROUPS)
    scale = HEAD_DIM ** -0.5
    c_refs = (c0_ref, c1_ref, c2_ref)
    hsl = [slice(h * HEAD_DIM, (h + 1) * HEAD_DIM) for h in range(ATTN_HEADS)]
    masks_p, masks_n = [], []
    for gi, (window, dil) in enumerate(ATTN_GROUPS):
        w = c_refs[gi].shape[-1]
        jmax = window // dil
        shift = dil.bit_length() - 1
        dist_p = w + _iota2((l, w), 0) - _iota2((l, w), 1)
        masks_p.append(((dist_p & (dil - 1)) == 0) & ((dist_p >> shift) <= jmax))
        dist_n = _iota2((l, l), 0) - _iota2((l, l), 1)
        masks_n.append((dist_n >= 0) & ((dist_n & (dil - 1)) == 0) & ((dist_n >> shift) <= jmax))
    gh = [(gi, h) for gi in range(ng) for h in range(ATTN_HEADS)]

    def elem(bi):
        x = x_ref[bi]
        qns, kns, vs = [], [], []
        for gi in range(ng):
            qns.append(_qk_norm(x[:, (1 + gi) * ATTN_W:(2 + gi) * ATTN_W], hm_ref, vec("q_norm")))
            kns.append(_qk_norm(x[:, (1 + ng + gi) * ATTN_W:(2 + ng + gi) * ATTN_W], hm_ref, vec("k_norm")))
            vs.append(x[:, (1 + 2 * ng + gi) * ATTN_W:(2 + 2 * ng + gi) * ATTN_W])
            kv_ref[bi, gi, :, 0:ATTN_W] = kns[gi]
            kv_ref[bi, gi, :, ATTN_W:2 * ATTN_W] = vs[gi]
        yield
        sps = [jnp.where(masks_p[gi], _dot(qns[gi][:, hsl[h]], c_refs[gi][0, bi, 0, h]) * scale, -jnp.inf)
               for gi, h in gh]
        sns = [jnp.where(masks_n[gi], _dot_nt(qns[gi][:, hsl[h]], kns[gi][:, hsl[h]]) * scale, -jnp.inf)
               for gi, h in gh]
        yield
        ms = [jnp.maximum(jnp.max(sp, axis=-1, keepdims=True), jnp.max(sn, axis=-1, keepdims=True))
              for sp, sn in zip(sps, sns)]
        pps = [jnp.exp(sp - m) for sp, m in zip(sps, ms)]
        pns = [jnp.exp(sn - m) for sn, m in zip(sns, ms)]
        dens = [jnp.sum(pp, axis=-1, keepdims=True) + jnp.sum(pn, axis=-1, keepdims=True)
                for pp, pn in zip(pps, pns)]
        yield
        og = [(_dot_nt(pp, c_refs[gi][0, bi, 1, h]) + _dot(pn, vs[gi][:, hsl[h]])) / den
              for (gi, h), pp, pn, den in zip(gh, pps, pns, dens)]
        lg = [m + jnp.log(den) for m, den in zip(ms, dens)]
        yield
        outs = [og[gi * ATTN_HEADS:(gi + 1) * ATTN_HEADS] for gi in range(ng)]
        lses = [lg[gi * ATTN_HEADS:(gi + 1) * ATTN_HEADS] for gi in range(ng)]
        for h in range(ATTN_HEADS):
            sl = hsl[h]
            mx = functools.reduce(jnp.maximum, [lses[gi][h] for gi in range(ng)])
            ws = [jnp.exp(lses[gi][h] - mx) for gi in range(ng)]
            tot = functools.reduce(lambda a, b: a + b, ws)
            o = functools.reduce(lambda a, b: a + b, [outs[gi][h] * (ws[gi] / tot) for gi in range(ng)])
            out_ref[bi, :, sl] = o * _silu(x[:, sl])

    _interleave(elem(bi) for bi in range(nb))


def _attn_step(proj, caches, P, layer, nb):
    b, l, wp = proj.shape
    ng = len(ATTN_GROUPS)
    cache_specs = [pl.BlockSpec((1, nb) + c.shape[2:], lambda i: (layer, i, 0, 0, 0, 0)) for c in caches]
    return pl.pallas_call(
        functools.partial(_attn_step_kernel, l=l, nb=nb),
        grid=(b // nb,),
        in_specs=[pl.BlockSpec((nb, l, wp), lambda i: (i, 0, 0))] + cache_specs
                 + [pl.BlockSpec((ATTN_W, ATTN_W), lambda i: (0, 0)), _layer_spec(P["attn_vec"].shape, layer, 1)],
        out_specs=[pl.BlockSpec((nb, l, ATTN_W), lambda i: (i, 0, 0)),
                   pl.BlockSpec((nb, ng, l, 2 * ATTN_W), lambda i: (i, 0, 0, 0))],
        out_shape=[jax.ShapeDtypeStruct((b, l, ATTN_W), f32),
                   jax.ShapeDtypeStruct((b, ng, l, 2 * ATTN_W), f32)],
        compiler_params=_params("parallel"),
        name="dilated_attn_step",
    )(proj, *caches, P["head_mean"], P["attn_vec"])


def _merge_kernel(oa_ref, ob_ref, oc_ref, od_ref, mg_ref, x_ref, gate_ref, wb_ref, wo_ref, o_ref):
    merged = None
    start = 0
    for bi, ref in enumerate((oa_ref, ob_ref, oc_ref, od_ref)):
        width = ref.shape[-1]
        t = jnp.dot(ref[0].astype(bf16), wb_ref[0, start:start + width, :], preferred_element_type=f32)
        term = jax.nn.sigmoid(mg_ref[0, :, bi * D_MODEL:(bi + 1) * D_MODEL].astype(f32)) * t
        merged = term if merged is None else merged + term
        start += width
    y = jnp.dot(merged.astype(bf16), wo_ref[0], preferred_element_type=f32)
    o_ref[0] = x_ref[0] + gate_ref[0] * y


def _merge(branches, mg3, x3, gate3, P, layer, tl):
    b, l, d = x3.shape
    per_row = gate3.shape[1] != 1
    rows = lambda w: pl.BlockSpec((1, tl, w), lambda i, j: (i, j, 0))
    gate_spec = pl.BlockSpec((1, tl if per_row else 1, d), (lambda i, j: (i, j, 0)) if per_row else (lambda i, j: (i, 0, 0)))
    return pl.pallas_call(
        _merge_kernel,
        grid=(b, l // tl),
        in_specs=[rows(o.shape[-1]) for o in branches] + [rows(N_BRANCH * d), rows(d), gate_spec,
                                                          _layer_spec(P["w_branch"].shape, layer, 2),
                                                          _layer_spec(P["w_out"].shape, layer, 2)],
        out_specs=rows(d),
        out_shape=jax.ShapeDtypeStruct((b, l, d), f32),
        compiler_params=_params("parallel", "parallel"),
        name="merge_out",
    )(*branches, mg3, x3, gate3, P["w_branch"], P["w_out"])


def _split_w_in(w_in):
    cols, start = {}, 0
    for name, size in IN_SEGMENTS:
        cols[name] = w_in[:, :, start:start + size]
        start += size
    zeros = lambda n: jnp.zeros(w_in.shape[:2] + (n,), w_in.dtype)
    groups = {
        "ssm": [cols["ssm_z"], cols["ssm_xbc"], cols["ssm_dt"], zeros(LANES - N_HEADS)],
        "rwkv": [cols["rwkv_z"], cols["rwkv_shift"]],
        "mlstm": [cols["mlstm_z"], cols["mlstm_qk"], cols["mlstm_v"], cols["mlstm_o"], cols["mlstm_if"],
                  zeros(LANES - 2 * N_HEADS)],
        "attn": [cols["attn_z"], cols["attn_qkv"]],
        "merge": [cols["merge"]],
    }
    return {k: jnp.concatenate(v, axis=2).astype(bf16) for k, v in groups.items()}


_PROJ_TN = {"ssm": 1408, "rwkv": 2176, "mlstm": 896, "attn": 1280, "merge": 1024}


def _layer(x, mod, st, sidx, caches, P, layer):
    b, l, d = x.shape
    prompt = caches is None
    shift, scale, gate = mod[:, :d], mod[:, d:2 * d], mod[:, 2 * d:]
    if prompt:
        x3, sc3, sh3, g3 = x, scale[:, None], shift[:, None], gate[:, None]
        tl, chunk, tl_merge, tm, proj_dtype = 512, 128, 256, 1024, bf16
        rwkv_rows, nb, nb_attn, nb_mlstm = 2 * RWKV_CHUNK, 1, 1, 1
    else:
        rep = lambda t: jnp.repeat(t, l, axis=0)[None]
        x3, sc3, sh3, g3 = x.reshape(1, b * l, d), rep(scale), rep(shift), rep(gate)
        tl = min(256, b * l)
        chunk, tl_merge, tm, proj_dtype = l, tl, min(512, b * l), f32
        rwkv_rows, nb, nb_attn = l, math.gcd(b, 4), math.gcd(b, 2)
        nb_mlstm = 1
    h2 = _norm(x3, P["norm_w"], layer, sc3, sh3, tl).reshape(b * l, d)
    proj = {k: _matmul(h2, P["w_in_" + k], layer, tm, _PROJ_TN[k], proj_dtype).reshape(b, l, -1) for k in _PROJ_TN}

    out_a, ssm_new, ssm_conv_new = _ssm(proj["ssm"], st["ssm_conv"], st["ssm"], sidx, P, layer, chunk, nb)
    out_b, rwkv_new, shift_new = _rwkv(proj["rwkv"], st["rwkv_shift"], st["rwkv"], sidx, P, layer,
                                       min(RWKV_CHUNK, l), rwkv_rows, nb)
    out_c, c_new, n_new, m_new, mconv_new = _mlstm(proj["mlstm"], st["mlstm_conv"], st["mlstm_c"], st["mlstm_n"],
                                                   st["mlstm_m"], sidx, P, layer, chunk, nb_mlstm)
    if prompt:
        out_d, kv = _attn_prompt(proj["attn"], P, layer)
        kv_new = [kv[:, gi, l - min(w, l):].reshape(b, min(w, l), 2, ATTN_HEADS, HEAD_DIM)
                  for gi, (w, _) in enumerate(ATTN_GROUPS)]
    else:
        out_d, kv = _attn_step(proj["attn"], caches, P, layer, nb_attn)
        kv_new = [kv[:, gi].reshape(b, l, 2, ATTN_HEADS, HEAD_DIM) for gi in range(len(ATTN_GROUPS))]

    branches = [o.reshape(x3.shape[0], x3.shape[1], -1) for o in (out_a, out_b, out_c, out_d)]
    mg3 = proj["merge"].reshape(x3.shape[0], x3.shape[1], -1)
    x_new = _merge(branches, mg3, x3, g3, P, layer, tl_merge).reshape(b, l, d)
    new_state = {"ssm": ssm_new, "ssm_conv": ssm_conv_new, "rwkv": rwkv_new, "rwkv_shift": shift_new[:, 0],
                 "mlstm_c": c_new, "mlstm_n": n_new, "mlstm_m": m_new, "mlstm_conv": mconv_new,
                 "kv_0": kv_new[0], "kv_1": kv_new[1], "kv_2": kv_new[2]}
    return x_new, new_state


_STATE_NAMES = ("ssm", "ssm_conv", "rwkv", "rwkv_shift", "mlstm_c", "mlstm_n", "mlstm_m", "mlstm_conv",
                "kv_0", "kv_1", "kv_2")


def _prepare_params(w):
    P = {"norm_w": w["norm_w"][:, None, :], "ada_w": w["ada_w"], "ada_b": w["ada_b"][:, None, :],
         "w_branch": w["w_branch"].astype(bf16), "w_out": w["w_out"].astype(bf16),
         "ssm_conv_w": w["ssm_conv_w"], "mlstm_conv_w": w["mlstm_conv_w"],
         "rwkv_w2": w["rwkv_w2"], "rwkv_a2": w["rwkv_a2"], "head_mean": _head_mean_matrix(),
         "ssm_vec": _pack_vecs(SSM_VEC, SSM_VEC_N, {"conv_b": w["ssm_conv_b"], "dt_bias": w["ssm_dt_bias"],
                                                    "a_log": w["ssm_a_log"], "d": w["ssm_d"],
                                                    "norm_w": w["ssm_norm_w"]}),
         "rwkv_vec": _pack_vecs(RWKV_VEC, RWKV_VEC_N, {"mu": w["rwkv_mu"], "w0": w["rwkv_w0"], "a0": w["rwkv_a0"],
                                                       "k_k": w["rwkv_k_k"], "k_a": w["rwkv_k_a"],
                                                       "r_k": w["rwkv_r_k"], "ln_w": w["rwkv_ln_w"],
                                                       "ln_b": w["rwkv_ln_b"]}),
         "mlstm_vec": _pack_vecs(MLSTM_VEC, MLSTM_VEC_N, {"conv_b": w["mlstm_conv_b"], "gate_b": w["mlstm_gate_b"],
                                                          "norm_w": w["mlstm_norm_w"]}),
         "attn_vec": _pack_vecs(ATTN_VEC, ATTN_VEC_N, {"q_norm": jnp.tile(w["attn_q_norm"], (1, ATTN_HEADS)),
                                                       "k_norm": jnp.tile(w["attn_k_norm"], (1, ATTN_HEADS))})}
    for name, mat in _split_w_in(w["w_in"]).items():
        P["w_in_" + name] = mat
    return P


def _pad_heads(m):
    return jnp.pad(m, [(0, 0)] * (m.ndim - 1) + [(0, LANES - N_HEADS)])[..., None, :]


def kernel(x_prompt, x_sample, c_prompt, c_sample, state_ssm, state_ssm_conv, state_rwkv, state_rwkv_shift, state_mlstm_c, state_mlstm_n, state_mlstm_m, state_mlstm_conv, cache_kv_w128, cache_kv_w512, cache_kv_w2048, norm_w, ada_w, ada_b, w_in, w_branch, w_out, ssm_conv_w, ssm_conv_b, ssm_dt_bias, ssm_a_log, ssm_d, ssm_norm_w, rwkv_mu, rwkv_w0, rwkv_w2, rwkv_a0, rwkv_a2, rwkv_k_k, rwkv_k_a, rwkv_r_k, rwkv_ln_w, rwkv_ln_b, mlstm_conv_w, mlstm_conv_b, mlstm_gate_b, mlstm_norm_w, attn_q_norm, attn_k_norm):
    P = _prepare_params(dict(
        norm_w=norm_w, ada_w=ada_w, ada_b=ada_b, w_in=w_in, w_branch=w_branch, w_out=w_out, ssm_conv_w=ssm_conv_w,
        ssm_conv_b=ssm_conv_b, ssm_dt_bias=ssm_dt_bias, ssm_a_log=ssm_a_log, ssm_d=ssm_d, ssm_norm_w=ssm_norm_w,
        rwkv_mu=rwkv_mu, rwkv_w0=rwkv_w0, rwkv_w2=rwkv_w2, rwkv_a0=rwkv_a0, rwkv_a2=rwkv_a2, rwkv_k_k=rwkv_k_k,
        rwkv_k_a=rwkv_k_a, rwkv_r_k=rwkv_r_k, rwkv_ln_w=rwkv_ln_w, rwkv_ln_b=rwkv_ln_b, mlstm_conv_w=mlstm_conv_w,
        mlstm_conv_b=mlstm_conv_b, mlstm_gate_b=mlstm_gate_b, mlstm_norm_w=mlstm_norm_w, attn_q_norm=attn_q_norm,
        attn_k_norm=attn_k_norm))
    bp = x_prompt.shape[0]
    fresh = {"ssm": jnp.zeros((1, bp, N_HEADS, HEAD_DIM, SSM_STATE), f32),
             "ssm_conv": jnp.zeros((1, bp, CONV_W - 1, SSM_CONV_CH), f32),
             "rwkv": jnp.zeros((1, bp, N_HEADS, HEAD_DIM, HEAD_DIM), f32),
             "rwkv_shift": jnp.zeros((1, bp, 1, RWKV_SHIFT_CH), f32),
             "mlstm_c": jnp.zeros((1, bp, N_HEADS, HEAD_DIM, HEAD_DIM), f32),
             "mlstm_n": jnp.zeros((1, bp, N_HEADS, HEAD_DIM), f32),
             "mlstm_m": jnp.zeros((1, bp, 1, LANES), f32),
             "mlstm_conv": jnp.zeros((1, bp, CONV_W - 1, 2 * WIDTH), f32)}
    carried = {"ssm": state_ssm, "ssm_conv": state_ssm_conv, "rwkv": state_rwkv,
               "rwkv_shift": state_rwkv_shift[:, :, None, :], "mlstm_c": state_mlstm_c, "mlstm_n": state_mlstm_n,
               "mlstm_m": _pad_heads(state_mlstm_m), "mlstm_conv": state_mlstm_conv}
    c_all = jnp.concatenate([c_prompt, c_sample], axis=0)
    caches_t = [jnp.transpose(c, (0, 1, 3, 4, 5, 2)) for c in (cache_kv_w128, cache_kv_w512, cache_kv_w2048)]
    y_prompt, y_sample = x_prompt, x_sample
    prompt_states, sample_states = [], []
    for layer in range(DEPTH):
        mod = _ada(c_all, P["ada_w"], P["ada_b"], layer)
        y_prompt, sp = _layer(y_prompt, mod[:bp], fresh, 0, None, P, layer)
        y_sample, ss = _layer(y_sample, mod[bp:], carried, layer, caches_t, P, layer)
        prompt_states.append(sp)
        sample_states.append(ss)
    stack = lambda states, name: jnp.stack([s[name] for s in states])
    return ((y_prompt, y_sample)
            + tuple(stack(prompt_states, n) for n in _STATE_NAMES)
            + tuple(stack(sample_states, n) for n in _STATE_NAMES))
```

```python
import functools
import math

import jax
import jax.numpy as jnp
from jax import lax
from jax.experimental import pallas as pl
from jax.experimental.pallas import tpu as pltpu

f32 = jnp.float32
bf16 = jnp.bfloat16
HI = lax.Precision.HIGHEST

D_MODEL = 1024
DEPTH = 2
HEAD_DIM = 64
NORM_EPS = 1e-6
CONV_W = 4
N_HEADS = 8
WIDTH = N_HEADS * HEAD_DIM
SSM_GROUPS = 2
SSM_STATE = 64
SSM_CONV_CH = WIDTH + 2 * SSM_GROUPS * SSM_STATE
RWKV_RANK = 64
RWKV_SHIFT_CH = 3 * WIDTH + 2 * RWKV_RANK
RWKV_LN_EPS = 64e-5
RWKV_CHUNK = 64
ATTN_GROUPS = ((128, 1), (512, 4), (2048, 16))
ATTN_HEADS = 4
ATTN_W = ATTN_HEADS * HEAD_DIM
ATTN_J = 128
N_BRANCH = 4
LANES = 128
VMEM_LIMIT = 52 * 1024 * 1024

IN_SEGMENTS = (
    ("ssm_z", WIDTH), ("ssm_xbc", SSM_CONV_CH), ("ssm_dt", N_HEADS),
    ("rwkv_z", WIDTH), ("rwkv_shift", RWKV_SHIFT_CH),
    ("mlstm_z", WIDTH), ("mlstm_qk", 2 * WIDTH), ("mlstm_v", WIDTH),
    ("mlstm_o", WIDTH), ("mlstm_if", 2 * N_HEADS),
    ("attn_z", ATTN_W), ("attn_qkv", 9 * ATTN_W),
    ("merge", N_BRANCH * D_MODEL),
)

NN = (((1,), (0,)), ((), ()))
NT = (((1,), (1,)), ((), ()))
TN = (((0,), (0,)), ((), ()))


def _mm(a, b, dims, prec):
    if prec is None:
        a, b = a.astype(bf16), b.astype(bf16)
    return lax.dot_general(a, b, dims, preferred_element_type=f32, precision=prec)


def _dot(a, b, prec=None):
    return _mm(a, b, NN, prec)


def _dot_nt(a, b, prec=None):
    return _mm(a, b, NT, prec)


def _dot_tn(a, b, transpose_on_mxu=False):
    if not transpose_on_mxu:
        return _mm(a, b, TN, None)
    n = a.shape[1]
    eye = (_iota2((n, n), 0) == _iota2((n, n), 1)).astype(bf16)
    a_t = lax.dot_general(eye, a.astype(bf16), NT, preferred_element_type=f32).astype(bf16)
    return lax.dot_general(a_t, b.astype(bf16), NN, preferred_element_type=f32)


def _silu(x):
    return x * jax.nn.sigmoid(x)


def _softplus(x):
    return jnp.maximum(x, 0.0) + jnp.log1p(jnp.exp(-jnp.abs(x)))


def _params(*sem):
    return pltpu.CompilerParams(dimension_semantics=sem, vmem_limit_bytes=VMEM_LIMIT)


def _iota2(shape, dim):
    return lax.broadcasted_iota(jnp.int32, shape, dim)


def _running_sum(x, c):
    tri = (_iota2((c, c), 0) >= _iota2((c, c), 1)).astype(f32)
    return _dot(tri, x, HI)


def _vec_layout(fields):
    out, off = {}, 0
    for name, size in fields:
        padded = -(-size // LANES) * LANES
        out[name] = (off, padded)
        off += padded
    return out, off


def _pack_vecs(layout, total, vecs):
    parts = []
    for name, (_, padded) in layout.items():
        v = vecs[name].reshape(DEPTH, -1)
        parts.append(jnp.pad(v, ((0, 0), (0, padded - v.shape[1]))))
    packed = jnp.concatenate(parts, axis=1)
    assert packed.shape[1] == total
    return packed[:, None, :]


def _vec_reader(ref, layout):
    return lambda name: ref[0, :, layout[name][0]:layout[name][0] + layout[name][1]]


SSM_VEC, SSM_VEC_N = _vec_layout((("conv_b", SSM_CONV_CH), ("dt_bias", N_HEADS), ("a_log", N_HEADS),
                                  ("d", N_HEADS), ("norm_w", WIDTH)))
RWKV_VEC, RWKV_VEC_N = _vec_layout((("mu", RWKV_SHIFT_CH), ("w0", WIDTH), ("a0", WIDTH), ("k_k", WIDTH),
                                    ("k_a", WIDTH), ("r_k", WIDTH), ("ln_w", WIDTH), ("ln_b", WIDTH)))
MLSTM_VEC, MLSTM_VEC_N = _vec_layout((("conv_b", 2 * WIDTH), ("gate_b", 2 * N_HEADS), ("norm_w", WIDTH)))
ATTN_VEC, ATTN_VEC_N = _vec_layout((("q_norm", ATTN_W), ("k_norm", ATTN_W)))


def _layer_spec(shape, layer, grid_rank):
    zeros = (0,) * (len(shape) - 1)
    if grid_rank == 1:
        return pl.BlockSpec((1,) + tuple(shape[1:]), lambda i: (layer,) + zeros)
    return pl.BlockSpec((1,) + tuple(shape[1:]), lambda i, j: (layer,) + zeros)


def _ada_kernel(c_ref, w_ref, b_ref, o_ref):
    o_ref[...] = _dot(_silu(c_ref[...]), w_ref[0], HI) + b_ref[0]


def _ada(c, w, b3, layer):
    n = c.shape[0]
    return pl.pallas_call(
        _ada_kernel,
        grid=(3,),
        in_specs=[pl.BlockSpec((n, D_MODEL), lambda j: (0, 0)),
                  pl.BlockSpec((1, D_MODEL, D_MODEL), lambda j: (layer, 0, j)),
                  pl.BlockSpec((1, 1, D_MODEL), lambda j: (layer, 0, j))],
        out_specs=pl.BlockSpec((n, D_MODEL), lambda j: (0, j)),
        out_shape=jax.ShapeDtypeStruct((n, 3 * D_MODEL), f32),
        compiler_params=_params("parallel"),
        name="ada_mod",
    )(c, w, b3)


def _norm_kernel(x_ref, nw_ref, sc_ref, sh_ref, o_ref):
    x = x_ref[0]
    r = x * lax.rsqrt(jnp.mean(x * x, axis=-1, keepdims=True) + NORM_EPS)
    o_ref[0] = (r * nw_ref[0] * (1.0 + sc_ref[0]) + sh_ref[0]).astype(o_ref.dtype)


def _norm(x3, nw3, layer, sc3, sh3, tl):
    b, l, d = x3.shape
    per_row = sc3.shape[1] != 1
    mod_spec = pl.BlockSpec((1, tl if per_row else 1, d), (lambda i, j: (i, j, 0)) if per_row else (lambda i, j: (i, 0, 0)))
    return pl.pallas_call(
        _norm_kernel,
        grid=(b, l // tl),
        in_specs=[pl.BlockSpec((1, tl, d), lambda i, j: (i, j, 0)),
                  _layer_spec(nw3.shape, layer, 2), mod_spec, mod_spec],
        out_specs=pl.BlockSpec((1, tl, d), lambda i, j: (i, j, 0)),
        out_shape=jax.ShapeDtypeStruct((b, l, d), bf16),
        compiler_params=_params("parallel", "parallel"),
        name="mod_rmsnorm",
    )(x3, nw3, sc3, sh3)


def _mm_kernel(x_ref, w_ref, o_ref):
    o_ref[...] = jnp.dot(x_ref[...], w_ref[0], preferred_element_type=f32).astype(o_ref.dtype)


def _matmul(x, w3, layer, tm, tn, out_dtype):
    n, k = x.shape
    m = w3.shape[2]
    return pl.pallas_call(
        _mm_kernel,
        grid=(m // tn, n // tm),
        in_specs=[pl.BlockSpec((tm, k), lambda j, i: (i, 0)),
                  pl.BlockSpec((1, k, tn), lambda j, i: (layer, 0, j))],
        out_specs=pl.BlockSpec((tm, tn), lambda j, i: (i, j)),
        out_shape=jax.ShapeDtypeStruct((n, m), out_dtype),
        compiler_params=_params("parallel", "parallel"),
        name="in_proj",
    )(x, w3)


def _causal_conv(ext_scr, u, c, cw_ref, bias):
    ext_scr[8:8 + c, :] = u
    acc = bias
    for i in range(CONV_W):
        acc = acc + ext_scr[5 + i:5 + i + c, :] * cw_ref[0, i:i + 1, :]
    tail = ext_scr[c + 5:c + 8, :]
    ext_scr[5:8, :] = tail
    return acc, tail


def _interleave(gens):
    gens = list(gens)
    while gens:
        alive = []
        for g in gens:
            try:
                next(g)
                alive.append(g)
            except StopIteration:
                pass
        gens = alive


def _ssm_kernel(proj_ref, conv0_ref, h0_ref, cw_ref, vec_ref, out_ref, hn_ref, convn_ref, ext_scr, y_scr, *, c, nb):
    vec = _vec_reader(vec_ref, SSM_VEC)

    @pl.when(pl.program_id(1) == 0)
    def _():
        for bi in range(nb):
            ext_scr[bi, 5:8, :] = conv0_ref[0, bi]
            hn_ref[bi] = h0_ref[0, bi]

    causal = _iota2((c, c), 0) >= _iota2((c, c), 1)
    heads = range(N_HEADS)
    per_group = N_HEADS // SSM_GROUPS
    sls = [slice(h * HEAD_DIM, (h + 1) * HEAD_DIM) for h in heads]

    def elem(bi):
        proj, out, hn, y_acc = proj_ref.at[bi], out_ref.at[bi], hn_ref.at[bi], y_scr.at[bi]
        u = proj[:, WIDTH:WIDTH + SSM_CONV_CH].astype(f32)
        acc, tail = _causal_conv(ext_scr.at[bi], u, c, cw_ref, vec("conv_b"))
        convn_ref[bi] = tail
        xbc = _silu(acc)
        xs = xbc[:, :WIDTH]
        bm = xbc[:, WIDTH:WIDTH + LANES]
        cm = xbc[:, WIDTH + LANES:]
        dt = _softplus(proj[:, WIDTH + SSM_CONV_CH:].astype(f32) + vec("dt_bias"))
        da = dt * (-jnp.exp(vec("a_log")))
        yield
        cs = _running_sum(da, c)
        cs_t = cs.T
        dt_t = dt.T
        cs_end = cs[c - 1:c, :]
        dsk = vec("d")
        bgs = [bm[:, g * SSM_STATE:(g + 1) * SSM_STATE] for g in range(SSM_GROUPS)]
        cgs = [cm[:, g * SSM_STATE:(g + 1) * SSM_STATE] for g in range(SSM_GROUPS)]
        yield
        gmats = [_dot_nt(cgs[g], bgs[g]) for g in range(SSM_GROUPS)]
        cs_cols = [cs[:, h:h + 1] for h in heads]
        yield
        lmats = [gmats[h // per_group] * jnp.exp(jnp.where(causal, cs_cols[h] - cs_t[h:h + 1, :], -jnp.inf))
                 * dt_t[h:h + 1, :] for h in heads]
        xhs = [xs[:, sl] for sl in sls]
        hprevs = [hn[h] for h in heads]
        yield
        y_in = [_dot(lmats[h], xhs[h]) for h in heads]
        y_st = [_dot_nt(cgs[h // per_group], hprevs[h]) for h in heads]
        yield
        for h in heads:
            y_acc[:, sls[h]] = y_in[h] + y_st[h] * jnp.exp(cs_cols[h]) + xhs[h] * dsk[:, h:h + 1]
        ces = [cs_end[:, h:h + 1] for h in heads]
        upd = [_dot_tn(xhs[h] * (jnp.exp(ces[h] - cs_cols[h]) * dt[:, h:h + 1]), bgs[h // per_group],
                       transpose_on_mxu=True) for h in heads]
        yield
        for h in heads:
            hn[h] = hprevs[h] * jnp.exp(ces[h]) + upd[h]
        gy = y_acc[...] * _silu(proj[:, :WIDTH].astype(f32))
        nw = vec("norm_w")
        gw = WIDTH // SSM_GROUPS
        for g in range(SSM_GROUPS):
            part = gy[:, g * gw:(g + 1) * gw]
            ms = jnp.mean(part * part, axis=-1, keepdims=True)
            out[:, g * gw:(g + 1) * gw] = part * lax.rsqrt(ms + NORM_EPS) * nw[:, g * gw:(g + 1) * gw]

    _interleave(elem(bi) for bi in range(nb))


def _state_spec(shape, sidx, nb):
    zeros = (0,) * (len(shape) - 2)
    return pl.BlockSpec((1, nb) + tuple(shape[2:]), lambda i, j: (sidx, i) + zeros)


def _batch_spec(nb, *dims):
    zeros = (0,) * len(dims)
    return pl.BlockSpec((nb,) + dims, lambda i, j: (i,) + zeros)


def _ssm(proj, conv0, h0, sidx, P, layer, c, nb):
    b, l, wp = proj.shape
    return pl.pallas_call(
        functools.partial(_ssm_kernel, c=c, nb=nb),
        grid=(b // nb, l // c),
        in_specs=[pl.BlockSpec((nb, c, wp), lambda i, j: (i, j, 0)),
                  _state_spec(conv0.shape, sidx, nb), _state_spec(h0.shape, sidx, nb),
                  _layer_spec(P["ssm_conv_w"].shape, layer, 2), _layer_spec(P["ssm_vec"].shape, layer, 2)],
        out_specs=[pl.BlockSpec((nb, c, WIDTH), lambda i, j: (i, j, 0)),
                   _batch_spec(nb, N_HEADS, HEAD_DIM, SSM_STATE), _batch_spec(nb, CONV_W - 1, SSM_CONV_CH)],
        out_shape=[jax.ShapeDtypeStruct((b, l, WIDTH), f32),
                   jax.ShapeDtypeStruct((b, N_HEADS, HEAD_DIM, SSM_STATE), f32),
                   jax.ShapeDtypeStruct((b, CONV_W - 1, SSM_CONV_CH), f32)],
        scratch_shapes=[pltpu.VMEM((nb, c + 8, SSM_CONV_CH), f32), pltpu.VMEM((nb, c, WIDTH), f32)],
        compiler_params=_params("parallel", "arbitrary"),
        name="ssd_branch",
    )(proj, conv0, h0, P["ssm_conv_w"], P["ssm_vec"])


def _rwkv_chunk_kernel(proj_ref, shift0_ref, s0_ref, vec_ref, w2_ref, a2_ref,
                       out_ref, sn_ref, shiftn_ref, ext_scr, *, c, rows, nb):
    vec = _vec_reader(vec_ref, RWKV_VEC)

    @pl.when(pl.program_id(1) == 0)
    def _():
        for bi in range(nb):
            ext_scr[bi, 7:8, :] = shift0_ref[0, bi]
            sn_ref[bi] = s0_ref[0, bi]

    row2 = _iota2((c, 2 * c), 0)
    col2 = _iota2((c, 2 * c), 1)
    col2 = jnp.where(col2 >= c, col2 - c, col2)
    strict2 = row2 > col2
    incl2 = row2 >= col2
    n_double = (c - 1).bit_length()
    heads = range(N_HEADS)
    sls = [slice(h * HEAD_DIM, (h + 1) * HEAD_DIM) for h in heads]

    def elem(bi):
        proj, out, sn, ext = proj_ref.at[bi], out_ref.at[bi], sn_ref.at[bi], ext_scr.at[bi]
        u = proj[:, WIDTH:].astype(f32)
        ext[8:8 + rows, :] = u
        sh = u + (ext[7:7 + rows, :] - u) * vec("mu")
        last = ext[rows + 7:rows + 8, :]
        ext[7:8, :] = last
        shiftn_ref[bi] = last
        r_all = sh[:, :WIDTH]
        k = sh[:, WIDTH:2 * WIDTH]
        v_all = sh[:, 2 * WIDTH:3 * WIDTH]
        w_lo = sh[:, 3 * WIDTH:3 * WIDTH + RWKV_RANK]
        a_lo = sh[:, 3 * WIDTH + RWKV_RANK:]
        yield
        w_log = -_softplus(-(vec("w0") + _dot(jnp.tanh(w_lo), w2_ref[0]))) - 0.5
        lw_all = -jnp.exp(w_log)
        a_in_all = jax.nn.sigmoid(vec("a0") + _dot(a_lo, a2_ref[0]))
        kk_all = k * vec("k_k")
        k2_all = k * (1.0 + (a_in_all - 1.0) * vec("k_a"))
        rk, lnw, lnb = vec("r_k"), vec("ln_w"), vec("ln_b")
        yield
        for c0 in range(0, rows, c):
            cr = slice(c0, c0 + c)
            r, v, lw, a_in, kk, k2 = r_all[cr], v_all[cr], lw_all[cr], a_in_all[cr], kk_all[cr], k2_all[cr]
            cl = _running_sum(lw, c)
            cl_end = cl[c - 1:c, :]
            g_fwd = jnp.exp(cl)
            g_inv = jnp.exp(-cl)
            g_rem = jnp.exp(cl_end - cl)
            g_prev = jnp.exp(cl - lw)
            g_end = jnp.exp(cl_end)
            yield
            s0s = [sn[h] for h in heads]
            bhs, ars, bks = [], [], []
            for sl in sls:
                kkh = kk[:, sl]
                kkn = kkh / jnp.maximum(jnp.sqrt(jnp.sum(kkh * kkh, axis=-1, keepdims=True)), 1e-12)
                bhs.append(kkn * a_in[:, sl])
                ars.append(jnp.concatenate([-kkn * g_prev[:, sl], r[:, sl] * g_fwd[:, sl]], axis=0))
                bks.append(jnp.concatenate([bhs[-1] * g_inv[:, sl], k2[:, sl] * g_inv[:, sl]], axis=0))
            yield
            m4s = [_dot_nt(ars[h], bks[h]) for h in heads]
            ahs = [_dot_nt(ars[h], s0s[h]) for h in heads]
            yield
            tops = [jnp.where(strict2, m4[:c, :], 0.0) for m4 in m4s]
            bots = [jnp.where(incl2, m4[c:, :], 0.0) for m4 in m4s]
            amats = [top[:, :c] for top in tops]
            xs = [ahs[h][:c] + _dot(tops[h][:, c:], v[:, sls[h]]) for h in heads]
            yield
            for i in range(n_double):
                xs = [xs[h] + _dot(amats[h], xs[h]) for h in heads]
                if i + 1 < n_double:
                    amats = [_dot(amats[h], amats[h]) for h in heads]
                yield
            pvs = [jnp.concatenate([xs[h], v[:, sls[h]]], axis=0) for h in heads]
            ys = [ahs[h][c:] + _dot(bots[h], pvs[h]) for h in heads]
            upd = [_dot_tn(pvs[h], jnp.concatenate([bhs[h] * g_rem[:, sls[h]], k2[:, sls[h]] * g_rem[:, sls[h]]],
                                                   axis=0)) for h in heads]
            yield
            for h in heads:
                sl = sls[h]
                sn[h] = s0s[h] * g_end[:, sl] + upd[h]
                y = ys[h]
                mu = jnp.mean(y, axis=-1, keepdims=True)
                var = jnp.mean(jnp.square(y - mu), axis=-1, keepdims=True)
                ln = (y - mu) * lax.rsqrt(var + RWKV_LN_EPS) * lnw[:, sl] + lnb[:, sl]
                bonus = jnp.sum(r[:, sl] * k2[:, sl] * rk[:, sl], axis=-1, keepdims=True) * v[:, sl]
                out[cr, sl] = (ln + bonus) * _silu(proj[cr, sl].astype(f32))
            yield

    _interleave(elem(bi) for bi in range(nb))


def _rwkv(proj, shift0, s0, sidx, P, layer, c, rows, nb):
    b, l, wp = proj.shape
    return pl.pallas_call(
        functools.partial(_rwkv_chunk_kernel, c=c, rows=rows, nb=nb),
        grid=(b // nb, l // rows),
        in_specs=[pl.BlockSpec((nb, rows, wp), lambda i, j: (i, j, 0)),
                  _state_spec(shift0.shape, sidx, nb), _state_spec(s0.shape, sidx, nb),
                  _layer_spec(P["rwkv_vec"].shape, layer, 2),
                  _layer_spec(P["rwkv_w2"].shape, layer, 2), _layer_spec(P["rwkv_a2"].shape, layer, 2)],
        out_specs=[pl.BlockSpec((nb, rows, WIDTH), lambda i, j: (i, j, 0)),
                   _batch_spec(nb, N_HEADS, HEAD_DIM, HEAD_DIM), _batch_spec(nb, 1, RWKV_SHIFT_CH)],
        out_shape=[jax.ShapeDtypeStruct((b, l, WIDTH), f32),
                   jax.ShapeDtypeStruct((b, N_HEADS, HEAD_DIM, HEAD_DIM), f32),
                   jax.ShapeDtypeStruct((b, 1, RWKV_SHIFT_CH), f32)],
        scratch_shapes=[pltpu.VMEM((nb, rows + 8, RWKV_SHIFT_CH), f32)],
        compiler_params=_params("parallel", "arbitrary"),
        name="rwkv7_chunked",
    )(proj, shift0, s0, P["rwkv_vec"], P["rwkv_w2"], P["rwkv_a2"])


def _mlstm_kernel(proj_ref, conv0_ref, c0_ref, n0_ref, m0_ref, cw_ref, vec_ref,
                  out_ref, cn_ref, nn_ref, mn_ref, convn_ref, ext_scr, h_scr, *, c, nb):
    vec = _vec_reader(vec_ref, MLSTM_VEC)

    @pl.when(pl.program_id(1) == 0)
    def _():
        for bi in range(nb):
            ext_scr[bi, 5:8, :] = conv0_ref[0, bi]
            cn_ref[bi] = c0_ref[0, bi]
            nn_ref[bi] = n0_ref[0, bi]
            mn_ref[bi] = m0_ref[0, bi]

    causal = _iota2((c, c), 0) >= _iota2((c, c), 1)
    lane = _iota2((1, LANES), 1)
    heads = range(N_HEADS)
    sls = [slice(h * HEAD_DIM, (h + 1) * HEAD_DIM) for h in heads]

    def elem(bi):
        proj, out, cn, nn, h_acc = proj_ref.at[bi], out_ref.at[bi], cn_ref.at[bi], nn_ref.at[bi], h_scr.at[bi]
        u = proj[:, WIDTH:3 * WIDTH].astype(f32)
        acc, tail = _causal_conv(ext_scr.at[bi], u, c, cw_ref, vec("conv_b"))
        convn_ref[bi] = tail
        qk = _silu(acc)
        gates = proj[:, 5 * WIDTH:].astype(f32) + vec("gate_b")
        lf = -_softplus(-gates)
        yield
        bc = _running_sum(lf, c)
        bc_t = bc.T
        ig_t = gates.T
        m_prev = mn_ref[bi]
        yield
        b_cols = [bc[:, N_HEADS + h:N_HEADS + h + 1] for h in heads]
        dmats = [jnp.where(causal, b_cols[h] - bc_t[N_HEADS + h:N_HEADS + h + 1, :] + ig_t[h:h + 1, :], -jnp.inf)
                 for h in heads]
        m0s = [m_prev[:, h:h + 1] for h in heads]
        gcols = [b_cols[h] + m0s[h] for h in heads]
        mts = [jnp.maximum(gcols[h], jnp.max(dmats[h], axis=-1, keepdims=True)) for h in heads]
        yield
        qhs = [qk[:, sl] for sl in sls]
        khs = [qk[:, WIDTH + h * HEAD_DIM:WIDTH + (h + 1) * HEAD_DIM] * HEAD_DIM ** -0.5 for h in heads]
        vhs = [proj[:, 3 * WIDTH + h * HEAD_DIM:3 * WIDTH + (h + 1) * HEAD_DIM].astype(f32) for h in heads]
        cprevs = [cn[h] for h in heads]
        nprevs = [nn[h:h + 1, :] for h in heads]
        yield
        sws = [jnp.exp(dmats[h] - mts[h]) * _dot_nt(qhs[h], khs[h]) for h in heads]
        wis = [jnp.exp(gcols[h] - mts[h]) for h in heads]
        qcs = [_dot(qhs[h], cprevs[h]) for h in heads]
        yield
        nums = [_dot(sws[h], vhs[h]) + wis[h] * qcs[h] for h in heads]
        yield
        for h in heads:
            den = (jnp.sum(sws[h], axis=-1, keepdims=True)
                   + wis[h] * jnp.sum(qhs[h] * nprevs[h], axis=-1, keepdims=True))
            h_acc[:, sls[h]] = nums[h] / jnp.maximum(jnp.abs(den), jnp.exp(-mts[h]))
        m_news = [mts[h][c - 1:c, :] for h in heads]
        b_ends = [b_cols[h][c - 1:c, :] for h in heads]
        kws = [khs[h] * jnp.exp(b_ends[h] - b_cols[h] + gates[:, h:h + 1] - m_news[h]) for h in heads]
        yield
        kvs = [_dot_tn(kws[h], vhs[h]) for h in heads]
        yield
        m_out = jnp.zeros((1, LANES), f32)
        for h in heads:
            dec = jnp.exp(b_ends[h] + m0s[h] - m_news[h])
            cn[h] = dec * cprevs[h] + kvs[h]
            nn[h:h + 1, :] = dec * nprevs[h] + jnp.sum(kws[h], axis=0, keepdims=True)
            m_out = jnp.where(lane == h, m_news[h], m_out)
        mn_ref[bi] = m_out
        nw = vec("norm_w")
        for h in heads:
            sl = sls[h]
            o_gate = proj[:, 4 * WIDTH + h * HEAD_DIM:4 * WIDTH + (h + 1) * HEAD_DIM].astype(f32)
            hm = jax.nn.sigmoid(o_gate) * h_acc[:, sl]
            ms = jnp.mean(hm * hm, axis=-1, keepdims=True)
            out[:, sl] = hm * lax.rsqrt(ms + NORM_EPS) * nw[:, sl] * _silu(proj[:, sl].astype(f32))

    _interleave(elem(bi) for bi in range(nb))


def _mlstm(proj, conv0, c0, n0, m0, sidx, P, layer, c, nb):
    b, l, wp = proj.shape
    outs = pl.pallas_call(
        functools.partial(_mlstm_kernel, c=c, nb=nb),
        grid=(b // nb, l // c),
        in_specs=[pl.BlockSpec((nb, c, wp), lambda i, j: (i, j, 0)),
                  _state_spec(conv0.shape, sidx, nb), _state_spec(c0.shape, sidx, nb),
                  _state_spec(n0.shape, sidx, nb), _state_spec(m0.shape, sidx, nb),
                  _layer_spec(P["mlstm_conv_w"].shape, layer, 2), _layer_spec(P["mlstm_vec"].shape, layer, 2)],
        out_specs=[pl.BlockSpec((nb, c, WIDTH), lambda i, j: (i, j, 0)),
                   _batch_spec(nb, N_HEADS, HEAD_DIM, HEAD_DIM), _batch_spec(nb, N_HEADS, HEAD_DIM),
                   _batch_spec(nb, 1, LANES), _batch_spec(nb, CONV_W - 1, 2 * WIDTH)],
        out_shape=[jax.ShapeDtypeStruct((b, l, WIDTH), f32),
                   jax.ShapeDtypeStruct((b, N_HEADS, HEAD_DIM, HEAD_DIM), f32),
                   jax.ShapeDtypeStruct((b, N_HEADS, HEAD_DIM), f32),
                   jax.ShapeDtypeStruct((b, 1, LANES), f32),
                   jax.ShapeDtypeStruct((b, CONV_W - 1, 2 * WIDTH), f32)],
        scratch_shapes=[pltpu.VMEM((nb, c + 8, 2 * WIDTH), f32), pltpu.VMEM((nb, c, WIDTH), f32)],
        compiler_params=_params("parallel", "arbitrary"),
        name="mlstm_branch",
    )(proj, conv0, c0, n0, m0, P["mlstm_conv_w"], P["mlstm_vec"])
    out, cn, nn, mn, convn = outs
    return out, cn, nn, mn[:, 0, :N_HEADS], convn


def _head_mean_matrix():
    hid = jnp.arange(ATTN_W) // HEAD_DIM
    return ((hid[:, None] == hid[None, :]).astype(f32) / HEAD_DIM).astype(bf16)


def _qk_norm(x, hm_ref, w):
    sq = x * x
    hi = sq.astype(bf16)
    lo = (sq - hi.astype(f32)).astype(bf16)
    ms = _dot(hi, hm_ref[...]) + _dot(lo, hm_ref[...])
    return x * lax.rsqrt(ms + NORM_EPS) * w


def _attn_prompt_kernel(q_ref, k_ref, v_ref, z_ref, hm_ref, vec_ref, out_ref, kv_ref,
                        qn_scr, kn_scr, vv_scr, acc_scr, m_scr, d_scr, to_scr, tl_scr, *, l):
    vec = _vec_reader(vec_ref, ATTN_VEC)
    g = pl.program_id(1)
    rb = 256
    n_pairs = ATTN_W // LANES

    def norm_body(i, carry):
        rows = pl.ds(pl.multiple_of(i * rb, rb), rb)
        qn = _qk_norm(q_ref[0, rows, :].astype(f32), hm_ref, vec("q_norm"))
        kn = _qk_norm(k_ref[0, rows, :].astype(f32), hm_ref, vec("k_norm"))
        v = v_ref[0, rows, :].astype(f32)
        kv_ref[0, 0, rows, 0:ATTN_W] = kn
        kv_ref[0, 0, rows, ATTN_W:2 * ATTN_W] = v
        for p in range(n_pairs):
            qn_scr[p, rows, :] = qn[:, p * LANES:(p + 1) * LANES]
            kn_scr[p, rows, :] = kn[:, p * LANES:(p + 1) * LANES]
            vv_scr[p, rows, :] = v[:, p * LANES:(p + 1) * LANES]
        return carry

    lax.fori_loop(0, l // rb, norm_body, 0)

    qb = ATTN_J
    qi = _iota2((qb, qb), 0)
    ki = _iota2((qb, qb), 1)
    scale = HEAD_DIM ** -0.5

    def run_group(dil, first):
        n_iter = l // qb

        def body(i, carry):
            r = i % dil
            blk = i // dil
            u0 = blk * qb
            rows_c = pl.ds(u0 * dil + r, qb, stride=dil)
            rows_p = pl.ds(jnp.maximum(u0 - qb, 0) * dil + r, qb, stride=dil)
            mask_c = ki <= qi
            mask_p = (ki >= qi) & (blk > 0)
            hsl = [slice(hh * HEAD_DIM, (hh + 1) * HEAD_DIM) for hh in range(LANES // HEAD_DIM)]
            heads = [(p, sl) for p in range(n_pairs) for sl in hsl]
            qv = [qn_scr[p, rows_c, :] for p in range(n_pairs)]
            kc = [kn_scr[p, rows_c, :] for p in range(n_pairs)]
            vc = [vv_scr[p, rows_c, :] for p in range(n_pairs)]
            kp = [kn_scr[p, rows_p, :] for p in range(n_pairs)]
            vp = [vv_scr[p, rows_p, :] for p in range(n_pairs)]
            scs = [jnp.where(mask_c, _dot_nt(qv[p][:, sl], kc[p][:, sl]) * scale, -jnp.inf) for p, sl in heads]
            sps = [jnp.where(mask_p, _dot_nt(qv[p][:, sl], kp[p][:, sl]) * scale, -jnp.inf) for p, sl in heads]
            ms = [jnp.maximum(jnp.max(sc, axis=-1, keepdims=True), jnp.max(sp, axis=-1, keepdims=True))
                  for sc, sp in zip(scs, sps)]
            pcs = [jnp.exp(sc - m) for sc, m in zip(scs, ms)]
            pps = [jnp.exp(sp - m) for sp, m in zip(sps, ms)]
            dens = [jnp.sum(pc, axis=-1, keepdims=True) + jnp.sum(pp, axis=-1, keepdims=True)
                    for pc, pp in zip(pcs, pps)]
            for (p, sl), pc, pp, m, den in zip(heads, pcs, pps, ms, dens):
                to_scr[p, :, sl] = (_dot(pc, vc[p][:, sl]) + _dot(pp, vp[p][:, sl])) / den
                tl_scr[p, :, sl] = jnp.broadcast_to(m + jnp.log(den), (qb, HEAD_DIM))
            for p in range(n_pairs):
                o = to_scr[p]
                lse = tl_scr[p]
                if first:
                    acc_scr[p, rows_c, :] = o
                    m_scr[p, rows_c, :] = lse
                    d_scr[p, rows_c, :] = jnp.ones_like(lse)
                else:
                    m_old = m_scr[p, rows_c, :]
                    m_new = jnp.maximum(m_old, lse)
                    a_old = jnp.exp(m_old - m_new)
                    a_new = jnp.exp(lse - m_new)
                    acc_scr[p, rows_c, :] = acc_scr[p, rows_c, :] * a_old + o * a_new
                    d_scr[p, rows_c, :] = d_scr[p, rows_c, :] * a_old + a_new
                    m_scr[p, rows_c, :] = m_new
            return carry

        lax.fori_loop(0, n_iter, body, 0)

    for gi, (_, dil) in enumerate(ATTN_GROUPS):
        pl.when(g == gi)(functools.partial(run_group, dil, gi == 0))

    @pl.when(g == len(ATTN_GROUPS) - 1)
    def _():
        def fin_body(i, carry):
            rows = pl.ds(pl.multiple_of(i * rb, rb), rb)
            for p in range(n_pairs):
                lanes = slice(p * LANES, (p + 1) * LANES)
                z = z_ref[0, rows, lanes].astype(f32)
                out_ref[0, rows, lanes] = acc_scr[p, rows, :] / d_scr[p, rows, :] * _silu(z)
            return carry

        lax.fori_loop(0, l // rb, fin_body, 0)


def _attn_prompt(proj, P, layer):
    b, l, _ = proj.shape
    ng = len(ATTN_GROUPS)
    col = lambda base: pl.BlockSpec((1, l, ATTN_W), lambda i, g: (i, 0, base + g))
    pair_scr = pltpu.VMEM((ATTN_W // LANES, l, LANES), f32)
    return pl.pallas_call(
        functools.partial(_attn_prompt_kernel, l=l),
        grid=(b, ng),
        in_specs=[col(1), col(1 + ng), col(1 + 2 * ng),
                  pl.BlockSpec((1, l, ATTN_W), lambda i, g: (i, 0, 0)),
                  pl.BlockSpec((ATTN_W, ATTN_W), lambda i, g: (0, 0)),
                  _layer_spec(P["attn_vec"].shape, layer, 2)],
        out_specs=[pl.BlockSpec((1, l, ATTN_W), lambda i, g: (i, 0, 0)),
                   pl.BlockSpec((1, 1, l, 2 * ATTN_W), lambda i, g: (i, g, 0, 0))],
        out_shape=[jax.ShapeDtypeStruct((b, l, ATTN_W), f32),
                   jax.ShapeDtypeStruct((b, ng, l, 2 * ATTN_W), f32)],
        scratch_shapes=[pair_scr] * 6 + [pltpu.VMEM((ATTN_W // LANES, ATTN_J, LANES), f32)] * 2,
        compiler_params=_params("parallel", "arbitrary"),
        name="dilated_attn_prompt",
    )(proj, proj, proj, proj, P["head_mean"], P["attn_vec"])


def _attn_step_kernel(x_ref, c0_ref, c1_ref, c2_ref, hm_ref, vec_ref, out_ref, kv_ref, *, l, nb):
    vec = _vec_reader(vec_ref, ATTN_VEC)
    ng = len(ATTN_GROUPS)
    scale = HEAD_DIM ** -0.5
    c_refs = (c0_ref, c1_ref, c2_ref)
    hsl = [slice(h * HEAD_DIM, (h + 1) * HEAD_DIM) for h in range(ATTN_HEADS)]
    masks_p, masks_n = [], []
    for gi, (window, dil) in enumerate(ATTN_GROUPS):
        w = c_refs[gi].shape[-1]
        jmax = window // dil
        shift = dil.bit_length() - 1
        dist_p = w + _iota2((l, w), 0) - _iota2((l, w), 1)
        masks_p.append(((dist_p & (dil - 1)) == 0) & ((dist_p >> shift) <= jmax))
        dist_n = _iota2((l, l), 0) - _iota2((l, l), 1)
        masks_n.append((dist_n >= 0) & ((dist_n & (dil - 1)) == 0) & ((dist_n >> shift) <= jmax))
    gh = [(gi, h) for gi in range(ng) for h in range(ATTN_HEADS)]

    def elem(bi):
        x = x_ref[bi]
        qns, kns, vs = [], [], []
        for gi in range(ng):
            qns.append(_qk_norm(x[:, (1 + gi) * ATTN_W:(2 + gi) * ATTN_W], hm_ref, vec("q_norm")))
            kns.append(_qk_norm(x[:, (1 + ng + gi) * ATTN_W:(2 + ng + gi) * ATTN_W], hm_ref, vec("k_norm")))
            vs.append(x[:, (1 + 2 * ng + gi) * ATTN_W:(2 + 2 * ng + gi) * ATTN_W])
            kv_ref[bi, gi, :, 0:ATTN_W] = kns[gi]
            kv_ref[bi, gi, :, ATTN_W:2 * ATTN_W] = vs[gi]
        yield
        sps = [jnp.where(masks_p[gi], _dot(qns[gi][:, hsl[h]], c_refs[gi][0, bi, 0, h]) * scale, -jnp.inf)
               for gi, h in gh]
        sns = [jnp.where(masks_n[gi], _dot_nt(qns[gi][:, hsl[h]], kns[gi][:, hsl[h]]) * scale, -jnp.inf)
               for gi, h in gh]
        yield
        ms = [jnp.maximum(jnp.max(sp, axis=-1, keepdims=True), jnp.max(sn, axis=-1, keepdims=True))
              for sp, sn in zip(sps, sns)]
        pps = [jnp.exp(sp - m) for sp, m in zip(sps, ms)]
        pns = [jnp.exp(sn - m) for sn, m in zip(sns, ms)]
        dens = [jnp.sum(pp, axis=-1, keepdims=True) + jnp.sum(pn, axis=-1, keepdims=True)
                for pp, pn in zip(pps, pns)]
        yield
        og = [(_dot_nt(pp, c_refs[gi][0, bi, 1, h]) + _dot(pn, vs[gi][:, hsl[h]])) / den
              for (gi, h), pp, pn, den in zip(gh, pps, pns, dens)]
        lg = [m + jnp.log(den) for m, den in zip(ms, dens)]
        yield
        outs = [og[gi * ATTN_HEADS:(gi + 1) * ATTN_HEADS] for gi in range(ng)]
        lses = [lg[gi * ATTN_HEADS:(gi + 1) * ATTN_HEADS] for gi in range(ng)]
        for h in range(ATTN_HEADS):
            sl = hsl[h]
            mx = functools.reduce(jnp.maximum, [lses[gi][h] for gi in range(ng)])
            ws = [jnp.exp(lses[gi][h] - mx) for gi in range(ng)]
            tot = functools.reduce(lambda a, b: a + b, ws)
            o = functools.reduce(lambda a, b: a + b, [outs[gi][h] * (ws[gi] / tot) for gi in range(ng)])
            out_ref[bi, :, sl] = o * _silu(x[:, sl])

    _interleave(elem(bi) for bi in range(nb))


def _attn_step(proj, caches, P, layer, nb):
    b, l, wp = proj.shape
    ng = len(ATTN_GROUPS)
    cache_specs = [pl.BlockSpec((1, nb) + c.shape[2:], lambda i: (layer, i, 0, 0, 0, 0)) for c in caches]
    return pl.pallas_call(
        functools.partial(_attn_step_kernel, l=l, nb=nb),
        grid=(b // nb,),
        in_specs=[pl.BlockSpec((nb, l, wp), lambda i: (i, 0, 0))] + cache_specs
                 + [pl.BlockSpec((ATTN_W, ATTN_W), lambda i: (0, 0)), _layer_spec(P["attn_vec"].shape, layer, 1)],
        out_specs=[pl.BlockSpec((nb, l, ATTN_W), lambda i: (i, 0, 0)),
                   pl.BlockSpec((nb, ng, l, 2 * ATTN_W), lambda i: (i, 0, 0, 0))],
        out_shape=[jax.ShapeDtypeStruct((b, l, ATTN_W), f32),
                   jax.ShapeDtypeStruct((b, ng, l, 2 * ATTN_W), f32)],
        compiler_params=_params("parallel"),
        name="dilated_attn_step",
    )(proj, *caches, P["head_mean"], P["attn_vec"])


def _merge_kernel(oa_ref, ob_ref, oc_ref, od_ref, mg_ref, x_ref, gate_ref, wb_ref, wo_ref, o_ref):
    merged = None
    start = 0
    for bi, ref in enumerate((oa_ref, ob_ref, oc_ref, od_ref)):
        width = ref.shape[-1]
        t = jnp.dot(ref[0].astype(bf16), wb_ref[0, start:start + width, :], preferred_element_type=f32)
        term = jax.nn.sigmoid(mg_ref[0, :, bi * D_MODEL:(bi + 1) * D_MODEL].astype(f32)) * t
        merged = term if merged is None else merged + term
        start += width
    y = jnp.dot(merged.astype(bf16), wo_ref[0], preferred_element_type=f32)
    o_ref[0] = x_ref[0] + gate_ref[0] * y


def _merge(branches, mg3, x3, gate3, P, layer, tl):
    b, l, d = x3.shape
    per_row = gate3.shape[1] != 1
    rows = lambda w: pl.BlockSpec((1, tl, w), lambda i, j: (i, j, 0))
    gate_spec = pl.BlockSpec((1, tl if per_row else 1, d), (lambda i, j: (i, j, 0)) if per_row else (lambda i, j: (i, 0, 0)))
    return pl.pallas_call(
        _merge_kernel,
        grid=(b, l // tl),
        in_specs=[rows(o.shape[-1]) for o in branches] + [rows(N_BRANCH * d), rows(d), gate_spec,
                                                          _layer_spec(P["w_branch"].shape, layer, 2),
                                                          _layer_spec(P["w_out"].shape, layer, 2)],
        out_specs=rows(d),
        out_shape=jax.ShapeDtypeStruct((b, l, d), f32),
        compiler_params=_params("parallel", "parallel"),
        name="merge_out",
    )(*branches, mg3, x3, gate3, P["w_branch"], P["w_out"])


def _split_w_in(w_in):
    cols, start = {}, 0
    for name, size in IN_SEGMENTS:
        cols[name] = w_in[:, :, start:start + size]
        start += size
    zeros = lambda n: jnp.zeros(w_in.shape[:2] + (n,), w_in.dtype)
    groups = {
        "ssm": [cols["ssm_z"], cols["ssm_xbc"], cols["ssm_dt"], zeros(LANES - N_HEADS)],
        "rwkv": [cols["rwkv_z"], cols["rwkv_shift"]],
        "mlstm": [cols["mlstm_z"], cols["mlstm_qk"], cols["mlstm_v"], cols["mlstm_o"], cols["mlstm_if"],
                  zeros(LANES - 2 * N_HEADS)],
        "attn": [cols["attn_z"], cols["attn_qkv"]],
        "merge": [cols["merge"]],
    }
    return {k: jnp.concatenate(v, axis=2).astype(bf16) for k, v in groups.items()}


_PROJ_TN = {"ssm": 1408, "rwkv": 2176, "mlstm": 896, "attn": 1280, "merge": 1024}


def _layer(x, mod, st, sidx, caches, P, layer):
    b, l, d = x.shape
    prompt = caches is None
    shift, scale, gate = mod[:, :d], mod[:, d:2 * d], mod[:, 2 * d:]
    if prompt:
        x3, sc3, sh3, g3 = x, scale[:, None], shift[:, None], gate[:, None]
        tl, chunk, tl_merge, tm, proj_dtype = 512, 128, 256, 1024, bf16
        rwkv_rows, nb_ssd, nb_rwkv, nb_mlstm, nb_attn = 2 * RWKV_CHUNK, math.gcd(b, 4), math.gcd(b, 2), 1, 1
    else:
        rep = lambda t: jnp.repeat(t, l, axis=0)[None]
        x3, sc3, sh3, g3 = x.reshape(1, b * l, d), rep(scale), rep(shift), rep(gate)
        tl = min(256, b * l)
        chunk, tl_merge, tm, proj_dtype = l, tl, min(512, b * l), f32
        rwkv_rows, nb_ssd, nb_rwkv, nb_mlstm, nb_attn = l, math.gcd(b, 4), math.gcd(b, 4), 1, math.gcd(b, 2)
    h2 = _norm(x3, P["norm_w"], layer, sc3, sh3, tl).reshape(b * l, d)
    proj = {k: _matmul(h2, P["w_in_" + k], layer, tm, _PROJ_TN[k], proj_dtype).reshape(b, l, -1) for k in _PROJ_TN}

    out_a, ssm_new, ssm_conv_new = _ssm(proj["ssm"], st["ssm_conv"], st["ssm"], sidx, P, layer, chunk, nb_ssd)
    out_b, rwkv_new, shift_new = _rwkv(proj["rwkv"], st["rwkv_shift"], st["rwkv"], sidx, P, layer,
                                       min(RWKV_CHUNK, l), rwkv_rows, nb_rwkv)
    out_c, c_new, n_new, m_new, mconv_new = _mlstm(proj["mlstm"], st["mlstm_conv"], st["mlstm_c"], st["mlstm_n"],
                                                   st["mlstm_m"], sidx, P, layer, chunk, nb_mlstm)
    if prompt:
        out_d, kv = _attn_prompt(proj["attn"], P, layer)
        kv_new = [kv[:, gi, l - min(w, l):].reshape(b, min(w, l), 2, ATTN_HEADS, HEAD_DIM)
                  for gi, (w, _) in enumerate(ATTN_GROUPS)]
    else:
        out_d, kv = _attn_step(proj["attn"], caches, P, layer, nb_attn)
        kv_new = [kv[:, gi].reshape(b, l, 2, ATTN_HEADS, HEAD_DIM) for gi in range(len(ATTN_GROUPS))]

    branches = [o.reshape(x3.shape[0], x3.shape[1], -1) for o in (out_a, out_b, out_c, out_d)]
    mg3 = proj["merge"].reshape(x3.shape[0], x3.shape[1], -1)
    x_new = _merge(branches, mg3, x3, g3, P, layer, tl_merge).reshape(b, l, d)
    new_state = {"ssm": ssm_new, "ssm_conv": ssm_conv_new, "rwkv": rwkv_new, "rwkv_shift": shift_new[:, 0],
                 "mlstm_c": c_new, "mlstm_n": n_new, "mlstm_m": m_new, "mlstm_conv": mconv_new,
                 "kv_0": kv_new[0], "kv_1": kv_new[1], "kv_2": kv_new[2]}
    return x_new, new_state


_STATE_NAMES = ("ssm", "ssm_conv", "rwkv", "rwkv_shift", "mlstm_c", "mlstm_n", "mlstm_m", "mlstm_conv",
                "kv_0", "kv_1", "kv_2")


def _prepare_params(w):
    P = {"norm_w": w["norm_w"][:, None, :], "ada_w": w["ada_w"], "ada_b": w["ada_b"][:, None, :],
         "w_branch": w["w_branch"].astype(bf16), "w_out": w["w_out"].astype(bf16),
         "ssm_conv_w": w["ssm_conv_w"], "mlstm_conv_w": w["mlstm_conv_w"],
         "rwkv_w2": w["rwkv_w2"], "rwkv_a2": w["rwkv_a2"], "head_mean": _head_mean_matrix(),
         "ssm_vec": _pack_vecs(SSM_VEC, SSM_VEC_N, {"conv_b": w["ssm_conv_b"], "dt_bias": w["ssm_dt_bias"],
                                                    "a_log": w["ssm_a_log"], "d": w["ssm_d"],
                                                    "norm_w": w["ssm_norm_w"]}),
         "rwkv_vec": _pack_vecs(RWKV_VEC, RWKV_VEC_N, {"mu": w["rwkv_mu"], "w0": w["rwkv_w0"], "a0": w["rwkv_a0"],
                                                       "k_k": w["rwkv_k_k"], "k_a": w["rwkv_k_a"],
                                                       "r_k": w["rwkv_r_k"], "ln_w": w["rwkv_ln_w"],
                                                       "ln_b": w["rwkv_ln_b"]}),
         "mlstm_vec": _pack_vecs(MLSTM_VEC, MLSTM_VEC_N, {"conv_b": w["mlstm_conv_b"], "gate_b": w["mlstm_gate_b"],
                                                          "norm_w": w["mlstm_norm_w"]}),
         "attn_vec": _pack_vecs(ATTN_VEC, ATTN_VEC_N, {"q_norm": jnp.tile(w["attn_q_norm"], (1, ATTN_HEADS)),
                                                       "k_norm": jnp.tile(w["attn_k_norm"], (1, ATTN_HEADS))})}
    for name, mat in _split_w_in(w["w_in"]).items():
        P["w_in_" + name] = mat
    return P


def _pad_heads(m):
    return jnp.pad(m, [(0, 0)] * (m.ndim - 1) + [(0, LANES - N_HEADS)])[..., None, :]


def kernel(x_prompt, x_sample, c_prompt, c_sample, state_ssm, state_ssm_conv, state_rwkv, state_rwkv_shift, state_mlstm_c, state_mlstm_n, state_mlstm_m, state_mlstm_conv, cache_kv_w128, cache_kv_w512, cache_kv_w2048, norm_w, ada_w, ada_b, w_in, w_branch, w_out, ssm_conv_w, ssm_conv_b, ssm_dt_bias, ssm_a_log, ssm_d, ssm_norm_w, rwkv_mu, rwkv_w0, rwkv_w2, rwkv_a0, rwkv_a2, rwkv_k_k, rwkv_k_a, rwkv_r_k, rwkv_ln_w, rwkv_ln_b, mlstm_conv_w, mlstm_conv_b, mlstm_gate_b, mlstm_norm_w, attn_q_norm, attn_k_norm):
    P = _prepare_params(dict(
        norm_w=norm_w, ada_w=ada_w, ada_b=ada_b, w_in=w_in, w_branch=w_branch, w_out=w_out, ssm_conv_w=ssm_conv_w,
        ssm_conv_b=ssm_conv_b, ssm_dt_bias=ssm_dt_bias, ssm_a_log=ssm_a_log, ssm_d=ssm_d, ssm_norm_w=ssm_norm_w,
        rwkv_mu=rwkv_mu, rwkv_w0=rwkv_w0, rwkv_w2=rwkv_w2, rwkv_a0=rwkv_a0, rwkv_a2=rwkv_a2, rwkv_k_k=rwkv_k_k,
        rwkv_k_a=rwkv_k_a, rwkv_r_k=rwkv_r_k, rwkv_ln_w=rwkv_ln_w, rwkv_ln_b=rwkv_ln_b, mlstm_conv_w=mlstm_conv_w,
        mlstm_conv_b=mlstm_conv_b, mlstm_gate_b=mlstm_gate_b, mlstm_norm_w=mlstm_norm_w, attn_q_norm=attn_q_norm,
        attn_k_norm=attn_k_norm))
    bp = x_prompt.shape[0]
    fresh = {"ssm": jnp.zeros((1, bp, N_HEADS, HEAD_DIM, SSM_STATE), f32),
             "ssm_conv": jnp.zeros((1, bp, CONV_W - 1, SSM_CONV_CH), f32),
             "rwkv": jnp.zeros((1, bp, N_HEADS, HEAD_DIM, HEAD_DIM), f32),
             "rwkv_shift": jnp.zeros((1, bp, 1, RWKV_SHIFT_CH), f32),
             "mlstm_c": jnp.zeros((1, bp, N_HEADS, HEAD_DIM, HEAD_DIM), f32),
             "mlstm_n": jnp.zeros((1, bp, N_HEADS, HEAD_DIM), f32),
             "mlstm_m": jnp.zeros((1, bp, 1, LANES), f32),
             "mlstm_conv": jnp.zeros((1, bp, CONV_W - 1, 2 * WIDTH), f32)}
    carried = {"ssm": state_ssm, "ssm_conv": state_ssm_conv, "rwkv": state_rwkv,
               "rwkv_shift": state_rwkv_shift[:, :, None, :], "mlstm_c": state_mlstm_c, "mlstm_n": state_mlstm_n,
               "mlstm_m": _pad_heads(state_mlstm_m), "mlstm_conv": state_mlstm_conv}
    c_all = jnp.concatenate([c_prompt, c_sample], axis=0)
    caches_t = [jnp.transpose(c, (0, 1, 3, 4, 5, 2)) for c in (cache_kv_w128, cache_kv_w512, cache_kv_w2048)]
    y_prompt, y_sample = x_prompt, x_sample
    prompt_states, sample_states = [], []
    for layer in range(DEPTH):
        mod = _ada(c_all, P["ada_w"], P["ada_b"], layer)
        y_prompt, sp = _layer(y_prompt, mod[:bp], fresh, 0, None, P, layer)
        y_sample, ss = _layer(y_sample, mod[bp:], carried, layer, caches_t, P, layer)
        prompt_states.append(sp)
        sample_states.append(ss)
    stack = lambda states, name: jnp.stack([s[name] for s in states])
    return ((y_prompt, y_sample)
            + tuple(stack(prompt_states, n) for n in _STATE_NAMES)
            + tuple(stack(sample_states, n) for n in _STATE_NAMES))
```

```python
import functools
import math

import jax
import jax.numpy as jnp
from jax import lax
from jax.experimental import pallas as pl
from jax.experimental.pallas import tpu as pltpu

f32 = jnp.float32
bf16 = jnp.bfloat16
HI = lax.Precision.HIGHEST

D_MODEL = 1024
DEPTH = 2
HEAD_DIM = 64
NORM_EPS = 1e-6
CONV_W = 4
N_HEADS = 8
WIDTH = N_HEADS * HEAD_DIM
SSM_GROUPS = 2
SSM_STATE = 64
SSM_CONV_CH = WIDTH + 2 * SSM_GROUPS * SSM_STATE
RWKV_RANK = 64
RWKV_SHIFT_CH = 3 * WIDTH + 2 * RWKV_RANK
RWKV_LN_EPS = 64e-5
RWKV_CHUNK = 64
ATTN_GROUPS = ((128, 1), (512, 4), (2048, 16))
ATTN_HEADS = 4
ATTN_W = ATTN_HEADS * HEAD_DIM
ATTN_J = 128
ATTN_UNROLL = 2
N_BRANCH = 4
LANES = 128
VMEM_LIMIT = 52 * 1024 * 1024

IN_SEGMENTS = (
    ("ssm_z", WIDTH), ("ssm_xbc", SSM_CONV_CH), ("ssm_dt", N_HEADS),
    ("rwkv_z", WIDTH), ("rwkv_shift", RWKV_SHIFT_CH),
    ("mlstm_z", WIDTH), ("mlstm_qk", 2 * WIDTH), ("mlstm_v", WIDTH),
    ("mlstm_o", WIDTH), ("mlstm_if", 2 * N_HEADS),
    ("attn_z", ATTN_W), ("attn_qkv", 9 * ATTN_W),
    ("merge", N_BRANCH * D_MODEL),
)

NN = (((1,), (0,)), ((), ()))
NT = (((1,), (1,)), ((), ()))
TN = (((0,), (0,)), ((), ()))


def _mm(a, b, dims, prec):
    if prec is None:
        a, b = a.astype(bf16), b.astype(bf16)
    return lax.dot_general(a, b, dims, preferred_element_type=f32, precision=prec)


def _dot(a, b, prec=None):
    return _mm(a, b, NN, prec)


def _dot_nt(a, b, prec=None):
    return _mm(a, b, NT, prec)


def _dot_tn(a, b, transpose_on_mxu=False):
    if not transpose_on_mxu:
        return _mm(a, b, TN, None)
    n = a.shape[1]
    eye = (_iota2((n, n), 0) == _iota2((n, n), 1)).astype(bf16)
    a_t = lax.dot_general(eye, a.astype(bf16), NT, preferred_element_type=f32).astype(bf16)
    return lax.dot_general(a_t, b.astype(bf16), NN, preferred_element_type=f32)


def _silu(x):
    return x * jax.nn.sigmoid(x)


def _softplus(x):
    return jnp.maximum(x, 0.0) + jnp.log1p(jnp.exp(-jnp.abs(x)))


def _params(*sem):
    return pltpu.CompilerParams(dimension_semantics=sem, vmem_limit_bytes=VMEM_LIMIT)


def _iota2(shape, dim):
    return lax.broadcasted_iota(jnp.int32, shape, dim)


def _running_sum(x, c):
    tri = (_iota2((c, c), 0) >= _iota2((c, c), 1)).astype(f32)
    return _dot(tri, x, HI)


def _vec_layout(fields):
    out, off = {}, 0
    for name, size in fields:
        padded = -(-size // LANES) * LANES
        out[name] = (off, padded)
        off += padded
    return out, off


def _pack_vecs(layout, total, vecs):
    parts = []
    for name, (_, padded) in layout.items():
        v = vecs[name].reshape(DEPTH, -1)
        parts.append(jnp.pad(v, ((0, 0), (0, padded - v.shape[1]))))
    packed = jnp.concatenate(parts, axis=1)
    assert packed.shape[1] == total
    return packed[:, None, :]


def _vec_reader(ref, layout):
    return lambda name: ref[0, :, layout[name][0]:layout[name][0] + layout[name][1]]


SSM_VEC, SSM_VEC_N = _vec_layout((("conv_b", SSM_CONV_CH), ("dt_bias", N_HEADS), ("a_log", N_HEADS),
                                  ("d", N_HEADS), ("norm_w", WIDTH)))
RWKV_VEC, RWKV_VEC_N = _vec_layout((("mu", RWKV_SHIFT_CH), ("w0", WIDTH), ("a0", WIDTH), ("k_k", WIDTH),
                                    ("k_a", WIDTH), ("r_k", WIDTH), ("ln_w", WIDTH), ("ln_b", WIDTH)))
MLSTM_VEC, MLSTM_VEC_N = _vec_layout((("conv_b", 2 * WIDTH), ("gate_b", 2 * N_HEADS), ("norm_w", WIDTH)))
ATTN_VEC, ATTN_VEC_N = _vec_layout((("q_norm", ATTN_W), ("k_norm", ATTN_W)))


def _layer_spec(shape, layer, grid_rank):
    zeros = (0,) * (len(shape) - 1)
    if grid_rank == 1:
        return pl.BlockSpec((1,) + tuple(shape[1:]), lambda i: (layer,) + zeros)
    return pl.BlockSpec((1,) + tuple(shape[1:]), lambda i, j: (layer,) + zeros)


def _ada_kernel(c_ref, w_ref, b_ref, o_ref):
    o_ref[...] = _dot(_silu(c_ref[...]), w_ref[0], HI) + b_ref[0]


def _ada(c, w, b3, layer):
    n = c.shape[0]
    return pl.pallas_call(
        _ada_kernel,
        grid=(3,),
        in_specs=[pl.BlockSpec((n, D_MODEL), lambda j: (0, 0)),
                  pl.BlockSpec((1, D_MODEL, D_MODEL), lambda j: (layer, 0, j)),
                  pl.BlockSpec((1, 1, D_MODEL), lambda j: (layer, 0, j))],
        out_specs=pl.BlockSpec((n, D_MODEL), lambda j: (0, j)),
        out_shape=jax.ShapeDtypeStruct((n, 3 * D_MODEL), f32),
        compiler_params=_params("parallel"),
        name="ada_mod",
    )(c, w, b3)


def _norm_kernel(x_ref, nw_ref, sc_ref, sh_ref, o_ref):
    x = x_ref[0]
    r = x * lax.rsqrt(jnp.mean(x * x, axis=-1, keepdims=True) + NORM_EPS)
    o_ref[0] = (r * nw_ref[0] * (1.0 + sc_ref[0]) + sh_ref[0]).astype(o_ref.dtype)


def _norm(x3, nw3, layer, sc3, sh3, tl):
    b, l, d = x3.shape
    per_row = sc3.shape[1] != 1
    mod_spec = pl.BlockSpec((1, tl if per_row else 1, d), (lambda i, j: (i, j, 0)) if per_row else (lambda i, j: (i, 0, 0)))
    return pl.pallas_call(
        _norm_kernel,
        grid=(b, l // tl),
        in_specs=[pl.BlockSpec((1, tl, d), lambda i, j: (i, j, 0)),
                  _layer_spec(nw3.shape, layer, 2), mod_spec, mod_spec],
        out_specs=pl.BlockSpec((1, tl, d), lambda i, j: (i, j, 0)),
        out_shape=jax.ShapeDtypeStruct((b, l, d), bf16),
        compiler_params=_params("parallel", "parallel"),
        name="mod_rmsnorm",
    )(x3, nw3, sc3, sh3)


def _mm_kernel(x_ref, w_ref, o_ref):
    o_ref[...] = jnp.dot(x_ref[...], w_ref[0], preferred_element_type=f32).astype(o_ref.dtype)


def _matmul(x, w3, layer, tm, tn, out_dtype):
    n, k = x.shape
    m = w3.shape[2]
    return pl.pallas_call(
        _mm_kernel,
        grid=(m // tn, n // tm),
        in_specs=[pl.BlockSpec((tm, k), lambda j, i: (i, 0)),
                  pl.BlockSpec((1, k, tn), lambda j, i: (layer, 0, j))],
        out_specs=pl.BlockSpec((tm, tn), lambda j, i: (i, j)),
        out_shape=jax.ShapeDtypeStruct((n, m), out_dtype),
        compiler_params=_params("parallel", "parallel"),
        name="in_proj",
    )(x, w3)


def _causal_conv(ext_scr, u, c, cw_ref, bias):
    ext_scr[8:8 + c, :] = u
    acc = bias
    for i in range(CONV_W):
        acc = acc + ext_scr[5 + i:5 + i + c, :] * cw_ref[0, i:i + 1, :]
    tail = ext_scr[c + 5:c + 8, :]
    ext_scr[5:8, :] = tail
    return acc, tail


def _interleave(gens):
    gens = list(gens)
    while gens:
        alive = []
        for g in gens:
            try:
                next(g)
                alive.append(g)
            except StopIteration:
                pass
        gens = alive


def _ssm_kernel(proj_ref, conv0_ref, h0_ref, cw_ref, vec_ref, out_ref, hn_ref, convn_ref, ext_scr, y_scr, *, c, nb):
    vec = _vec_reader(vec_ref, SSM_VEC)

    @pl.when(pl.program_id(1) == 0)
    def _():
        for bi in range(nb):
            ext_scr[bi, 5:8, :] = conv0_ref[0, bi]
            hn_ref[bi] = h0_ref[0, bi]

    causal = _iota2((c, c), 0) >= _iota2((c, c), 1)
    heads = range(N_HEADS)
    per_group = N_HEADS // SSM_GROUPS
    sls = [slice(h * HEAD_DIM, (h + 1) * HEAD_DIM) for h in heads]

    def elem(bi):
        proj, out, hn, y_acc = proj_ref.at[bi], out_ref.at[bi], hn_ref.at[bi], y_scr.at[bi]
        u = proj[:, WIDTH:WIDTH + SSM_CONV_CH].astype(f32)
        acc, tail = _causal_conv(ext_scr.at[bi], u, c, cw_ref, vec("conv_b"))
        convn_ref[bi] = tail
        xbc = _silu(acc)
        xs = xbc[:, :WIDTH]
        bm = xbc[:, WIDTH:WIDTH + LANES]
        cm = xbc[:, WIDTH + LANES:]
        dt = _softplus(proj[:, WIDTH + SSM_CONV_CH:].astype(f32) + vec("dt_bias"))
        da = dt * (-jnp.exp(vec("a_log")))
        yield
        cs = _running_sum(da, c)
        cs_t = cs.T
        dt_t = dt.T
        cs_end = cs[c - 1:c, :]
        dsk = vec("d")
        bgs = [bm[:, g * SSM_STATE:(g + 1) * SSM_STATE] for g in range(SSM_GROUPS)]
        cgs = [cm[:, g * SSM_STATE:(g + 1) * SSM_STATE] for g in range(SSM_GROUPS)]
        yield
        gmats = [_dot_nt(cgs[g], bgs[g]) for g in range(SSM_GROUPS)]
        cs_cols = [cs[:, h:h + 1] for h in heads]
        yield
        lmats = [gmats[h // per_group] * jnp.exp(jnp.where(causal, cs_cols[h] - cs_t[h:h + 1, :], -jnp.inf))
                 * dt_t[h:h + 1, :] for h in heads]
        xhs = [xs[:, sl] for sl in sls]
        hprevs = [hn[h] for h in heads]
        yield
        y_in = [_dot(lmats[h], xhs[h]) for h in heads]
        y_st = [_dot_nt(cgs[h // per_group], hprevs[h]) for h in heads]
        yield
        for h in heads:
            y_acc[:, sls[h]] = y_in[h] + y_st[h] * jnp.exp(cs_cols[h]) + xhs[h] * dsk[:, h:h + 1]
        ces = [cs_end[:, h:h + 1] for h in heads]
        upd = [_dot_tn(xhs[h] * (jnp.exp(ces[h] - cs_cols[h]) * dt[:, h:h + 1]), bgs[h // per_group],
                       transpose_on_mxu=True) for h in heads]
        yield
        for h in heads:
            hn[h] = hprevs[h] * jnp.exp(ces[h]) + upd[h]
        gy = y_acc[...] * _silu(proj[:, :WIDTH].astype(f32))
        nw = vec("norm_w")
        gw = WIDTH // SSM_GROUPS
        for g in range(SSM_GROUPS):
            part = gy[:, g * gw:(g + 1) * gw]
            ms = jnp.mean(part * part, axis=-1, keepdims=True)
            out[:, g * gw:(g + 1) * gw] = part * lax.rsqrt(ms + NORM_EPS) * nw[:, g * gw:(g + 1) * gw]

    _interleave(elem(bi) for bi in range(nb))


def _state_spec(shape, sidx, nb):
    zeros = (0,) * (len(shape) - 2)
    return pl.BlockSpec((1, nb) + tuple(shape[2:]), lambda i, j: (sidx, i) + zeros)


def _batch_spec(nb, *dims):
    zeros = (0,) * len(dims)
    return pl.BlockSpec((nb,) + dims, lambda i, j: (i,) + zeros)


def _ssm(proj, conv0, h0, sidx, P, layer, c, nb):
    b, l, wp = proj.shape
    return pl.pallas_call(
        functools.partial(_ssm_kernel, c=c, nb=nb),
        grid=(b // nb, l // c),
        in_specs=[pl.BlockSpec((nb, c, wp), lambda i, j: (i, j, 0)),
                  _state_spec(conv0.shape, sidx, nb), _state_spec(h0.shape, sidx, nb),
                  _layer_spec(P["ssm_conv_w"].shape, layer, 2), _layer_spec(P["ssm_vec"].shape, layer, 2)],
        out_specs=[pl.BlockSpec((nb, c, WIDTH), lambda i, j: (i, j, 0)),
                   _batch_spec(nb, N_HEADS, HEAD_DIM, SSM_STATE), _batch_spec(nb, CONV_W - 1, SSM_CONV_CH)],
        out_shape=[jax.ShapeDtypeStruct((b, l, WIDTH), f32),
                   jax.ShapeDtypeStruct((b, N_HEADS, HEAD_DIM, SSM_STATE), f32),
                   jax.ShapeDtypeStruct((b, CONV_W - 1, SSM_CONV_CH), f32)],
        scratch_shapes=[pltpu.VMEM((nb, c + 8, SSM_CONV_CH), f32), pltpu.VMEM((nb, c, WIDTH), f32)],
        compiler_params=_params("parallel", "arbitrary"),
        name="ssd_branch",
    )(proj, conv0, h0, P["ssm_conv_w"], P["ssm_vec"])


def _rwkv_chunk_kernel(proj_ref, shift0_ref, s0_ref, vec_ref, w2_ref, a2_ref,
                       out_ref, sn_ref, shiftn_ref, ext_scr, *, c, rows, nb):
    vec = _vec_reader(vec_ref, RWKV_VEC)

    @pl.when(pl.program_id(1) == 0)
    def _():
        for bi in range(nb):
            ext_scr[bi, 7:8, :] = shift0_ref[0, bi]
            sn_ref[bi] = s0_ref[0, bi]

    row2 = _iota2((c, 2 * c), 0)
    col2 = _iota2((c, 2 * c), 1)
    col2 = jnp.where(col2 >= c, col2 - c, col2)
    strict2 = row2 > col2
    incl2 = row2 >= col2
    n_double = (c - 1).bit_length()
    heads = range(N_HEADS)
    sls = [slice(h * HEAD_DIM, (h + 1) * HEAD_DIM) for h in heads]

    def elem(bi):
        proj, out, sn, ext = proj_ref.at[bi], out_ref.at[bi], sn_ref.at[bi], ext_scr.at[bi]
        u = proj[:, WIDTH:].astype(f32)
        ext[8:8 + rows, :] = u
        sh = u + (ext[7:7 + rows, :] - u) * vec("mu")
        last = ext[rows + 7:rows + 8, :]
        ext[7:8, :] = last
        shiftn_ref[bi] = last
        r_all = sh[:, :WIDTH]
        k = sh[:, WIDTH:2 * WIDTH]
        v_all = sh[:, 2 * WIDTH:3 * WIDTH]
        w_lo = sh[:, 3 * WIDTH:3 * WIDTH + RWKV_RANK]
        a_lo = sh[:, 3 * WIDTH + RWKV_RANK:]
        yield
        w_log = -_softplus(-(vec("w0") + _dot(jnp.tanh(w_lo), w2_ref[0]))) - 0.5
        lw_all = -jnp.exp(w_log)
        a_in_all = jax.nn.sigmoid(vec("a0") + _dot(a_lo, a2_ref[0]))
        kk_all = k * vec("k_k")
        k2_all = k * (1.0 + (a_in_all - 1.0) * vec("k_a"))
        rk, lnw, lnb = vec("r_k"), vec("ln_w"), vec("ln_b")
        yield
        for c0 in range(0, rows, c):
            cr = slice(c0, c0 + c)
            r, v, lw, a_in, kk, k2 = r_all[cr], v_all[cr], lw_all[cr], a_in_all[cr], kk_all[cr], k2_all[cr]
            cl = _running_sum(lw, c)
            cl_end = cl[c - 1:c, :]
            g_fwd = jnp.exp(cl)
            g_inv = jnp.exp(-cl)
            g_rem = jnp.exp(cl_end - cl)
            g_prev = jnp.exp(cl - lw)
            g_end = jnp.exp(cl_end)
            yield
            s0s = [sn[h] for h in heads]
            bhs, ars, bks = [], [], []
            for sl in sls:
                kkh = kk[:, sl]
                kkn = kkh / jnp.maximum(jnp.sqrt(jnp.sum(kkh * kkh, axis=-1, keepdims=True)), 1e-12)
                bhs.append(kkn * a_in[:, sl])
                ars.append(jnp.concatenate([-kkn * g_prev[:, sl], r[:, sl] * g_fwd[:, sl]], axis=0))
                bks.append(jnp.concatenate([bhs[-1] * g_inv[:, sl], k2[:, sl] * g_inv[:, sl]], axis=0))
            yield
            m4s = [_dot_nt(ars[h], bks[h]) for h in heads]
            ahs = [_dot_nt(ars[h], s0s[h]) for h in heads]
            yield
            tops = [jnp.where(strict2, m4[:c, :], 0.0) for m4 in m4s]
            bots = [jnp.where(incl2, m4[c:, :], 0.0) for m4 in m4s]
            amats = [top[:, :c] for top in tops]
            xs = [ahs[h][:c] + _dot(tops[h][:, c:], v[:, sls[h]]) for h in heads]
            yield
            for i in range(n_double):
                xs = [xs[h] + _dot(amats[h], xs[h]) for h in heads]
                if i + 1 < n_double:
                    amats = [_dot(amats[h], amats[h]) for h in heads]
                yield
            pvs = [jnp.concatenate([xs[h], v[:, sls[h]]], axis=0) for h in heads]
            ys = [ahs[h][c:] + _dot(bots[h], pvs[h]) for h in heads]
            upd = [_dot_tn(pvs[h], jnp.concatenate([bhs[h] * g_rem[:, sls[h]], k2[:, sls[h]] * g_rem[:, sls[h]]],
                                                   axis=0)) for h in heads]
            yield
            for h in heads:
                sl = sls[h]
                sn[h] = s0s[h] * g_end[:, sl] + upd[h]
                y = ys[h]
                mu = jnp.mean(y, axis=-1, keepdims=True)
                var = jnp.mean(jnp.square(y - mu), axis=-1, keepdims=True)
                ln = (y - mu) * lax.rsqrt(var + RWKV_LN_EPS) * lnw[:, sl] + lnb[:, sl]
                bonus = jnp.sum(r[:, sl] * k2[:, sl] * rk[:, sl], axis=-1, keepdims=True) * v[:, sl]
                out[cr, sl] = (ln + bonus) * _silu(proj[cr, sl].astype(f32))
            yield

    _interleave(elem(bi) for bi in range(nb))


def _rwkv(proj, shift0, s0, sidx, P, layer, c, rows, nb):
    b, l, wp = proj.shape
    return pl.pallas_call(
        functools.partial(_rwkv_chunk_kernel, c=c, rows=rows, nb=nb),
        grid=(b // nb, l // rows),
        in_specs=[pl.BlockSpec((nb, rows, wp), lambda i, j: (i, j, 0)),
                  _state_spec(shift0.shape, sidx, nb), _state_spec(s0.shape, sidx, nb),
                  _layer_spec(P["rwkv_vec"].shape, layer, 2),
                  _layer_spec(P["rwkv_w2"].shape, layer, 2), _layer_spec(P["rwkv_a2"].shape, layer, 2)],
        out_specs=[pl.BlockSpec((nb, rows, WIDTH), lambda i, j: (i, j, 0)),
                   _batch_spec(nb, N_HEADS, HEAD_DIM, HEAD_DIM), _batch_spec(nb, 1, RWKV_SHIFT_CH)],
        out_shape=[jax.ShapeDtypeStruct((b, l, WIDTH), f32),
                   jax.ShapeDtypeStruct((b, N_HEADS, HEAD_DIM, HEAD_DIM), f32),
                   jax.ShapeDtypeStruct((b, 1, RWKV_SHIFT_CH), f32)],
        scratch_shapes=[pltpu.VMEM((nb, rows + 8, RWKV_SHIFT_CH), f32)],
        compiler_params=_params("parallel", "arbitrary"),
        name="rwkv7_chunked",
    )(proj, shift0, s0, P["rwkv_vec"], P["rwkv_w2"], P["rwkv_a2"])


def _mlstm_kernel(proj_ref, conv0_ref, c0_ref, n0_ref, m0_ref, cw_ref, vec_ref,
                  out_ref, cn_ref, nn_ref, mn_ref, convn_ref, ext_scr, h_scr, *, c, nb):
    vec = _vec_reader(vec_ref, MLSTM_VEC)

    @pl.when(pl.program_id(1) == 0)
    def _():
        for bi in range(nb):
            ext_scr[bi, 5:8, :] = conv0_ref[0, bi]
            cn_ref[bi] = c0_ref[0, bi]
            nn_ref[bi] = n0_ref[0, bi]
            mn_ref[bi] = m0_ref[0, bi]

    causal = _iota2((c, c), 0) >= _iota2((c, c), 1)
    lane = _iota2((1, LANES), 1)
    heads = range(N_HEADS)
    sls = [slice(h * HEAD_DIM, (h + 1) * HEAD_DIM) for h in heads]

    def elem(bi):
        proj, out, cn, nn, h_acc = proj_ref.at[bi], out_ref.at[bi], cn_ref.at[bi], nn_ref.at[bi], h_scr.at[bi]
        u = proj[:, WIDTH:3 * WIDTH].astype(f32)
        acc, tail = _causal_conv(ext_scr.at[bi], u, c, cw_ref, vec("conv_b"))
        convn_ref[bi] = tail
        qk = _silu(acc)
        gates = proj[:, 5 * WIDTH:].astype(f32) + vec("gate_b")
        lf = -_softplus(-gates)
        yield
        bc = _running_sum(lf, c)
        bc_t = bc.T
        ig_t = gates.T
        m_prev = mn_ref[bi]
        yield
        b_cols = [bc[:, N_HEADS + h:N_HEADS + h + 1] for h in heads]
        dmats = [jnp.where(causal, b_cols[h] - bc_t[N_HEADS + h:N_HEADS + h + 1, :] + ig_t[h:h + 1, :], -jnp.inf)
                 for h in heads]
        m0s = [m_prev[:, h:h + 1] for h in heads]
        gcols = [b_cols[h] + m0s[h] for h in heads]
        mts = [jnp.maximum(gcols[h], jnp.max(dmats[h], axis=-1, keepdims=True)) for h in heads]
        yield
        qhs = [qk[:, sl] for sl in sls]
        khs = [qk[:, WIDTH + h * HEAD_DIM:WIDTH + (h + 1) * HEAD_DIM] * HEAD_DIM ** -0.5 for h in heads]
        vhs = [proj[:, 3 * WIDTH + h * HEAD_DIM:3 * WIDTH + (h + 1) * HEAD_DIM].astype(f32) for h in heads]
        cprevs = [cn[h] for h in heads]
        nprevs = [nn[h:h + 1, :] for h in heads]
        yield
        sws = [jnp.exp(dmats[h] - mts[h]) * _dot_nt(qhs[h], khs[h]) for h in heads]
        wis = [jnp.exp(gcols[h] - mts[h]) for h in heads]
        qcs = [_dot(qhs[h], cprevs[h]) for h in heads]
        yield
        nums = [_dot(sws[h], vhs[h]) + wis[h] * qcs[h] for h in heads]
        yield
        for h in heads:
            den = (jnp.sum(sws[h], axis=-1, keepdims=True)
                   + wis[h] * jnp.sum(qhs[h] * nprevs[h], axis=-1, keepdims=True))
            h_acc[:, sls[h]] = nums[h] / jnp.maximum(jnp.abs(den), jnp.exp(-mts[h]))
        m_news = [mts[h][c - 1:c, :] for h in heads]
        b_ends = [b_cols[h][c - 1:c, :] for h in heads]
        kws = [khs[h] * jnp.exp(b_ends[h] - b_cols[h] + gates[:, h:h + 1] - m_news[h]) for h in heads]
        yield
        kvs = [_dot_tn(kws[h], vhs[h]) for h in heads]
        yield
        m_out = jnp.zeros((1, LANES), f32)
        for h in heads:
            dec = jnp.exp(b_ends[h] + m0s[h] - m_news[h])
            cn[h] = dec * cprevs[h] + kvs[h]
            nn[h:h + 1, :] = dec * nprevs[h] + jnp.sum(kws[h], axis=0, keepdims=True)
            m_out = jnp.where(lane == h, m_news[h], m_out)
        mn_ref[bi] = m_out
        nw = vec("norm_w")
        for h in heads:
            sl = sls[h]
            o_gate = proj[:, 4 * WIDTH + h * HEAD_DIM:4 * WIDTH + (h + 1) * HEAD_DIM].astype(f32)
            hm = jax.nn.sigmoid(o_gate) * h_acc[:, sl]
            ms = jnp.mean(hm * hm, axis=-1, keepdims=True)
            out[:, sl] = hm * lax.rsqrt(ms + NORM_EPS) * nw[:, sl] * _silu(proj[:, sl].astype(f32))

    _interleave(elem(bi) for bi in range(nb))


def _mlstm(proj, conv0, c0, n0, m0, sidx, P, layer, c, nb):
    b, l, wp = proj.shape
    outs = pl.pallas_call(
        functools.partial(_mlstm_kernel, c=c, nb=nb),
        grid=(b // nb, l // c),
        in_specs=[pl.BlockSpec((nb, c, wp), lambda i, j: (i, j, 0)),
                  _state_spec(conv0.shape, sidx, nb), _state_spec(c0.shape, sidx, nb),
                  _state_spec(n0.shape, sidx, nb), _state_spec(m0.shape, sidx, nb),
                  _layer_spec(P["mlstm_conv_w"].shape, layer, 2), _layer_spec(P["mlstm_vec"].shape, layer, 2)],
        out_specs=[pl.BlockSpec((nb, c, WIDTH), lambda i, j: (i, j, 0)),
                   _batch_spec(nb, N_HEADS, HEAD_DIM, HEAD_DIM), _batch_spec(nb, N_HEADS, HEAD_DIM),
                   _batch_spec(nb, 1, LANES), _batch_spec(nb, CONV_W - 1, 2 * WIDTH)],
        out_shape=[jax.ShapeDtypeStruct((b, l, WIDTH), f32),
                   jax.ShapeDtypeStruct((b, N_HEADS, HEAD_DIM, HEAD_DIM), f32),
                   jax.ShapeDtypeStruct((b, N_HEADS, HEAD_DIM), f32),
                   jax.ShapeDtypeStruct((b, 1, LANES), f32),
                   jax.ShapeDtypeStruct((b, CONV_W - 1, 2 * WIDTH), f32)],
        scratch_shapes=[pltpu.VMEM((nb, c + 8, 2 * WIDTH), f32), pltpu.VMEM((nb, c, WIDTH), f32)],
        compiler_params=_params("parallel", "arbitrary"),
        name="mlstm_branch",
    )(proj, conv0, c0, n0, m0, P["mlstm_conv_w"], P["mlstm_vec"])
    out, cn, nn, mn, convn = outs
    return out, cn, nn, mn[:, 0, :N_HEADS], convn


def _head_mean_matrix():
    hid = jnp.arange(ATTN_W) // HEAD_DIM
    return ((hid[:, None] == hid[None, :]).astype(f32) / HEAD_DIM).astype(bf16)


def _qk_norm(x, hm_ref, w):
    sq = x * x
    hi = sq.astype(bf16)
    lo = (sq - hi.astype(f32)).astype(bf16)
    ms = _dot(hi, hm_ref[...]) + _dot(lo, hm_ref[...])
    return x * lax.rsqrt(ms + NORM_EPS) * w


def _attn_prompt_kernel(q_ref, k_ref, v_ref, z_ref, hm_ref, vec_ref, out_ref, kv_ref,
                        qn_scr, kn_scr, vv_scr, acc_scr, m_scr, d_scr, to_scr, tl_scr, *, l):
    vec = _vec_reader(vec_ref, ATTN_VEC)
    g = pl.program_id(1)
    rb = 256
    n_pairs = ATTN_W // LANES

    def norm_body(i, carry):
        rows = pl.ds(pl.multiple_of(i * rb, rb), rb)
        qn = _qk_norm(q_ref[0, rows, :].astype(f32), hm_ref, vec("q_norm"))
        kn = _qk_norm(k_ref[0, rows, :].astype(f32), hm_ref, vec("k_norm"))
        v = v_ref[0, rows, :].astype(f32)
        kv_ref[0, 0, rows, 0:ATTN_W] = kn
        kv_ref[0, 0, rows, ATTN_W:2 * ATTN_W] = v
        for p in range(n_pairs):
            qn_scr[p, rows, :] = qn[:, p * LANES:(p + 1) * LANES]
            kn_scr[p, rows, :] = kn[:, p * LANES:(p + 1) * LANES]
            vv_scr[p, rows, :] = v[:, p * LANES:(p + 1) * LANES]
        return carry

    lax.fori_loop(0, l // rb, norm_body, 0)

    qb = ATTN_J
    qi = _iota2((qb, qb), 0)
    ki = _iota2((qb, qb), 1)
    scale = HEAD_DIM ** -0.5

    def run_group(dil, first):
        n_iter = l // qb

        def block(i, slot):
            r = i % dil
            blk = i // dil
            u0 = blk * qb
            rows_c = pl.ds(u0 * dil + r, qb, stride=dil)
            rows_p = pl.ds(jnp.maximum(u0 - qb, 0) * dil + r, qb, stride=dil)
            mask_c = ki <= qi
            mask_p = (ki >= qi) & (blk > 0)
            hsl = [slice(hh * HEAD_DIM, (hh + 1) * HEAD_DIM) for hh in range(LANES // HEAD_DIM)]
            heads = [(p, sl) for p in range(n_pairs) for sl in hsl]
            qv = [qn_scr[p, rows_c, :] for p in range(n_pairs)]
            kc = [kn_scr[p, rows_c, :] for p in range(n_pairs)]
            vc = [vv_scr[p, rows_c, :] for p in range(n_pairs)]
            kp = [kn_scr[p, rows_p, :] for p in range(n_pairs)]
            vp = [vv_scr[p, rows_p, :] for p in range(n_pairs)]
            yield
            scs = [jnp.where(mask_c, _dot_nt(qv[p][:, sl], kc[p][:, sl]) * scale, -jnp.inf) for p, sl in heads]
            sps = [jnp.where(mask_p, _dot_nt(qv[p][:, sl], kp[p][:, sl]) * scale, -jnp.inf) for p, sl in heads]
            yield
            ms = [jnp.max(jnp.maximum(sc, sp), axis=-1, keepdims=True) for sc, sp in zip(scs, sps)]
            pcs = [jnp.exp(sc - m) for sc, m in zip(scs, ms)]
            pps = [jnp.exp(sp - m) for sp, m in zip(sps, ms)]
            dens = [jnp.sum(pc + pp, axis=-1, keepdims=True) for pc, pp in zip(pcs, pps)]
            yield
            for (p, sl), pc, pp, m, den in zip(heads, pcs, pps, ms, dens):
                to_scr[slot, p, :, sl] = (_dot(pc, vc[p][:, sl]) + _dot(pp, vp[p][:, sl])) / den
                tl_scr[slot, p, :, sl] = jnp.broadcast_to(m + jnp.log(den), (qb, HEAD_DIM))
            yield
            for p in range(n_pairs):
                o = to_scr[slot, p]
                lse = tl_scr[slot, p]
                if first:
                    acc_scr[p, rows_c, :] = o
                    m_scr[p, rows_c, :] = lse
                    d_scr[p, rows_c, :] = jnp.ones_like(lse)
                else:
                    m_old = m_scr[p, rows_c, :]
                    m_new = jnp.maximum(m_old, lse)
                    a_old = jnp.exp(m_old - m_new)
                    a_new = jnp.exp(lse - m_new)
                    acc_scr[p, rows_c, :] = acc_scr[p, rows_c, :] * a_old + o * a_new
                    d_scr[p, rows_c, :] = d_scr[p, rows_c, :] * a_old + a_new
                    m_scr[p, rows_c, :] = m_new

        def body(i, carry):
            _interleave(block(i * ATTN_UNROLL + s, s) for s in range(ATTN_UNROLL))
            return carry

        lax.fori_loop(0, n_iter // ATTN_UNROLL, body, 0)

    for gi, (_, dil) in enumerate(ATTN_GROUPS):
        pl.when(g == gi)(functools.partial(run_group, dil, gi == 0))

    @pl.when(g == len(ATTN_GROUPS) - 1)
    def _():
        def fin_body(i, carry):
            rows = pl.ds(pl.multiple_of(i * rb, rb), rb)
            for p in range(n_pairs):
                lanes = slice(p * LANES, (p + 1) * LANES)
                z = z_ref[0, rows, lanes].astype(f32)
                out_ref[0, rows, lanes] = acc_scr[p, rows, :] / d_scr[p, rows, :] * _silu(z)
            return carry

        lax.fori_loop(0, l // rb, fin_body, 0)


def _attn_prompt(proj, P, layer):
    b, l, _ = proj.shape
    ng = len(ATTN_GROUPS)
    col = lambda base: pl.BlockSpec((1, l, ATTN_W), lambda i, g: (i, 0, base + g))
    pair_scr = pltpu.VMEM((ATTN_W // LANES, l, LANES), f32)
    return pl.pallas_call(
        functools.partial(_attn_prompt_kernel, l=l),
        grid=(b, ng),
        in_specs=[col(1), col(1 + ng), col(1 + 2 * ng),
                  pl.BlockSpec((1, l, ATTN_W), lambda i, g: (i, 0, 0)),
                  pl.BlockSpec((ATTN_W, ATTN_W), lambda i, g: (0, 0)),
                  _layer_spec(P["attn_vec"].shape, layer, 2)],
        out_specs=[pl.BlockSpec((1, l, ATTN_W), lambda i, g: (i, 0, 0)),
                   pl.BlockSpec((1, 1, l, 2 * ATTN_W), lambda i, g: (i, g, 0, 0))],
        out_shape=[jax.ShapeDtypeStruct((b, l, ATTN_W), f32),
                   jax.ShapeDtypeStruct((b, ng, l, 2 * ATTN_W), f32)],
        scratch_shapes=[pair_scr] * 6 + [pltpu.VMEM((ATTN_UNROLL, ATTN_W // LANES, ATTN_J, LANES), f32)] * 2,
        compiler_params=_params("parallel", "arbitrary"),
        name="dilated_attn_prompt",
    )(proj, proj, proj, proj, P["head_mean"], P["attn_vec"])


def _attn_step_kernel(x_ref, c0_ref, c1_ref, c2_ref, hm_ref, vec_ref, out_ref, kv_ref, *, l, nb):
    vec = _vec_reader(vec_ref, ATTN_VEC)
    ng = len(ATTN_GROUPS)
    scale = HEAD_DIM ** -0.5
    c_refs = (c0_ref, c1_ref, c2_ref)
    hsl = [slice(h * HEAD_DIM, (h + 1) * HEAD_DIM) for h in range(ATTN_HEADS)]
    masks_p, masks_n = [], []
    for gi, (window, dil) in enumerate(ATTN_GROUPS):
        w = c_refs[gi].shape[-1]
        jmax = window // dil
        shift = dil.bit_length() - 1
        dist_p = w + _iota2((l, w), 0) - _iota2((l, w), 1)
        masks_p.append(((dist_p & (dil - 1)) == 0) & ((dist_p >> shift) <= jmax))
        dist_n = _iota2((l, l), 0) - _iota2((l, l), 1)
        masks_n.append((dist_n >= 0) & ((dist_n & (dil - 1)) == 0) & ((dist_n >> shift) <= jmax))
    gh = [(gi, h) for gi in range(ng) for h in range(ATTN_HEADS)]

    def elem(bi):
        x = x_ref[bi]
        qns, kns, vs = [], [], []
        for gi in range(ng):
            qns.append(_qk_norm(x[:, (1 + gi) * ATTN_W:(2 + gi) * ATTN_W], hm_ref, vec("q_norm")))
            kns.append(_qk_norm(x[:, (1 + ng + gi) * ATTN_W:(2 + ng + gi) * ATTN_W], hm_ref, vec("k_norm")))
            vs.append(x[:, (1 + 2 * ng + gi) * ATTN_W:(2 + 2 * ng + gi) * ATTN_W])
            kv_ref[bi, gi, :, 0:ATTN_W] = kns[gi]
            kv_ref[bi, gi, :, ATTN_W:2 * ATTN_W] = vs[gi]
        yield
        sps = [jnp.where(masks_p[gi], _dot(qns[gi][:, hsl[h]], c_refs[gi][0, bi, 0, h]) * scale, -jnp.inf)
               for gi, h in gh]
        sns = [jnp.where(masks_n[gi], _dot_nt(qns[gi][:, hsl[h]], kns[gi][:, hsl[h]]) * scale, -jnp.inf)
               for gi, h in gh]
        yield
        ms = [jnp.maximum(jnp.max(sp, axis=-1, keepdims=True), jnp.max(sn, axis=-1, keepdims=True))
              for sp, sn in zip(sps, sns)]
        pps = [jnp.exp(sp - m) for sp, m in zip(sps, ms)]
        pns = [jnp.exp(sn - m) for sn, m in zip(sns, ms)]
        dens = [jnp.sum(pp, axis=-1, keepdims=True) + jnp.sum(pn, axis=-1, keepdims=True)
                for pp, pn in zip(pps, pns)]
        yield
        og = [(_dot_nt(pp, c_refs[gi][0, bi, 1, h]) + _dot(pn, vs[gi][:, hsl[h]])) / den
              for (gi, h), pp, pn, den in zip(gh, pps, pns, dens)]
        lg = [m + jnp.log(den) for m, den in zip(ms, dens)]
        yield
        outs = [og[gi * ATTN_HEADS:(gi + 1) * ATTN_HEADS] for gi in range(ng)]
        lses = [lg[gi * ATTN_HEADS:(gi + 1) * ATTN_HEADS] for gi in range(ng)]
        for h in range(ATTN_HEADS):
            sl = hsl[h]
            mx = functools.reduce(jnp.maximum, [lses[gi][h] for gi in range(ng)])
            ws = [jnp.exp(lses[gi][h] - mx) for gi in range(ng)]
            tot = functools.reduce(lambda a, b: a + b, ws)
            o = functools.reduce(lambda a, b: a + b, [outs[gi][h] * (ws[gi] / tot) for gi in range(ng)])
            out_ref[bi, :, sl] = o * _silu(x[:, sl])

    _interleave(elem(bi) for bi in range(nb))


def _attn_step(proj, caches, P, layer, nb):
    b, l, wp = proj.shape
    ng = len(ATTN_GROUPS)
    cache_specs = [pl.BlockSpec((1, nb) + c.shape[2:], lambda i: (layer, i, 0, 0, 0, 0)) for c in caches]
    return pl.pallas_call(
        functools.partial(_attn_step_kernel, l=l, nb=nb),
        grid=(b // nb,),
        in_specs=[pl.BlockSpec((nb, l, wp), lambda i: (i, 0, 0))] + cache_specs
                 + [pl.BlockSpec((ATTN_W, ATTN_W), lambda i: (0, 0)), _layer_spec(P["attn_vec"].shape, layer, 1)],
        out_specs=[pl.BlockSpec((nb, l, ATTN_W), lambda i: (i, 0, 0)),
                   pl.BlockSpec((nb, ng, l, 2 * ATTN_W), lambda i: (i, 0, 0, 0))],
        out_shape=[jax.ShapeDtypeStruct((b, l, ATTN_W), f32),
                   jax.ShapeDtypeStruct((b, ng, l, 2 * ATTN_W), f32)],
        compiler_params=_params("parallel"),
        name="dilated_attn_step",
    )(proj, *caches, P["head_mean"], P["attn_vec"])


def _merge_kernel(oa_ref, ob_ref, oc_ref, od_ref, mg_ref, x_ref, gate_ref, wb_ref, wo_ref, o_ref):
    merged = None
    start = 0
    for bi, ref in enumerate((oa_ref, ob_ref, oc_ref, od_ref)):
        width = ref.shape[-1]
        t = jnp.dot(ref[0].astype(bf16), wb_ref[0, start:start + width, :], preferred_element_type=f32)
        term = jax.nn.sigmoid(mg_ref[0, :, bi * D_MODEL:(bi + 1) * D_MODEL].astype(f32)) * t
        merged = term if merged is None else merged + term
        start += width
    y = jnp.dot(merged.astype(bf16), wo_ref[0], preferred_element_type=f32)
    o_ref[0] = x_ref[0] + gate_ref[0] * y


def _merge(branches, mg3, x3, gate3, P, layer, tl):
    b, l, d = x3.shape
    per_row = gate3.shape[1] != 1
    rows = lambda w: pl.BlockSpec((1, tl, w), lambda i, j: (i, j, 0))
    gate_spec = pl.BlockSpec((1, tl if per_row else 1, d), (lambda i, j: (i, j, 0)) if per_row else (lambda i, j: (i, 0, 0)))
    return pl.pallas_call(
        _merge_kernel,
        grid=(b, l // tl),
        in_specs=[rows(o.shape[-1]) for o in branches] + [rows(N_BRANCH * d), rows(d), gate_spec,
                                                          _layer_spec(P["w_branch"].shape, layer, 2),
                                                          _layer_spec(P["w_out"].shape, layer, 2)],
        out_specs=rows(d),
        out_shape=jax.ShapeDtypeStruct((b, l, d), f32),
        compiler_params=_params("parallel", "parallel"),
        name="merge_out",
    )(*branches, mg3, x3, gate3, P["w_branch"], P["w_out"])


def _split_w_in(w_in):
    cols, start = {}, 0
    for name, size in IN_SEGMENTS:
        cols[name] = w_in[:, :, start:start + size]
        start += size
    zeros = lambda n: jnp.zeros(w_in.shape[:2] + (n,), w_in.dtype)
    groups = {
        "ssm": [cols["ssm_z"], cols["ssm_xbc"], cols["ssm_dt"], zeros(LANES - N_HEADS)],
        "rwkv": [cols["rwkv_z"], cols["rwkv_shift"]],
        "mlstm": [cols["mlstm_z"], cols["mlstm_qk"], cols["mlstm_v"], cols["mlstm_o"], cols["mlstm_if"],
                  zeros(LANES - 2 * N_HEADS)],
        "attn": [cols["attn_z"], cols["attn_qkv"]],
        "merge": [cols["merge"]],
    }
    return {k: jnp.concatenate(v, axis=2).astype(bf16) for k, v in groups.items()}


_PROJ_TN = {"ssm": 1408, "rwkv": 2176, "mlstm": 896, "attn": 1280, "merge": 1024}


def _layer(x, mod, st, sidx, caches, P, layer):
    b, l, d = x.shape
    prompt = caches is None
    shift, scale, gate = mod[:, :d], mod[:, d:2 * d], mod[:, 2 * d:]
    if prompt:
        x3, sc3, sh3, g3 = x, scale[:, None], shift[:, None], gate[:, None]
        tl, chunk, tl_merge, tm, proj_dtype = 512, 128, 512, 1024, bf16
        rwkv_rows, nb_ssd, nb_rwkv, nb_mlstm, nb_attn = 2 * RWKV_CHUNK, math.gcd(b, 4), math.gcd(b, 2), 1, 1
    else:
        rep = lambda t: jnp.repeat(t, l, axis=0)[None]
        x3, sc3, sh3, g3 = x.reshape(1, b * l, d), rep(scale), rep(shift), rep(gate)
        tl = min(256, b * l)
        chunk, tl_merge, tm, proj_dtype = l, tl, min(512, b * l), f32
        rwkv_rows, nb_ssd, nb_rwkv, nb_mlstm, nb_attn = l, math.gcd(b, 4), math.gcd(b, 4), 1, math.gcd(b, 2)
    h2 = _norm(x3, P["norm_w"], layer, sc3, sh3, tl).reshape(b * l, d)
    proj = {k: _matmul(h2, P["w_in_" + k], layer, tm, _PROJ_TN[k], proj_dtype).reshape(b, l, -1) for k in _PROJ_TN}

    out_a, ssm_new, ssm_conv_new = _ssm(proj["ssm"], st["ssm_conv"], st["ssm"], sidx, P, layer, chunk, nb_ssd)
    out_b, rwkv_new, shift_new = _rwkv(proj["rwkv"], st["rwkv_shift"], st["rwkv"], sidx, P, layer,
                                       min(RWKV_CHUNK, l), rwkv_rows, nb_rwkv)
    out_c, c_new, n_new, m_new, mconv_new = _mlstm(proj["mlstm"], st["mlstm_conv"], st["mlstm_c"], st["mlstm_n"],
                                                   st["mlstm_m"], sidx, P, layer, chunk, nb_mlstm)
    if prompt:
        out_d, kv = _attn_prompt(proj["attn"], P, layer)
        kv_new = [kv[:, gi, l - min(w, l):].reshape(b, min(w, l), 2, ATTN_HEADS, HEAD_DIM)
                  for gi, (w, _) in enumerate(ATTN_GROUPS)]
    else:
        out_d, kv = _attn_step(proj["attn"], caches, P, layer, nb_attn)
        kv_new = [kv[:, gi].reshape(b, l, 2, ATTN_HEADS, HEAD_DIM) for gi in range(len(ATTN_GROUPS))]

    branches = [o.reshape(x3.shape[0], x3.shape[1], -1) for o in (out_a, out_b, out_c, out_d)]
    mg3 = proj["merge"].reshape(x3.shape[0], x3.shape[1], -1)
    x_new = _merge(branches, mg3, x3, g3, P, layer, tl_merge).reshape(b, l, d)
    new_state = {"ssm": ssm_new, "ssm_conv": ssm_conv_new, "rwkv": rwkv_new, "rwkv_shift": shift_new[:, 0],
                 "mlstm_c": c_new, "mlstm_n": n_new, "mlstm_m": m_new, "mlstm_conv": mconv_new,
                 "kv_0": kv_new[0], "kv_1": kv_new[1], "kv_2": kv_new[2]}
    return x_new, new_state


_STATE_NAMES = ("ssm", "ssm_conv", "rwkv", "rwkv_shift", "mlstm_c", "mlstm_n", "mlstm_m", "mlstm_conv",
                "kv_0", "kv_1", "kv_2")


def _prepare_params(w):
    P = {"norm_w": w["norm_w"][:, None, :], "ada_w": w["ada_w"], "ada_b": w["ada_b"][:, None, :],
         "w_branch": w["w_branch"].astype(bf16), "w_out": w["w_out"].astype(bf16),
         "ssm_conv_w": w["ssm_conv_w"], "mlstm_conv_w": w["mlstm_conv_w"],
         "rwkv_w2": w["rwkv_w2"], "rwkv_a2": w["rwkv_a2"], "head_mean": _head_mean_matrix(),
         "ssm_vec": _pack_vecs(SSM_VEC, SSM_VEC_N, {"conv_b": w["ssm_conv_b"], "dt_bias": w["ssm_dt_bias"],
                                                    "a_log": w["ssm_a_log"], "d": w["ssm_d"],
                                                    "norm_w": w["ssm_norm_w"]}),
         "rwkv_vec": _pack_vecs(RWKV_VEC, RWKV_VEC_N, {"mu": w["rwkv_mu"], "w0": w["rwkv_w0"], "a0": w["rwkv_a0"],
                                                       "k_k": w["rwkv_k_k"], "k_a": w["rwkv_k_a"],
                                                       "r_k": w["rwkv_r_k"], "ln_w": w["rwkv_ln_w"],
                                                       "ln_b": w["rwkv_ln_b"]}),
         "mlstm_vec": _pack_vecs(MLSTM_VEC, MLSTM_VEC_N, {"conv_b": w["mlstm_conv_b"], "gate_b": w["mlstm_gate_b"],
                                                          "norm_w": w["mlstm_norm_w"]}),
         "attn_vec": _pack_vecs(ATTN_VEC, ATTN_VEC_N, {"q_norm": jnp.tile(w["attn_q_norm"], (1, ATTN_HEADS)),
                                                       "k_norm": jnp.tile(w["attn_k_norm"], (1, ATTN_HEADS))})}
    for name, mat in _split_w_in(w["w_in"]).items():
        P["w_in_" + name] = mat
    return P


def _pad_heads(m):
    return jnp.pad(m, [(0, 0)] * (m.ndim - 1) + [(0, LANES - N_HEADS)])[..., None, :]


def kernel(x_prompt, x_sample, c_prompt, c_sample, state_ssm, state_ssm_conv, state_rwkv, state_rwkv_shift, state_mlstm_c, state_mlstm_n, state_mlstm_m, state_mlstm_conv, cache_kv_w128, cache_kv_w512, cache_kv_w2048, norm_w, ada_w, ada_b, w_in, w_branch, w_out, ssm_conv_w, ssm_conv_b, ssm_dt_bias, ssm_a_log, ssm_d, ssm_norm_w, rwkv_mu, rwkv_w0, rwkv_w2, rwkv_a0, rwkv_a2, rwkv_k_k, rwkv_k_a, rwkv_r_k, rwkv_ln_w, rwkv_ln_b, mlstm_conv_w, mlstm_conv_b, mlstm_gate_b, mlstm_norm_w, attn_q_norm, attn_k_norm):
    P = _prepare_params(dict(
        norm_w=norm_w, ada_w=ada_w, ada_b=ada_b, w_in=w_in, w_branch=w_branch, w_out=w_out, ssm_conv_w=ssm_conv_w,
        ssm_conv_b=ssm_conv_b, ssm_dt_bias=ssm_dt_bias, ssm_a_log=ssm_a_log, ssm_d=ssm_d, ssm_norm_w=ssm_norm_w,
        rwkv_mu=rwkv_mu, rwkv_w0=rwkv_w0, rwkv_w2=rwkv_w2, rwkv_a0=rwkv_a0, rwkv_a2=rwkv_a2, rwkv_k_k=rwkv_k_k,
        rwkv_k_a=rwkv_k_a, rwkv_r_k=rwkv_r_k, rwkv_ln_w=rwkv_ln_w, rwkv_ln_b=rwkv_ln_b, mlstm_conv_w=mlstm_conv_w,
        mlstm_conv_b=mlstm_conv_b, mlstm_gate_b=mlstm_gate_b, mlstm_norm_w=mlstm_norm_w, attn_q_norm=attn_q_norm,
        attn_k_norm=attn_k_norm))
    bp = x_prompt.shape[0]
    fresh = {"ssm": jnp.zeros((1, bp, N_HEADS, HEAD_DIM, SSM_STATE), f32),
             "ssm_conv": jnp.zeros((1, bp, CONV_W - 1, SSM_CONV_CH), f32),
             "rwkv": jnp.zeros((1, bp, N_HEADS, HEAD_DIM, HEAD_DIM), f32),
             "rwkv_shift": jnp.zeros((1, bp, 1, RWKV_SHIFT_CH), f32),
             "mlstm_c": jnp.zeros((1, bp, N_HEADS, HEAD_DIM, HEAD_DIM), f32),
             "mlstm_n": jnp.zeros((1, bp, N_HEADS, HEAD_DIM), f32),
             "mlstm_m": jnp.zeros((1, bp, 1, LANES), f32),
             "mlstm_conv": jnp.zeros((1, bp, CONV_W - 1, 2 * WIDTH), f32)}
    carried = {"ssm": state_ssm, "ssm_conv": state_ssm_conv, "rwkv": state_rwkv,
               "rwkv_shift": state_rwkv_shift[:, :, None, :], "mlstm_c": state_mlstm_c, "mlstm_n": state_mlstm_n,
               "mlstm_m": _pad_heads(state_mlstm_m), "mlstm_conv": state_mlstm_conv}
    c_all = jnp.concatenate([c_prompt, c_sample], axis=0)
    caches_t = [jnp.transpose(c, (0, 1, 3, 4, 5, 2)) for c in (cache_kv_w128, cache_kv_w512, cache_kv_w2048)]
    y_prompt, y_sample = x_prompt, x_sample
    prompt_states, sample_states = [], []
    for layer in range(DEPTH):
        mod = _ada(c_all, P["ada_w"], P["ada_b"], layer)
        y_prompt, sp = _layer(y_prompt, mod[:bp], fresh, 0, None, P, layer)
        y_sample, ss = _layer(y_sample, mod[bp:], carried, layer, caches_t, P, layer)
        prompt_states.append(sp)
        sample_states.append(ss)
    stack = lambda states, name: jnp.stack([s[name] for s in states])
    return ((y_prompt, y_sample)
            + tuple(stack(prompt_states, n) for n in _STATE_NAMES)
            + tuple(stack(sample_states, n) for n in _STATE_NAMES))
```

```python
import functools
import math

import jax
import jax.numpy as jnp
from jax import lax
from jax.experimental import pallas as pl
from jax.experimental.pallas import tpu as pltpu

f32 = jnp.float32
bf16 = jnp.bfloat16
HI = lax.Precision.HIGHEST

D_MODEL = 1024
DEPTH = 2
HEAD_DIM = 64
NORM_EPS = 1e-6
CONV_W = 4
N_HEADS = 8
WIDTH = N_HEADS * HEAD_DIM
SSM_GROUPS = 2
SSM_STATE = 64
SSM_CONV_CH = WIDTH + 2 * SSM_GROUPS * SSM_STATE
RWKV_RANK = 64
RWKV_SHIFT_CH = 3 * WIDTH + 2 * RWKV_RANK
RWKV_LN_EPS = 64e-5
RWKV_CHUNK = 64
ATTN_GROUPS = ((128, 1), (512, 4), (2048, 16))
ATTN_HEADS = 4
ATTN_W = ATTN_HEADS * HEAD_DIM
ATTN_J = 128
MLSTM_HEAD_GROUP = 8
ATTN_UNROLL = 2
N_BRANCH = 4
LANES = 128
VMEM_LIMIT = 52 * 1024 * 1024

IN_SEGMENTS = (
    ("ssm_z", WIDTH), ("ssm_xbc", SSM_CONV_CH), ("ssm_dt", N_HEADS),
    ("rwkv_z", WIDTH), ("rwkv_shift", RWKV_SHIFT_CH),
    ("mlstm_z", WIDTH), ("mlstm_qk", 2 * WIDTH), ("mlstm_v", WIDTH),
    ("mlstm_o", WIDTH), ("mlstm_if", 2 * N_HEADS),
    ("attn_z", ATTN_W), ("attn_qkv", 9 * ATTN_W),
    ("merge", N_BRANCH * D_MODEL),
)

NN = (((1,), (0,)), ((), ()))
NT = (((1,), (1,)), ((), ()))
TN = (((0,), (0,)), ((), ()))


def _mm(a, b, dims, prec):
    if prec is None:
        a, b = a.astype(bf16), b.astype(bf16)
    return lax.dot_general(a, b, dims, preferred_element_type=f32, precision=prec)


def _dot(a, b, prec=None):
    return _mm(a, b, NN, prec)


def _dot_nt(a, b, prec=None):
    return _mm(a, b, NT, prec)


def _dot_tn(a, b, transpose_on_mxu=False):
    if not transpose_on_mxu:
        return _mm(a, b, TN, None)
    n = a.shape[1]
    eye = (_iota2((n, n), 0) == _iota2((n, n), 1)).astype(bf16)
    a_t = lax.dot_general(eye, a.astype(bf16), NT, preferred_element_type=f32).astype(bf16)
    return lax.dot_general(a_t, b.astype(bf16), NN, preferred_element_type=f32)


def _sigmoid(x):
    return 0.5 * jnp.tanh(0.5 * x) + 0.5


def _silu(x):
    return x * _sigmoid(x)


def _softplus(x):
    return jnp.maximum(x, 0.0) + jnp.log1p(jnp.exp(-jnp.abs(x)))


def _params(*sem):
    return pltpu.CompilerParams(dimension_semantics=sem, vmem_limit_bytes=VMEM_LIMIT)


def _iota2(shape, dim):
    return lax.broadcasted_iota(jnp.int32, shape, dim)


def _running_sum(x, c):
    tri = (_iota2((c, c), 0) >= _iota2((c, c), 1)).astype(f32)
    return _dot(tri, x, HI)


def _vec_layout(fields):
    out, off = {}, 0
    for name, size in fields:
        padded = -(-size // LANES) * LANES
        out[name] = (off, padded)
        off += padded
    return out, off


def _pack_vecs(layout, total, vecs):
    parts = []
    for name, (_, padded) in layout.items():
        v = vecs[name].reshape(DEPTH, -1)
        parts.append(jnp.pad(v, ((0, 0), (0, padded - v.shape[1]))))
    packed = jnp.concatenate(parts, axis=1)
    assert packed.shape[1] == total
    return packed[:, None, :]


def _vec_reader(ref, layout):
    return lambda name: ref[0, :, layout[name][0]:layout[name][0] + layout[name][1]]


SSM_VEC, SSM_VEC_N = _vec_layout((("conv_b", SSM_CONV_CH), ("dt_bias", N_HEADS), ("a_log", N_HEADS),
                                  ("d", N_HEADS), ("norm_w", WIDTH)))
RWKV_VEC, RWKV_VEC_N = _vec_layout((("mu", RWKV_SHIFT_CH), ("w0", WIDTH), ("a0", WIDTH), ("k_k", WIDTH),
                                    ("k_a", WIDTH), ("r_k", WIDTH), ("ln_w", WIDTH), ("ln_b", WIDTH)))
MLSTM_VEC, MLSTM_VEC_N = _vec_layout((("conv_b", 2 * WIDTH), ("gate_b", 2 * N_HEADS), ("norm_w", WIDTH)))
ATTN_VEC, ATTN_VEC_N = _vec_layout((("q_norm", ATTN_W), ("k_norm", ATTN_W)))


def _layer_spec(shape, layer, grid_rank):
    zeros = (0,) * (len(shape) - 1)
    if grid_rank == 1:
        return pl.BlockSpec((1,) + tuple(shape[1:]), lambda i: (layer,) + zeros)
    return pl.BlockSpec((1,) + tuple(shape[1:]), lambda i, j: (layer,) + zeros)


def _ada_kernel(c_ref, w_ref, b_ref, o_ref):
    o_ref[...] = _dot(_silu(c_ref[...]), w_ref[0], HI) + b_ref[0]


def _ada(c, w, b3, layer):
    n = c.shape[0]
    return pl.pallas_call(
        _ada_kernel,
        grid=(3,),
        in_specs=[pl.BlockSpec((n, D_MODEL), lambda j: (0, 0)),
                  pl.BlockSpec((1, D_MODEL, D_MODEL), lambda j: (layer, 0, j)),
                  pl.BlockSpec((1, 1, D_MODEL), lambda j: (layer, 0, j))],
        out_specs=pl.BlockSpec((n, D_MODEL), lambda j: (0, j)),
        out_shape=jax.ShapeDtypeStruct((n, 3 * D_MODEL), f32),
        compiler_params=_params("parallel"),
        name="ada_mod",
    )(c, w, b3)


def _norm_kernel(x_ref, nw_ref, sc_ref, sh_ref, o_ref):
    x = x_ref[0]
    r = x * lax.rsqrt(jnp.mean(x * x, axis=-1, keepdims=True) + NORM_EPS)
    o_ref[0] = (r * nw_ref[0] * (1.0 + sc_ref[0]) + sh_ref[0]).astype(o_ref.dtype)


def _norm(x3, nw3, layer, sc3, sh3, tl):
    b, l, d = x3.shape
    per_row = sc3.shape[1] != 1
    mod_spec = pl.BlockSpec((1, tl if per_row else 1, d), (lambda i, j: (i, j, 0)) if per_row else (lambda i, j: (i, 0, 0)))
    return pl.pallas_call(
        _norm_kernel,
        grid=(b, l // tl),
        in_specs=[pl.BlockSpec((1, tl, d), lambda i, j: (i, j, 0)),
                  _layer_spec(nw3.shape, layer, 2), mod_spec, mod_spec],
        out_specs=pl.BlockSpec((1, tl, d), lambda i, j: (i, j, 0)),
        out_shape=jax.ShapeDtypeStruct((b, l, d), bf16),
        compiler_params=_params("parallel", "parallel"),
        name="mod_rmsnorm",
    )(x3, nw3, sc3, sh3)


def _mm_kernel(x_ref, w_ref, o_ref):
    o_ref[...] = jnp.dot(x_ref[...], w_ref[0], preferred_element_type=f32).astype(o_ref.dtype)


def _matmul(x, w3, layer, tm, tn, out_dtype):
    n, k = x.shape
    m = w3.shape[2]
    return pl.pallas_call(
        _mm_kernel,
        grid=(m // tn, n // tm),
        in_specs=[pl.BlockSpec((tm, k), lambda j, i: (i, 0)),
                  pl.BlockSpec((1, k, tn), lambda j, i: (layer, 0, j))],
        out_specs=pl.BlockSpec((tm, tn), lambda j, i: (i, j)),
        out_shape=jax.ShapeDtypeStruct((n, m), out_dtype),
        compiler_params=_params("parallel", "parallel"),
        name="in_proj",
    )(x, w3)


def _causal_conv(ext_scr, u, c, cw_ref, bias):
    ext_scr[8:8 + c, :] = u
    acc = bias
    for i in range(CONV_W):
        acc = acc + ext_scr[5 + i:5 + i + c, :] * cw_ref[0, i:i + 1, :]
    tail = ext_scr[c + 5:c + 8, :]
    ext_scr[5:8, :] = tail
    return acc, tail


def _interleave(gens):
    gens = list(gens)
    while gens:
        alive = []
        for g in gens:
            try:
                next(g)
                alive.append(g)
            except StopIteration:
                pass
        gens = alive


def _ssm_kernel(proj_ref, conv0_ref, h0_ref, cw_ref, vec_ref, out_ref, hn_ref, convn_ref, ext_scr, y_scr, *, c, nb):
    vec = _vec_reader(vec_ref, SSM_VEC)

    @pl.when(pl.program_id(1) == 0)
    def _():
        for bi in range(nb):
            ext_scr[bi, 5:8, :] = conv0_ref[0, bi]
            hn_ref[bi] = h0_ref[0, bi]

    causal = _iota2((c, c), 0) >= _iota2((c, c), 1)
    heads = range(N_HEADS)
    per_group = N_HEADS // SSM_GROUPS
    sls = [slice(h * HEAD_DIM, (h + 1) * HEAD_DIM) for h in heads]

    def elem(bi):
        proj, out, hn, y_acc = proj_ref.at[bi], out_ref.at[bi], hn_ref.at[bi], y_scr.at[bi]
        u = proj[:, WIDTH:WIDTH + SSM_CONV_CH].astype(f32)
        acc, tail = _causal_conv(ext_scr.at[bi], u, c, cw_ref, vec("conv_b"))
        convn_ref[bi] = tail
        xbc = _silu(acc)
        xs = xbc[:, :WIDTH]
        bm = xbc[:, WIDTH:WIDTH + LANES]
        cm = xbc[:, WIDTH + LANES:]
        dt = _softplus(proj[:, WIDTH + SSM_CONV_CH:].astype(f32) + vec("dt_bias"))
        da = dt * (-jnp.exp(vec("a_log")))
        yield
        cs = _running_sum(da, c)
        cs_t = cs.T
        dt_t = dt.T
        cs_end = cs[c - 1:c, :]
        dsk = vec("d")
        bgs = [bm[:, g * SSM_STATE:(g + 1) * SSM_STATE] for g in range(SSM_GROUPS)]
        cgs = [cm[:, g * SSM_STATE:(g + 1) * SSM_STATE] for g in range(SSM_GROUPS)]
        yield
        gmats = [_dot_nt(cgs[g], bgs[g]) for g in range(SSM_GROUPS)]
        cs_cols = [cs[:, h:h + 1] for h in heads]
        yield
        lmats = [gmats[h // per_group] * jnp.exp(jnp.where(causal, cs_cols[h] - cs_t[h:h + 1, :], -jnp.inf))
                 * dt_t[h:h + 1, :] for h in heads]
        xhs = [xs[:, sl] for sl in sls]
        hprevs = [hn[h] for h in heads]
        yield
        y_in = [_dot(lmats[h], xhs[h]) for h in heads]
        y_st = [_dot_nt(cgs[h // per_group], hprevs[h]) for h in heads]
        yield
        for h in heads:
            y_acc[:, sls[h]] = y_in[h] + y_st[h] * jnp.exp(cs_cols[h]) + xhs[h] * dsk[:, h:h + 1]
        ces = [cs_end[:, h:h + 1] for h in heads]
        upd = [_dot_tn(xhs[h] * (jnp.exp(ces[h] - cs_cols[h]) * dt[:, h:h + 1]), bgs[h // per_group],
                       transpose_on_mxu=True) for h in heads]
        yield
        for h in heads:
            hn[h] = hprevs[h] * jnp.exp(ces[h]) + upd[h]
        gy = y_acc[...] * _silu(proj[:, :WIDTH].astype(f32))
        nw = vec("norm_w")
        gw = WIDTH // SSM_GROUPS
        for g in range(SSM_GROUPS):
            part = gy[:, g * gw:(g + 1) * gw]
            ms = jnp.mean(part * part, axis=-1, keepdims=True)
            out[:, g * gw:(g + 1) * gw] = part * lax.rsqrt(ms + NORM_EPS) * nw[:, g * gw:(g + 1) * gw]

    _interleave(elem(bi) for bi in range(nb))


def _state_spec(shape, sidx, nb):
    zeros = (0,) * (len(shape) - 2)
    return pl.BlockSpec((1, nb) + tuple(shape[2:]), lambda i, j: (sidx, i) + zeros)


def _batch_spec(nb, *dims):
    zeros = (0,) * len(dims)
    return pl.BlockSpec((nb,) + dims, lambda i, j: (i,) + zeros)


def _ssm(proj, conv0, h0, sidx, P, layer, c, nb):
    b, l, wp = proj.shape
    return pl.pallas_call(
        functools.partial(_ssm_kernel, c=c, nb=nb),
        grid=(b // nb, l // c),
        in_specs=[pl.BlockSpec((nb, c, wp), lambda i, j: (i, j, 0)),
                  _state_spec(conv0.shape, sidx, nb), _state_spec(h0.shape, sidx, nb),
                  _layer_spec(P["ssm_conv_w"].shape, layer, 2), _layer_spec(P["ssm_vec"].shape, layer, 2)],
        out_specs=[pl.BlockSpec((nb, c, WIDTH), lambda i, j: (i, j, 0)),
                   _batch_spec(nb, N_HEADS, HEAD_DIM, SSM_STATE), _batch_spec(nb, CONV_W - 1, SSM_CONV_CH)],
        out_shape=[jax.ShapeDtypeStruct((b, l, WIDTH), f32),
                   jax.ShapeDtypeStruct((b, N_HEADS, HEAD_DIM, SSM_STATE), f32),
                   jax.ShapeDtypeStruct((b, CONV_W - 1, SSM_CONV_CH), f32)],
        scratch_shapes=[pltpu.VMEM((nb, c + 8, SSM_CONV_CH), f32), pltpu.VMEM((nb, c, WIDTH), f32)],
        compiler_params=_params("parallel", "arbitrary"),
        name="ssd_branch",
    )(proj, conv0, h0, P["ssm_conv_w"], P["ssm_vec"])


def _rwkv_chunk_kernel(proj_ref, shift0_ref, s0_ref, vec_ref, w2_ref, a2_ref,
                       out_ref, sn_ref, shiftn_ref, ext_scr, *, c, rows, nb):
    vec = _vec_reader(vec_ref, RWKV_VEC)

    @pl.when(pl.program_id(1) == 0)
    def _():
        for bi in range(nb):
            ext_scr[bi, 7:8, :] = shift0_ref[0, bi]
            sn_ref[bi] = s0_ref[0, bi]

    row2 = _iota2((c, 2 * c), 0)
    col2 = _iota2((c, 2 * c), 1)
    col2 = jnp.where(col2 >= c, col2 - c, col2)
    strict2 = row2 > col2
    incl2 = row2 >= col2
    n_double = (c - 1).bit_length()
    heads = range(N_HEADS)
    sls = [slice(h * HEAD_DIM, (h + 1) * HEAD_DIM) for h in heads]

    def elem(bi):
        proj, out, sn, ext = proj_ref.at[bi], out_ref.at[bi], sn_ref.at[bi], ext_scr.at[bi]
        u = proj[:, WIDTH:].astype(f32)
        ext[8:8 + rows, :] = u
        sh = u + (ext[7:7 + rows, :] - u) * vec("mu")
        last = ext[rows + 7:rows + 8, :]
        ext[7:8, :] = last
        shiftn_ref[bi] = last
        r_all = sh[:, :WIDTH]
        k = sh[:, WIDTH:2 * WIDTH]
        v_all = sh[:, 2 * WIDTH:3 * WIDTH]
        w_lo = sh[:, 3 * WIDTH:3 * WIDTH + RWKV_RANK]
        a_lo = sh[:, 3 * WIDTH + RWKV_RANK:]
        yield
        w_log = -_softplus(-(vec("w0") + _dot(jnp.tanh(w_lo), w2_ref[0]))) - 0.5
        lw_all = -jnp.exp(w_log)
        a_in_all = _sigmoid(vec("a0") + _dot(a_lo, a2_ref[0]))
        kk_all = k * vec("k_k")
        k2_all = k * (1.0 + (a_in_all - 1.0) * vec("k_a"))
        rk, lnw, lnb = vec("r_k"), vec("ln_w"), vec("ln_b")
        yield
        for c0 in range(0, rows, c):
            cr = slice(c0, c0 + c)
            r, v, lw, a_in, kk, k2 = r_all[cr], v_all[cr], lw_all[cr], a_in_all[cr], kk_all[cr], k2_all[cr]
            cl = _running_sum(lw, c)
            cl_end = cl[c - 1:c, :]
            g_fwd = jnp.exp(cl)
            g_inv = jnp.exp(-cl)
            g_rem = jnp.exp(cl_end - cl)
            g_prev = jnp.exp(cl - lw)
            g_end = jnp.exp(cl_end)
            yield
            s0s = [sn[h] for h in heads]
            bhs, ars, bks = [], [], []
            for sl in sls:
                kkh = kk[:, sl]
                kkn = kkh * (1.0 / jnp.maximum(jnp.sqrt(jnp.sum(kkh * kkh, axis=-1, keepdims=True)), 1e-12))
                bhs.append(kkn * a_in[:, sl])
                ars.append(jnp.concatenate([-kkn * g_prev[:, sl], r[:, sl] * g_fwd[:, sl]], axis=0))
                bks.append(jnp.concatenate([bhs[-1] * g_inv[:, sl], k2[:, sl] * g_inv[:, sl]], axis=0))
            yield
            m4s = [_dot_nt(ars[h], bks[h]) for h in heads]
            ahs = [_dot_nt(ars[h], s0s[h]) for h in heads]
            yield
            tops = [jnp.where(strict2, m4[:c, :], 0.0) for m4 in m4s]
            bots = [jnp.where(incl2, m4[c:, :], 0.0) for m4 in m4s]
            amats = [top[:, :c] for top in tops]
            xs = [ahs[h][:c] + _dot(tops[h][:, c:], v[:, sls[h]]) for h in heads]
            yield
            for i in range(n_double):
                xs = [xs[h] + _dot(amats[h], xs[h]) for h in heads]
                if i + 1 < n_double:
                    amats = [_dot(amats[h], amats[h]) for h in heads]
                yield
            pvs = [jnp.concatenate([xs[h], v[:, sls[h]]], axis=0) for h in heads]
            ys = [ahs[h][c:] + _dot(bots[h], pvs[h]) for h in heads]
            upd = [_dot_tn(pvs[h], jnp.concatenate([bhs[h] * g_rem[:, sls[h]], k2[:, sls[h]] * g_rem[:, sls[h]]],
                                                   axis=0)) for h in heads]
            yield
            for h in heads:
                sl = sls[h]
                sn[h] = s0s[h] * g_end[:, sl] + upd[h]
                y = ys[h]
                mu = jnp.mean(y, axis=-1, keepdims=True)
                var = jnp.mean(jnp.square(y - mu), axis=-1, keepdims=True)
                ln = (y - mu) * lax.rsqrt(var + RWKV_LN_EPS) * lnw[:, sl] + lnb[:, sl]
                bonus = jnp.sum(r[:, sl] * k2[:, sl] * rk[:, sl], axis=-1, keepdims=True) * v[:, sl]
                out[cr, sl] = (ln + bonus) * _silu(proj[cr, sl].astype(f32))
            yield

    _interleave(elem(bi) for bi in range(nb))


def _rwkv(proj, shift0, s0, sidx, P, layer, c, rows, nb):
    b, l, wp = proj.shape
    return pl.pallas_call(
        functools.partial(_rwkv_chunk_kernel, c=c, rows=rows, nb=nb),
        grid=(b // nb, l // rows),
        in_specs=[pl.BlockSpec((nb, rows, wp), lambda i, j: (i, j, 0)),
                  _state_spec(shift0.shape, sidx, nb), _state_spec(s0.shape, sidx, nb),
                  _layer_spec(P["rwkv_vec"].shape, layer, 2),
                  _layer_spec(P["rwkv_w2"].shape, layer, 2), _layer_spec(P["rwkv_a2"].shape, layer, 2)],
        out_specs=[pl.BlockSpec((nb, rows, WIDTH), lambda i, j: (i, j, 0)),
                   _batch_spec(nb, N_HEADS, HEAD_DIM, HEAD_DIM), _batch_spec(nb, 1, RWKV_SHIFT_CH)],
        out_shape=[jax.ShapeDtypeStruct((b, l, WIDTH), f32),
                   jax.ShapeDtypeStruct((b, N_HEADS, HEAD_DIM, HEAD_DIM), f32),
                   jax.ShapeDtypeStruct((b, 1, RWKV_SHIFT_CH), f32)],
        scratch_shapes=[pltpu.VMEM((nb, rows + 8, RWKV_SHIFT_CH), f32)],
        compiler_params=_params("parallel", "arbitrary"),
        name="rwkv7_chunked",
    )(proj, shift0, s0, P["rwkv_vec"], P["rwkv_w2"], P["rwkv_a2"])


def _mlstm_kernel(proj_ref, conv0_ref, c0_ref, n0_ref, m0_ref, cw_ref, vec_ref,
                  out_ref, cn_ref, nn_ref, mn_ref, convn_ref, ext_scr, h_scr, *, c, nb):
    vec = _vec_reader(vec_ref, MLSTM_VEC)

    @pl.when(pl.program_id(1) == 0)
    def _():
        for bi in range(nb):
            ext_scr[bi, 5:8, :] = conv0_ref[0, bi]
            cn_ref[bi] = c0_ref[0, bi]
            nn_ref[bi] = n0_ref[0, bi]
            mn_ref[bi] = m0_ref[0, bi]

    causal = _iota2((c, c), 0) >= _iota2((c, c), 1)
    lane = _iota2((1, LANES), 1)
    heads = range(N_HEADS)
    sls = [slice(h * HEAD_DIM, (h + 1) * HEAD_DIM) for h in heads]

    def elem(bi):
        proj, out, cn, nn, h_acc = proj_ref.at[bi], out_ref.at[bi], cn_ref.at[bi], nn_ref.at[bi], h_scr.at[bi]
        u = proj[:, WIDTH:3 * WIDTH].astype(f32)
        acc, tail = _causal_conv(ext_scr.at[bi], u, c, cw_ref, vec("conv_b"))
        convn_ref[bi] = tail
        qk = _silu(acc)
        gates = proj[:, 5 * WIDTH:].astype(f32) + vec("gate_b")
        lf = -_softplus(-gates)
        yield
        bc = _running_sum(lf, c)
        bc_t = bc.T
        ig_t = gates.T
        m_prev = mn_ref[bi]
        yield
        m_out = jnp.zeros((1, LANES), f32)
        for grp in range(0, N_HEADS, MLSTM_HEAD_GROUP):
            heads = range(grp, grp + MLSTM_HEAD_GROUP)
            b_cols = {h: bc[:, N_HEADS + h:N_HEADS + h + 1] for h in heads}
            dmats = {h: jnp.where(causal, b_cols[h] - bc_t[N_HEADS + h:N_HEADS + h + 1, :] + ig_t[h:h + 1, :],
                                  -jnp.inf) for h in heads}
            m0s = {h: m_prev[:, h:h + 1] for h in heads}
            gcols = {h: b_cols[h] + m0s[h] for h in heads}
            mts = {h: jnp.maximum(gcols[h], jnp.max(dmats[h], axis=-1, keepdims=True)) for h in heads}
            yield
            qhs = {h: qk[:, sls[h]] for h in heads}
            khs = {h: qk[:, WIDTH + h * HEAD_DIM:WIDTH + (h + 1) * HEAD_DIM] * HEAD_DIM ** -0.5 for h in heads}
            vhs = {h: proj[:, 3 * WIDTH + h * HEAD_DIM:3 * WIDTH + (h + 1) * HEAD_DIM].astype(f32) for h in heads}
            cprevs = {h: cn[h] for h in heads}
            nprevs = {h: nn[h:h + 1, :] for h in heads}
            yield
            sws = {h: jnp.exp(dmats[h] - mts[h]) * _dot_nt(qhs[h], khs[h]) for h in heads}
            wis = {h: jnp.exp(gcols[h] - mts[h]) for h in heads}
            qcs = {h: _dot(qhs[h], cprevs[h]) for h in heads}
            yield
            nums = {h: _dot(sws[h], vhs[h]) + wis[h] * qcs[h] for h in heads}
            yield
            for h in heads:
                den = (jnp.sum(sws[h], axis=-1, keepdims=True)
                       + wis[h] * jnp.sum(qhs[h] * nprevs[h], axis=-1, keepdims=True))
                h_acc[:, sls[h]] = nums[h] * (1.0 / jnp.maximum(jnp.abs(den), jnp.exp(-mts[h])))
            m_news = {h: mts[h][c - 1:c, :] for h in heads}
            b_ends = {h: b_cols[h][c - 1:c, :] for h in heads}
            kws = {h: khs[h] * jnp.exp(b_ends[h] - b_cols[h] + gates[:, h:h + 1] - m_news[h]) for h in heads}
            yield
            kvs = {h: _dot_tn(kws[h], vhs[h]) for h in heads}
            yield
            for h in heads:
                dec = jnp.exp(b_ends[h] + m0s[h] - m_news[h])
                cn[h] = dec * cprevs[h] + kvs[h]
                nn[h:h + 1, :] = dec * nprevs[h] + jnp.sum(kws[h], axis=0, keepdims=True)
                m_out = jnp.where(lane == h, m_news[h], m_out)
        mn_ref[bi] = m_out
        nw = vec("norm_w")
        for h in range(N_HEADS):
            sl = sls[h]
            o_gate = proj[:, 4 * WIDTH + h * HEAD_DIM:4 * WIDTH + (h + 1) * HEAD_DIM].astype(f32)
            hm = _sigmoid(o_gate) * h_acc[:, sl]
            ms = jnp.mean(hm * hm, axis=-1, keepdims=True)
            out[:, sl] = hm * lax.rsqrt(ms + NORM_EPS) * nw[:, sl] * _silu(proj[:, sl].astype(f32))

    _interleave(elem(bi) for bi in range(nb))


def _mlstm(proj, conv0, c0, n0, m0, sidx, P, layer, c, nb):
    b, l, wp = proj.shape
    outs = pl.pallas_call(
        functools.partial(_mlstm_kernel, c=c, nb=nb),
        grid=(b // nb, l // c),
        in_specs=[pl.BlockSpec((nb, c, wp), lambda i, j: (i, j, 0)),
                  _state_spec(conv0.shape, sidx, nb), _state_spec(c0.shape, sidx, nb),
                  _state_spec(n0.shape, sidx, nb), _state_spec(m0.shape, sidx, nb),
                  _layer_spec(P["mlstm_conv_w"].shape, layer, 2), _layer_spec(P["mlstm_vec"].shape, layer, 2)],
        out_specs=[pl.BlockSpec((nb, c, WIDTH), lambda i, j: (i, j, 0)),
                   _batch_spec(nb, N_HEADS, HEAD_DIM, HEAD_DIM), _batch_spec(nb, N_HEADS, HEAD_DIM),
                   _batch_spec(nb, 1, LANES), _batch_spec(nb, CONV_W - 1, 2 * WIDTH)],
        out_shape=[jax.ShapeDtypeStruct((b, l, WIDTH), f32),
                   jax.ShapeDtypeStruct((b, N_HEADS, HEAD_DIM, HEAD_DIM), f32),
                   jax.ShapeDtypeStruct((b, N_HEADS, HEAD_DIM), f32),
                   jax.ShapeDtypeStruct((b, 1, LANES), f32),
                   jax.ShapeDtypeStruct((b, CONV_W - 1, 2 * WIDTH), f32)],
        scratch_shapes=[pltpu.VMEM((nb, c + 8, 2 * WIDTH), f32), pltpu.VMEM((nb, c, WIDTH), f32)],
        compiler_params=_params("parallel", "arbitrary"),
        name="mlstm_branch",
    )(proj, conv0, c0, n0, m0, P["mlstm_conv_w"], P["mlstm_vec"])
    out, cn, nn, mn, convn = outs
    return out, cn, nn, mn[:, 0, :N_HEADS], convn


def _head_mean_matrix():
    hid = jnp.arange(ATTN_W) // HEAD_DIM
    return ((hid[:, None] == hid[None, :]).astype(f32) / HEAD_DIM).astype(bf16)


def _qk_norm(x, hm_ref, w):
    sq = x * x
    hi = sq.astype(bf16)
    lo = (sq - hi.astype(f32)).astype(bf16)
    ms = _dot(hi, hm_ref[...]) + _dot(lo, hm_ref[...])
    return x * lax.rsqrt(ms + NORM_EPS) * w


def _attn_prompt_kernel(q_ref, k_ref, v_ref, z_ref, hm_ref, vec_ref, out_ref, kv0_ref, kv1_ref, kv2_ref,
                        qn_scr, kn_scr, vv_scr, acc_scr, m_scr, d_scr, to_scr, tl_scr, *, l):
    vec = _vec_reader(vec_ref, ATTN_VEC)
    g = pl.program_id(1)
    rb = 256
    n_pairs = ATTN_W // LANES

    def norm_body(i, carry):
        rows = pl.ds(pl.multiple_of(i * rb, rb), rb)
        qn = _qk_norm(q_ref[0, rows, :].astype(f32), hm_ref, vec("q_norm"))
        kn = _qk_norm(k_ref[0, rows, :].astype(f32), hm_ref, vec("k_norm"))
        v = v_ref[0, rows, :].astype(f32)
        for p in range(n_pairs):
            qn_scr[p, rows, :] = qn[:, p * LANES:(p + 1) * LANES]
            kn_scr[p, rows, :] = kn[:, p * LANES:(p + 1) * LANES]
            vv_scr[p, rows, :] = v[:, p * LANES:(p + 1) * LANES]
        return carry

    lax.fori_loop(0, l // rb, norm_body, 0)

    qb = ATTN_J
    qi = _iota2((qb, qb), 0)
    ki = _iota2((qb, qb), 1)
    scale = HEAD_DIM ** -0.5

    def run_group(dil, first):
        n_iter = l // qb

        def block(i, slot):
            r = i % dil
            blk = i // dil
            u0 = blk * qb
            rows_c = pl.ds(u0 * dil + r, qb, stride=dil)
            rows_p = pl.ds(jnp.maximum(u0 - qb, 0) * dil + r, qb, stride=dil)
            mask_c = ki <= qi
            mask_p = (ki >= qi) & (blk > 0)
            hsl = [slice(hh * HEAD_DIM, (hh + 1) * HEAD_DIM) for hh in range(LANES // HEAD_DIM)]
            heads = [(p, sl) for p in range(n_pairs) for sl in hsl]
            qv = [qn_scr[p, rows_c, :] for p in range(n_pairs)]
            kc = [kn_scr[p, rows_c, :] for p in range(n_pairs)]
            vc = [vv_scr[p, rows_c, :] for p in range(n_pairs)]
            kp = [kn_scr[p, rows_p, :] for p in range(n_pairs)]
            vp = [vv_scr[p, rows_p, :] for p in range(n_pairs)]
            yield
            scs = [jnp.where(mask_c, _dot_nt(qv[p][:, sl], kc[p][:, sl]) * scale, -jnp.inf) for p, sl in heads]
            sps = [jnp.where(mask_p, _dot_nt(qv[p][:, sl], kp[p][:, sl]) * scale, -jnp.inf) for p, sl in heads]
            yield
            ms = [jnp.max(jnp.maximum(sc, sp), axis=-1, keepdims=True) for sc, sp in zip(scs, sps)]
            pcs = [jnp.exp(sc - m) for sc, m in zip(scs, ms)]
            pps = [jnp.exp(sp - m) for sp, m in zip(sps, ms)]
            dens = [jnp.sum(pc + pp, axis=-1, keepdims=True) for pc, pp in zip(pcs, pps)]
            yield
            for (p, sl), pc, pp, m, den in zip(heads, pcs, pps, ms, dens):
                to_scr[slot, p, :, sl] = (_dot(pc, vc[p][:, sl]) + _dot(pp, vp[p][:, sl])) * (1.0 / den)
                tl_scr[slot, p, :, sl] = jnp.broadcast_to(m + jnp.log(den), (qb, HEAD_DIM))
            yield
            for p in range(n_pairs):
                o = to_scr[slot, p]
                lse = tl_scr[slot, p]
                if first:
                    acc_scr[p, rows_c, :] = o
                    m_scr[p, rows_c, :] = lse
                    d_scr[p, rows_c, :] = jnp.ones_like(lse)
                else:
                    m_old = m_scr[p, rows_c, :]
                    m_new = jnp.maximum(m_old, lse)
                    a_old = jnp.exp(m_old - m_new)
                    a_new = jnp.exp(lse - m_new)
                    acc_scr[p, rows_c, :] = acc_scr[p, rows_c, :] * a_old + o * a_new
                    d_scr[p, rows_c, :] = d_scr[p, rows_c, :] * a_old + a_new
                    m_scr[p, rows_c, :] = m_new

        def body(i, carry):
            _interleave(block(i * ATTN_UNROLL + s, s) for s in range(ATTN_UNROLL))
            return carry

        lax.fori_loop(0, n_iter // ATTN_UNROLL, body, 0)

    def save_window(kv_ref):
        w = kv_ref.shape[1]
        wb = min(w, rb)
        for r0 in range(0, w, wb):
            src = slice(l - w + r0, l - w + r0 + wb)
            for p in range(n_pairs):
                kv_ref[0, r0:r0 + wb, p * LANES:(p + 1) * LANES] = kn_scr[p, src, :]
                kv_ref[0, r0:r0 + wb, ATTN_W + p * LANES:ATTN_W + (p + 1) * LANES] = vv_scr[p, src, :]

    def group_step(gi, dil):
        save_window((kv0_ref, kv1_ref, kv2_ref)[gi])
        run_group(dil, gi == 0)

    for gi, (_, dil) in enumerate(ATTN_GROUPS):
        pl.when(g == gi)(functools.partial(group_step, gi, dil))

    @pl.when(g == len(ATTN_GROUPS) - 1)
    def _():
        def fin_body(i, carry):
            rows = pl.ds(pl.multiple_of(i * rb, rb), rb)
            for p in range(n_pairs):
                lanes = slice(p * LANES, (p + 1) * LANES)
                z = z_ref[0, rows, lanes].astype(f32)
                out_ref[0, rows, lanes] = acc_scr[p, rows, :] / d_scr[p, rows, :] * _silu(z)
            return carry

        lax.fori_loop(0, l // rb, fin_body, 0)


def _attn_prompt(proj, P, layer):
    b, l, _ = proj.shape
    ng = len(ATTN_GROUPS)
    col = lambda base: pl.BlockSpec((1, l, ATTN_W), lambda i, g: (i, 0, base + g))
    pair_scr = pltpu.VMEM((ATTN_W // LANES, l, LANES), f32)
    windows = [min(w, l) for w, _ in ATTN_GROUPS]
    return pl.pallas_call(
        functools.partial(_attn_prompt_kernel, l=l),
        grid=(b, ng),
        in_specs=[col(1), col(1 + ng), col(1 + 2 * ng),
                  pl.BlockSpec((1, l, ATTN_W), lambda i, g: (i, 0, 0)),
                  pl.BlockSpec((ATTN_W, ATTN_W), lambda i, g: (0, 0)),
                  _layer_spec(P["attn_vec"].shape, layer, 2)],
        out_specs=[pl.BlockSpec((1, l, ATTN_W), lambda i, g: (i, 0, 0))]
                  + [pl.BlockSpec((1, w, 2 * ATTN_W), lambda i, g: (i, 0, 0)) for w in windows],
        out_shape=[jax.ShapeDtypeStruct((b, l, ATTN_W), f32)]
                  + [jax.ShapeDtypeStruct((b, w, 2 * ATTN_W), f32) for w in windows],
        scratch_shapes=[pair_scr] * 6 + [pltpu.VMEM((ATTN_UNROLL, ATTN_W // LANES, ATTN_J, LANES), f32)] * 2,
        compiler_params=_params("parallel", "arbitrary"),
        name="dilated_attn_prompt",
    )(proj, proj, proj, proj, P["head_mean"], P["attn_vec"])


def _attn_step_kernel(x_ref, c0_ref, c1_ref, c2_ref, hm_ref, vec_ref, out_ref, kv_ref, *, l, nb):
    vec = _vec_reader(vec_ref, ATTN_VEC)
    ng = len(ATTN_GROUPS)
    scale = HEAD_DIM ** -0.5
    c_refs = (c0_ref, c1_ref, c2_ref)
    hsl = [slice(h * HEAD_DIM, (h + 1) * HEAD_DIM) for h in range(ATTN_HEADS)]
    masks_p, masks_n = [], []
    for gi, (window, dil) in enumerate(ATTN_GROUPS):
        w = c_refs[gi].shape[-1]
        jmax = window // dil
        shift = dil.bit_length() - 1
        dist_p = w + _iota2((l, w), 0) - _iota2((l, w), 1)
        masks_p.append(((dist_p & (dil - 1)) == 0) & ((dist_p >> shift) <= jmax))
        dist_n = _iota2((l, l), 0) - _iota2((l, l), 1)
        masks_n.append((dist_n >= 0) & ((dist_n & (dil - 1)) == 0) & ((dist_n >> shift) <= jmax))
    gh = [(gi, h) for gi in range(ng) for h in range(ATTN_HEADS)]

    def elem(bi):
        x = x_ref[bi]
        qns, kns, vs = [], [], []
        for gi in range(ng):
            qns.append(_qk_norm(x[:, (1 + gi) * ATTN_W:(2 + gi) * ATTN_W], hm_ref, vec("q_norm")))
            kns.append(_qk_norm(x[:, (1 + ng + gi) * ATTN_W:(2 + ng + gi) * ATTN_W], hm_ref, vec("k_norm")))
            vs.append(x[:, (1 + 2 * ng + gi) * ATTN_W:(2 + 2 * ng + gi) * ATTN_W])
            kv_ref[bi, gi, :, 0:ATTN_W] = kns[gi]
            kv_ref[bi, gi, :, ATTN_W:2 * ATTN_W] = vs[gi]
        yield
        sps = [jnp.where(masks_p[gi], _dot(qns[gi][:, hsl[h]], c_refs[gi][0, bi, 0, h]) * scale, -jnp.inf)
               for gi, h in gh]
        sns = [jnp.where(masks_n[gi], _dot_nt(qns[gi][:, hsl[h]], kns[gi][:, hsl[h]]) * scale, -jnp.inf)
               for gi, h in gh]
        yield
        ms = [jnp.maximum(jnp.max(sp, axis=-1, keepdims=True), jnp.max(sn, axis=-1, keepdims=True))
              for sp, sn in zip(sps, sns)]
        pps = [jnp.exp(sp - m) for sp, m in zip(sps, ms)]
        pns = [jnp.exp(sn - m) for sn, m in zip(sns, ms)]
        dens = [jnp.sum(pp, axis=-1, keepdims=True) + jnp.sum(pn, axis=-1, keepdims=True)
                for pp, pn in zip(pps, pns)]
        yield
        og = [(_dot_nt(pp, c_refs[gi][0, bi, 1, h]) + _dot(pn, vs[gi][:, hsl[h]])) * (1.0 / den)
              for (gi, h), pp, pn, den in zip(gh, pps, pns, dens)]
        lg = [m + jnp.log(den) for m, den in zip(ms, dens)]
        yield
        outs = [og[gi * ATTN_HEADS:(gi + 1) * ATTN_HEADS] for gi in range(ng)]
        lses = [lg[gi * ATTN_HEADS:(gi + 1) * ATTN_HEADS] for gi in range(ng)]
        for h in range(ATTN_HEADS):
            sl = hsl[h]
            mx = functools.reduce(jnp.maximum, [lses[gi][h] for gi in range(ng)])
            ws = [jnp.exp(lses[gi][h] - mx) for gi in range(ng)]
            tot = functools.reduce(lambda a, b: a + b, ws)
            o = functools.reduce(lambda a, b: a + b, [outs[gi][h] * (ws[gi] / tot) for gi in range(ng)])
            out_ref[bi, :, sl] = o * _silu(x[:, sl])

    _interleave(elem(bi) for bi in range(nb))


def _attn_step(proj, caches, P, layer, nb):
    b, l, wp = proj.shape
    ng = len(ATTN_GROUPS)
    cache_specs = [pl.BlockSpec((1, nb) + c.shape[2:], lambda i: (layer, i, 0, 0, 0, 0)) for c in caches]
    return pl.pallas_call(
        functools.partial(_attn_step_kernel, l=l, nb=nb),
        grid=(b // nb,),
        in_specs=[pl.BlockSpec((nb, l, wp), lambda i: (i, 0, 0))] + cache_specs
                 + [pl.BlockSpec((ATTN_W, ATTN_W), lambda i: (0, 0)), _layer_spec(P["attn_vec"].shape, layer, 1)],
        out_specs=[pl.BlockSpec((nb, l, ATTN_W), lambda i: (i, 0, 0)),
                   pl.BlockSpec((nb, ng, l, 2 * ATTN_W), lambda i: (i, 0, 0, 0))],
        out_shape=[jax.ShapeDtypeStruct((b, l, ATTN_W), f32),
                   jax.ShapeDtypeStruct((b, ng, l, 2 * ATTN_W), f32)],
        compiler_params=_params("parallel"),
        name="dilated_attn_step",
    )(proj, *caches, P["head_mean"], P["attn_vec"])


def _merge_kernel(oa_ref, ob_ref, oc_ref, od_ref, mg_ref, x_ref, gate_ref, wb_ref, wo_ref, o_ref):
    merged = None
    start = 0
    for bi, ref in enumerate((oa_ref, ob_ref, oc_ref, od_ref)):
        width = ref.shape[-1]
        t = jnp.dot(ref[0].astype(bf16), wb_ref[0, start:start + width, :], preferred_element_type=f32)
        term = _sigmoid(mg_ref[0, :, bi * D_MODEL:(bi + 1) * D_MODEL].astype(f32)) * t
        merged = term if merged is None else merged + term
        start += width
    y = jnp.dot(merged.astype(bf16), wo_ref[0], preferred_element_type=f32)
    o_ref[0] = x_ref[0] + gate_ref[0] * y


def _merge(branches, mg3, x3, gate3, P, layer, tl):
    b, l, d = x3.shape
    per_row = gate3.shape[1] != 1
    rows = lambda w: pl.BlockSpec((1, tl, w), lambda i, j: (i, j, 0))
    gate_spec = pl.BlockSpec((1, tl if per_row else 1, d), (lambda i, j: (i, j, 0)) if per_row else (lambda i, j: (i, 0, 0)))
    return pl.pallas_call(
        _merge_kernel,
        grid=(b, l // tl),
        in_specs=[rows(o.shape[-1]) for o in branches] + [rows(N_BRANCH * d), rows(d), gate_spec,
                                                          _layer_spec(P["w_branch"].shape, layer, 2),
                                                          _layer_spec(P["w_out"].shape, layer, 2)],
        out_specs=rows(d),
        out_shape=jax.ShapeDtypeStruct((b, l, d), f32),
        compiler_params=_params("parallel", "parallel"),
        name="merge_out",
    )(*branches, mg3, x3, gate3, P["w_branch"], P["w_out"])


def _split_w_in(w_in):
    cols, start = {}, 0
    for name, size in IN_SEGMENTS:
        cols[name] = w_in[:, :, start:start + size]
        start += size
    zeros = lambda n: jnp.zeros(w_in.shape[:2] + (n,), w_in.dtype)
    groups = {
        "ssm": [cols["ssm_z"], cols["ssm_xbc"], cols["ssm_dt"], zeros(LANES - N_HEADS)],
        "rwkv": [cols["rwkv_z"], cols["rwkv_shift"]],
        "mlstm": [cols["mlstm_z"], cols["mlstm_qk"], cols["mlstm_v"], cols["mlstm_o"], cols["mlstm_if"],
                  zeros(LANES - 2 * N_HEADS)],
        "attn": [cols["attn_z"], cols["attn_qkv"]],
        "merge": [cols["merge"]],
    }
    return {k: jnp.concatenate(v, axis=2).astype(bf16) for k, v in groups.items()}


_PROJ_TN = {"ssm": 1408, "rwkv": 2176, "mlstm": 896, "attn": 1280, "merge": 1024}


def _layer(x, mod, st, sidx, caches, P, layer):
    b, l, d = x.shape
    prompt = caches is None
    shift, scale, gate = mod[:, :d], mod[:, d:2 * d], mod[:, 2 * d:]
    if prompt:
        x3, sc3, sh3, g3 = x, scale[:, None], shift[:, None], gate[:, None]
        tl, chunk, tl_merge, tm, proj_dtype = 512, 128, 512, 1024, bf16
        rwkv_rows, nb_ssd, nb_rwkv, nb_mlstm, nb_attn = 2 * RWKV_CHUNK, math.gcd(b, 4), math.gcd(b, 2), 1, 1
    else:
        rep = lambda t: jnp.repeat(t, l, axis=0)[None]
        x3, sc3, sh3, g3 = x.reshape(1, b * l, d), rep(scale), rep(shift), rep(gate)
        tl = min(256, b * l)
        chunk, tl_merge, tm, proj_dtype = l, tl, min(512, b * l), f32
        rwkv_rows, nb_ssd, nb_rwkv, nb_mlstm, nb_attn = l, math.gcd(b, 4), math.gcd(b, 4), 1, math.gcd(b, 2)
    h2 = _norm(x3, P["norm_w"], layer, sc3, sh3, tl).reshape(b * l, d)
    proj = {k: _matmul(h2, P["w_in_" + k], layer, tm, _PROJ_TN[k], proj_dtype).reshape(b, l, -1) for k in _PROJ_TN}

    out_a, ssm_new, ssm_conv_new = _ssm(proj["ssm"], st["ssm_conv"], st["ssm"], sidx, P, layer, chunk, nb_ssd)
    out_b, rwkv_new, shift_new = _rwkv(proj["rwkv"], st["rwkv_shift"], st["rwkv"], sidx, P, layer,
                                       min(RWKV_CHUNK, l), rwkv_rows, nb_rwkv)
    out_c, c_new, n_new, m_new, mconv_new = _mlstm(proj["mlstm"], st["mlstm_conv"], st["mlstm_c"], st["mlstm_n"],
                                                   st["mlstm_m"], sidx, P, layer, chunk, nb_mlstm)
    if prompt:
        out_d, *kv = _attn_prompt(proj["attn"], P, layer)
        kv_new = [t.reshape(b, t.shape[1], 2, ATTN_HEADS, HEAD_DIM) for t in kv]
    else:
        out_d, kv = _attn_step(proj["attn"], caches, P, layer, nb_attn)
        kv_new = [kv[:, gi].reshape(b, l, 2, ATTN_HEADS, HEAD_DIM) for gi in range(len(ATTN_GROUPS))]

    branches = [o.reshape(x3.shape[0], x3.shape[1], -1) for o in (out_a, out_b, out_c, out_d)]
    mg3 = proj["merge"].reshape(x3.shape[0], x3.shape[1], -1)
    x_new = _merge(branches, mg3, x3, g3, P, layer, tl_merge).reshape(b, l, d)
    new_state = {"ssm": ssm_new, "ssm_conv": ssm_conv_new, "rwkv": rwkv_new, "rwkv_shift": shift_new[:, 0],
                 "mlstm_c": c_new, "mlstm_n": n_new, "mlstm_m": m_new, "mlstm_conv": mconv_new,
                 "kv_0": kv_new[0], "kv_1": kv_new[1], "kv_2": kv_new[2]}
    return x_new, new_state


_STATE_NAMES = ("ssm", "ssm_conv", "rwkv", "rwkv_shift", "mlstm_c", "mlstm_n", "mlstm_m", "mlstm_conv",
                "kv_0", "kv_1", "kv_2")


def _prepare_params(w):
    P = {"norm_w": w["norm_w"][:, None, :], "ada_w": w["ada_w"], "ada_b": w["ada_b"][:, None, :],
         "w_branch": w["w_branch"].astype(bf16), "w_out": w["w_out"].astype(bf16),
         "ssm_conv_w": w["ssm_conv_w"], "mlstm_conv_w": w["mlstm_conv_w"],
         "rwkv_w2": w["rwkv_w2"], "rwkv_a2": w["rwkv_a2"], "head_mean": _head_mean_matrix(),
         "ssm_vec": _pack_vecs(SSM_VEC, SSM_VEC_N, {"conv_b": w["ssm_conv_b"], "dt_bias": w["ssm_dt_bias"],
                                                    "a_log": w["ssm_a_log"], "d": w["ssm_d"],
                                                    "norm_w": w["ssm_norm_w"]}),
         "rwkv_vec": _pack_vecs(RWKV_VEC, RWKV_VEC_N, {"mu": w["rwkv_mu"], "w0": w["rwkv_w0"], "a0": w["rwkv_a0"],
                                                       "k_k": w["rwkv_k_k"], "k_a": w["rwkv_k_a"],
                                                       "r_k": w["rwkv_r_k"], "ln_w": w["rwkv_ln_w"],
                                                       "ln_b": w["rwkv_ln_b"]}),
         "mlstm_vec": _pack_vecs(MLSTM_VEC, MLSTM_VEC_N, {"conv_b": w["mlstm_conv_b"], "gate_b": w["mlstm_gate_b"],
                                                          "norm_w": w["mlstm_norm_w"]}),
         "attn_vec": _pack_vecs(ATTN_VEC, ATTN_VEC_N, {"q_norm": jnp.tile(w["attn_q_norm"], (1, ATTN_HEADS)),
                                                       "k_norm": jnp.tile(w["attn_k_norm"], (1, ATTN_HEADS))})}
    for name, mat in _split_w_in(w["w_in"]).items():
        P["w_in_" + name] = mat
    return P


def _pad_heads(m):
    return jnp.pad(m, [(0, 0)] * (m.ndim - 1) + [(0, LANES - N_HEADS)])[..., None, :]


def kernel(x_prompt, x_sample, c_prompt, c_sample, state_ssm, state_ssm_conv, state_rwkv, state_rwkv_shift, state_mlstm_c, state_mlstm_n, state_mlstm_m, state_mlstm_conv, cache_kv_w128, cache_kv_w512, cache_kv_w2048, norm_w, ada_w, ada_b, w_in, w_branch, w_out, ssm_conv_w, ssm_conv_b, ssm_dt_bias, ssm_a_log, ssm_d, ssm_norm_w, rwkv_mu, rwkv_w0, rwkv_w2, rwkv_a0, rwkv_a2, rwkv_k_k, rwkv_k_a, rwkv_r_k, rwkv_ln_w, rwkv_ln_b, mlstm_conv_w, mlstm_conv_b, mlstm_gate_b, mlstm_norm_w, attn_q_norm, attn_k_norm):
    P = _prepare_params(dict(
        norm_w=norm_w, ada_w=ada_w, ada_b=ada_b, w_in=w_in, w_branch=w_branch, w_out=w_out, ssm_conv_w=ssm_conv_w,
        ssm_conv_b=ssm_conv_b, ssm_dt_bias=ssm_dt_bias, ssm_a_log=ssm_a_log, ssm_d=ssm_d, ssm_norm_w=ssm_norm_w,
        rwkv_mu=rwkv_mu, rwkv_w0=rwkv_w0, rwkv_w2=rwkv_w2, rwkv_a0=rwkv_a0, rwkv_a2=rwkv_a2, rwkv_k_k=rwkv_k_k,
        rwkv_k_a=rwkv_k_a, rwkv_r_k=rwkv_r_k, rwkv_ln_w=rwkv_ln_w, rwkv_ln_b=rwkv_ln_b, mlstm_conv_w=mlstm_conv_w,
        mlstm_conv_b=mlstm_conv_b, mlstm_gate_b=mlstm_gate_b, mlstm_norm_w=mlstm_norm_w, attn_q_norm=attn_q_norm,
        attn_k_norm=attn_k_norm))
    bp = x_prompt.shape[0]
    fresh = {"ssm": jnp.zeros((1, bp, N_HEADS, HEAD_DIM, SSM_STATE), f32),
             "ssm_conv": jnp.zeros((1, bp, CONV_W - 1, SSM_CONV_CH), f32),
             "rwkv": jnp.zeros((1, bp, N_HEADS, HEAD_DIM, HEAD_DIM), f32),
             "rwkv_shift": jnp.zeros((1, bp, 1, RWKV_SHIFT_CH), f32),
             "mlstm_c": jnp.zeros((1, bp, N_HEADS, HEAD_DIM, HEAD_DIM), f32),
             "mlstm_n": jnp.zeros((1, bp, N_HEADS, HEAD_DIM), f32),
             "mlstm_m": jnp.zeros((1, bp, 1, LANES), f32),
             "mlstm_conv": jnp.zeros((1, bp, CONV_W - 1, 2 * WIDTH), f32)}
    carried = {"ssm": state_ssm, "ssm_conv": state_ssm_conv, "rwkv": state_rwkv,
               "rwkv_shift": state_rwkv_shift[:, :, None, :], "mlstm_c": state_mlstm_c, "mlstm_n": state_mlstm_n,
               "mlstm_m": _pad_heads(state_mlstm_m), "mlstm_conv": state_mlstm_conv}
    c_all = jnp.concatenate([c_prompt, c_sample], axis=0)
    caches_t = [jnp.transpose(c, (0, 1, 3, 4, 5, 2)) for c in (cache_kv_w128, cache_kv_w512, cache_kv_w2048)]
    y_prompt, y_sample = x_prompt, x_sample
    prompt_states, sample_states = [], []
    for layer in range(DEPTH):
        mod = _ada(c_all, P["ada_w"], P["ada_b"], layer)
        y_prompt, sp = _layer(y_prompt, mod[:bp], fresh, 0, None, P, layer)
        y_sample, ss = _layer(y_sample, mod[bp:], carried, layer, caches_t, P, layer)
        prompt_states.append(sp)
        sample_states.append(ss)
    stack = lambda states, name: jnp.stack([s[name] for s in states])
    return ((y_prompt, y_sample)
            + tuple(stack(prompt_states, n) for n in _STATE_NAMES)
            + tuple(stack(sample_states, n) for n in _STATE_NAMES))
```

```python
import functools
import math

import jax
import jax.numpy as jnp
from jax import lax
from jax.experimental import pallas as pl
from jax.experimental.pallas import tpu as pltpu

f32 = jnp.float32
bf16 = jnp.bfloat16
HI = lax.Precision.HIGHEST

D_MODEL = 1024
DEPTH = 2
HEAD_DIM = 64
NORM_EPS = 1e-6
CONV_W = 4
N_HEADS = 8
WIDTH = N_HEADS * HEAD_DIM
SSM_GROUPS = 2
SSM_STATE = 64
SSM_CONV_CH = WIDTH + 2 * SSM_GROUPS * SSM_STATE
RWKV_RANK = 64
RWKV_SHIFT_CH = 3 * WIDTH + 2 * RWKV_RANK
RWKV_LN_EPS = 64e-5
RWKV_CHUNK = 64
ATTN_GROUPS = ((128, 1), (512, 4), (2048, 16))
ATTN_HEADS = 4
ATTN_W = ATTN_HEADS * HEAD_DIM
ATTN_J = 128
MLSTM_HEAD_GROUP = 8
ATTN_UNROLL = 2
N_BRANCH = 4
LANES = 128
VMEM_LIMIT = 52 * 1024 * 1024

IN_SEGMENTS = (
    ("ssm_z", WIDTH), ("ssm_xbc", SSM_CONV_CH), ("ssm_dt", N_HEADS),
    ("rwkv_z", WIDTH), ("rwkv_shift", RWKV_SHIFT_CH),
    ("mlstm_z", WIDTH), ("mlstm_qk", 2 * WIDTH), ("mlstm_v", WIDTH),
    ("mlstm_o", WIDTH), ("mlstm_if", 2 * N_HEADS),
    ("attn_z", ATTN_W), ("attn_qkv", 9 * ATTN_W),
    ("merge", N_BRANCH * D_MODEL),
)

NN = (((1,), (0,)), ((), ()))
NT = (((1,), (1,)), ((), ()))
TN = (((0,), (0,)), ((), ()))


def _mm(a, b, dims, prec):
    if prec is None:
        a, b = a.astype(bf16), b.astype(bf16)
    return lax.dot_general(a, b, dims, preferred_element_type=f32, precision=prec)


def _dot(a, b, prec=None):
    return _mm(a, b, NN, prec)


def _dot_nt(a, b, prec=None):
    return _mm(a, b, NT, prec)


def _dot_tn(a, b, transpose_on_mxu=False):
    if not transpose_on_mxu:
        return _mm(a, b, TN, None)
    n = a.shape[1]
    eye = (_iota2((n, n), 0) == _iota2((n, n), 1)).astype(bf16)
    a_t = lax.dot_general(eye, a.astype(bf16), NT, preferred_element_type=f32).astype(bf16)
    return lax.dot_general(a_t, b.astype(bf16), NN, preferred_element_type=f32)


def _sigmoid(x):
    return 0.5 * jnp.tanh(0.5 * x) + 0.5


def _silu(x):
    return x * _sigmoid(x)


def _softplus(x):
    return jnp.maximum(x, 0.0) + jnp.log1p(jnp.exp(-jnp.abs(x)))


def _params(*sem):
    return pltpu.CompilerParams(dimension_semantics=sem, vmem_limit_bytes=VMEM_LIMIT)


def _iota2(shape, dim):
    return lax.broadcasted_iota(jnp.int32, shape, dim)


def _running_sum(x, c):
    tri = (_iota2((c, c), 0) >= _iota2((c, c), 1)).astype(f32)
    return _dot(tri, x, HI)


def _vec_layout(fields):
    out, off = {}, 0
    for name, size in fields:
        padded = -(-size // LANES) * LANES
        out[name] = (off, padded)
        off += padded
    return out, off


def _pack_vecs(layout, total, vecs):
    parts = []
    for name, (_, padded) in layout.items():
        v = vecs[name].reshape(DEPTH, -1)
        parts.append(jnp.pad(v, ((0, 0), (0, padded - v.shape[1]))))
    packed = jnp.concatenate(parts, axis=1)
    assert packed.shape[1] == total
    return packed[:, None, :]


def _vec_reader(ref, layout):
    return lambda name: ref[0, :, layout[name][0]:layout[name][0] + layout[name][1]]


SSM_VEC, SSM_VEC_N = _vec_layout((("conv_b", SSM_CONV_CH), ("dt_bias", N_HEADS), ("a_log", N_HEADS),
                                  ("d", N_HEADS), ("norm_w", WIDTH)))
RWKV_VEC, RWKV_VEC_N = _vec_layout((("mu", RWKV_SHIFT_CH), ("w0", WIDTH), ("a0", WIDTH), ("k_k", WIDTH),
                                    ("k_a", WIDTH), ("r_k", WIDTH), ("ln_w", WIDTH), ("ln_b", WIDTH)))
MLSTM_VEC, MLSTM_VEC_N = _vec_layout((("conv_b", 2 * WIDTH), ("gate_b", 2 * N_HEADS), ("norm_w", WIDTH)))
ATTN_VEC, ATTN_VEC_N = _vec_layout((("q_norm", ATTN_W), ("k_norm", ATTN_W)))


def _layer_spec(shape, layer, grid_rank):
    zeros = (0,) * (len(shape) - 1)
    if grid_rank == 1:
        return pl.BlockSpec((1,) + tuple(shape[1:]), lambda i: (layer,) + zeros)
    return pl.BlockSpec((1,) + tuple(shape[1:]), lambda i, j: (layer,) + zeros)


def _ada_kernel(c_ref, w_ref, b_ref, o_ref):
    o_ref[...] = _dot(_silu(c_ref[...]), w_ref[0], HI) + b_ref[0]


def _ada(c, w, b3, layer):
    n = c.shape[0]
    return pl.pallas_call(
        _ada_kernel,
        grid=(3,),
        in_specs=[pl.BlockSpec((n, D_MODEL), lambda j: (0, 0)),
                  pl.BlockSpec((1, D_MODEL, D_MODEL), lambda j: (layer, 0, j)),
                  pl.BlockSpec((1, 1, D_MODEL), lambda j: (layer, 0, j))],
        out_specs=pl.BlockSpec((n, D_MODEL), lambda j: (0, j)),
        out_shape=jax.ShapeDtypeStruct((n, 3 * D_MODEL), f32),
        compiler_params=_params("parallel"),
        name="ada_mod",
    )(c, w, b3)


def _norm_kernel(x_ref, nw_ref, sc_ref, sh_ref, o_ref):
    x = x_ref[0]
    r = x * lax.rsqrt(jnp.mean(x * x, axis=-1, keepdims=True) + NORM_EPS)
    o_ref[0] = (r * nw_ref[0] * (1.0 + sc_ref[0]) + sh_ref[0]).astype(o_ref.dtype)


def _norm(x3, nw3, layer, sc3, sh3, tl):
    b, l, d = x3.shape
    per_row = sc3.shape[1] != 1
    mod_spec = pl.BlockSpec((1, tl if per_row else 1, d), (lambda i, j: (i, j, 0)) if per_row else (lambda i, j: (i, 0, 0)))
    return pl.pallas_call(
        _norm_kernel,
        grid=(b, l // tl),
        in_specs=[pl.BlockSpec((1, tl, d), lambda i, j: (i, j, 0)),
                  _layer_spec(nw3.shape, layer, 2), mod_spec, mod_spec],
        out_specs=pl.BlockSpec((1, tl, d), lambda i, j: (i, j, 0)),
        out_shape=jax.ShapeDtypeStruct((b, l, d), bf16),
        compiler_params=_params("parallel", "parallel"),
        name="mod_rmsnorm",
    )(x3, nw3, sc3, sh3)


def _mm_kernel(x_ref, w_ref, o_ref):
    o_ref[...] = jnp.dot(x_ref[...], w_ref[0], preferred_element_type=f32).astype(o_ref.dtype)


def _matmul(x, w3, layer, tm, tn, out_dtype):
    n, k = x.shape
    m = w3.shape[2]
    return pl.pallas_call(
        _mm_kernel,
        grid=(m // tn, n // tm),
        in_specs=[pl.BlockSpec((tm, k), lambda j, i: (i, 0)),
                  pl.BlockSpec((1, k, tn), lambda j, i: (layer, 0, j))],
        out_specs=pl.BlockSpec((tm, tn), lambda j, i: (i, j)),
        out_shape=jax.ShapeDtypeStruct((n, m), out_dtype),
        compiler_params=_params("parallel", "parallel"),
        name="in_proj",
    )(x, w3)


def _causal_conv(ext_scr, u, c, cw_ref, bias):
    ext_scr[8:8 + c, :] = u
    acc = bias
    for i in range(CONV_W):
        acc = acc + ext_scr[5 + i:5 + i + c, :] * cw_ref[0, i:i + 1, :]
    tail = ext_scr[c + 5:c + 8, :]
    ext_scr[5:8, :] = tail
    return acc, tail


def _interleave(gens):
    gens = list(gens)
    while gens:
        alive = []
        for g in gens:
            try:
                next(g)
                alive.append(g)
            except StopIteration:
                pass
        gens = alive


def _ssm_kernel(proj_ref, conv0_ref, h0_ref, cw_ref, vec_ref, out_ref, hn_ref, convn_ref, ext_scr, y_scr, *, c, nb):
    vec = _vec_reader(vec_ref, SSM_VEC)

    @pl.when(pl.program_id(1) == 0)
    def _():
        for bi in range(nb):
            ext_scr[bi, 5:8, :] = conv0_ref[0, bi]
            hn_ref[bi] = h0_ref[0, bi]

    causal = _iota2((c, c), 0) >= _iota2((c, c), 1)
    heads = range(N_HEADS)
    per_group = N_HEADS // SSM_GROUPS
    sls = [slice(h * HEAD_DIM, (h + 1) * HEAD_DIM) for h in heads]

    def elem(bi):
        proj, out, hn, y_acc = proj_ref.at[bi], out_ref.at[bi], hn_ref.at[bi], y_scr.at[bi]
        u = proj[:, WIDTH:WIDTH + SSM_CONV_CH].astype(f32)
        acc, tail = _causal_conv(ext_scr.at[bi], u, c, cw_ref, vec("conv_b"))
        convn_ref[bi] = tail
        xbc = _silu(acc)
        xs = xbc[:, :WIDTH]
        bm = xbc[:, WIDTH:WIDTH + LANES]
        cm = xbc[:, WIDTH + LANES:]
        dt = _softplus(proj[:, WIDTH + SSM_CONV_CH:].astype(f32) + vec("dt_bias"))
        da = dt * (-jnp.exp(vec("a_log")))
        yield
        cs = _running_sum(da, c)
        cs_t = cs.T
        dt_t = dt.T
        cs_end = cs[c - 1:c, :]
        dsk = vec("d")
        bgs = [bm[:, g * SSM_STATE:(g + 1) * SSM_STATE] for g in range(SSM_GROUPS)]
        cgs = [cm[:, g * SSM_STATE:(g + 1) * SSM_STATE] for g in range(SSM_GROUPS)]
        yield
        gmats = [_dot_nt(cgs[g], bgs[g]) for g in range(SSM_GROUPS)]
        cs_cols = [cs[:, h:h + 1] for h in heads]
        yield
        lmats = [gmats[h // per_group] * jnp.exp(jnp.where(causal, cs_cols[h] - cs_t[h:h + 1, :], -jnp.inf))
                 * dt_t[h:h + 1, :] for h in heads]
        xhs = [xs[:, sl] for sl in sls]
        hprevs = [hn[h] for h in heads]
        yield
        y_in = [_dot(lmats[h], xhs[h]) for h in heads]
        y_st = [_dot_nt(cgs[h // per_group], hprevs[h]) for h in heads]
        yield
        for h in heads:
            y_acc[:, sls[h]] = y_in[h] + y_st[h] * jnp.exp(cs_cols[h]) + xhs[h] * dsk[:, h:h + 1]
        ces = [cs_end[:, h:h + 1] for h in heads]
        upd = [_dot_tn(xhs[h] * (jnp.exp(ces[h] - cs_cols[h]) * dt[:, h:h + 1]), bgs[h // per_group],
                       transpose_on_mxu=True) for h in heads]
        yield
        for h in heads:
            hn[h] = hprevs[h] * jnp.exp(ces[h]) + upd[h]
        gy = y_acc[...] * _silu(proj[:, :WIDTH].astype(f32))
        nw = vec("norm_w")
        gw = WIDTH // SSM_GROUPS
        for g in range(SSM_GROUPS):
            part = gy[:, g * gw:(g + 1) * gw]
            ms = jnp.mean(part * part, axis=-1, keepdims=True)
            out[:, g * gw:(g + 1) * gw] = part * lax.rsqrt(ms + NORM_EPS) * nw[:, g * gw:(g + 1) * gw]

    _interleave(elem(bi) for bi in range(nb))


def _state_spec(shape, sidx, nb):
    zeros = (0,) * (len(shape) - 2)
    return pl.BlockSpec((1, nb) + tuple(shape[2:]), lambda i, j: (sidx, i) + zeros)


def _batch_spec(nb, *dims):
    zeros = (0,) * len(dims)
    return pl.BlockSpec((nb,) + dims, lambda i, j: (i,) + zeros)


def _ssm(proj, conv0, h0, sidx, P, layer, c, nb):
    b, l, wp = proj.shape
    return pl.pallas_call(
        functools.partial(_ssm_kernel, c=c, nb=nb),
        grid=(b // nb, l // c),
        in_specs=[pl.BlockSpec((nb, c, wp), lambda i, j: (i, j, 0)),
                  _state_spec(conv0.shape, sidx, nb), _state_spec(h0.shape, sidx, nb),
                  _layer_spec(P["ssm_conv_w"].shape, layer, 2), _layer_spec(P["ssm_vec"].shape, layer, 2)],
        out_specs=[pl.BlockSpec((nb, c, WIDTH), lambda i, j: (i, j, 0)),
                   _batch_spec(nb, N_HEADS, HEAD_DIM, SSM_STATE), _batch_spec(nb, CONV_W - 1, SSM_CONV_CH)],
        out_shape=[jax.ShapeDtypeStruct((b, l, WIDTH), f32),
                   jax.ShapeDtypeStruct((b, N_HEADS, HEAD_DIM, SSM_STATE), f32),
                   jax.ShapeDtypeStruct((b, CONV_W - 1, SSM_CONV_CH), f32)],
        scratch_shapes=[pltpu.VMEM((nb, c + 8, SSM_CONV_CH), f32), pltpu.VMEM((nb, c, WIDTH), f32)],
        compiler_params=_params("parallel", "arbitrary"),
        name="ssd_branch",
    )(proj, conv0, h0, P["ssm_conv_w"], P["ssm_vec"])


def _rwkv_chunk_kernel(proj_ref, shift0_ref, s0_ref, vec_ref, w2_ref, a2_ref,
                       out_ref, sn_ref, shiftn_ref, ext_scr, *, c, rows, nb):
    vec = _vec_reader(vec_ref, RWKV_VEC)

    @pl.when(pl.program_id(1) == 0)
    def _():
        for bi in range(nb):
            ext_scr[bi, 7:8, :] = shift0_ref[0, bi]
            sn_ref[bi] = s0_ref[0, bi]

    row2 = _iota2((c, 2 * c), 0)
    col2 = _iota2((c, 2 * c), 1)
    col2 = jnp.where(col2 >= c, col2 - c, col2)
    strict2 = row2 > col2
    incl2 = row2 >= col2
    n_double = (c - 1).bit_length()
    heads = range(N_HEADS)
    sls = [slice(h * HEAD_DIM, (h + 1) * HEAD_DIM) for h in heads]

    def elem(bi):
        proj, out, sn, ext = proj_ref.at[bi], out_ref.at[bi], sn_ref.at[bi], ext_scr.at[bi]
        u = proj[:, WIDTH:].astype(f32)
        ext[8:8 + rows, :] = u
        sh = u + (ext[7:7 + rows, :] - u) * vec("mu")
        last = ext[rows + 7:rows + 8, :]
        ext[7:8, :] = last
        shiftn_ref[bi] = last
        r_all = sh[:, :WIDTH]
        k = sh[:, WIDTH:2 * WIDTH]
        v_all = sh[:, 2 * WIDTH:3 * WIDTH]
        w_lo = sh[:, 3 * WIDTH:3 * WIDTH + RWKV_RANK]
        a_lo = sh[:, 3 * WIDTH + RWKV_RANK:]
        yield
        w_log = -_softplus(-(vec("w0") + _dot(jnp.tanh(w_lo), w2_ref[0]))) - 0.5
        lw_all = -jnp.exp(w_log)
        a_in_all = _sigmoid(vec("a0") + _dot(a_lo, a2_ref[0]))
        kk_all = k * vec("k_k")
        k2_all = k * (1.0 + (a_in_all - 1.0) * vec("k_a"))
        rk, lnw, lnb = vec("r_k"), vec("ln_w"), vec("ln_b")
        yield
        for c0 in range(0, rows, c):
            cr = slice(c0, c0 + c)
            r, v, lw, a_in, kk, k2 = r_all[cr], v_all[cr], lw_all[cr], a_in_all[cr], kk_all[cr], k2_all[cr]
            cl = _running_sum(lw, c)
            cl_end = cl[c - 1:c, :]
            g_fwd = jnp.exp(cl)
            g_inv = jnp.exp(-cl)
            g_rem = jnp.exp(cl_end - cl)
            g_prev = jnp.exp(cl - lw)
            g_end = jnp.exp(cl_end)
            yield
            s0s = [sn[h] for h in heads]
            bhs, ars, bks = [], [], []
            for sl in sls:
                kkh = kk[:, sl]
                kkn = kkh * (1.0 / jnp.maximum(jnp.sqrt(jnp.sum(kkh * kkh, axis=-1, keepdims=True)), 1e-12))
                bhs.append(kkn * a_in[:, sl])
                ars.append(jnp.concatenate([-kkn * g_prev[:, sl], r[:, sl] * g_fwd[:, sl]], axis=0))
                bks.append(jnp.concatenate([bhs[-1] * g_inv[:, sl], k2[:, sl] * g_inv[:, sl]], axis=0))
            yield
            m4s = [_dot_nt(ars[h], bks[h]) for h in heads]
            ahs = [_dot_nt(ars[h], s0s[h]) for h in heads]
            yield
            tops = [jnp.where(strict2, m4[:c, :], 0.0) for m4 in m4s]
            bots = [jnp.where(incl2, m4[c:, :], 0.0) for m4 in m4s]
            amats = [top[:, :c].astype(bf16) for top in tops]
            xs = [ahs[h][:c] + _dot(tops[h][:, c:], v[:, sls[h]]) for h in heads]
            yield
            for i in range(n_double):
                xs = [xs[h] + _dot(amats[h], xs[h]) for h in heads]
                if i + 1 < n_double:
                    amats = [_dot(amats[h], amats[h]).astype(bf16) for h in heads]
                yield
            pvs = [jnp.concatenate([xs[h], v[:, sls[h]]], axis=0) for h in heads]
            ys = [ahs[h][c:] + _dot(bots[h], pvs[h]) for h in heads]
            upd = [_dot_tn(pvs[h], jnp.concatenate([bhs[h] * g_rem[:, sls[h]], k2[:, sls[h]] * g_rem[:, sls[h]]],
                                                   axis=0)) for h in heads]
            yield
            for h in heads:
                sl = sls[h]
                sn[h] = s0s[h] * g_end[:, sl] + upd[h]
                y = ys[h]
                mu = jnp.mean(y, axis=-1, keepdims=True)
                var = jnp.mean(jnp.square(y - mu), axis=-1, keepdims=True)
                ln = (y - mu) * lax.rsqrt(var + RWKV_LN_EPS) * lnw[:, sl] + lnb[:, sl]
                bonus = jnp.sum(r[:, sl] * k2[:, sl] * rk[:, sl], axis=-1, keepdims=True) * v[:, sl]
                out[cr, sl] = (ln + bonus) * _silu(proj[cr, sl].astype(f32))
            yield

    _interleave(elem(bi) for bi in range(nb))


def _rwkv(proj, shift0, s0, sidx, P, layer, c, rows, nb):
    b, l, wp = proj.shape
    return pl.pallas_call(
        functools.partial(_rwkv_chunk_kernel, c=c, rows=rows, nb=nb),
        grid=(b // nb, l // rows),
        in_specs=[pl.BlockSpec((nb, rows, wp), lambda i, j: (i, j, 0)),
                  _state_spec(shift0.shape, sidx, nb), _state_spec(s0.shape, sidx, nb),
                  _layer_spec(P["rwkv_vec"].shape, layer, 2),
                  _layer_spec(P["rwkv_w2"].shape, layer, 2), _layer_spec(P["rwkv_a2"].shape, layer, 2)],
        out_specs=[pl.BlockSpec((nb, rows, WIDTH), lambda i, j: (i, j, 0)),
                   _batch_spec(nb, N_HEADS, HEAD_DIM, HEAD_DIM), _batch_spec(nb, 1, RWKV_SHIFT_CH)],
        out_shape=[jax.ShapeDtypeStruct((b, l, WIDTH), f32),
                   jax.ShapeDtypeStruct((b, N_HEADS, HEAD_DIM, HEAD_DIM), f32),
                   jax.ShapeDtypeStruct((b, 1, RWKV_SHIFT_CH), f32)],
        scratch_shapes=[pltpu.VMEM((nb, rows + 8, RWKV_SHIFT_CH), f32)],
        compiler_params=_params("parallel", "arbitrary"),
        name="rwkv7_chunked",
    )(proj, shift0, s0, P["rwkv_vec"], P["rwkv_w2"], P["rwkv_a2"])


def _mlstm_kernel(proj_ref, conv0_ref, c0_ref, n0_ref, m0_ref, cw_ref, vec_ref,
                  out_ref, cn_ref, nn_ref, mn_ref, convn_ref, ext_scr, h_scr, *, c, nb):
    vec = _vec_reader(vec_ref, MLSTM_VEC)

    @pl.when(pl.program_id(1) == 0)
    def _():
        for bi in range(nb):
            ext_scr[bi, 5:8, :] = conv0_ref[0, bi]
            cn_ref[bi] = c0_ref[0, bi]
            nn_ref[bi] = n0_ref[0, bi]
            mn_ref[bi] = m0_ref[0, bi]

    causal = _iota2((c, c), 0) >= _iota2((c, c), 1)
    lane = _iota2((1, LANES), 1)
    heads = range(N_HEADS)
    sls = [slice(h * HEAD_DIM, (h + 1) * HEAD_DIM) for h in heads]

    def elem(bi):
        proj, out, cn, nn, h_acc = proj_ref.at[bi], out_ref.at[bi], cn_ref.at[bi], nn_ref.at[bi], h_scr.at[bi]
        u = proj[:, WIDTH:3 * WIDTH].astype(f32)
        acc, tail = _causal_conv(ext_scr.at[bi], u, c, cw_ref, vec("conv_b"))
        convn_ref[bi] = tail
        qk = _silu(acc)
        gates = proj[:, 5 * WIDTH:].astype(f32) + vec("gate_b")
        lf = -_softplus(-gates)
        yield
        bc = _running_sum(lf, c)
        bc_t = bc.T
        ig_t = gates.T
        m_prev = mn_ref[bi]
        yield
        m_out = jnp.zeros((1, LANES), f32)
        for grp in range(0, N_HEADS, MLSTM_HEAD_GROUP):
            heads = range(grp, grp + MLSTM_HEAD_GROUP)
            b_cols = {h: bc[:, N_HEADS + h:N_HEADS + h + 1] for h in heads}
            fold = c <= 8
            emats = {h: jnp.where(causal, (b_cols[h] if fold else 0.0) + ig_t[h:h + 1, :]
                                  - bc_t[N_HEADS + h:N_HEADS + h + 1, :], -jnp.inf) for h in heads}
            ecols = {h: 0.0 if fold else b_cols[h] for h in heads}
            m0s = {h: m_prev[:, h:h + 1] for h in heads}
            gcols = {h: b_cols[h] + m0s[h] for h in heads}
            mts = {h: jnp.maximum(gcols[h], ecols[h] + jnp.max(emats[h], axis=-1, keepdims=True)) for h in heads}
            yield
            qhs = {h: qk[:, sls[h]] for h in heads}
            khs = {h: qk[:, WIDTH + h * HEAD_DIM:WIDTH + (h + 1) * HEAD_DIM] * HEAD_DIM ** -0.5 for h in heads}
            vhs = {h: proj[:, 3 * WIDTH + h * HEAD_DIM:3 * WIDTH + (h + 1) * HEAD_DIM].astype(f32) for h in heads}
            cprevs = {h: cn[h] for h in heads}
            nprevs = {h: nn[h:h + 1, :] for h in heads}
            yield
            sws = {h: jnp.exp(emats[h] + (ecols[h] - mts[h])) * _dot_nt(qhs[h], khs[h]) for h in heads}
            wis = {h: jnp.exp(gcols[h] - mts[h]) for h in heads}
            qcs = {h: _dot(qhs[h], cprevs[h]) for h in heads}
            yield
            nums = {h: _dot(sws[h], vhs[h]) + wis[h] * qcs[h] for h in heads}
            yield
            for h in heads:
                den = (jnp.sum(sws[h], axis=-1, keepdims=True)
                       + wis[h] * jnp.sum(qhs[h] * nprevs[h], axis=-1, keepdims=True))
                h_acc[:, sls[h]] = nums[h] * (1.0 / jnp.maximum(jnp.abs(den), jnp.exp(-mts[h])))
            m_news = {h: mts[h][c - 1:c, :] for h in heads}
            b_ends = {h: b_cols[h][c - 1:c, :] for h in heads}
            kws = {h: khs[h] * jnp.exp(b_ends[h] - b_cols[h] + gates[:, h:h + 1] - m_news[h]) for h in heads}
            yield
            kvs = {h: _dot_tn(kws[h], vhs[h]) for h in heads}
            yield
            for h in heads:
                dec = jnp.exp(b_ends[h] + m0s[h] - m_news[h])
                cn[h] = dec * cprevs[h] + kvs[h]
                nn[h:h + 1, :] = dec * nprevs[h] + jnp.sum(kws[h], axis=0, keepdims=True)
                m_out = jnp.where(lane == h, m_news[h], m_out)
        mn_ref[bi] = m_out
        nw = vec("norm_w")
        for h in range(N_HEADS):
            sl = sls[h]
            o_gate = proj[:, 4 * WIDTH + h * HEAD_DIM:4 * WIDTH + (h + 1) * HEAD_DIM].astype(f32)
            hm = _sigmoid(o_gate) * h_acc[:, sl]
            ms = jnp.mean(hm * hm, axis=-1, keepdims=True)
            out[:, sl] = hm * lax.rsqrt(ms + NORM_EPS) * nw[:, sl] * _silu(proj[:, sl].astype(f32))

    _interleave(elem(bi) for bi in range(nb))


def _mlstm(proj, conv0, c0, n0, m0, sidx, P, layer, c, nb):
    b, l, wp = proj.shape
    outs = pl.pallas_call(
        functools.partial(_mlstm_kernel, c=c, nb=nb),
        grid=(b // nb, l // c),
        in_specs=[pl.BlockSpec((nb, c, wp), lambda i, j: (i, j, 0)),
                  _state_spec(conv0.shape, sidx, nb), _state_spec(c0.shape, sidx, nb),
                  _state_spec(n0.shape, sidx, nb), _state_spec(m0.shape, sidx, nb),
                  _layer_spec(P["mlstm_conv_w"].shape, layer, 2), _layer_spec(P["mlstm_vec"].shape, layer, 2)],
        out_specs=[pl.BlockSpec((nb, c, WIDTH), lambda i, j: (i, j, 0)),
                   _batch_spec(nb, N_HEADS, HEAD_DIM, HEAD_DIM), _batch_spec(nb, N_HEADS, HEAD_DIM),
                   _batch_spec(nb, 1, LANES), _batch_spec(nb, CONV_W - 1, 2 * WIDTH)],
        out_shape=[jax.ShapeDtypeStruct((b, l, WIDTH), f32),
                   jax.ShapeDtypeStruct((b, N_HEADS, HEAD_DIM, HEAD_DIM), f32),
                   jax.ShapeDtypeStruct((b, N_HEADS, HEAD_DIM), f32),
                   jax.ShapeDtypeStruct((b, 1, LANES), f32),
                   jax.ShapeDtypeStruct((b, CONV_W - 1, 2 * WIDTH), f32)],
        scratch_shapes=[pltpu.VMEM((nb, c + 8, 2 * WIDTH), f32), pltpu.VMEM((nb, c, WIDTH), f32)],
        compiler_params=_params("parallel", "arbitrary"),
        name="mlstm_branch",
    )(proj, conv0, c0, n0, m0, P["mlstm_conv_w"], P["mlstm_vec"])
    out, cn, nn, mn, convn = outs
    return out, cn, nn, mn[:, 0, :N_HEADS], convn


def _head_mean_matrix():
    hid = jnp.arange(ATTN_W) // HEAD_DIM
    return ((hid[:, None] == hid[None, :]).astype(f32) / HEAD_DIM).astype(bf16)


def _qk_norm(x, hm_ref, w):
    sq = x * x
    hi = sq.astype(bf16)
    lo = (sq - hi.astype(f32)).astype(bf16)
    ms = _dot(hi, hm_ref[...]) + _dot(lo, hm_ref[...])
    return x * lax.rsqrt(ms + NORM_EPS) * w


def _attn_prompt_kernel(q_ref, k_ref, v_ref, z_ref, hm_ref, vec_ref, out_ref, kv0_ref, kv1_ref, kv2_ref,
                        qn_scr, kn_scr, vv_scr, acc_scr, m_scr, d_scr, to_scr, tl_scr, *, l):
    vec = _vec_reader(vec_ref, ATTN_VEC)
    g = pl.program_id(1)
    rb = 256
    n_pairs = ATTN_W // LANES

    def norm_body(i, carry):
        rows = pl.ds(pl.multiple_of(i * rb, rb), rb)
        qn = _qk_norm(q_ref[0, rows, :].astype(f32), hm_ref, vec("q_norm"))
        kn = _qk_norm(k_ref[0, rows, :].astype(f32), hm_ref, vec("k_norm"))
        v = v_ref[0, rows, :].astype(f32)
        for p in range(n_pairs):
            qn_scr[p, rows, :] = qn[:, p * LANES:(p + 1) * LANES]
            kn_scr[p, rows, :] = kn[:, p * LANES:(p + 1) * LANES]
            vv_scr[p, rows, :] = v[:, p * LANES:(p + 1) * LANES]
        return carry

    lax.fori_loop(0, l // rb, norm_body, 0)

    qb = ATTN_J
    qi = _iota2((qb, qb), 0)
    ki = _iota2((qb, qb), 1)
    scale = HEAD_DIM ** -0.5

    def run_group(dil, first):
        n_iter = l // qb

        def block(i, slot):
            r = i % dil
            blk = i // dil
            u0 = blk * qb
            rows_c = pl.ds(u0 * dil + r, qb, stride=dil)
            rows_p = pl.ds(jnp.maximum(u0 - qb, 0) * dil + r, qb, stride=dil)
            mask_c = ki <= qi
            mask_p = (ki >= qi) & (blk > 0)
            hsl = [slice(hh * HEAD_DIM, (hh + 1) * HEAD_DIM) for hh in range(LANES // HEAD_DIM)]
            heads = [(p, sl) for p in range(n_pairs) for sl in hsl]
            qv = [qn_scr[p, rows_c, :] for p in range(n_pairs)]
            kc = [kn_scr[p, rows_c, :] for p in range(n_pairs)]
            vc = [vv_scr[p, rows_c, :] for p in range(n_pairs)]
            kp = [kn_scr[p, rows_p, :] for p in range(n_pairs)]
            vp = [vv_scr[p, rows_p, :] for p in range(n_pairs)]
            yield
            scs = [jnp.where(mask_c, _dot_nt(qv[p][:, sl], kc[p][:, sl]) * scale, -jnp.inf) for p, sl in heads]
            sps = [jnp.where(mask_p, _dot_nt(qv[p][:, sl], kp[p][:, sl]) * scale, -jnp.inf) for p, sl in heads]
            yield
            ms = [jnp.max(jnp.maximum(sc, sp), axis=-1, keepdims=True) for sc, sp in zip(scs, sps)]
            pcs = [jnp.exp(sc - m) for sc, m in zip(scs, ms)]
            pps = [jnp.exp(sp - m) for sp, m in zip(sps, ms)]
            dens = [jnp.sum(pc + pp, axis=-1, keepdims=True) for pc, pp in zip(pcs, pps)]
            yield
            for (p, sl), pc, pp, m, den in zip(heads, pcs, pps, ms, dens):
                to_scr[slot, p, :, sl] = (_dot(pc, vc[p][:, sl]) + _dot(pp, vp[p][:, sl])) * (1.0 / den)
                tl_scr[slot, p, :, sl] = jnp.broadcast_to(m + jnp.log(den), (qb, HEAD_DIM))
            yield
            for p in range(n_pairs):
                o = to_scr[slot, p]
                lse = tl_scr[slot, p]
                if first:
                    acc_scr[p, rows_c, :] = o
                    m_scr[p, rows_c, :] = lse
                    d_scr[p, rows_c, :] = jnp.ones_like(lse)
                else:
                    m_old = m_scr[p, rows_c, :]
                    m_new = jnp.maximum(m_old, lse)
                    a_old = jnp.exp(m_old - m_new)
                    a_new = jnp.exp(lse - m_new)
                    acc_scr[p, rows_c, :] = acc_scr[p, rows_c, :] * a_old + o * a_new
                    d_scr[p, rows_c, :] = d_scr[p, rows_c, :] * a_old + a_new
                    m_scr[p, rows_c, :] = m_new

        def body(i, carry):
            _interleave(block(i * ATTN_UNROLL + s, s) for s in range(ATTN_UNROLL))
            return carry

        lax.fori_loop(0, n_iter // ATTN_UNROLL, body, 0)

    def save_window(kv_ref):
        w = kv_ref.shape[1]
        wb = min(w, rb)
        for r0 in range(0, w, wb):
            src = slice(l - w + r0, l - w + r0 + wb)
            for p in range(n_pairs):
                kv_ref[0, r0:r0 + wb, p * LANES:(p + 1) * LANES] = kn_scr[p, src, :]
                kv_ref[0, r0:r0 + wb, ATTN_W + p * LANES:ATTN_W + (p + 1) * LANES] = vv_scr[p, src, :]

    def group_step(gi, dil):
        save_window((kv0_ref, kv1_ref, kv2_ref)[gi])
        run_group(dil, gi == 0)

    for gi, (_, dil) in enumerate(ATTN_GROUPS):
        pl.when(g == gi)(functools.partial(group_step, gi, dil))

    @pl.when(g == len(ATTN_GROUPS) - 1)
    def _():
        def fin_body(i, carry):
            rows = pl.ds(pl.multiple_of(i * rb, rb), rb)
            for p in range(n_pairs):
                lanes = slice(p * LANES, (p + 1) * LANES)
                z = z_ref[0, rows, lanes].astype(f32)
                out_ref[0, rows, lanes] = acc_scr[p, rows, :] / d_scr[p, rows, :] * _silu(z)
            return carry

        lax.fori_loop(0, l // rb, fin_body, 0)


def _attn_prompt(proj, P, layer):
    b, l, _ = proj.shape
    ng = len(ATTN_GROUPS)
    col = lambda base: pl.BlockSpec((1, l, ATTN_W), lambda i, g: (i, 0, base + g))
    pair_scr = pltpu.VMEM((ATTN_W // LANES, l, LANES), f32)
    windows = [min(w, l) for w, _ in ATTN_GROUPS]
    return pl.pallas_call(
        functools.partial(_attn_prompt_kernel, l=l),
        grid=(b, ng),
        in_specs=[col(1), col(1 + ng), col(1 + 2 * ng),
                  pl.BlockSpec((1, l, ATTN_W), lambda i, g: (i, 0, 0)),
                  pl.BlockSpec((ATTN_W, ATTN_W), lambda i, g: (0, 0)),
                  _layer_spec(P["attn_vec"].shape, layer, 2)],
        out_specs=[pl.BlockSpec((1, l, ATTN_W), lambda i, g: (i, 0, 0))]
                  + [pl.BlockSpec((1, w, 2 * ATTN_W), lambda i, g: (i, 0, 0)) for w in windows],
        out_shape=[jax.ShapeDtypeStruct((b, l, ATTN_W), f32)]
                  + [jax.ShapeDtypeStruct((b, w, 2 * ATTN_W), f32) for w in windows],
        scratch_shapes=[pair_scr] * 6 + [pltpu.VMEM((ATTN_UNROLL, ATTN_W // LANES, ATTN_J, LANES), f32)] * 2,
        compiler_params=_params("parallel", "arbitrary"),
        name="dilated_attn_prompt",
    )(proj, proj, proj, proj, P["head_mean"], P["attn_vec"])


def _attn_step_kernel(x_ref, c0_ref, c1_ref, c2_ref, hm_ref, vec_ref, out_ref, kv_ref, *, l, nb):
    vec = _vec_reader(vec_ref, ATTN_VEC)
    ng = len(ATTN_GROUPS)
    scale = HEAD_DIM ** -0.5
    c_refs = (c0_ref, c1_ref, c2_ref)
    hsl = [slice(h * HEAD_DIM, (h + 1) * HEAD_DIM) for h in range(ATTN_HEADS)]
    masks_p, masks_n = [], []
    for gi, (window, dil) in enumerate(ATTN_GROUPS):
        w = c_refs[gi].shape[-1]
        jmax = window // dil
        shift = dil.bit_length() - 1
        dist_p = w + _iota2((l, w), 0) - _iota2((l, w), 1)
        masks_p.append(((dist_p & (dil - 1)) == 0) & ((dist_p >> shift) <= jmax))
        dist_n = _iota2((l, l), 0) - _iota2((l, l), 1)
        masks_n.append((dist_n >= 0) & ((dist_n & (dil - 1)) == 0) & ((dist_n >> shift) <= jmax))
    gh = [(gi, h) for gi in range(ng) for h in range(ATTN_HEADS)]

    def elem(bi):
        x = x_ref[bi]
        qns, kns, vs = [], [], []
        for gi in range(ng):
            qns.append(_qk_norm(x[:, (1 + gi) * ATTN_W:(2 + gi) * ATTN_W], hm_ref, vec("q_norm")))
            kns.append(_qk_norm(x[:, (1 + ng + gi) * ATTN_W:(2 + ng + gi) * ATTN_W], hm_ref, vec("k_norm")))
            vs.append(x[:, (1 + 2 * ng + gi) * ATTN_W:(2 + 2 * ng + gi) * ATTN_W])
            kv_ref[bi, gi, :, 0:ATTN_W] = kns[gi]
            kv_ref[bi, gi, :, ATTN_W:2 * ATTN_W] = vs[gi]
        yield
        sps = [jnp.where(masks_p[gi], _dot(qns[gi][:, hsl[h]], c_refs[gi][0, bi, 0, h]) * scale, -jnp.inf)
               for gi, h in gh]
        sns = [jnp.where(masks_n[gi], _dot_nt(qns[gi][:, hsl[h]], kns[gi][:, hsl[h]]) * scale, -jnp.inf)
               for gi, h in gh]
        yield
        ms = [jnp.maximum(jnp.max(sp, axis=-1, keepdims=True), jnp.max(sn, axis=-1, keepdims=True))
              for sp, sn in zip(sps, sns)]
        pps = [jnp.exp(sp - m) for sp, m in zip(sps, ms)]
        pns = [jnp.exp(sn - m) for sn, m in zip(sns, ms)]
        dens = [jnp.sum(pp, axis=-1, keepdims=True) + jnp.sum(pn, axis=-1, keepdims=True)
                for pp, pn in zip(pps, pns)]
        yield
        og = [(_dot_nt(pp, c_refs[gi][0, bi, 1, h]) + _dot(pn, vs[gi][:, hsl[h]])) * (1.0 / den)
              for (gi, h), pp, pn, den in zip(gh, pps, pns, dens)]
        lg = [m + jnp.log(den) for m, den in zip(ms, dens)]
        yield
        outs = [og[gi * ATTN_HEADS:(gi + 1) * ATTN_HEADS] for gi in range(ng)]
        lses = [lg[gi * ATTN_HEADS:(gi + 1) * ATTN_HEADS] for gi in range(ng)]
        for h in range(ATTN_HEADS):
            sl = hsl[h]
            mx = functools.reduce(jnp.maximum, [lses[gi][h] for gi in range(ng)])
            ws = [jnp.exp(lses[gi][h] - mx) for gi in range(ng)]
            tot = functools.reduce(lambda a, b: a + b, ws)
            o = functools.reduce(lambda a, b: a + b, [outs[gi][h] * (ws[gi] / tot) for gi in range(ng)])
            out_ref[bi, :, sl] = o * _silu(x[:, sl])

    _interleave(elem(bi) for bi in range(nb))


def _attn_step(proj, caches, P, layer, nb):
    b, l, wp = proj.shape
    ng = len(ATTN_GROUPS)
    cache_specs = [pl.BlockSpec((1, nb) + c.shape[2:], lambda i: (layer, i, 0, 0, 0, 0)) for c in caches]
    return pl.pallas_call(
        functools.partial(_attn_step_kernel, l=l, nb=nb),
        grid=(b // nb,),
        in_specs=[pl.BlockSpec((nb, l, wp), lambda i: (i, 0, 0))] + cache_specs
                 + [pl.BlockSpec((ATTN_W, ATTN_W), lambda i: (0, 0)), _layer_spec(P["attn_vec"].shape, layer, 1)],
        out_specs=[pl.BlockSpec((nb, l, ATTN_W), lambda i: (i, 0, 0)),
                   pl.BlockSpec((nb, ng, l, 2 * ATTN_W), lambda i: (i, 0, 0, 0))],
        out_shape=[jax.ShapeDtypeStruct((b, l, ATTN_W), f32),
                   jax.ShapeDtypeStruct((b, ng, l, 2 * ATTN_W), f32)],
        compiler_params=_params("parallel"),
        name="dilated_attn_step",
    )(proj, *caches, P["head_mean"], P["attn_vec"])


def _merge_kernel(oa_ref, ob_ref, oc_ref, od_ref, mg_ref, x_ref, gate_ref, wb_ref, wo_ref, o_ref):
    merged = None
    start = 0
    for bi, ref in enumerate((oa_ref, ob_ref, oc_ref, od_ref)):
        width = ref.shape[-1]
        t = jnp.dot(ref[0].astype(bf16), wb_ref[0, start:start + width, :], preferred_element_type=f32)
        term = _sigmoid(mg_ref[0, :, bi * D_MODEL:(bi + 1) * D_MODEL].astype(f32)) * t
        merged = term if merged is None else merged + term
        start += width
    y = jnp.dot(merged.astype(bf16), wo_ref[0], preferred_element_type=f32)
    o_ref[0] = x_ref[0] + gate_ref[0] * y


def _merge(branches, mg3, x3, gate3, P, layer, tl):
    b, l, d = x3.shape
    per_row = gate3.shape[1] != 1
    rows = lambda w: pl.BlockSpec((1, tl, w), lambda i, j: (i, j, 0))
    gate_spec = pl.BlockSpec((1, tl if per_row else 1, d), (lambda i, j: (i, j, 0)) if per_row else (lambda i, j: (i, 0, 0)))
    return pl.pallas_call(
        _merge_kernel,
        grid=(b, l // tl),
        in_specs=[rows(o.shape[-1]) for o in branches] + [rows(N_BRANCH * d), rows(d), gate_spec,
                                                          _layer_spec(P["w_branch"].shape, layer, 2),
                                                          _layer_spec(P["w_out"].shape, layer, 2)],
        out_specs=rows(d),
        out_shape=jax.ShapeDtypeStruct((b, l, d), f32),
        compiler_params=_params("parallel", "parallel"),
        name="merge_out",
    )(*branches, mg3, x3, gate3, P["w_branch"], P["w_out"])


def _split_w_in(w_in):
    cols, start = {}, 0
    for name, size in IN_SEGMENTS:
        cols[name] = w_in[:, :, start:start + size]
        start += size
    zeros = lambda n: jnp.zeros(w_in.shape[:2] + (n,), w_in.dtype)
    groups = {
        "ssm": [cols["ssm_z"], cols["ssm_xbc"], cols["ssm_dt"], zeros(LANES - N_HEADS)],
        "rwkv": [cols["rwkv_z"], cols["rwkv_shift"]],
        "mlstm": [cols["mlstm_z"], cols["mlstm_qk"], cols["mlstm_v"], cols["mlstm_o"], cols["mlstm_if"],
                  zeros(LANES - 2 * N_HEADS)],
        "attn": [cols["attn_z"], cols["attn_qkv"]],
        "merge": [cols["merge"]],
    }
    return {k: jnp.concatenate(v, axis=2).astype(bf16) for k, v in groups.items()}


_PROJ_TN = {"ssm": 1408, "rwkv": 2176, "mlstm": 896, "attn": 1280, "merge": 1024}


def _layer(x, mod, st, sidx, caches, P, layer):
    b, l, d = x.shape
    prompt = caches is None
    shift, scale, gate = mod[:, :d], mod[:, d:2 * d], mod[:, 2 * d:]
    if prompt:
        x3, sc3, sh3, g3 = x, scale[:, None], shift[:, None], gate[:, None]
        tl, chunk, tl_merge, tm, proj_dtype = 512, 128, 512, 1024, bf16
        rwkv_rows, nb_ssd, nb_rwkv, nb_mlstm, nb_attn = 2 * RWKV_CHUNK, math.gcd(b, 4), math.gcd(b, 2), 1, 1
    else:
        rep = lambda t: jnp.repeat(t, l, axis=0)[None]
        x3, sc3, sh3, g3 = x.reshape(1, b * l, d), rep(scale), rep(shift), rep(gate)
        tl = min(256, b * l)
        chunk, tl_merge, tm, proj_dtype = l, tl, min(1024, b * l), f32
        rwkv_rows, nb_ssd, nb_rwkv, nb_mlstm, nb_attn = l, math.gcd(b, 4), math.gcd(b, 4), 1, math.gcd(b, 2)
    h2 = _norm(x3, P["norm_w"], layer, sc3, sh3, tl).reshape(b * l, d)
    proj = {k: _matmul(h2, P["w_in_" + k], layer, tm, _PROJ_TN[k], proj_dtype).reshape(b, l, -1) for k in _PROJ_TN}

    out_a, ssm_new, ssm_conv_new = _ssm(proj["ssm"], st["ssm_conv"], st["ssm"], sidx, P, layer, chunk, nb_ssd)
    out_b, rwkv_new, shift_new = _rwkv(proj["rwkv"], st["rwkv_shift"], st["rwkv"], sidx, P, layer,
                                       min(RWKV_CHUNK, l), rwkv_rows, nb_rwkv)
    out_c, c_new, n_new, m_new, mconv_new = _mlstm(proj["mlstm"], st["mlstm_conv"], st["mlstm_c"], st["mlstm_n"],
                                                   st["mlstm_m"], sidx, P, layer, chunk, nb_mlstm)
    if prompt:
        out_d, *kv = _attn_prompt(proj["attn"], P, layer)
        kv_new = [t.reshape(b, t.shape[1], 2, ATTN_HEADS, HEAD_DIM) for t in kv]
    else:
        out_d, kv = _attn_step(proj["attn"], caches, P, layer, nb_attn)
        kv_new = [kv[:, gi].reshape(b, l, 2, ATTN_HEADS, HEAD_DIM) for gi in range(len(ATTN_GROUPS))]

    branches = [o.reshape(x3.shape[0], x3.shape[1], -1) for o in (out_a, out_b, out_c, out_d)]
    mg3 = proj["merge"].reshape(x3.shape[0], x3.shape[1], -1)
    x_new = _merge(branches, mg3, x3, g3, P, layer, tl_merge).reshape(b, l, d)
    new_state = {"ssm": ssm_new, "ssm_conv": ssm_conv_new, "rwkv": rwkv_new, "rwkv_shift": shift_new[:, 0],
                 "mlstm_c": c_new, "mlstm_n": n_new, "mlstm_m": m_new, "mlstm_conv": mconv_new,
                 "kv_0": kv_new[0], "kv_1": kv_new[1], "kv_2": kv_new[2]}
    return x_new, new_state


_STATE_NAMES = ("ssm", "ssm_conv", "rwkv", "rwkv_shift", "mlstm_c", "mlstm_n", "mlstm_m", "mlstm_conv",
                "kv_0", "kv_1", "kv_2")


def _prepare_params(w):
    P = {"norm_w": w["norm_w"][:, None, :], "ada_w": w["ada_w"], "ada_b": w["ada_b"][:, None, :],
         "w_branch": w["w_branch"].astype(bf16), "w_out": w["w_out"].astype(bf16),
         "ssm_conv_w": w["ssm_conv_w"], "mlstm_conv_w": w["mlstm_conv_w"],
         "rwkv_w2": w["rwkv_w2"], "rwkv_a2": w["rwkv_a2"], "head_mean": _head_mean_matrix(),
         "ssm_vec": _pack_vecs(SSM_VEC, SSM_VEC_N, {"conv_b": w["ssm_conv_b"], "dt_bias": w["ssm_dt_bias"],
                                                    "a_log": w["ssm_a_log"], "d": w["ssm_d"],
                                                    "norm_w": w["ssm_norm_w"]}),
         "rwkv_vec": _pack_vecs(RWKV_VEC, RWKV_VEC_N, {"mu": w["rwkv_mu"], "w0": w["rwkv_w0"], "a0": w["rwkv_a0"],
                                                       "k_k": w["rwkv_k_k"], "k_a": w["rwkv_k_a"],
                                                       "r_k": w["rwkv_r_k"], "ln_w": w["rwkv_ln_w"],
                                                       "ln_b": w["rwkv_ln_b"]}),
         "mlstm_vec": _pack_vecs(MLSTM_VEC, MLSTM_VEC_N, {"conv_b": w["mlstm_conv_b"], "gate_b": w["mlstm_gate_b"],
                                                          "norm_w": w["mlstm_norm_w"]}),
         "attn_vec": _pack_vecs(ATTN_VEC, ATTN_VEC_N, {"q_norm": jnp.tile(w["attn_q_norm"], (1, ATTN_HEADS)),
                                                       "k_norm": jnp.tile(w["attn_k_norm"], (1, ATTN_HEADS))})}
    for name, mat in _split_w_in(w["w_in"]).items():
        P["w_in_" + name] = mat
    return P


def _pad_heads(m):
    return jnp.pad(m, [(0, 0)] * (m.ndim - 1) + [(0, LANES - N_HEADS)])[..., None, :]


def kernel(x_prompt, x_sample, c_prompt, c_sample, state_ssm, state_ssm_conv, state_rwkv, state_rwkv_shift, state_mlstm_c, state_mlstm_n, state_mlstm_m, state_mlstm_conv, cache_kv_w128, cache_kv_w512, cache_kv_w2048, norm_w, ada_w, ada_b, w_in, w_branch, w_out, ssm_conv_w, ssm_conv_b, ssm_dt_bias, ssm_a_log, ssm_d, ssm_norm_w, rwkv_mu, rwkv_w0, rwkv_w2, rwkv_a0, rwkv_a2, rwkv_k_k, rwkv_k_a, rwkv_r_k, rwkv_ln_w, rwkv_ln_b, mlstm_conv_w, mlstm_conv_b, mlstm_gate_b, mlstm_norm_w, attn_q_norm, attn_k_norm):
    P = _prepare_params(dict(
        norm_w=norm_w, ada_w=ada_w, ada_b=ada_b, w_in=w_in, w_branch=w_branch, w_out=w_out, ssm_conv_w=ssm_conv_w,
        ssm_conv_b=ssm_conv_b, ssm_dt_bias=ssm_dt_bias, ssm_a_log=ssm_a_log, ssm_d=ssm_d, ssm_norm_w=ssm_norm_w,
        rwkv_mu=rwkv_mu, rwkv_w0=rwkv_w0, rwkv_w2=rwkv_w2, rwkv_a0=rwkv_a0, rwkv_a2=rwkv_a2, rwkv_k_k=rwkv_k_k,
        rwkv_k_a=rwkv_k_a, rwkv_r_k=rwkv_r_k, rwkv_ln_w=rwkv_ln_w, rwkv_ln_b=rwkv_ln_b, mlstm_conv_w=mlstm_conv_w,
        mlstm_conv_b=mlstm_conv_b, mlstm_gate_b=mlstm_gate_b, mlstm_norm_w=mlstm_norm_w, attn_q_norm=attn_q_norm,
        attn_k_norm=attn_k_norm))
    bp = x_prompt.shape[0]
    fresh = {"ssm": jnp.zeros((1, bp, N_HEADS, HEAD_DIM, SSM_STATE), f32),
             "ssm_conv": jnp.zeros((1, bp, CONV_W - 1, SSM_CONV_CH), f32),
             "rwkv": jnp.zeros((1, bp, N_HEADS, HEAD_DIM, HEAD_DIM), f32),
             "rwkv_shift": jnp.zeros((1, bp, 1, RWKV_SHIFT_CH), f32),
             "mlstm_c": jnp.zeros((1, bp, N_HEADS, HEAD_DIM, HEAD_DIM), f32),
             "mlstm_n": jnp.zeros((1, bp, N_HEADS, HEAD_DIM), f32),
             "mlstm_m": jnp.zeros((1, bp, 1, LANES), f32),
             "mlstm_conv": jnp.zeros((1, bp, CONV_W - 1, 2 * WIDTH), f32)}
    carried = {"ssm": state_ssm, "ssm_conv": state_ssm_conv, "rwkv": state_rwkv,
               "rwkv_shift": state_rwkv_shift[:, :, None, :], "mlstm_c": state_mlstm_c, "mlstm_n": state_mlstm_n,
               "mlstm_m": _pad_heads(state_mlstm_m), "mlstm_conv": state_mlstm_conv}
    c_all = jnp.concatenate([c_prompt, c_sample], axis=0)
    caches_t = [jnp.transpose(c, (0, 1, 3, 4, 5, 2)) for c in (cache_kv_w128, cache_kv_w512, cache_kv_w2048)]
    y_prompt, y_sample = x_prompt, x_sample
    prompt_states, sample_states = [], []
    for layer in range(DEPTH):
        mod = _ada(c_all, P["ada_w"], P["ada_b"], layer)
        y_prompt, sp = _layer(y_prompt, mod[:bp], fresh, 0, None, P, layer)
        y_sample, ss = _layer(y_sample, mod[bp:], carried, layer, caches_t, P, layer)
        prompt_states.append(sp)
        sample_states.append(ss)
    stack = lambda states, name: jnp.stack([s[name] for s in states])
    return ((y_prompt, y_sample)
            + tuple(stack(prompt_states, n) for n in _STATE_NAMES)
            + tuple(stack(sample_states, n) for n in _STATE_NAMES))
```

```python
import functools
import math

import jax
import jax.numpy as jnp
from jax import lax
from jax.experimental import pallas as pl
from jax.experimental.pallas import tpu as pltpu

f32 = jnp.float32
bf16 = jnp.bfloat16
HI = lax.Precision.HIGHEST

D_MODEL = 1024
DEPTH = 2
HEAD_DIM = 64
NORM_EPS = 1e-6
CONV_W = 4
N_HEADS = 8
WIDTH = N_HEADS * HEAD_DIM
SSM_GROUPS = 2
SSM_STATE = 64
SSM_CONV_CH = WIDTH + 2 * SSM_GROUPS * SSM_STATE
RWKV_RANK = 64
RWKV_SHIFT_CH = 3 * WIDTH + 2 * RWKV_RANK
RWKV_LN_EPS = 64e-5
RWKV_CHUNK = 64
ATTN_GROUPS = ((128, 1), (512, 4), (2048, 16))
ATTN_HEADS = 4
ATTN_W = ATTN_HEADS * HEAD_DIM
ATTN_J = 128
MLSTM_HEAD_GROUP = 8
ATTN_UNROLL = 2
N_BRANCH = 4
LANES = 128
VMEM_LIMIT = 52 * 1024 * 1024

IN_SEGMENTS = (
    ("ssm_z", WIDTH), ("ssm_xbc", SSM_CONV_CH), ("ssm_dt", N_HEADS),
    ("rwkv_z", WIDTH), ("rwkv_shift", RWKV_SHIFT_CH),
    ("mlstm_z", WIDTH), ("mlstm_qk", 2 * WIDTH), ("mlstm_v", WIDTH),
    ("mlstm_o", WIDTH), ("mlstm_if", 2 * N_HEADS),
    ("attn_z", ATTN_W), ("attn_qkv", 9 * ATTN_W),
    ("merge", N_BRANCH * D_MODEL),
)

NN = (((1,), (0,)), ((), ()))
NT = (((1,), (1,)), ((), ()))
TN = (((0,), (0,)), ((), ()))


def _mm(a, b, dims, prec):
    if prec is None:
        a, b = a.astype(bf16), b.astype(bf16)
    return lax.dot_general(a, b, dims, preferred_element_type=f32, precision=prec)


def _dot(a, b, prec=None):
    return _mm(a, b, NN, prec)


def _dot_nt(a, b, prec=None):
    return _mm(a, b, NT, prec)


def _dot_tn(a, b, transpose_on_mxu=False):
    if not transpose_on_mxu:
        return _mm(a, b, TN, None)
    n = a.shape[1]
    eye = (_iota2((n, n), 0) == _iota2((n, n), 1)).astype(bf16)
    a_t = lax.dot_general(eye, a.astype(bf16), NT, preferred_element_type=f32).astype(bf16)
    return lax.dot_general(a_t, b.astype(bf16), NN, preferred_element_type=f32)


def _sigmoid(x):
    return 0.5 * jnp.tanh(0.5 * x) + 0.5


def _silu(x):
    return x * _sigmoid(x)


def _softplus(x):
    return jnp.maximum(x, 0.0) + jnp.log1p(jnp.exp(-jnp.abs(x)))


def _params(*sem):
    return pltpu.CompilerParams(dimension_semantics=sem, vmem_limit_bytes=VMEM_LIMIT)


def _iota2(shape, dim):
    return lax.broadcasted_iota(jnp.int32, shape, dim)


def _running_sum(x, c):
    tri = (_iota2((c, c), 0) >= _iota2((c, c), 1)).astype(f32)
    return _dot(tri, x, HI)


def _vec_layout(fields):
    out, off = {}, 0
    for name, size in fields:
        padded = -(-size // LANES) * LANES
        out[name] = (off, padded)
        off += padded
    return out, off


def _pack_vecs(layout, total, vecs):
    parts = []
    for name, (_, padded) in layout.items():
        v = vecs[name].reshape(DEPTH, -1)
        parts.append(jnp.pad(v, ((0, 0), (0, padded - v.shape[1]))))
    packed = jnp.concatenate(parts, axis=1)
    assert packed.shape[1] == total
    return packed[:, None, :]


def _vec_reader(ref, layout):
    return lambda name: ref[0, :, layout[name][0]:layout[name][0] + layout[name][1]]


SSM_VEC, SSM_VEC_N = _vec_layout((("conv_b", SSM_CONV_CH), ("dt_bias", N_HEADS), ("a_log", N_HEADS),
                                  ("d", N_HEADS), ("norm_w", WIDTH)))
RWKV_VEC, RWKV_VEC_N = _vec_layout((("mu", RWKV_SHIFT_CH), ("w0", WIDTH), ("a0", WIDTH), ("k_k", WIDTH),
                                    ("k_a", WIDTH), ("r_k", WIDTH), ("ln_w", WIDTH), ("ln_b", WIDTH)))
MLSTM_VEC, MLSTM_VEC_N = _vec_layout((("conv_b", 2 * WIDTH), ("gate_b", 2 * N_HEADS), ("norm_w", WIDTH)))
ATTN_VEC, ATTN_VEC_N = _vec_layout((("q_norm", ATTN_W), ("k_norm", ATTN_W)))


def _layer_spec(shape, layer, grid_rank):
    zeros = (0,) * (len(shape) - 1)
    if grid_rank == 1:
        return pl.BlockSpec((1,) + tuple(shape[1:]), lambda i: (layer,) + zeros)
    return pl.BlockSpec((1,) + tuple(shape[1:]), lambda i, j: (layer,) + zeros)


def _ada_kernel(c_ref, w_ref, b_ref, o_ref):
    o_ref[...] = _dot(_silu(c_ref[...]), w_ref[0], HI) + b_ref[0]


def _ada(c, w, b3, layer):
    n = c.shape[0]
    return pl.pallas_call(
        _ada_kernel,
        grid=(3,),
        in_specs=[pl.BlockSpec((n, D_MODEL), lambda j: (0, 0)),
                  pl.BlockSpec((1, D_MODEL, D_MODEL), lambda j: (layer, 0, j)),
                  pl.BlockSpec((1, 1, D_MODEL), lambda j: (layer, 0, j))],
        out_specs=pl.BlockSpec((n, D_MODEL), lambda j: (0, j)),
        out_shape=jax.ShapeDtypeStruct((n, 3 * D_MODEL), f32),
        compiler_params=_params("parallel"),
        name="ada_mod",
    )(c, w, b3)


def _norm_kernel(x_ref, nw_ref, sc_ref, sh_ref, o_ref):
    x = x_ref[0]
    r = x * lax.rsqrt(jnp.mean(x * x, axis=-1, keepdims=True) + NORM_EPS)
    o_ref[0] = (r * nw_ref[0] * (1.0 + sc_ref[0]) + sh_ref[0]).astype(o_ref.dtype)


def _norm(x3, nw3, layer, sc3, sh3, tl):
    b, l, d = x3.shape
    per_row = sc3.shape[1] != 1
    mod_spec = pl.BlockSpec((1, tl if per_row else 1, d), (lambda i, j: (i, j, 0)) if per_row else (lambda i, j: (i, 0, 0)))
    return pl.pallas_call(
        _norm_kernel,
        grid=(b, l // tl),
        in_specs=[pl.BlockSpec((1, tl, d), lambda i, j: (i, j, 0)),
                  _layer_spec(nw3.shape, layer, 2), mod_spec, mod_spec],
        out_specs=pl.BlockSpec((1, tl, d), lambda i, j: (i, j, 0)),
        out_shape=jax.ShapeDtypeStruct((b, l, d), bf16),
        compiler_params=_params("parallel", "parallel"),
        name="mod_rmsnorm",
    )(x3, nw3, sc3, sh3)


def _mm_kernel(x_ref, wt_ref, o_ref, w_scr):
    @pl.when(pl.program_id(1) == 0)
    def _():
        w_scr[...] = wt_ref[0].T.astype(bf16)

    o_ref[...] = jnp.dot(x_ref[...], w_scr[...], preferred_element_type=f32).astype(o_ref.dtype)


def _matmul(x, w_t, layer, col0, width, tm, tn, out_dtype):
    n, k = x.shape
    w_spec = pl.BlockSpec((pl.Element(1), pl.Element(tn), pl.Element(k)),
                          lambda j, i: (layer, pl.multiple_of(col0 + j * tn, 8), 0))
    return pl.pallas_call(
        _mm_kernel,
        grid=(width // tn, n // tm),
        in_specs=[pl.BlockSpec((tm, k), lambda j, i: (i, 0)), w_spec],
        out_specs=pl.BlockSpec((tm, tn), lambda j, i: (i, j)),
        out_shape=jax.ShapeDtypeStruct((n, width), out_dtype),
        scratch_shapes=[pltpu.VMEM((k, tn), bf16)],
        compiler_params=_params("parallel", "arbitrary"),
        name="in_proj",
    )(x, w_t)


def _causal_conv(ext_scr, u, c, cw_ref, bias):
    ext_scr[8:8 + c, :] = u
    acc = bias
    for i in range(CONV_W):
        acc = acc + ext_scr[5 + i:5 + i + c, :] * cw_ref[0, i:i + 1, :]
    tail = ext_scr[c + 5:c + 8, :]
    ext_scr[5:8, :] = tail
    return acc, tail


def _interleave(gens):
    gens = list(gens)
    while gens:
        alive = []
        for g in gens:
            try:
                next(g)
                alive.append(g)
            except StopIteration:
                pass
        gens = alive


def _ssm_kernel(proj_ref, conv0_ref, h0_ref, cw_ref, vec_ref, out_ref, hn_ref, convn_ref, ext_scr, y_scr, *, c, nb):
    vec = _vec_reader(vec_ref, SSM_VEC)

    @pl.when(pl.program_id(1) == 0)
    def _():
        for bi in range(nb):
            ext_scr[bi, 5:8, :] = conv0_ref[0, bi]
            hn_ref[bi] = h0_ref[0, bi]

    causal = _iota2((c, c), 0) >= _iota2((c, c), 1)
    heads = range(N_HEADS)
    per_group = N_HEADS // SSM_GROUPS
    sls = [slice(h * HEAD_DIM, (h + 1) * HEAD_DIM) for h in heads]

    def elem(bi):
        proj, out, hn, y_acc = proj_ref.at[bi], out_ref.at[bi], hn_ref.at[bi], y_scr.at[bi]
        u = proj[:, WIDTH:WIDTH + SSM_CONV_CH].astype(f32)
        acc, tail = _causal_conv(ext_scr.at[bi], u, c, cw_ref, vec("conv_b"))
        convn_ref[bi] = tail
        xbc = _silu(acc)
        xs = xbc[:, :WIDTH]
        bm = xbc[:, WIDTH:WIDTH + LANES]
        cm = xbc[:, WIDTH + LANES:]
        dt = _softplus(proj[:, WIDTH + SSM_CONV_CH:].astype(f32) + vec("dt_bias"))
        da = dt * (-jnp.exp(vec("a_log")))
        yield
        cs = _running_sum(da, c)
        cs_t = cs.T
        dt_t = dt.T
        cs_end = cs[c - 1:c, :]
        dsk = vec("d")
        bgs = [bm[:, g * SSM_STATE:(g + 1) * SSM_STATE] for g in range(SSM_GROUPS)]
        cgs = [cm[:, g * SSM_STATE:(g + 1) * SSM_STATE] for g in range(SSM_GROUPS)]
        yield
        gmats = [_dot_nt(cgs[g], bgs[g]) for g in range(SSM_GROUPS)]
        cs_cols = [cs[:, h:h + 1] for h in heads]
        yield
        lmats = [gmats[h // per_group] * jnp.exp(jnp.where(causal, cs_cols[h] - cs_t[h:h + 1, :], -jnp.inf))
                 * dt_t[h:h + 1, :] for h in heads]
        xhs = [xs[:, sl] for sl in sls]
        hprevs = [hn[h] for h in heads]
        yield
        y_in = [_dot(lmats[h], xhs[h]) for h in heads]
        y_st = [_dot_nt(cgs[h // per_group], hprevs[h]) for h in heads]
        yield
        for h in heads:
            y_acc[:, sls[h]] = y_in[h] + y_st[h] * jnp.exp(cs_cols[h]) + xhs[h] * dsk[:, h:h + 1]
        ces = [cs_end[:, h:h + 1] for h in heads]
        upd = [_dot_tn(xhs[h] * (jnp.exp(ces[h] - cs_cols[h]) * dt[:, h:h + 1]), bgs[h // per_group],
                       transpose_on_mxu=True) for h in heads]
        yield
        for h in heads:
            hn[h] = hprevs[h] * jnp.exp(ces[h]) + upd[h]
        gy = y_acc[...] * _silu(proj[:, :WIDTH].astype(f32))
        nw = vec("norm_w")
        gw = WIDTH // SSM_GROUPS
        for g in range(SSM_GROUPS):
            part = gy[:, g * gw:(g + 1) * gw]
            ms = jnp.mean(part * part, axis=-1, keepdims=True)
            out[:, g * gw:(g + 1) * gw] = part * lax.rsqrt(ms + NORM_EPS) * nw[:, g * gw:(g + 1) * gw]

    _interleave(elem(bi) for bi in range(nb))


def _state_spec(shape, sidx, nb):
    zeros = (0,) * (len(shape) - 2)
    return pl.BlockSpec((1, nb) + tuple(shape[2:]), lambda i, j: (sidx, i) + zeros)


def _batch_spec(nb, *dims):
    zeros = (0,) * len(dims)
    return pl.BlockSpec((nb,) + dims, lambda i, j: (i,) + zeros)


def _ssm(proj, conv0, h0, sidx, P, layer, c, nb):
    b, l, wp = proj.shape
    return pl.pallas_call(
        functools.partial(_ssm_kernel, c=c, nb=nb),
        grid=(b // nb, l // c),
        in_specs=[pl.BlockSpec((nb, c, wp), lambda i, j: (i, j, 0)),
                  _state_spec(conv0.shape, sidx, nb), _state_spec(h0.shape, sidx, nb),
                  _layer_spec(P["ssm_conv_w"].shape, layer, 2), _layer_spec(P["ssm_vec"].shape, layer, 2)],
        out_specs=[pl.BlockSpec((nb, c, WIDTH), lambda i, j: (i, j, 0)),
                   _batch_spec(nb, N_HEADS, HEAD_DIM, SSM_STATE), _batch_spec(nb, CONV_W - 1, SSM_CONV_CH)],
        out_shape=[jax.ShapeDtypeStruct((b, l, WIDTH), f32),
                   jax.ShapeDtypeStruct((b, N_HEADS, HEAD_DIM, SSM_STATE), f32),
                   jax.ShapeDtypeStruct((b, CONV_W - 1, SSM_CONV_CH), f32)],
        scratch_shapes=[pltpu.VMEM((nb, c + 8, SSM_CONV_CH), f32), pltpu.VMEM((nb, c, WIDTH), f32)],
        compiler_params=_params("parallel", "arbitrary"),
        name="ssd_branch",
    )(proj, conv0, h0, P["ssm_conv_w"], P["ssm_vec"])


def _rwkv_chunk_kernel(proj_ref, shift0_ref, s0_ref, vec_ref, w2_ref, a2_ref,
                       out_ref, sn_ref, shiftn_ref, ext_scr, *, c, rows, nb):
    vec = _vec_reader(vec_ref, RWKV_VEC)

    @pl.when(pl.program_id(1) == 0)
    def _():
        for bi in range(nb):
            ext_scr[bi, 7:8, :] = shift0_ref[0, bi]
            sn_ref[bi] = s0_ref[0, bi]

    row2 = _iota2((c, 2 * c), 0)
    col2 = _iota2((c, 2 * c), 1)
    col2 = jnp.where(col2 >= c, col2 - c, col2)
    strict2 = row2 > col2
    incl2 = row2 >= col2
    n_double = (c - 1).bit_length()
    heads = range(N_HEADS)
    sls = [slice(h * HEAD_DIM, (h + 1) * HEAD_DIM) for h in heads]

    def elem(bi):
        proj, out, sn, ext = proj_ref.at[bi], out_ref.at[bi], sn_ref.at[bi], ext_scr.at[bi]
        u = proj[:, WIDTH:].astype(f32)
        ext[8:8 + rows, :] = u
        sh = u + (ext[7:7 + rows, :] - u) * vec("mu")
        last = ext[rows + 7:rows + 8, :]
        ext[7:8, :] = last
        shiftn_ref[bi] = last
        r_all = sh[:, :WIDTH]
        k = sh[:, WIDTH:2 * WIDTH]
        v_all = sh[:, 2 * WIDTH:3 * WIDTH]
        w_lo = sh[:, 3 * WIDTH:3 * WIDTH + RWKV_RANK]
        a_lo = sh[:, 3 * WIDTH + RWKV_RANK:]
        yield
        w_log = -_softplus(-(vec("w0") + _dot(jnp.tanh(w_lo), w2_ref[0]))) - 0.5
        lw_all = -jnp.exp(w_log)
        a_in_all = _sigmoid(vec("a0") + _dot(a_lo, a2_ref[0]))
        kk_all = k * vec("k_k")
        k2_all = k * (1.0 + (a_in_all - 1.0) * vec("k_a"))
        rk, lnw, lnb = vec("r_k"), vec("ln_w"), vec("ln_b")
        yield
        for c0 in range(0, rows, c):
            cr = slice(c0, c0 + c)
            r, v, lw, a_in, kk, k2 = r_all[cr], v_all[cr], lw_all[cr], a_in_all[cr], kk_all[cr], k2_all[cr]
            cl = _running_sum(lw, c)
            cl_end = cl[c - 1:c, :]
            g_fwd = jnp.exp(cl)
            g_inv = jnp.exp(-cl)
            g_rem = jnp.exp(cl_end - cl)
            g_prev = jnp.exp(cl - lw)
            g_end = jnp.exp(cl_end)
            yield
            s0s = [sn[h] for h in heads]
            bhs, ars, bks = [], [], []
            for sl in sls:
                kkh = kk[:, sl]
                kkn = kkh * (1.0 / jnp.maximum(jnp.sqrt(jnp.sum(kkh * kkh, axis=-1, keepdims=True)), 1e-12))
                bhs.append(kkn * a_in[:, sl])
                ars.append(jnp.concatenate([-kkn * g_prev[:, sl], r[:, sl] * g_fwd[:, sl]], axis=0))
                bks.append(jnp.concatenate([bhs[-1] * g_inv[:, sl], k2[:, sl] * g_inv[:, sl]], axis=0))
            yield
            m4s = [_dot_nt(ars[h], bks[h]) for h in heads]
            ahs = [_dot_nt(ars[h], s0s[h]) for h in heads]
            yield
            tops = [jnp.where(strict2, m4[:c, :], 0.0) for m4 in m4s]
            bots = [jnp.where(incl2, m4[c:, :], 0.0) for m4 in m4s]
            amats = [top[:, :c].astype(bf16) for top in tops]
            xs = [ahs[h][:c] + _dot(tops[h][:, c:], v[:, sls[h]]) for h in heads]
            yield
            for i in range(n_double):
                xs = [xs[h] + _dot(amats[h], xs[h]) for h in heads]
                if i + 1 < n_double:
                    amats = [_dot(amats[h], amats[h]).astype(bf16) for h in heads]
                yield
            pvs = [jnp.concatenate([xs[h], v[:, sls[h]]], axis=0) for h in heads]
            ys = [ahs[h][c:] + _dot(bots[h], pvs[h]) for h in heads]
            upd = [_dot_tn(pvs[h], jnp.concatenate([bhs[h] * g_rem[:, sls[h]], k2[:, sls[h]] * g_rem[:, sls[h]]],
                                                   axis=0)) for h in heads]
            yield
            for h in heads:
                sl = sls[h]
                sn[h] = s0s[h] * g_end[:, sl] + upd[h]
                y = ys[h]
                mu = jnp.mean(y, axis=-1, keepdims=True)
                var = jnp.mean(jnp.square(y - mu), axis=-1, keepdims=True)
                ln = (y - mu) * lax.rsqrt(var + RWKV_LN_EPS) * lnw[:, sl] + lnb[:, sl]
                bonus = jnp.sum(r[:, sl] * k2[:, sl] * rk[:, sl], axis=-1, keepdims=True) * v[:, sl]
                out[cr, sl] = (ln + bonus) * _silu(proj[cr, sl].astype(f32))
            yield

    _interleave(elem(bi) for bi in range(nb))


def _rwkv(proj, shift0, s0, sidx, P, layer, c, rows, nb):
    b, l, wp = proj.shape
    return pl.pallas_call(
        functools.partial(_rwkv_chunk_kernel, c=c, rows=rows, nb=nb),
        grid=(b // nb, l // rows),
        in_specs=[pl.BlockSpec((nb, rows, wp), lambda i, j: (i, j, 0)),
                  _state_spec(shift0.shape, sidx, nb), _state_spec(s0.shape, sidx, nb),
                  _layer_spec(P["rwkv_vec"].shape, layer, 2),
                  _layer_spec(P["rwkv_w2"].shape, layer, 2), _layer_spec(P["rwkv_a2"].shape, layer, 2)],
        out_specs=[pl.BlockSpec((nb, rows, WIDTH), lambda i, j: (i, j, 0)),
                   _batch_spec(nb, N_HEADS, HEAD_DIM, HEAD_DIM), _batch_spec(nb, 1, RWKV_SHIFT_CH)],
        out_shape=[jax.ShapeDtypeStruct((b, l, WIDTH), f32),
                   jax.ShapeDtypeStruct((b, N_HEADS, HEAD_DIM, HEAD_DIM), f32),
                   jax.ShapeDtypeStruct((b, 1, RWKV_SHIFT_CH), f32)],
        scratch_shapes=[pltpu.VMEM((nb, rows + 8, RWKV_SHIFT_CH), f32)],
        compiler_params=_params("parallel", "arbitrary"),
        name="rwkv7_chunked",
    )(proj, shift0, s0, P["rwkv_vec"], P["rwkv_w2"], P["rwkv_a2"])


def _mlstm_kernel(proj_ref, conv0_ref, c0_ref, n0_ref, m0_ref, cw_ref, vec_ref,
                  out_ref, cn_ref, nn_ref, mn_ref, convn_ref, ext_scr, h_scr, *, c, nb):
    vec = _vec_reader(vec_ref, MLSTM_VEC)

    @pl.when(pl.program_id(1) == 0)
    def _():
        for bi in range(nb):
            ext_scr[bi, 5:8, :] = conv0_ref[0, bi]
            cn_ref[bi] = c0_ref[0, bi]
            nn_ref[bi] = n0_ref[0, bi]
            mn_ref[bi] = m0_ref[0, bi]

    causal = _iota2((c, c), 0) >= _iota2((c, c), 1)
    lane = _iota2((1, LANES), 1)
    heads = range(N_HEADS)
    sls = [slice(h * HEAD_DIM, (h + 1) * HEAD_DIM) for h in heads]

    def elem(bi):
        proj, out, cn, nn, h_acc = proj_ref.at[bi], out_ref.at[bi], cn_ref.at[bi], nn_ref.at[bi], h_scr.at[bi]
        u = proj[:, WIDTH:3 * WIDTH].astype(f32)
        acc, tail = _causal_conv(ext_scr.at[bi], u, c, cw_ref, vec("conv_b"))
        convn_ref[bi] = tail
        qk = _silu(acc)
        gates = proj[:, 5 * WIDTH:].astype(f32) + vec("gate_b")
        lf = -_softplus(-gates)
        yield
        bc = _running_sum(lf, c)
        bc_t = bc.T
        ig_t = gates.T
        m_prev = mn_ref[bi]
        yield
        m_out = jnp.zeros((1, LANES), f32)
        for grp in range(0, N_HEADS, MLSTM_HEAD_GROUP):
            heads = range(grp, grp + MLSTM_HEAD_GROUP)
            b_cols = {h: bc[:, N_HEADS + h:N_HEADS + h + 1] for h in heads}
            fold = c <= 8
            emats = {h: jnp.where(causal, (b_cols[h] if fold else 0.0) + ig_t[h:h + 1, :]
                                  - bc_t[N_HEADS + h:N_HEADS + h + 1, :], -jnp.inf) for h in heads}
            ecols = {h: 0.0 if fold else b_cols[h] for h in heads}
            m0s = {h: m_prev[:, h:h + 1] for h in heads}
            gcols = {h: b_cols[h] + m0s[h] for h in heads}
            mts = {h: jnp.maximum(gcols[h], ecols[h] + jnp.max(emats[h], axis=-1, keepdims=True)) for h in heads}
            yield
            qhs = {h: qk[:, sls[h]] for h in heads}
            khs = {h: qk[:, WIDTH + h * HEAD_DIM:WIDTH + (h + 1) * HEAD_DIM] * HEAD_DIM ** -0.5 for h in heads}
            vhs = {h: proj[:, 3 * WIDTH + h * HEAD_DIM:3 * WIDTH + (h + 1) * HEAD_DIM].astype(f32) for h in heads}
            cprevs = {h: cn[h] for h in heads}
            nprevs = {h: nn[h:h + 1, :] for h in heads}
            yield
            sws = {h: jnp.exp(emats[h] + (ecols[h] - mts[h])) * _dot_nt(qhs[h], khs[h]) for h in heads}
            wis = {h: jnp.exp(gcols[h] - mts[h]) for h in heads}
            qcs = {h: _dot(qhs[h], cprevs[h]) for h in heads}
            yield
            nums = {h: _dot(sws[h], vhs[h]) + wis[h] * qcs[h] for h in heads}
            yield
            for h in heads:
                den = (jnp.sum(sws[h], axis=-1, keepdims=True)
                       + wis[h] * jnp.sum(qhs[h] * nprevs[h], axis=-1, keepdims=True))
                h_acc[:, sls[h]] = nums[h] * (1.0 / jnp.maximum(jnp.abs(den), jnp.exp(-mts[h])))
            m_news = {h: mts[h][c - 1:c, :] for h in heads}
            b_ends = {h: b_cols[h][c - 1:c, :] for h in heads}
            kws = {h: khs[h] * jnp.exp(b_ends[h] - b_cols[h] + gates[:, h:h + 1] - m_news[h]) for h in heads}
            yield
            kvs = {h: _dot_tn(kws[h], vhs[h]) for h in heads}
            yield
            for h in heads:
                dec = jnp.exp(b_ends[h] + m0s[h] - m_news[h])
                cn[h] = dec * cprevs[h] + kvs[h]
                nn[h:h + 1, :] = dec * nprevs[h] + jnp.sum(kws[h], axis=0, keepdims=True)
                m_out = jnp.where(lane == h, m_news[h], m_out)
        mn_ref[bi] = m_out
        nw = vec("norm_w")
        for h in range(N_HEADS):
            sl = sls[h]
            o_gate = proj[:, 4 * WIDTH + h * HEAD_DIM:4 * WIDTH + (h + 1) * HEAD_DIM].astype(f32)
            hm = _sigmoid(o_gate) * h_acc[:, sl]
            ms = jnp.mean(hm * hm, axis=-1, keepdims=True)
            out[:, sl] = hm * lax.rsqrt(ms + NORM_EPS) * nw[:, sl] * _silu(proj[:, sl].astype(f32))

    _interleave(elem(bi) for bi in range(nb))


def _mlstm(proj, conv0, c0, n0, m0, sidx, P, layer, c, nb):
    b, l, wp = proj.shape
    outs = pl.pallas_call(
        functools.partial(_mlstm_kernel, c=c, nb=nb),
        grid=(b // nb, l // c),
        in_specs=[pl.BlockSpec((nb, c, wp), lambda i, j: (i, j, 0)),
                  _state_spec(conv0.shape, sidx, nb), _state_spec(c0.shape, sidx, nb),
                  _state_spec(n0.shape, sidx, nb), _state_spec(m0.shape, sidx, nb),
                  _layer_spec(P["mlstm_conv_w"].shape, layer, 2), _layer_spec(P["mlstm_vec"].shape, layer, 2)],
        out_specs=[pl.BlockSpec((nb, c, WIDTH), lambda i, j: (i, j, 0)),
                   _batch_spec(nb, N_HEADS, HEAD_DIM, HEAD_DIM), _batch_spec(nb, N_HEADS, HEAD_DIM),
                   _batch_spec(nb, 1, LANES), _batch_spec(nb, CONV_W - 1, 2 * WIDTH)],
        out_shape=[jax.ShapeDtypeStruct((b, l, WIDTH), f32),
                   jax.ShapeDtypeStruct((b, N_HEADS, HEAD_DIM, HEAD_DIM), f32),
                   jax.ShapeDtypeStruct((b, N_HEADS, HEAD_DIM), f32),
                   jax.ShapeDtypeStruct((b, 1, LANES), f32),
                   jax.ShapeDtypeStruct((b, CONV_W - 1, 2 * WIDTH), f32)],
        scratch_shapes=[pltpu.VMEM((nb, c + 8, 2 * WIDTH), f32), pltpu.VMEM((nb, c, WIDTH), f32)],
        compiler_params=_params("parallel", "arbitrary"),
        name="mlstm_branch",
    )(proj, conv0, c0, n0, m0, P["mlstm_conv_w"], P["mlstm_vec"])
    out, cn, nn, mn, convn = outs
    return out, cn, nn, mn[:, 0, :N_HEADS], convn


def _head_mean_matrix():
    hid = jnp.arange(ATTN_W) // HEAD_DIM
    return ((hid[:, None] == hid[None, :]).astype(f32) / HEAD_DIM).astype(bf16)


def _qk_norm(x, hm_ref, w):
    sq = x * x
    hi = sq.astype(bf16)
    lo = (sq - hi.astype(f32)).astype(bf16)
    ms = _dot(hi, hm_ref[...]) + _dot(lo, hm_ref[...])
    return x * lax.rsqrt(ms + NORM_EPS) * w


def _attn_prompt_kernel(q_ref, k_ref, v_ref, z_ref, hm_ref, vec_ref, out_ref, kv0_ref, kv1_ref, kv2_ref,
                        qn_scr, kn_scr, vv_scr, acc_scr, m_scr, d_scr, to_scr, tl_scr, *, l):
    vec = _vec_reader(vec_ref, ATTN_VEC)
    g = pl.program_id(1)
    rb = 256
    n_pairs = ATTN_W // LANES

    def norm_body(i, carry):
        rows = pl.ds(pl.multiple_of(i * rb, rb), rb)
        qn = _qk_norm(q_ref[0, rows, :].astype(f32), hm_ref, vec("q_norm"))
        kn = _qk_norm(k_ref[0, rows, :].astype(f32), hm_ref, vec("k_norm"))
        v = v_ref[0, rows, :].astype(f32)
        for p in range(n_pairs):
            qn_scr[p, rows, :] = qn[:, p * LANES:(p + 1) * LANES]
            kn_scr[p, rows, :] = kn[:, p * LANES:(p + 1) * LANES]
            vv_scr[p, rows, :] = v[:, p * LANES:(p + 1) * LANES]
        return carry

    lax.fori_loop(0, l // rb, norm_body, 0)

    qb = ATTN_J
    qi = _iota2((qb, qb), 0)
    ki = _iota2((qb, qb), 1)
    scale = HEAD_DIM ** -0.5

    def run_group(dil, first):
        n_iter = l // qb

        def block(i, slot):
            r = i % dil
            blk = i // dil
            u0 = blk * qb
            rows_c = pl.ds(u0 * dil + r, qb, stride=dil)
            rows_p = pl.ds(jnp.maximum(u0 - qb, 0) * dil + r, qb, stride=dil)
            mask_c = ki <= qi
            mask_p = (ki >= qi) & (blk > 0)
            hsl = [slice(hh * HEAD_DIM, (hh + 1) * HEAD_DIM) for hh in range(LANES // HEAD_DIM)]
            heads = [(p, sl) for p in range(n_pairs) for sl in hsl]
            qv = [qn_scr[p, rows_c, :] for p in range(n_pairs)]
            kc = [kn_scr[p, rows_c, :] for p in range(n_pairs)]
            vc = [vv_scr[p, rows_c, :] for p in range(n_pairs)]
            kp = [kn_scr[p, rows_p, :] for p in range(n_pairs)]
            vp = [vv_scr[p, rows_p, :] for p in range(n_pairs)]
            yield
            scs = [jnp.where(mask_c, _dot_nt(qv[p][:, sl], kc[p][:, sl]) * scale, -jnp.inf) for p, sl in heads]
            sps = [jnp.where(mask_p, _dot_nt(qv[p][:, sl], kp[p][:, sl]) * scale, -jnp.inf) for p, sl in heads]
            yield
            ms = [jnp.max(jnp.maximum(sc, sp), axis=-1, keepdims=True) for sc, sp in zip(scs, sps)]
            pcs = [jnp.exp(sc - m) for sc, m in zip(scs, ms)]
            pps = [jnp.exp(sp - m) for sp, m in zip(sps, ms)]
            dens = [jnp.sum(pc + pp, axis=-1, keepdims=True) for pc, pp in zip(pcs, pps)]
            yield
            for (p, sl), pc, pp, m, den in zip(heads, pcs, pps, ms, dens):
                to_scr[slot, p, :, sl] = (_dot(pc, vc[p][:, sl]) + _dot(pp, vp[p][:, sl])) * (1.0 / den)
                tl_scr[slot, p, :, sl] = jnp.broadcast_to(m + jnp.log(den), (qb, HEAD_DIM))
            yield
            for p in range(n_pairs):
                o = to_scr[slot, p]
                lse = tl_scr[slot, p]
                if first:
                    acc_scr[p, rows_c, :] = o
                    m_scr[p, rows_c, :] = lse
                    d_scr[p, rows_c, :] = jnp.ones_like(lse)
                else:
                    m_old = m_scr[p, rows_c, :]
                    m_new = jnp.maximum(m_old, lse)
                    a_old = jnp.exp(m_old - m_new)
                    a_new = jnp.exp(lse - m_new)
                    acc_scr[p, rows_c, :] = acc_scr[p, rows_c, :] * a_old + o * a_new
                    d_scr[p, rows_c, :] = d_scr[p, rows_c, :] * a_old + a_new
                    m_scr[p, rows_c, :] = m_new

        def body(i, carry):
            _interleave(block(i * ATTN_UNROLL + s, s) for s in range(ATTN_UNROLL))
            return carry

        lax.fori_loop(0, n_iter // ATTN_UNROLL, body, 0)

    def save_window(kv_ref):
        w = kv_ref.shape[1]
        wb = min(w, rb)
        for r0 in range(0, w, wb):
            src = slice(l - w + r0, l - w + r0 + wb)
            for p in range(n_pairs):
                kv_ref[0, r0:r0 + wb, p * LANES:(p + 1) * LANES] = kn_scr[p, src, :]
                kv_ref[0, r0:r0 + wb, ATTN_W + p * LANES:ATTN_W + (p + 1) * LANES] = vv_scr[p, src, :]

    def group_step(gi, dil):
        save_window((kv0_ref, kv1_ref, kv2_ref)[gi])
        run_group(dil, gi == 0)

    for gi, (_, dil) in enumerate(ATTN_GROUPS):
        pl.when(g == gi)(functools.partial(group_step, gi, dil))

    @pl.when(g == len(ATTN_GROUPS) - 1)
    def _():
        def fin_body(i, carry):
            rows = pl.ds(pl.multiple_of(i * rb, rb), rb)
            for p in range(n_pairs):
                lanes = slice(p * LANES, (p + 1) * LANES)
                z = z_ref[0, rows, lanes].astype(f32)
                out_ref[0, rows, lanes] = acc_scr[p, rows, :] / d_scr[p, rows, :] * _silu(z)
            return carry

        lax.fori_loop(0, l // rb, fin_body, 0)


def _attn_prompt(proj, P, layer):
    b, l, _ = proj.shape
    ng = len(ATTN_GROUPS)
    col = lambda base: pl.BlockSpec((1, l, ATTN_W), lambda i, g: (i, 0, base + g))
    pair_scr = pltpu.VMEM((ATTN_W // LANES, l, LANES), f32)
    windows = [min(w, l) for w, _ in ATTN_GROUPS]
    return pl.pallas_call(
        functools.partial(_attn_prompt_kernel, l=l),
        grid=(b, ng),
        in_specs=[col(1), col(1 + ng), col(1 + 2 * ng),
                  pl.BlockSpec((1, l, ATTN_W), lambda i, g: (i, 0, 0)),
                  pl.BlockSpec((ATTN_W, ATTN_W), lambda i, g: (0, 0)),
                  _layer_spec(P["attn_vec"].shape, layer, 2)],
        out_specs=[pl.BlockSpec((1, l, ATTN_W), lambda i, g: (i, 0, 0))]
                  + [pl.BlockSpec((1, w, 2 * ATTN_W), lambda i, g: (i, 0, 0)) for w in windows],
        out_shape=[jax.ShapeDtypeStruct((b, l, ATTN_W), f32)]
                  + [jax.ShapeDtypeStruct((b, w, 2 * ATTN_W), f32) for w in windows],
        scratch_shapes=[pair_scr] * 6 + [pltpu.VMEM((ATTN_UNROLL, ATTN_W // LANES, ATTN_J, LANES), f32)] * 2,
        compiler_params=_params("parallel", "arbitrary"),
        name="dilated_attn_prompt",
    )(proj, proj, proj, proj, P["head_mean"], P["attn_vec"])


def _attn_step_kernel(x_ref, c0_ref, c1_ref, c2_ref, hm_ref, vec_ref, out_ref, kv_ref, *, l, nb):
    vec = _vec_reader(vec_ref, ATTN_VEC)
    ng = len(ATTN_GROUPS)
    scale = HEAD_DIM ** -0.5
    c_refs = (c0_ref, c1_ref, c2_ref)
    hsl = [slice(h * HEAD_DIM, (h + 1) * HEAD_DIM) for h in range(ATTN_HEADS)]
    masks_p, masks_n = [], []
    for gi, (window, dil) in enumerate(ATTN_GROUPS):
        w = c_refs[gi].shape[-1]
        jmax = window // dil
        shift = dil.bit_length() - 1
        dist_p = w + _iota2((l, w), 0) - _iota2((l, w), 1)
        masks_p.append(((dist_p & (dil - 1)) == 0) & ((dist_p >> shift) <= jmax))
        dist_n = _iota2((l, l), 0) - _iota2((l, l), 1)
        masks_n.append((dist_n >= 0) & ((dist_n & (dil - 1)) == 0) & ((dist_n >> shift) <= jmax))
    gh = [(gi, h) for gi in range(ng) for h in range(ATTN_HEADS)]

    def elem(bi):
        x = x_ref[bi]
        qns, kns, vs = [], [], []
        for gi in range(ng):
            qns.append(_qk_norm(x[:, (1 + gi) * ATTN_W:(2 + gi) * ATTN_W], hm_ref, vec("q_norm")))
            kns.append(_qk_norm(x[:, (1 + ng + gi) * ATTN_W:(2 + ng + gi) * ATTN_W], hm_ref, vec("k_norm")))
            vs.append(x[:, (1 + 2 * ng + gi) * ATTN_W:(2 + 2 * ng + gi) * ATTN_W])
            kv_ref[bi, gi, :, 0:ATTN_W] = kns[gi]
            kv_ref[bi, gi, :, ATTN_W:2 * ATTN_W] = vs[gi]
        yield
        sps = [jnp.where(masks_p[gi], _dot(qns[gi][:, hsl[h]], c_refs[gi][0, bi, 0, h]) * scale, -jnp.inf)
               for gi, h in gh]
        sns = [jnp.where(masks_n[gi], _dot_nt(qns[gi][:, hsl[h]], kns[gi][:, hsl[h]]) * scale, -jnp.inf)
               for gi, h in gh]
        yield
        ms = [jnp.maximum(jnp.max(sp, axis=-1, keepdims=True), jnp.max(sn, axis=-1, keepdims=True))
              for sp, sn in zip(sps, sns)]
        pps = [jnp.exp(sp - m) for sp, m in zip(sps, ms)]
        pns = [jnp.exp(sn - m) for sn, m in zip(sns, ms)]
        dens = [jnp.sum(pp, axis=-1, keepdims=True) + jnp.sum(pn, axis=-1, keepdims=True)
                for pp, pn in zip(pps, pns)]
        yield
        og = [(_dot_nt(pp, c_refs[gi][0, bi, 1, h]) + _dot(pn, vs[gi][:, hsl[h]])) * (1.0 / den)
              for (gi, h), pp, pn, den in zip(gh, pps, pns, dens)]
        lg = [m + jnp.log(den) for m, den in zip(ms, dens)]
        yield
        outs = [og[gi * ATTN_HEADS:(gi + 1) * ATTN_HEADS] for gi in range(ng)]
        lses = [lg[gi * ATTN_HEADS:(gi + 1) * ATTN_HEADS] for gi in range(ng)]
        for h in range(ATTN_HEADS):
            sl = hsl[h]
            mx = functools.reduce(jnp.maximum, [lses[gi][h] for gi in range(ng)])
            ws = [jnp.exp(lses[gi][h] - mx) for gi in range(ng)]
            tot = functools.reduce(lambda a, b: a + b, ws)
            o = functools.reduce(lambda a, b: a + b, [outs[gi][h] * (ws[gi] / tot) for gi in range(ng)])
            out_ref[bi, :, sl] = o * _silu(x[:, sl])

    _interleave(elem(bi) for bi in range(nb))


def _attn_step(proj, caches, P, layer, nb):
    b, l, wp = proj.shape
    ng = len(ATTN_GROUPS)
    cache_specs = [pl.BlockSpec((1, nb) + c.shape[2:], lambda i: (layer, i, 0, 0, 0, 0)) for c in caches]
    return pl.pallas_call(
        functools.partial(_attn_step_kernel, l=l, nb=nb),
        grid=(b // nb,),
        in_specs=[pl.BlockSpec((nb, l, wp), lambda i: (i, 0, 0))] + cache_specs
                 + [pl.BlockSpec((ATTN_W, ATTN_W), lambda i: (0, 0)), _layer_spec(P["attn_vec"].shape, layer, 1)],
        out_specs=[pl.BlockSpec((nb, l, ATTN_W), lambda i: (i, 0, 0)),
                   pl.BlockSpec((nb, ng, l, 2 * ATTN_W), lambda i: (i, 0, 0, 0))],
        out_shape=[jax.ShapeDtypeStruct((b, l, ATTN_W), f32),
                   jax.ShapeDtypeStruct((b, ng, l, 2 * ATTN_W), f32)],
        compiler_params=_params("parallel"),
        name="dilated_attn_step",
    )(proj, *caches, P["head_mean"], P["attn_vec"])


def _merge_kernel(oa_ref, ob_ref, oc_ref, od_ref, mg_ref, x_ref, gate_ref, wb_ref, wo_ref, o_ref):
    merged = None
    start = 0
    for bi, ref in enumerate((oa_ref, ob_ref, oc_ref, od_ref)):
        width = ref.shape[-1]
        t = jnp.dot(ref[0].astype(bf16), wb_ref[0, start:start + width, :], preferred_element_type=f32)
        term = _sigmoid(mg_ref[0, :, bi * D_MODEL:(bi + 1) * D_MODEL].astype(f32)) * t
        merged = term if merged is None else merged + term
        start += width
    y = jnp.dot(merged.astype(bf16), wo_ref[0], preferred_element_type=f32)
    o_ref[0] = x_ref[0] + gate_ref[0] * y


def _merge(branches, mg3, x3, gate3, P, layer, tl):
    b, l, d = x3.shape
    per_row = gate3.shape[1] != 1
    rows = lambda w: pl.BlockSpec((1, tl, w), lambda i, j: (i, j, 0))
    gate_spec = pl.BlockSpec((1, tl if per_row else 1, d), (lambda i, j: (i, j, 0)) if per_row else (lambda i, j: (i, 0, 0)))
    return pl.pallas_call(
        _merge_kernel,
        grid=(b, l // tl),
        in_specs=[rows(o.shape[-1]) for o in branches] + [rows(N_BRANCH * d), rows(d), gate_spec,
                                                          _layer_spec(P["w_branch"].shape, layer, 2),
                                                          _layer_spec(P["w_out"].shape, layer, 2)],
        out_specs=rows(d),
        out_shape=jax.ShapeDtypeStruct((b, l, d), f32),
        compiler_params=_params("parallel", "parallel"),
        name="merge_out",
    )(*branches, mg3, x3, gate3, P["w_branch"], P["w_out"])


def _proj_windows():
    start, first = {}, 0
    for name, size in IN_SEGMENTS:
        start[name] = first
        first += size
    spans = {"ssm": ("ssm_z", "rwkv_z", 1408), "rwkv": ("rwkv_z", "mlstm_z", 2176),
             "mlstm": ("mlstm_z", "attn_z", 896), "attn": ("attn_z", "merge", 1280),
             "merge": ("merge", None, 1024)}
    out = {}
    for key, (lo, hi, tn) in spans.items():
        end = first if hi is None else start[hi]
        width = -(-(end - start[lo]) // LANES) * LANES
        assert start[lo] % 8 == 0 and width % tn == 0 and start[lo] + width <= first
        out[key] = (start[lo], width, tn)
    return out


_PROJ = _proj_windows()


def _layer(x, mod, st, sidx, caches, P, layer):
    b, l, d = x.shape
    prompt = caches is None
    shift, scale, gate = mod[:, :d], mod[:, d:2 * d], mod[:, 2 * d:]
    if prompt:
        x3, sc3, sh3, g3 = x, scale[:, None], shift[:, None], gate[:, None]
        tl, chunk, tl_merge, tm, proj_dtype = 512, 128, 512, 1024, bf16
        rwkv_rows, nb_ssd, nb_rwkv, nb_mlstm, nb_attn = 2 * RWKV_CHUNK, math.gcd(b, 4), math.gcd(b, 2), 1, 1
    else:
        rep = lambda t: jnp.repeat(t, l, axis=0)[None]
        x3, sc3, sh3, g3 = x.reshape(1, b * l, d), rep(scale), rep(shift), rep(gate)
        tl = min(256, b * l)
        chunk, tl_merge, tm, proj_dtype = l, tl, min(1024, b * l), f32
        rwkv_rows, nb_ssd, nb_rwkv, nb_mlstm, nb_attn = l, math.gcd(b, 8), math.gcd(b, 4), 1, math.gcd(b, 2)
    h2 = _norm(x3, P["norm_w"], layer, sc3, sh3, tl).reshape(b * l, d)
    proj = {k: _matmul(h2, P["w_in_t"], layer, col0, width, tm, tn, proj_dtype).reshape(b, l, -1)
            for k, (col0, width, tn) in _PROJ.items()}

    out_a, ssm_new, ssm_conv_new = _ssm(proj["ssm"], st["ssm_conv"], st["ssm"], sidx, P, layer, chunk, nb_ssd)
    out_b, rwkv_new, shift_new = _rwkv(proj["rwkv"], st["rwkv_shift"], st["rwkv"], sidx, P, layer,
                                       min(RWKV_CHUNK, l), rwkv_rows, nb_rwkv)
    out_c, c_new, n_new, m_new, mconv_new = _mlstm(proj["mlstm"], st["mlstm_conv"], st["mlstm_c"], st["mlstm_n"],
                                                   st["mlstm_m"], sidx, P, layer, chunk, nb_mlstm)
    if prompt:
        out_d, *kv = _attn_prompt(proj["attn"], P, layer)
        kv_new = [t.reshape(b, t.shape[1], 2, ATTN_HEADS, HEAD_DIM) for t in kv]
    else:
        out_d, kv = _attn_step(proj["attn"], caches, P, layer, nb_attn)
        kv_new = [kv[:, gi].reshape(b, l, 2, ATTN_HEADS, HEAD_DIM) for gi in range(len(ATTN_GROUPS))]

    branches = [o.reshape(x3.shape[0], x3.shape[1], -1) for o in (out_a, out_b, out_c, out_d)]
    mg3 = proj["merge"].reshape(x3.shape[0], x3.shape[1], -1)
    x_new = _merge(branches, mg3, x3, g3, P, layer, tl_merge).reshape(b, l, d)
    new_state = {"ssm": ssm_new, "ssm_conv": ssm_conv_new, "rwkv": rwkv_new, "rwkv_shift": shift_new[:, 0],
                 "mlstm_c": c_new, "mlstm_n": n_new, "mlstm_m": m_new, "mlstm_conv": mconv_new,
                 "kv_0": kv_new[0], "kv_1": kv_new[1], "kv_2": kv_new[2]}
    return x_new, new_state


_STATE_NAMES = ("ssm", "ssm_conv", "rwkv", "rwkv_shift", "mlstm_c", "mlstm_n", "mlstm_m", "mlstm_conv",
                "kv_0", "kv_1", "kv_2")


def _prepare_params(w):
    P = {"norm_w": w["norm_w"][:, None, :], "ada_w": w["ada_w"], "ada_b": w["ada_b"][:, None, :],
         "w_branch": w["w_branch"].astype(bf16), "w_out": w["w_out"].astype(bf16),
         "ssm_conv_w": w["ssm_conv_w"], "mlstm_conv_w": w["mlstm_conv_w"],
         "rwkv_w2": w["rwkv_w2"], "rwkv_a2": w["rwkv_a2"], "head_mean": _head_mean_matrix(),
         "ssm_vec": _pack_vecs(SSM_VEC, SSM_VEC_N, {"conv_b": w["ssm_conv_b"], "dt_bias": w["ssm_dt_bias"],
                                                    "a_log": w["ssm_a_log"], "d": w["ssm_d"],
                                                    "norm_w": w["ssm_norm_w"]}),
         "rwkv_vec": _pack_vecs(RWKV_VEC, RWKV_VEC_N, {"mu": w["rwkv_mu"], "w0": w["rwkv_w0"], "a0": w["rwkv_a0"],
                                                       "k_k": w["rwkv_k_k"], "k_a": w["rwkv_k_a"],
                                                       "r_k": w["rwkv_r_k"], "ln_w": w["rwkv_ln_w"],
                                                       "ln_b": w["rwkv_ln_b"]}),
         "mlstm_vec": _pack_vecs(MLSTM_VEC, MLSTM_VEC_N, {"conv_b": w["mlstm_conv_b"], "gate_b": w["mlstm_gate_b"],
                                                          "norm_w": w["mlstm_norm_w"]}),
         "attn_vec": _pack_vecs(ATTN_VEC, ATTN_VEC_N, {"q_norm": jnp.tile(w["attn_q_norm"], (1, ATTN_HEADS)),
                                                       "k_norm": jnp.tile(w["attn_k_norm"], (1, ATTN_HEADS))})}
    P["w_in_t"] = jnp.transpose(w["w_in"], (0, 2, 1))
    return P


def _pad_heads(m):
    return jnp.pad(m, [(0, 0)] * (m.ndim - 1) + [(0, LANES - N_HEADS)])[..., None, :]


def kernel(x_prompt, x_sample, c_prompt, c_sample, state_ssm, state_ssm_conv, state_rwkv, state_rwkv_shift, state_mlstm_c, state_mlstm_n, state_mlstm_m, state_mlstm_conv, cache_kv_w128, cache_kv_w512, cache_kv_w2048, norm_w, ada_w, ada_b, w_in, w_branch, w_out, ssm_conv_w, ssm_conv_b, ssm_dt_bias, ssm_a_log, ssm_d, ssm_norm_w, rwkv_mu, rwkv_w0, rwkv_w2, rwkv_a0, rwkv_a2, rwkv_k_k, rwkv_k_a, rwkv_r_k, rwkv_ln_w, rwkv_ln_b, mlstm_conv_w, mlstm_conv_b, mlstm_gate_b, mlstm_norm_w, attn_q_norm, attn_k_norm):
    P = _prepare_params(dict(
        norm_w=norm_w, ada_w=ada_w, ada_b=ada_b, w_in=w_in, w_branch=w_branch, w_out=w_out, ssm_conv_w=ssm_conv_w,
        ssm_conv_b=ssm_conv_b, ssm_dt_bias=ssm_dt_bias, ssm_a_log=ssm_a_log, ssm_d=ssm_d, ssm_norm_w=ssm_norm_w,
        rwkv_mu=rwkv_mu, rwkv_w0=rwkv_w0, rwkv_w2=rwkv_w2, rwkv_a0=rwkv_a0, rwkv_a2=rwkv_a2, rwkv_k_k=rwkv_k_k,
        rwkv_k_a=rwkv_k_a, rwkv_r_k=rwkv_r_k, rwkv_ln_w=rwkv_ln_w, rwkv_ln_b=rwkv_ln_b, mlstm_conv_w=mlstm_conv_w,
        mlstm_conv_b=mlstm_conv_b, mlstm_gate_b=mlstm_gate_b, mlstm_norm_w=mlstm_norm_w, attn_q_norm=attn_q_norm,
        attn_k_norm=attn_k_norm))
    bp = x_prompt.shape[0]
    fresh = {"ssm": jnp.zeros((1, bp, N_HEADS, HEAD_DIM, SSM_STATE), f32),
             "ssm_conv": jnp.zeros((1, bp, CONV_W - 1, SSM_CONV_CH), f32),
             "rwkv": jnp.zeros((1, bp, N_HEADS, HEAD_DIM, HEAD_DIM), f32),
             "rwkv_shift": jnp.zeros((1, bp, 1, RWKV_SHIFT_CH), f32),
             "mlstm_c": jnp.zeros((1, bp, N_HEADS, HEAD_DIM, HEAD_DIM), f32),
             "mlstm_n": jnp.zeros((1, bp, N_HEADS, HEAD_DIM), f32),
             "mlstm_m": jnp.zeros((1, bp, 1, LANES), f32),
             "mlstm_conv": jnp.zeros((1, bp, CONV_W - 1, 2 * WIDTH), f32)}
    carried = {"ssm": state_ssm, "ssm_conv": state_ssm_conv, "rwkv": state_rwkv,
               "rwkv_shift": state_rwkv_shift[:, :, None, :], "mlstm_c": state_mlstm_c, "mlstm_n": state_mlstm_n,
               "mlstm_m": _pad_heads(state_mlstm_m), "mlstm_conv": state_mlstm_conv}
    c_all = jnp.concatenate([c_prompt, c_sample], axis=0)
    caches_t = [jnp.transpose(c, (0, 1, 3, 4, 5, 2)) for c in (cache_kv_w128, cache_kv_w512, cache_kv_w2048)]
    y_prompt, y_sample = x_prompt, x_sample
    prompt_states, sample_states = [], []
    for layer in range(DEPTH):
        mod = _ada(c_all, P["ada_w"], P["ada_b"], layer)
        y_prompt, sp = _layer(y_prompt, mod[:bp], fresh, 0, None, P, layer)
        y_sample, ss = _layer(y_sample, mod[bp:], carried, layer, caches_t, P, layer)
        prompt_states.append(sp)
        sample_states.append(ss)
    stack = lambda states, name: jnp.stack([s[name] for s in states])
    return ((y_prompt, y_sample)
            + tuple(stack(prompt_states, n) for n in _STATE_NAMES)
            + tuple(stack(sample_states, n) for n in _STATE_NAMES))
```

```python
import functools
import math

import jax
import jax.numpy as jnp
from jax import lax
from jax.experimental import pallas as pl
from jax.experimental.pallas import tpu as pltpu

f32 = jnp.float32
bf16 = jnp.bfloat16
HI = lax.Precision.HIGHEST

D_MODEL = 1024
DEPTH = 2
HEAD_DIM = 64
NORM_EPS = 1e-6
CONV_W = 4
N_HEADS = 8
WIDTH = N_HEADS * HEAD_DIM
SSM_GROUPS = 2
SSM_STATE = 64
SSM_CONV_CH = WIDTH + 2 * SSM_GROUPS * SSM_STATE
RWKV_RANK = 64
RWKV_SHIFT_CH = 3 * WIDTH + 2 * RWKV_RANK
RWKV_LN_EPS = 64e-5
RWKV_CHUNK = 64
ATTN_GROUPS = ((128, 1), (512, 4), (2048, 16))
ATTN_HEADS = 4
ATTN_W = ATTN_HEADS * HEAD_DIM
ATTN_J = 128
MLSTM_HEAD_GROUP = 8
ATTN_UNROLL = 2
N_BRANCH = 4
LANES = 128
VMEM_LIMIT = 52 * 1024 * 1024

IN_SEGMENTS = (
    ("ssm_z", WIDTH), ("ssm_xbc", SSM_CONV_CH), ("ssm_dt", N_HEADS),
    ("rwkv_z", WIDTH), ("rwkv_shift", RWKV_SHIFT_CH),
    ("mlstm_z", WIDTH), ("mlstm_qk", 2 * WIDTH), ("mlstm_v", WIDTH),
    ("mlstm_o", WIDTH), ("mlstm_if", 2 * N_HEADS),
    ("attn_z", ATTN_W), ("attn_qkv", 9 * ATTN_W),
    ("merge", N_BRANCH * D_MODEL),
)

NN = (((1,), (0,)), ((), ()))
NT = (((1,), (1,)), ((), ()))
TN = (((0,), (0,)), ((), ()))


def _mm(a, b, dims, prec):
    if prec is None:
        a, b = a.astype(bf16), b.astype(bf16)
    return lax.dot_general(a, b, dims, preferred_element_type=f32, precision=prec)


def _dot(a, b, prec=None):
    return _mm(a, b, NN, prec)


def _dot_nt(a, b, prec=None):
    return _mm(a, b, NT, prec)


def _dot_tn(a, b, transpose_on_mxu=False):
    if not transpose_on_mxu:
        return _mm(a, b, TN, None)
    n = a.shape[1]
    eye = (_iota2((n, n), 0) == _iota2((n, n), 1)).astype(bf16)
    a_t = lax.dot_general(eye, a.astype(bf16), NT, preferred_element_type=f32).astype(bf16)
    return lax.dot_general(a_t, b.astype(bf16), NN, preferred_element_type=f32)


def _sigmoid(x):
    return 0.5 * jnp.tanh(0.5 * x) + 0.5


def _silu(x):
    return x * _sigmoid(x)


def _softplus(x):
    return jnp.maximum(x, 0.0) + jnp.log1p(jnp.exp(-jnp.abs(x)))


def _params(*sem):
    return pltpu.CompilerParams(dimension_semantics=sem, vmem_limit_bytes=VMEM_LIMIT)


def _iota2(shape, dim):
    return lax.broadcasted_iota(jnp.int32, shape, dim)


def _running_sum(x, c):
    tri = (_iota2((c, c), 0) >= _iota2((c, c), 1)).astype(f32)
    return _dot(tri, x, HI)


def _vec_layout(fields):
    out, off = {}, 0
    for name, size in fields:
        padded = -(-size // LANES) * LANES
        out[name] = (off, padded)
        off += padded
    return out, off


def _pack_vecs(layout, total, vecs):
    parts = []
    for name, (_, padded) in layout.items():
        v = vecs[name].reshape(DEPTH, -1)
        parts.append(jnp.pad(v, ((0, 0), (0, padded - v.shape[1]))))
    packed = jnp.concatenate(parts, axis=1)
    assert packed.shape[1] == total
    return packed[:, None, :]


def _vec_reader(ref, layout):
    return lambda name: ref[0, :, layout[name][0]:layout[name][0] + layout[name][1]]


SSM_VEC, SSM_VEC_N = _vec_layout((("conv_b", SSM_CONV_CH), ("dt_bias", N_HEADS), ("a_log", N_HEADS),
                                  ("d", N_HEADS), ("norm_w", WIDTH)))
RWKV_VEC, RWKV_VEC_N = _vec_layout((("mu", RWKV_SHIFT_CH), ("w0", WIDTH), ("a0", WIDTH), ("k_k", WIDTH),
                                    ("k_a", WIDTH), ("r_k", WIDTH), ("ln_w", WIDTH), ("ln_b", WIDTH)))
MLSTM_VEC, MLSTM_VEC_N = _vec_layout((("conv_b", 2 * WIDTH), ("gate_b", 2 * N_HEADS), ("norm_w", WIDTH)))
ATTN_VEC, ATTN_VEC_N = _vec_layout((("q_norm", ATTN_W), ("k_norm", ATTN_W)))


def _layer_spec(shape, layer, grid_rank):
    zeros = (0,) * (len(shape) - 1)
    if grid_rank == 1:
        return pl.BlockSpec((1,) + tuple(shape[1:]), lambda i: (layer,) + zeros)
    return pl.BlockSpec((1,) + tuple(shape[1:]), lambda i, j: (layer,) + zeros)


def _ada_kernel(c_ref, w_ref, b_ref, o_ref):
    o_ref[...] = _dot(_silu(c_ref[...]), w_ref[0], HI) + b_ref[0]


def _ada(c, w, b3, layer):
    n = c.shape[0]
    return pl.pallas_call(
        _ada_kernel,
        grid=(3,),
        in_specs=[pl.BlockSpec((n, D_MODEL), lambda j: (0, 0)),
                  pl.BlockSpec((1, D_MODEL, D_MODEL), lambda j: (layer, 0, j)),
                  pl.BlockSpec((1, 1, D_MODEL), lambda j: (layer, 0, j))],
        out_specs=pl.BlockSpec((n, D_MODEL), lambda j: (0, j)),
        out_shape=jax.ShapeDtypeStruct((n, 3 * D_MODEL), f32),
        compiler_params=_params("parallel"),
        name="ada_mod",
    )(c, w, b3)


def _norm_kernel(x_ref, nw_ref, sc_ref, sh_ref, o_ref):
    x = x_ref[0]
    r = x * lax.rsqrt(jnp.mean(x * x, axis=-1, keepdims=True) + NORM_EPS)
    o_ref[0] = (r * nw_ref[0] * (1.0 + sc_ref[0]) + sh_ref[0]).astype(o_ref.dtype)


def _norm(x3, nw3, layer, sc3, sh3, tl):
    b, l, d = x3.shape
    per_row = sc3.shape[1] != 1
    mod_spec = pl.BlockSpec((1, tl if per_row else 1, d), (lambda i, j: (i, j, 0)) if per_row else (lambda i, j: (i, 0, 0)))
    return pl.pallas_call(
        _norm_kernel,
        grid=(b, l // tl),
        in_specs=[pl.BlockSpec((1, tl, d), lambda i, j: (i, j, 0)),
                  _layer_spec(nw3.shape, layer, 2), mod_spec, mod_spec],
        out_specs=pl.BlockSpec((1, tl, d), lambda i, j: (i, j, 0)),
        out_shape=jax.ShapeDtypeStruct((b, l, d), bf16),
        compiler_params=_params("parallel", "parallel"),
        name="mod_rmsnorm",
    )(x3, nw3, sc3, sh3)


def _mm_t_kernel(x_ref, wt_ref, o_ref, w16_ref):
    @pl.when(pl.program_id(1) == 0)
    def _():
        w16_ref[...] = wt_ref[0].T.astype(bf16)

    o_ref[...] = jnp.dot(x_ref[...], w16_ref[...], preferred_element_type=f32).astype(o_ref.dtype)


def _matmul_t(x, w_t, layer, col0, width, tm, tn, out_dtype):
    n, k = x.shape
    w_spec = pl.BlockSpec((pl.Element(1), pl.Element(tn), pl.Element(k)),
                          lambda j, i: (layer, pl.multiple_of(col0 + j * tn, 8), 0))
    return pl.pallas_call(
        _mm_t_kernel,
        grid=(width // tn, n // tm),
        in_specs=[pl.BlockSpec((tm, k), lambda j, i: (i, 0)), w_spec],
        out_specs=[pl.BlockSpec((tm, tn), lambda j, i: (i, j)), pl.BlockSpec((k, tn), lambda j, i: (0, j))],
        out_shape=[jax.ShapeDtypeStruct((n, width), out_dtype), jax.ShapeDtypeStruct((k, width), bf16)],
        compiler_params=_params("parallel", "arbitrary"),
        name="in_proj",
    )(x, w_t)


def _mm_kernel(x_ref, w_ref, o_ref):
    o_ref[...] = jnp.dot(x_ref[...], w_ref[...], preferred_element_type=f32).astype(o_ref.dtype)


def _matmul(x, w16, tm, tn, out_dtype):
    n, k = x.shape
    width = w16.shape[1]
    return pl.pallas_call(
        _mm_kernel,
        grid=(width // tn, n // tm),
        in_specs=[pl.BlockSpec((tm, k), lambda j, i: (i, 0)), pl.BlockSpec((k, tn), lambda j, i: (0, j))],
        out_specs=pl.BlockSpec((tm, tn), lambda j, i: (i, j)),
        out_shape=jax.ShapeDtypeStruct((n, width), out_dtype),
        compiler_params=_params("parallel", "parallel"),
        name="in_proj",
    )(x, w16)


def _causal_conv(ext_scr, u, c, cw_ref, bias):
    ext_scr[8:8 + c, :] = u
    acc = bias
    for i in range(CONV_W):
        acc = acc + ext_scr[5 + i:5 + i + c, :] * cw_ref[0, i:i + 1, :]
    tail = ext_scr[c + 5:c + 8, :]
    ext_scr[5:8, :] = tail
    return acc, tail


def _interleave(gens):
    gens = list(gens)
    while gens:
        alive = []
        for g in gens:
            try:
                next(g)
                alive.append(g)
            except StopIteration:
                pass
        gens = alive


def _ssm_kernel(proj_ref, conv0_ref, h0_ref, cw_ref, vec_ref, out_ref, hn_ref, convn_ref, ext_scr, y_scr, *, c, nb):
    vec = _vec_reader(vec_ref, SSM_VEC)

    @pl.when(pl.program_id(1) == 0)
    def _():
        for bi in range(nb):
            ext_scr[bi, 5:8, :] = conv0_ref[0, bi]
            hn_ref[bi] = h0_ref[0, bi]

    causal = _iota2((c, c), 0) >= _iota2((c, c), 1)
    heads = range(N_HEADS)
    per_group = N_HEADS // SSM_GROUPS
    sls = [slice(h * HEAD_DIM, (h + 1) * HEAD_DIM) for h in heads]

    def elem(bi):
        proj, out, hn, y_acc = proj_ref.at[bi], out_ref.at[bi], hn_ref.at[bi], y_scr.at[bi]
        u = proj[:, WIDTH:WIDTH + SSM_CONV_CH].astype(f32)
        acc, tail = _causal_conv(ext_scr.at[bi], u, c, cw_ref, vec("conv_b"))
        convn_ref[bi] = tail
        xbc = _silu(acc)
        xs = xbc[:, :WIDTH]
        bm = xbc[:, WIDTH:WIDTH + LANES]
        cm = xbc[:, WIDTH + LANES:]
        dt = _softplus(proj[:, WIDTH + SSM_CONV_CH:].astype(f32) + vec("dt_bias"))
        da = dt * (-jnp.exp(vec("a_log")))
        yield
        cs = _running_sum(da, c)
        cs_t = cs.T
        dt_t = dt.T
        cs_end = cs[c - 1:c, :]
        dsk = vec("d")
        bgs = [bm[:, g * SSM_STATE:(g + 1) * SSM_STATE] for g in range(SSM_GROUPS)]
        cgs = [cm[:, g * SSM_STATE:(g + 1) * SSM_STATE] for g in range(SSM_GROUPS)]
        yield
        gmats = [_dot_nt(cgs[g], bgs[g]) for g in range(SSM_GROUPS)]
        cs_cols = [cs[:, h:h + 1] for h in heads]
        yield
        lmats = [gmats[h // per_group] * jnp.exp(jnp.where(causal, cs_cols[h] - cs_t[h:h + 1, :], -jnp.inf))
                 * dt_t[h:h + 1, :] for h in heads]
        xhs = [xs[:, sl] for sl in sls]
        hprevs = [hn[h] for h in heads]
        yield
        y_in = [_dot(lmats[h], xhs[h]) for h in heads]
        y_st = [_dot_nt(cgs[h // per_group], hprevs[h]) for h in heads]
        yield
        for h in heads:
            y_acc[:, sls[h]] = y_in[h] + y_st[h] * jnp.exp(cs_cols[h]) + xhs[h] * dsk[:, h:h + 1]
        ces = [cs_end[:, h:h + 1] for h in heads]
        upd = [_dot_tn(xhs[h] * (jnp.exp(ces[h] - cs_cols[h]) * dt[:, h:h + 1]), bgs[h // per_group],
                       transpose_on_mxu=True) for h in heads]
        yield
        for h in heads:
            hn[h] = hprevs[h] * jnp.exp(ces[h]) + upd[h]
        gy = y_acc[...] * _silu(proj[:, :WIDTH].astype(f32))
        nw = vec("norm_w")
        gw = WIDTH // SSM_GROUPS
        for g in range(SSM_GROUPS):
            part = gy[:, g * gw:(g + 1) * gw]
            ms = jnp.mean(part * part, axis=-1, keepdims=True)
            out[:, g * gw:(g + 1) * gw] = part * lax.rsqrt(ms + NORM_EPS) * nw[:, g * gw:(g + 1) * gw]

    _interleave(elem(bi) for bi in range(nb))


def _state_spec(shape, sidx, nb):
    zeros = (0,) * (len(shape) - 2)
    return pl.BlockSpec((1, nb) + tuple(shape[2:]), lambda i, j: (sidx, i) + zeros)


def _batch_spec(nb, *dims):
    zeros = (0,) * len(dims)
    return pl.BlockSpec((nb,) + dims, lambda i, j: (i,) + zeros)


def _ssm(proj, conv0, h0, sidx, P, layer, c, nb):
    b, l, wp = proj.shape
    return pl.pallas_call(
        functools.partial(_ssm_kernel, c=c, nb=nb),
        grid=(b // nb, l // c),
        in_specs=[pl.BlockSpec((nb, c, wp), lambda i, j: (i, j, 0)),
                  _state_spec(conv0.shape, sidx, nb), _state_spec(h0.shape, sidx, nb),
                  _layer_spec(P["ssm_conv_w"].shape, layer, 2), _layer_spec(P["ssm_vec"].shape, layer, 2)],
        out_specs=[pl.BlockSpec((nb, c, WIDTH), lambda i, j: (i, j, 0)),
                   _batch_spec(nb, N_HEADS, HEAD_DIM, SSM_STATE), _batch_spec(nb, CONV_W - 1, SSM_CONV_CH)],
        out_shape=[jax.ShapeDtypeStruct((b, l, WIDTH), f32),
                   jax.ShapeDtypeStruct((b, N_HEADS, HEAD_DIM, SSM_STATE), f32),
                   jax.ShapeDtypeStruct((b, CONV_W - 1, SSM_CONV_CH), f32)],
        scratch_shapes=[pltpu.VMEM((nb, c + 8, SSM_CONV_CH), f32), pltpu.VMEM((nb, c, WIDTH), f32)],
        compiler_params=_params("parallel", "arbitrary"),
        name="ssd_branch",
    )(proj, conv0, h0, P["ssm_conv_w"], P["ssm_vec"])


def _rwkv_chunk_kernel(proj_ref, shift0_ref, s0_ref, vec_ref, w2_ref, a2_ref,
                       out_ref, sn_ref, shiftn_ref, ext_scr, *, c, rows, nb):
    vec = _vec_reader(vec_ref, RWKV_VEC)

    @pl.when(pl.program_id(1) == 0)
    def _():
        for bi in range(nb):
            ext_scr[bi, 7:8, :] = shift0_ref[0, bi]
            sn_ref[bi] = s0_ref[0, bi]

    row2 = _iota2((c, 2 * c), 0)
    col2 = _iota2((c, 2 * c), 1)
    col2 = jnp.where(col2 >= c, col2 - c, col2)
    strict2 = row2 > col2
    incl2 = row2 >= col2
    n_double = (c - 1).bit_length()
    heads = range(N_HEADS)
    sls = [slice(h * HEAD_DIM, (h + 1) * HEAD_DIM) for h in heads]

    def elem(bi):
        proj, out, sn, ext = proj_ref.at[bi], out_ref.at[bi], sn_ref.at[bi], ext_scr.at[bi]
        u = proj[:, WIDTH:].astype(f32)
        ext[8:8 + rows, :] = u
        sh = u + (ext[7:7 + rows, :] - u) * vec("mu")
        last = ext[rows + 7:rows + 8, :]
        ext[7:8, :] = last
        shiftn_ref[bi] = last
        r_all = sh[:, :WIDTH]
        k = sh[:, WIDTH:2 * WIDTH]
        v_all = sh[:, 2 * WIDTH:3 * WIDTH]
        w_lo = sh[:, 3 * WIDTH:3 * WIDTH + RWKV_RANK]
        a_lo = sh[:, 3 * WIDTH + RWKV_RANK:]
        yield
        w_log = -_softplus(-(vec("w0") + _dot(jnp.tanh(w_lo), w2_ref[0]))) - 0.5
        lw_all = -jnp.exp(w_log)
        a_in_all = _sigmoid(vec("a0") + _dot(a_lo, a2_ref[0]))
        kk_all = k * vec("k_k")
        k2_all = k * (1.0 + (a_in_all - 1.0) * vec("k_a"))
        rk, lnw, lnb = vec("r_k"), vec("ln_w"), vec("ln_b")
        yield
        for c0 in range(0, rows, c):
            cr = slice(c0, c0 + c)
            r, v, lw, a_in, kk, k2 = r_all[cr], v_all[cr], lw_all[cr], a_in_all[cr], kk_all[cr], k2_all[cr]
            cl = _running_sum(lw, c)
            cl_end = cl[c - 1:c, :]
            g_fwd = jnp.exp(cl)
            g_inv = jnp.exp(-cl)
            g_rem = jnp.exp(cl_end - cl)
            g_prev = jnp.exp(cl - lw)
            g_end = jnp.exp(cl_end)
            yield
            s0s = [sn[h] for h in heads]
            bhs, ars, bks = [], [], []
            for sl in sls:
                kkh = kk[:, sl]
                kkn = kkh * (1.0 / jnp.maximum(jnp.sqrt(jnp.sum(kkh * kkh, axis=-1, keepdims=True)), 1e-12))
                bhs.append(kkn * a_in[:, sl])
                ars.append(jnp.concatenate([-kkn * g_prev[:, sl], r[:, sl] * g_fwd[:, sl]], axis=0))
                bks.append(jnp.concatenate([bhs[-1] * g_inv[:, sl], k2[:, sl] * g_inv[:, sl]], axis=0))
            yield
            m4s = [_dot_nt(ars[h], bks[h]) for h in heads]
            ahs = [_dot_nt(ars[h], s0s[h]) for h in heads]
            yield
            tops = [jnp.where(strict2, m4[:c, :], 0.0) for m4 in m4s]
            bots = [jnp.where(incl2, m4[c:, :], 0.0) for m4 in m4s]
            amats = [top[:, :c].astype(bf16) for top in tops]
            xs = [ahs[h][:c] + _dot(tops[h][:, c:], v[:, sls[h]]) for h in heads]
            yield
            for i in range(n_double):
                xs = [xs[h] + _dot(amats[h], xs[h]) for h in heads]
                if i + 1 < n_double:
                    amats = [_dot(amats[h], amats[h]).astype(bf16) for h in heads]
                yield
            pvs = [jnp.concatenate([xs[h], v[:, sls[h]]], axis=0) for h in heads]
            ys = [ahs[h][c:] + _dot(bots[h], pvs[h]) for h in heads]
            upd = [_dot_tn(pvs[h], jnp.concatenate([bhs[h] * g_rem[:, sls[h]], k2[:, sls[h]] * g_rem[:, sls[h]]],
                                                   axis=0)) for h in heads]
            yield
            for h in heads:
                sl = sls[h]
                sn[h] = s0s[h] * g_end[:, sl] + upd[h]
                y = ys[h]
                mu = jnp.mean(y, axis=-1, keepdims=True)
                var = jnp.mean(jnp.square(y - mu), axis=-1, keepdims=True)
                ln = (y - mu) * lax.rsqrt(var + RWKV_LN_EPS) * lnw[:, sl] + lnb[:, sl]
                bonus = jnp.sum(r[:, sl] * k2[:, sl] * rk[:, sl], axis=-1, keepdims=True) * v[:, sl]
                out[cr, sl] = (ln + bonus) * _silu(proj[cr, sl].astype(f32))
            yield

    _interleave(elem(bi) for bi in range(nb))


def _rwkv(proj, shift0, s0, sidx, P, layer, c, rows, nb):
    b, l, wp = proj.shape
    return pl.pallas_call(
        functools.partial(_rwkv_chunk_kernel, c=c, rows=rows, nb=nb),
        grid=(b // nb, l // rows),
        in_specs=[pl.BlockSpec((nb, rows, wp), lambda i, j: (i, j, 0)),
                  _state_spec(shift0.shape, sidx, nb), _state_spec(s0.shape, sidx, nb),
                  _layer_spec(P["rwkv_vec"].shape, layer, 2),
                  _layer_spec(P["rwkv_w2"].shape, layer, 2), _layer_spec(P["rwkv_a2"].shape, layer, 2)],
        out_specs=[pl.BlockSpec((nb, rows, WIDTH), lambda i, j: (i, j, 0)),
                   _batch_spec(nb, N_HEADS, HEAD_DIM, HEAD_DIM), _batch_spec(nb, 1, RWKV_SHIFT_CH)],
        out_shape=[jax.ShapeDtypeStruct((b, l, WIDTH), f32),
                   jax.ShapeDtypeStruct((b, N_HEADS, HEAD_DIM, HEAD_DIM), f32),
                   jax.ShapeDtypeStruct((b, 1, RWKV_SHIFT_CH), f32)],
        scratch_shapes=[pltpu.VMEM((nb, rows + 8, RWKV_SHIFT_CH), f32)],
        compiler_params=_params("parallel", "arbitrary"),
        name="rwkv7_chunked",
    )(proj, shift0, s0, P["rwkv_vec"], P["rwkv_w2"], P["rwkv_a2"])


def _mlstm_kernel(proj_ref, conv0_ref, c0_ref, n0_ref, m0_ref, cw_ref, vec_ref,
                  out_ref, cn_ref, nn_ref, mn_ref, convn_ref, ext_scr, h_scr, *, c, nb):
    vec = _vec_reader(vec_ref, MLSTM_VEC)

    @pl.when(pl.program_id(1) == 0)
    def _():
        for bi in range(nb):
            ext_scr[bi, 5:8, :] = conv0_ref[0, bi]
            cn_ref[bi] = c0_ref[0, bi]
            nn_ref[bi] = n0_ref[0, bi]
            mn_ref[bi] = m0_ref[0, bi]

    causal = _iota2((c, c), 0) >= _iota2((c, c), 1)
    lane = _iota2((1, LANES), 1)
    heads = range(N_HEADS)
    sls = [slice(h * HEAD_DIM, (h + 1) * HEAD_DIM) for h in heads]

    def elem(bi):
        proj, out, cn, nn, h_acc = proj_ref.at[bi], out_ref.at[bi], cn_ref.at[bi], nn_ref.at[bi], h_scr.at[bi]
        u = proj[:, WIDTH:3 * WIDTH].astype(f32)
        acc, tail = _causal_conv(ext_scr.at[bi], u, c, cw_ref, vec("conv_b"))
        convn_ref[bi] = tail
        qk = _silu(acc)
        gates = proj[:, 5 * WIDTH:].astype(f32) + vec("gate_b")
        lf = -_softplus(-gates)
        yield
        bc = _running_sum(lf, c)
        bc_t = bc.T
        ig_t = gates.T
        m_prev = mn_ref[bi]
        yield
        m_out = jnp.zeros((1, LANES), f32)
        for grp in range(0, N_HEADS, MLSTM_HEAD_GROUP):
            heads = range(grp, grp + MLSTM_HEAD_GROUP)
            b_cols = {h: bc[:, N_HEADS + h:N_HEADS + h + 1] for h in heads}
            fold = c <= 8
            emats = {h: jnp.where(causal, (b_cols[h] if fold else 0.0) + ig_t[h:h + 1, :]
                                  - bc_t[N_HEADS + h:N_HEADS + h + 1, :], -jnp.inf) for h in heads}
            ecols = {h: 0.0 if fold else b_cols[h] for h in heads}
            m0s = {h: m_prev[:, h:h + 1] for h in heads}
            gcols = {h: b_cols[h] + m0s[h] for h in heads}
            mts = {h: jnp.maximum(gcols[h], ecols[h] + jnp.max(emats[h], axis=-1, keepdims=True)) for h in heads}
            yield
            qhs = {h: qk[:, sls[h]] for h in heads}
            khs = {h: qk[:, WIDTH + h * HEAD_DIM:WIDTH + (h + 1) * HEAD_DIM] * HEAD_DIM ** -0.5 for h in heads}
            vhs = {h: proj[:, 3 * WIDTH + h * HEAD_DIM:3 * WIDTH + (h + 1) * HEAD_DIM].astype(f32) for h in heads}
            cprevs = {h: cn[h] for h in heads}
            nprevs = {h: nn[h:h + 1, :] for h in heads}
            yield
            sws = {h: jnp.exp(emats[h] + (ecols[h] - mts[h])) * _dot_nt(qhs[h], khs[h]) for h in heads}
            wis = {h: jnp.exp(gcols[h] - mts[h]) for h in heads}
            qcs = {h: _dot(qhs[h], cprevs[h]) for h in heads}
            yield
            nums = {h: _dot(sws[h], vhs[h]) + wis[h] * qcs[h] for h in heads}
            yield
            for h in heads:
                den = (jnp.sum(sws[h], axis=-1, keepdims=True)
                       + wis[h] * jnp.sum(qhs[h] * nprevs[h], axis=-1, keepdims=True))
                h_acc[:, sls[h]] = nums[h] * (1.0 / jnp.maximum(jnp.abs(den), jnp.exp(-mts[h])))
            m_news = {h: mts[h][c - 1:c, :] for h in heads}
            b_ends = {h: b_cols[h][c - 1:c, :] for h in heads}
            kws = {h: khs[h] * jnp.exp(b_ends[h] - b_cols[h] + gates[:, h:h + 1] - m_news[h]) for h in heads}
            yield
            kvs = {h: _dot_tn(kws[h], vhs[h]) for h in heads}
            yield
            for h in heads:
                dec = jnp.exp(b_ends[h] + m0s[h] - m_news[h])
                cn[h] = dec * cprevs[h] + kvs[h]
                nn[h:h + 1, :] = dec * nprevs[h] + jnp.sum(kws[h], axis=0, keepdims=True)
                m_out = jnp.where(lane == h, m_news[h], m_out)
        mn_ref[bi] = m_out
        nw = vec("norm_w")
        for h in range(N_HEADS):
            sl = sls[h]
            o_gate = proj[:, 4 * WIDTH + h * HEAD_DIM:4 * WIDTH + (h + 1) * HEAD_DIM].astype(f32)
            hm = _sigmoid(o_gate) * h_acc[:, sl]
            ms = jnp.mean(hm * hm, axis=-1, keepdims=True)
            out[:, sl] = hm * lax.rsqrt(ms + NORM_EPS) * nw[:, sl] * _silu(proj[:, sl].astype(f32))

    _interleave(elem(bi) for bi in range(nb))


def _mlstm(proj, conv0, c0, n0, m0, sidx, P, layer, c, nb):
    b, l, wp = proj.shape
    outs = pl.pallas_call(
        functools.partial(_mlstm_kernel, c=c, nb=nb),
        grid=(b // nb, l // c),
        in_specs=[pl.BlockSpec((nb, c, wp), lambda i, j: (i, j, 0)),
                  _state_spec(conv0.shape, sidx, nb), _state_spec(c0.shape, sidx, nb),
                  _state_spec(n0.shape, sidx, nb), _state_spec(m0.shape, sidx, nb),
                  _layer_spec(P["mlstm_conv_w"].shape, layer, 2), _layer_spec(P["mlstm_vec"].shape, layer, 2)],
        out_specs=[pl.BlockSpec((nb, c, WIDTH), lambda i, j: (i, j, 0)),
                   _batch_spec(nb, N_HEADS, HEAD_DIM, HEAD_DIM), _batch_spec(nb, N_HEADS, HEAD_DIM),
                   _batch_spec(nb, 1, LANES), _batch_spec(nb, CONV_W - 1, 2 * WIDTH)],
        out_shape=[jax.ShapeDtypeStruct((b, l, WIDTH), f32),
                   jax.ShapeDtypeStruct((b, N_HEADS, HEAD_DIM, HEAD_DIM), f32),
                   jax.ShapeDtypeStruct((b, N_HEADS, HEAD_DIM), f32),
                   jax.ShapeDtypeStruct((b, 1, LANES), f32),
                   jax.ShapeDtypeStruct((b, CONV_W - 1, 2 * WIDTH), f32)],
        scratch_shapes=[pltpu.VMEM((nb, c + 8, 2 * WIDTH), f32), pltpu.VMEM((nb, c, WIDTH), f32)],
        compiler_params=_params("parallel", "arbitrary"),
        name="mlstm_branch",
    )(proj, conv0, c0, n0, m0, P["mlstm_conv_w"], P["mlstm_vec"])
    out, cn, nn, mn, convn = outs
    return out, cn, nn, mn[:, 0, :N_HEADS], convn


def _head_mean_matrix():
    hid = jnp.arange(ATTN_W) // HEAD_DIM
    return ((hid[:, None] == hid[None, :]).astype(f32) / HEAD_DIM).astype(bf16)


def _qk_norm(x, hm_ref, w):
    sq = x * x
    hi = sq.astype(bf16)
    lo = (sq - hi.astype(f32)).astype(bf16)
    ms = _dot(hi, hm_ref[...]) + _dot(lo, hm_ref[...])
    return x * lax.rsqrt(ms + NORM_EPS) * w


def _attn_prompt_kernel(q_ref, k_ref, v_ref, z_ref, hm_ref, vec_ref, out_ref, kv0_ref, kv1_ref, kv2_ref,
                        qn_scr, kn_scr, vv_scr, acc_scr, m_scr, d_scr, to_scr, tl_scr, *, l):
    vec = _vec_reader(vec_ref, ATTN_VEC)
    g = pl.program_id(1)
    rb = 256
    n_pairs = ATTN_W // LANES

    def norm_body(i, carry):
        rows = pl.ds(pl.multiple_of(i * rb, rb), rb)
        qn = _qk_norm(q_ref[0, rows, :].astype(f32), hm_ref, vec("q_norm"))
        kn = _qk_norm(k_ref[0, rows, :].astype(f32), hm_ref, vec("k_norm"))
        v = v_ref[0, rows, :].astype(f32)
        for p in range(n_pairs):
            qn_scr[p, rows, :] = qn[:, p * LANES:(p + 1) * LANES]
            kn_scr[p, rows, :] = kn[:, p * LANES:(p + 1) * LANES]
            vv_scr[p, rows, :] = v[:, p * LANES:(p + 1) * LANES]
        return carry

    lax.fori_loop(0, l // rb, norm_body, 0)

    qb = ATTN_J
    qi = _iota2((qb, qb), 0)
    ki = _iota2((qb, qb), 1)
    scale = HEAD_DIM ** -0.5

    def run_group(dil, first):
        n_iter = l // qb

        def block(i, slot):
            r = i % dil
            blk = i // dil
            u0 = blk * qb
            rows_c = pl.ds(u0 * dil + r, qb, stride=dil)
            rows_p = pl.ds(jnp.maximum(u0 - qb, 0) * dil + r, qb, stride=dil)
            mask_c = ki <= qi
            mask_p = (ki >= qi) & (blk > 0)
            hsl = [slice(hh * HEAD_DIM, (hh + 1) * HEAD_DIM) for hh in range(LANES // HEAD_DIM)]
            heads = [(p, sl) for p in range(n_pairs) for sl in hsl]
            qv = [qn_scr[p, rows_c, :] for p in range(n_pairs)]
            kc = [kn_scr[p, rows_c, :] for p in range(n_pairs)]
            vc = [vv_scr[p, rows_c, :] for p in range(n_pairs)]
            kp = [kn_scr[p, rows_p, :] for p in range(n_pairs)]
            vp = [vv_scr[p, rows_p, :] for p in range(n_pairs)]
            yield
            scs = [jnp.where(mask_c, _dot_nt(qv[p][:, sl], kc[p][:, sl]) * scale, -jnp.inf) for p, sl in heads]
            sps = [jnp.where(mask_p, _dot_nt(qv[p][:, sl], kp[p][:, sl]) * scale, -jnp.inf) for p, sl in heads]
            yield
            ms = [jnp.max(jnp.maximum(sc, sp), axis=-1, keepdims=True) for sc, sp in zip(scs, sps)]
            pcs = [jnp.exp(sc - m) for sc, m in zip(scs, ms)]
            pps = [jnp.exp(sp - m) for sp, m in zip(sps, ms)]
            dens = [jnp.sum(pc + pp, axis=-1, keepdims=True) for pc, pp in zip(pcs, pps)]
            yield
            for (p, sl), pc, pp, m, den in zip(heads, pcs, pps, ms, dens):
                to_scr[slot, p, :, sl] = (_dot(pc, vc[p][:, sl]) + _dot(pp, vp[p][:, sl])) * (1.0 / den)
                tl_scr[slot, p, :, sl] = jnp.broadcast_to(m + jnp.log(den), (qb, HEAD_DIM))
            yield
            for p in range(n_pairs):
                o = to_scr[slot, p]
                lse = tl_scr[slot, p]
                if first:
                    acc_scr[p, rows_c, :] = o
                    m_scr[p, rows_c, :] = lse
                    d_scr[p, rows_c, :] = jnp.ones_like(lse)
                else:
                    m_old = m_scr[p, rows_c, :]
                    m_new = jnp.maximum(m_old, lse)
                    a_old = jnp.exp(m_old - m_new)
                    a_new = jnp.exp(lse - m_new)
                    acc_scr[p, rows_c, :] = acc_scr[p, rows_c, :] * a_old + o * a_new
                    d_scr[p, rows_c, :] = d_scr[p, rows_c, :] * a_old + a_new
                    m_scr[p, rows_c, :] = m_new

        def body(i, carry):
            _interleave(block(i * ATTN_UNROLL + s, s) for s in range(ATTN_UNROLL))
            return carry

        lax.fori_loop(0, n_iter // ATTN_UNROLL, body, 0)

    def save_window(kv_ref):
        w = kv_ref.shape[1]
        wb = min(w, rb)
        for r0 in range(0, w, wb):
            src = slice(l - w + r0, l - w + r0 + wb)
            for p in range(n_pairs):
                kv_ref[0, r0:r0 + wb, p * LANES:(p + 1) * LANES] = kn_scr[p, src, :]
                kv_ref[0, r0:r0 + wb, ATTN_W + p * LANES:ATTN_W + (p + 1) * LANES] = vv_scr[p, src, :]

    def group_step(gi, dil):
        save_window((kv0_ref, kv1_ref, kv2_ref)[gi])
        run_group(dil, gi == 0)

    for gi, (_, dil) in enumerate(ATTN_GROUPS):
        pl.when(g == gi)(functools.partial(group_step, gi, dil))

    @pl.when(g == len(ATTN_GROUPS) - 1)
    def _():
        def fin_body(i, carry):
            rows = pl.ds(pl.multiple_of(i * rb, rb), rb)
            for p in range(n_pairs):
                lanes = slice(p * LANES, (p + 1) * LANES)
                z = z_ref[0, rows, lanes].astype(f32)
                out_ref[0, rows, lanes] = acc_scr[p, rows, :] / d_scr[p, rows, :] * _silu(z)
            return carry

        lax.fori_loop(0, l // rb, fin_body, 0)


def _attn_prompt(proj, P, layer):
    b, l, _ = proj.shape
    ng = len(ATTN_GROUPS)
    col = lambda base: pl.BlockSpec((1, l, ATTN_W), lambda i, g: (i, 0, base + g))
    pair_scr = pltpu.VMEM((ATTN_W // LANES, l, LANES), f32)
    windows = [min(w, l) for w, _ in ATTN_GROUPS]
    return pl.pallas_call(
        functools.partial(_attn_prompt_kernel, l=l),
        grid=(b, ng),
        in_specs=[col(1), col(1 + ng), col(1 + 2 * ng),
                  pl.BlockSpec((1, l, ATTN_W), lambda i, g: (i, 0, 0)),
                  pl.BlockSpec((ATTN_W, ATTN_W), lambda i, g: (0, 0)),
                  _layer_spec(P["attn_vec"].shape, layer, 2)],
        out_specs=[pl.BlockSpec((1, l, ATTN_W), lambda i, g: (i, 0, 0))]
                  + [pl.BlockSpec((1, w, 2 * ATTN_W), lambda i, g: (i, 0, 0)) for w in windows],
        out_shape=[jax.ShapeDtypeStruct((b, l, ATTN_W), f32)]
                  + [jax.ShapeDtypeStruct((b, w, 2 * ATTN_W), f32) for w in windows],
        scratch_shapes=[pair_scr] * 6 + [pltpu.VMEM((ATTN_UNROLL, ATTN_W // LANES, ATTN_J, LANES), f32)] * 2,
        compiler_params=_params("parallel", "arbitrary"),
        name="dilated_attn_prompt",
    )(proj, proj, proj, proj, P["head_mean"], P["attn_vec"])


def _attn_step_kernel(x_ref, c0_ref, c1_ref, c2_ref, hm_ref, vec_ref, out_ref, kv_ref, *, l, nb):
    vec = _vec_reader(vec_ref, ATTN_VEC)
    ng = len(ATTN_GROUPS)
    scale = HEAD_DIM ** -0.5
    c_refs = (c0_ref, c1_ref, c2_ref)
    hsl = [slice(h * HEAD_DIM, (h + 1) * HEAD_DIM) for h in range(ATTN_HEADS)]
    masks_p, masks_n = [], []
    for gi, (window, dil) in enumerate(ATTN_GROUPS):
        w = c_refs[gi].shape[-1]
        jmax = window // dil
        shift = dil.bit_length() - 1
        dist_p = w + _iota2((l, w), 0) - _iota2((l, w), 1)
        masks_p.append(((dist_p & (dil - 1)) == 0) & ((dist_p >> shift) <= jmax))
        dist_n = _iota2((l, l), 0) - _iota2((l, l), 1)
        masks_n.append((dist_n >= 0) & ((dist_n & (dil - 1)) == 0) & ((dist_n >> shift) <= jmax))
    gh = [(gi, h) for gi in range(ng) for h in range(ATTN_HEADS)]

    def elem(bi):
        x = x_ref[bi]
        qns, kns, vs = [], [], []
        for gi in range(ng):
            qns.append(_qk_norm(x[:, (1 + gi) * ATTN_W:(2 + gi) * ATTN_W], hm_ref, vec("q_norm")))
            kns.append(_qk_norm(x[:, (1 + ng + gi) * ATTN_W:(2 + ng + gi) * ATTN_W], hm_ref, vec("k_norm")))
            vs.append(x[:, (1 + 2 * ng + gi) * ATTN_W:(2 + 2 * ng + gi) * ATTN_W])
            kv_ref[bi, gi, :, 0:ATTN_W] = kns[gi]
            kv_ref[bi, gi, :, ATTN_W:2 * ATTN_W] = vs[gi]
        yield
        sps = [jnp.where(masks_p[gi], _dot(qns[gi][:, hsl[h]], c_refs[gi][0, bi, 0, h]) * scale, -jnp.inf)
               for gi, h in gh]
        sns = [jnp.where(masks_n[gi], _dot_nt(qns[gi][:, hsl[h]], kns[gi][:, hsl[h]]) * scale, -jnp.inf)
               for gi, h in gh]
        yield
        ms = [jnp.maximum(jnp.max(sp, axis=-1, keepdims=True), jnp.max(sn, axis=-1, keepdims=True))
              for sp, sn in zip(sps, sns)]
        pps = [jnp.exp(sp - m) for sp, m in zip(sps, ms)]
        pns = [jnp.exp(sn - m) for sn, m in zip(sns, ms)]
        dens = [jnp.sum(pp, axis=-1, keepdims=True) + jnp.sum(pn, axis=-1, keepdims=True)
                for pp, pn in zip(pps, pns)]
        yield
        og = [(_dot_nt(pp, c_refs[gi][0, bi, 1, h]) + _dot(pn, vs[gi][:, hsl[h]])) * (1.0 / den)
              for (gi, h), pp, pn, den in zip(gh, pps, pns, dens)]
        lg = [m + jnp.log(den) for m, den in zip(ms, dens)]
        yield
        outs = [og[gi * ATTN_HEADS:(gi + 1) * ATTN_HEADS] for gi in range(ng)]
        lses = [lg[gi * ATTN_HEADS:(gi + 1) * ATTN_HEADS] for gi in range(ng)]
        for h in range(ATTN_HEADS):
            sl = hsl[h]
            mx = functools.reduce(jnp.maximum, [lses[gi][h] for gi in range(ng)])
            ws = [jnp.exp(lses[gi][h] - mx) for gi in range(ng)]
            tot = functools.reduce(lambda a, b: a + b, ws)
            o = functools.reduce(lambda a, b: a + b, [outs[gi][h] * (ws[gi] / tot) for gi in range(ng)])
            out_ref[bi, :, sl] = o * _silu(x[:, sl])

    _interleave(elem(bi) for bi in range(nb))


def _attn_step(proj, caches, P, layer, nb):
    b, l, wp = proj.shape
    ng = len(ATTN_GROUPS)
    cache_specs = [pl.BlockSpec((1, nb) + c.shape[2:], lambda i: (layer, i, 0, 0, 0, 0)) for c in caches]
    return pl.pallas_call(
        functools.partial(_attn_step_kernel, l=l, nb=nb),
        grid=(b // nb,),
        in_specs=[pl.BlockSpec((nb, l, wp), lambda i: (i, 0, 0))] + cache_specs
                 + [pl.BlockSpec((ATTN_W, ATTN_W), lambda i: (0, 0)), _layer_spec(P["attn_vec"].shape, layer, 1)],
        out_specs=[pl.BlockSpec((nb, l, ATTN_W), lambda i: (i, 0, 0)),
                   pl.BlockSpec((nb, ng, l, 2 * ATTN_W), lambda i: (i, 0, 0, 0))],
        out_shape=[jax.ShapeDtypeStruct((b, l, ATTN_W), f32),
                   jax.ShapeDtypeStruct((b, ng, l, 2 * ATTN_W), f32)],
        compiler_params=_params("parallel"),
        name="dilated_attn_step",
    )(proj, *caches, P["head_mean"], P["attn_vec"])


def _merge_kernel(oa_ref, ob_ref, oc_ref, od_ref, mg_ref, x_ref, gate_ref, wb_ref, wo_ref, o_ref):
    merged = None
    start = 0
    for bi, ref in enumerate((oa_ref, ob_ref, oc_ref, od_ref)):
        width = ref.shape[-1]
        t = jnp.dot(ref[0].astype(bf16), wb_ref[0, start:start + width, :], preferred_element_type=f32)
        term = _sigmoid(mg_ref[0, :, bi * D_MODEL:(bi + 1) * D_MODEL].astype(f32)) * t
        merged = term if merged is None else merged + term
        start += width
    y = jnp.dot(merged.astype(bf16), wo_ref[0], preferred_element_type=f32)
    o_ref[0] = x_ref[0] + gate_ref[0] * y


def _merge(branches, mg3, x3, gate3, P, layer, tl):
    b, l, d = x3.shape
    per_row = gate3.shape[1] != 1
    rows = lambda w: pl.BlockSpec((1, tl, w), lambda i, j: (i, j, 0))
    gate_spec = pl.BlockSpec((1, tl if per_row else 1, d), (lambda i, j: (i, j, 0)) if per_row else (lambda i, j: (i, 0, 0)))
    return pl.pallas_call(
        _merge_kernel,
        grid=(b, l // tl),
        in_specs=[rows(o.shape[-1]) for o in branches] + [rows(N_BRANCH * d), rows(d), gate_spec,
                                                          _layer_spec(P["w_branch"].shape, layer, 2),
                                                          _layer_spec(P["w_out"].shape, layer, 2)],
        out_specs=rows(d),
        out_shape=jax.ShapeDtypeStruct((b, l, d), f32),
        compiler_params=_params("parallel", "parallel"),
        name="merge_out",
    )(*branches, mg3, x3, gate3, P["w_branch"], P["w_out"])


def _proj_windows():
    start, first = {}, 0
    for name, size in IN_SEGMENTS:
        start[name] = first
        first += size
    spans = {"ssm": ("ssm_z", "rwkv_z", 1408), "rwkv": ("rwkv_z", "mlstm_z", 2176),
             "mlstm": ("mlstm_z", "attn_z", 896), "attn": ("attn_z", "merge", 1280),
             "merge": ("merge", None, 1024)}
    out = {}
    for key, (lo, hi, tn) in spans.items():
        end = first if hi is None else start[hi]
        width = -(-(end - start[lo]) // LANES) * LANES
        assert start[lo] % 8 == 0 and width % tn == 0 and start[lo] + width <= first
        out[key] = (start[lo], width, tn)
    return out


_PROJ = _proj_windows()


def _layer(x, mod, st, sidx, caches, P, layer, w16=None):
    b, l, d = x.shape
    prompt = caches is None
    shift, scale, gate = mod[:, :d], mod[:, d:2 * d], mod[:, 2 * d:]
    if prompt:
        x3, sc3, sh3, g3 = x, scale[:, None], shift[:, None], gate[:, None]
        tl, chunk, tl_merge, tm, proj_dtype = 512, 128, 512, 1024, bf16
        rwkv_rows, nb_ssd, nb_rwkv, nb_mlstm, nb_attn = 2 * RWKV_CHUNK, math.gcd(b, 4), math.gcd(b, 2), 1, 1
    else:
        rep = lambda t: jnp.repeat(t, l, axis=0)[None]
        x3, sc3, sh3, g3 = x.reshape(1, b * l, d), rep(scale), rep(shift), rep(gate)
        tl = min(256, b * l)
        chunk, tl_merge, tm, proj_dtype = l, tl, min(1024, b * l), f32
        rwkv_rows, nb_ssd, nb_rwkv, nb_mlstm, nb_attn = l, math.gcd(b, 8), math.gcd(b, 4), 1, math.gcd(b, 2)
    h2 = _norm(x3, P["norm_w"], layer, sc3, sh3, tl).reshape(b * l, d)
    if w16 is None:
        both = {k: _matmul_t(h2, P["w_in_t"], layer, col0, width, tm, tn, proj_dtype)
                for k, (col0, width, tn) in _PROJ.items()}
        proj = {k: v[0].reshape(b, l, -1) for k, v in both.items()}
        w16 = {k: v[1] for k, v in both.items()}
    else:
        proj = {k: _matmul(h2, w16[k], tm, tn, proj_dtype).reshape(b, l, -1) for k, (_, _, tn) in _PROJ.items()}

    out_a, ssm_new, ssm_conv_new = _ssm(proj["ssm"], st["ssm_conv"], st["ssm"], sidx, P, layer, chunk, nb_ssd)
    out_b, rwkv_new, shift_new = _rwkv(proj["rwkv"], st["rwkv_shift"], st["rwkv"], sidx, P, layer,
                                       min(RWKV_CHUNK, l), rwkv_rows, nb_rwkv)
    out_c, c_new, n_new, m_new, mconv_new = _mlstm(proj["mlstm"], st["mlstm_conv"], st["mlstm_c"], st["mlstm_n"],
                                                   st["mlstm_m"], sidx, P, layer, chunk, nb_mlstm)
    if prompt:
        out_d, *kv = _attn_prompt(proj["attn"], P, layer)
        kv_new = [t.reshape(b, t.shape[1], 2, ATTN_HEADS, HEAD_DIM) for t in kv]
    else:
        out_d, kv = _attn_step(proj["attn"], caches, P, layer, nb_attn)
        kv_new = [kv[:, gi].reshape(b, l, 2, ATTN_HEADS, HEAD_DIM) for gi in range(len(ATTN_GROUPS))]

    branches = [o.reshape(x3.shape[0], x3.shape[1], -1) for o in (out_a, out_b, out_c, out_d)]
    mg3 = proj["merge"].reshape(x3.shape[0], x3.shape[1], -1)
    x_new = _merge(branches, mg3, x3, g3, P, layer, tl_merge).reshape(b, l, d)
    new_state = {"ssm": ssm_new, "ssm_conv": ssm_conv_new, "rwkv": rwkv_new, "rwkv_shift": shift_new[:, 0],
                 "mlstm_c": c_new, "mlstm_n": n_new, "mlstm_m": m_new, "mlstm_conv": mconv_new,
                 "kv_0": kv_new[0], "kv_1": kv_new[1], "kv_2": kv_new[2]}
    return x_new, new_state, w16


_STATE_NAMES = ("ssm", "ssm_conv", "rwkv", "rwkv_shift", "mlstm_c", "mlstm_n", "mlstm_m", "mlstm_conv",
                "kv_0", "kv_1", "kv_2")


def _prepare_params(w):
    P = {"norm_w": w["norm_w"][:, None, :], "ada_w": w["ada_w"], "ada_b": w["ada_b"][:, None, :],
         "w_branch": w["w_branch"].astype(bf16), "w_out": w["w_out"].astype(bf16),
         "ssm_conv_w": w["ssm_conv_w"], "mlstm_conv_w": w["mlstm_conv_w"],
         "rwkv_w2": w["rwkv_w2"], "rwkv_a2": w["rwkv_a2"], "head_mean": _head_mean_matrix(),
         "ssm_vec": _pack_vecs(SSM_VEC, SSM_VEC_N, {"conv_b": w["ssm_conv_b"], "dt_bias": w["ssm_dt_bias"],
                                                    "a_log": w["ssm_a_log"], "d": w["ssm_d"],
                                                    "norm_w": w["ssm_norm_w"]}),
         "rwkv_vec": _pack_vecs(RWKV_VEC, RWKV_VEC_N, {"mu": w["rwkv_mu"], "w0": w["rwkv_w0"], "a0": w["rwkv_a0"],
                                                       "k_k": w["rwkv_k_k"], "k_a": w["rwkv_k_a"],
                                                       "r_k": w["rwkv_r_k"], "ln_w": w["rwkv_ln_w"],
                                                       "ln_b": w["rwkv_ln_b"]}),
         "mlstm_vec": _pack_vecs(MLSTM_VEC, MLSTM_VEC_N, {"conv_b": w["mlstm_conv_b"], "gate_b": w["mlstm_gate_b"],
                                                          "norm_w": w["mlstm_norm_w"]}),
         "attn_vec": _pack_vecs(ATTN_VEC, ATTN_VEC_N, {"q_norm": jnp.tile(w["attn_q_norm"], (1, ATTN_HEADS)),
                                                       "k_norm": jnp.tile(w["attn_k_norm"], (1, ATTN_HEADS))})}
    P["w_in_t"] = jnp.transpose(w["w_in"], (0, 2, 1))
    return P


def _pad_heads(m):
    return jnp.pad(m, [(0, 0)] * (m.ndim - 1) + [(0, LANES - N_HEADS)])[..., None, :]


def kernel(x_prompt, x_sample, c_prompt, c_sample, state_ssm, state_ssm_conv, state_rwkv, state_rwkv_shift, state_mlstm_c, state_mlstm_n, state_mlstm_m, state_mlstm_conv, cache_kv_w128, cache_kv_w512, cache_kv_w2048, norm_w, ada_w, ada_b, w_in, w_branch, w_out, ssm_conv_w, ssm_conv_b, ssm_dt_bias, ssm_a_log, ssm_d, ssm_norm_w, rwkv_mu, rwkv_w0, rwkv_w2, rwkv_a0, rwkv_a2, rwkv_k_k, rwkv_k_a, rwkv_r_k, rwkv_ln_w, rwkv_ln_b, mlstm_conv_w, mlstm_conv_b, mlstm_gate_b, mlstm_norm_w, attn_q_norm, attn_k_norm):
    P = _prepare_params(dict(
        norm_w=norm_w, ada_w=ada_w, ada_b=ada_b, w_in=w_in, w_branch=w_branch, w_out=w_out, ssm_conv_w=ssm_conv_w,
        ssm_conv_b=ssm_conv_b, ssm_dt_bias=ssm_dt_bias, ssm_a_log=ssm_a_log, ssm_d=ssm_d, ssm_norm_w=ssm_norm_w,
        rwkv_mu=rwkv_mu, rwkv_w0=rwkv_w0, rwkv_w2=rwkv_w2, rwkv_a0=rwkv_a0, rwkv_a2=rwkv_a2, rwkv_k_k=rwkv_k_k,
        rwkv_k_a=rwkv_k_a, rwkv_r_k=rwkv_r_k, rwkv_ln_w=rwkv_ln_w, rwkv_ln_b=rwkv_ln_b, mlstm_conv_w=mlstm_conv_w,
        mlstm_conv_b=mlstm_conv_b, mlstm_gate_b=mlstm_gate_b, mlstm_norm_w=mlstm_norm_w, attn_q_norm=attn_q_norm,
        attn_k_norm=attn_k_norm))
    bp = x_prompt.shape[0]
    fresh = {"ssm": jnp.zeros((1, bp, N_HEADS, HEAD_DIM, SSM_STATE), f32),
             "ssm_conv": jnp.zeros((1, bp, CONV_W - 1, SSM_CONV_CH), f32),
             "rwkv": jnp.zeros((1, bp, N_HEADS, HEAD_DIM, HEAD_DIM), f32),
             "rwkv_shift": jnp.zeros((1, bp, 1, RWKV_SHIFT_CH), f32),
             "mlstm_c": jnp.zeros((1, bp, N_HEADS, HEAD_DIM, HEAD_DIM), f32),
             "mlstm_n": jnp.zeros((1, bp, N_HEADS, HEAD_DIM), f32),
             "mlstm_m": jnp.zeros((1, bp, 1, LANES), f32),
             "mlstm_conv": jnp.zeros((1, bp, CONV_W - 1, 2 * WIDTH), f32)}
    carried = {"ssm": state_ssm, "ssm_conv": state_ssm_conv, "rwkv": state_rwkv,
               "rwkv_shift": state_rwkv_shift[:, :, None, :], "mlstm_c": state_mlstm_c, "mlstm_n": state_mlstm_n,
               "mlstm_m": _pad_heads(state_mlstm_m), "mlstm_conv": state_mlstm_conv}
    c_all = jnp.concatenate([c_prompt, c_sample], axis=0)
    caches_t = [jnp.transpose(c, (0, 1, 3, 4, 5, 2)) for c in (cache_kv_w128, cache_kv_w512, cache_kv_w2048)]
    y_prompt, y_sample = x_prompt, x_sample
    prompt_states, sample_states = [], []
    for layer in range(DEPTH):
        mod = _ada(c_all, P["ada_w"], P["ada_b"], layer)
        y_prompt, sp, w16 = _layer(y_prompt, mod[:bp], fresh, 0, None, P, layer)
        y_sample, ss, _ = _layer(y_sample, mod[bp:], carried, layer, caches_t, P, layer, w16)
        prompt_states.append(sp)
        sample_states.append(ss)
    stack = lambda states, name: jnp.stack([s[name] for s in states])
    return ((y_prompt, y_sample)
            + tuple(stack(prompt_states, n) for n in _STATE_NAMES)
            + tuple(stack(sample_states, n) for n in _STATE_NAMES))
```

```python
import functools
import math

import jax
import jax.numpy as jnp
from jax import lax
from jax.experimental import pallas as pl
from jax.experimental.pallas import tpu as pltpu

f32 = jnp.float32
bf16 = jnp.bfloat16
HI = lax.Precision.HIGHEST

D_MODEL = 1024
DEPTH = 2
HEAD_DIM = 64
NORM_EPS = 1e-6
CONV_W = 4
N_HEADS = 8
WIDTH = N_HEADS * HEAD_DIM
SSM_GROUPS = 2
SSM_STATE = 64
SSM_CONV_CH = WIDTH + 2 * SSM_GROUPS * SSM_STATE
RWKV_RANK = 64
RWKV_SHIFT_CH = 3 * WIDTH + 2 * RWKV_RANK
RWKV_LN_EPS = 64e-5
RWKV_CHUNK = 64
ATTN_GROUPS = ((128, 1), (512, 4), (2048, 16))
ATTN_HEADS = 4
ATTN_W = ATTN_HEADS * HEAD_DIM
ATTN_J = 128
MLSTM_HEAD_GROUP = 8
ATTN_UNROLL = 4
N_BRANCH = 4
LANES = 128
VMEM_LIMIT = 52 * 1024 * 1024

IN_SEGMENTS = (
    ("ssm_z", WIDTH), ("ssm_xbc", SSM_CONV_CH), ("ssm_dt", N_HEADS),
    ("rwkv_z", WIDTH), ("rwkv_shift", RWKV_SHIFT_CH),
    ("mlstm_z", WIDTH), ("mlstm_qk", 2 * WIDTH), ("mlstm_v", WIDTH),
    ("mlstm_o", WIDTH), ("mlstm_if", 2 * N_HEADS),
    ("attn_z", ATTN_W), ("attn_qkv", 9 * ATTN_W),
    ("merge", N_BRANCH * D_MODEL),
)

NN = (((1,), (0,)), ((), ()))
NT = (((1,), (1,)), ((), ()))
TN = (((0,), (0,)), ((), ()))


def _mm(a, b, dims, prec):
    if prec is None:
        a, b = a.astype(bf16), b.astype(bf16)
    return lax.dot_general(a, b, dims, preferred_element_type=f32, precision=prec)


def _dot(a, b, prec=None):
    return _mm(a, b, NN, prec)


def _dot_nt(a, b, prec=None):
    return _mm(a, b, NT, prec)


def _dot_tn(a, b, transpose_on_mxu=False):
    if not transpose_on_mxu:
        return _mm(a, b, TN, None)
    n = a.shape[1]
    eye = (_iota2((n, n), 0) == _iota2((n, n), 1)).astype(bf16)
    a_t = lax.dot_general(eye, a.astype(bf16), NT, preferred_element_type=f32).astype(bf16)
    return lax.dot_general(a_t, b.astype(bf16), NN, preferred_element_type=f32)


def _sigmoid(x):
    return 0.5 * jnp.tanh(0.5 * x) + 0.5


def _silu(x):
    return x * _sigmoid(x)


def _softplus(x):
    return jnp.maximum(x, 0.0) + jnp.log1p(jnp.exp(-jnp.abs(x)))


def _params(*sem):
    return pltpu.CompilerParams(dimension_semantics=sem, vmem_limit_bytes=VMEM_LIMIT)


def _iota2(shape, dim):
    return lax.broadcasted_iota(jnp.int32, shape, dim)


def _running_sum(x, c):
    tri = (_iota2((c, c), 0) >= _iota2((c, c), 1)).astype(f32)
    return _dot(tri, x, HI)


def _vec_layout(fields):
    out, off = {}, 0
    for name, size in fields:
        padded = -(-size // LANES) * LANES
        out[name] = (off, padded)
        off += padded
    return out, off


def _pack_vecs(layout, total, vecs):
    parts = []
    for name, (_, padded) in layout.items():
        v = vecs[name].reshape(DEPTH, -1)
        parts.append(jnp.pad(v, ((0, 0), (0, padded - v.shape[1]))))
    packed = jnp.concatenate(parts, axis=1)
    assert packed.shape[1] == total
    return packed[:, None, :]


def _vec_reader(ref, layout):
    return lambda name: ref[0, :, layout[name][0]:layout[name][0] + layout[name][1]]


SSM_VEC, SSM_VEC_N = _vec_layout((("conv_b", SSM_CONV_CH), ("dt_bias", N_HEADS), ("a_log", N_HEADS),
                                  ("d", N_HEADS), ("norm_w", WIDTH)))
RWKV_VEC, RWKV_VEC_N = _vec_layout((("mu", RWKV_SHIFT_CH), ("w0", WIDTH), ("a0", WIDTH), ("k_k", WIDTH),
                                    ("k_a", WIDTH), ("r_k", WIDTH), ("ln_w", WIDTH), ("ln_b", WIDTH)))
MLSTM_VEC, MLSTM_VEC_N = _vec_layout((("conv_b", 2 * WIDTH), ("gate_b", 2 * N_HEADS), ("norm_w", WIDTH)))
ATTN_VEC, ATTN_VEC_N = _vec_layout((("q_norm", ATTN_W), ("k_norm", ATTN_W)))


def _layer_spec(shape, layer, grid_rank):
    zeros = (0,) * (len(shape) - 1)
    if grid_rank == 1:
        return pl.BlockSpec((1,) + tuple(shape[1:]), lambda i: (layer,) + zeros)
    return pl.BlockSpec((1,) + tuple(shape[1:]), lambda i, j: (layer,) + zeros)


def _ada_kernel(c_ref, w_ref, b_ref, o_ref):
    o_ref[...] = _dot(_silu(c_ref[...]), w_ref[0], HI) + b_ref[0]


def _ada(c, w, b3, layer):
    n = c.shape[0]
    return pl.pallas_call(
        _ada_kernel,
        grid=(3,),
        in_specs=[pl.BlockSpec((n, D_MODEL), lambda j: (0, 0)),
                  pl.BlockSpec((1, D_MODEL, D_MODEL), lambda j: (layer, 0, j)),
                  pl.BlockSpec((1, 1, D_MODEL), lambda j: (layer, 0, j))],
        out_specs=pl.BlockSpec((n, D_MODEL), lambda j: (0, j)),
        out_shape=jax.ShapeDtypeStruct((n, 3 * D_MODEL), f32),
        compiler_params=_params("parallel"),
        name="ada_mod",
    )(c, w, b3)


def _norm_kernel(x_ref, nw_ref, sc_ref, sh_ref, o_ref):
    x = x_ref[0]
    r = x * lax.rsqrt(jnp.mean(x * x, axis=-1, keepdims=True) + NORM_EPS)
    o_ref[0] = (r * nw_ref[0] * (1.0 + sc_ref[0]) + sh_ref[0]).astype(o_ref.dtype)


def _norm(x3, nw3, layer, sc3, sh3, tl):
    b, l, d = x3.shape
    per_row = sc3.shape[1] != 1
    mod_spec = pl.BlockSpec((1, tl if per_row else 1, d), (lambda i, j: (i, j, 0)) if per_row else (lambda i, j: (i, 0, 0)))
    return pl.pallas_call(
        _norm_kernel,
        grid=(b, l // tl),
        in_specs=[pl.BlockSpec((1, tl, d), lambda i, j: (i, j, 0)),
                  _layer_spec(nw3.shape, layer, 2), mod_spec, mod_spec],
        out_specs=pl.BlockSpec((1, tl, d), lambda i, j: (i, j, 0)),
        out_shape=jax.ShapeDtypeStruct((b, l, d), bf16),
        compiler_params=_params("parallel", "parallel"),
        name="mod_rmsnorm",
    )(x3, nw3, sc3, sh3)


def _mm_t_kernel(x_ref, wt_ref, o_ref, w16_ref):
    @pl.when(pl.program_id(1) == 0)
    def _():
        w16_ref[...] = wt_ref[0].T.astype(bf16)

    o_ref[...] = jnp.dot(x_ref[...], w16_ref[...], preferred_element_type=f32).astype(o_ref.dtype)


def _matmul_t(x, w_t, layer, col0, width, tm, tn, out_dtype):
    n, k = x.shape
    w_spec = pl.BlockSpec((pl.Element(1), pl.Element(tn), pl.Element(k)),
                          lambda j, i: (layer, pl.multiple_of(col0 + j * tn, 8), 0))
    return pl.pallas_call(
        _mm_t_kernel,
        grid=(width // tn, n // tm),
        in_specs=[pl.BlockSpec((tm, k), lambda j, i: (i, 0)), w_spec],
        out_specs=[pl.BlockSpec((tm, tn), lambda j, i: (i, j)), pl.BlockSpec((k, tn), lambda j, i: (0, j))],
        out_shape=[jax.ShapeDtypeStruct((n, width), out_dtype), jax.ShapeDtypeStruct((k, width), bf16)],
        compiler_params=_params("parallel", "arbitrary"),
        name="in_proj",
    )(x, w_t)


def _mm_kernel(x_ref, w_ref, o_ref):
    o_ref[...] = jnp.dot(x_ref[...], w_ref[...], preferred_element_type=f32).astype(o_ref.dtype)


def _matmul(x, w16, tm, tn, out_dtype):
    n, k = x.shape
    width = w16.shape[1]
    return pl.pallas_call(
        _mm_kernel,
        grid=(width // tn, n // tm),
        in_specs=[pl.BlockSpec((tm, k), lambda j, i: (i, 0)), pl.BlockSpec((k, tn), lambda j, i: (0, j))],
        out_specs=pl.BlockSpec((tm, tn), lambda j, i: (i, j)),
        out_shape=jax.ShapeDtypeStruct((n, width), out_dtype),
        compiler_params=_params("parallel", "parallel"),
        name="in_proj",
    )(x, w16)


def _causal_conv(ext_scr, u, c, cw_ref, bias):
    ext_scr[8:8 + c, :] = u
    acc = bias
    for i in range(CONV_W):
        acc = acc + ext_scr[5 + i:5 + i + c, :] * cw_ref[0, i:i + 1, :]
    tail = ext_scr[c + 5:c + 8, :]
    ext_scr[5:8, :] = tail
    return acc, tail


def _interleave(gens):
    gens = list(gens)
    while gens:
        alive = []
        for g in gens:
            try:
                next(g)
                alive.append(g)
            except StopIteration:
                pass
        gens = alive


def _ssm_kernel(proj_ref, conv0_ref, h0_ref, cw_ref, vec_ref, out_ref, hn_ref, convn_ref, ext_scr, y_scr, *, c, nb):
    vec = _vec_reader(vec_ref, SSM_VEC)

    @pl.when(pl.program_id(1) == 0)
    def _():
        for bi in range(nb):
            ext_scr[bi, 5:8, :] = conv0_ref[0, bi]
            hn_ref[bi] = h0_ref[0, bi]

    causal = _iota2((c, c), 0) >= _iota2((c, c), 1)
    heads = range(N_HEADS)
    per_group = N_HEADS // SSM_GROUPS
    sls = [slice(h * HEAD_DIM, (h + 1) * HEAD_DIM) for h in heads]

    def elem(bi):
        proj, out, hn, y_acc = proj_ref.at[bi], out_ref.at[bi], hn_ref.at[bi], y_scr.at[bi]
        u = proj[:, WIDTH:WIDTH + SSM_CONV_CH].astype(f32)
        acc, tail = _causal_conv(ext_scr.at[bi], u, c, cw_ref, vec("conv_b"))
        convn_ref[bi] = tail
        xbc = _silu(acc)
        xs = xbc[:, :WIDTH]
        bm = xbc[:, WIDTH:WIDTH + LANES]
        cm = xbc[:, WIDTH + LANES:]
        dt = _softplus(proj[:, WIDTH + SSM_CONV_CH:].astype(f32) + vec("dt_bias"))
        da = dt * (-jnp.exp(vec("a_log")))
        yield
        cs = _running_sum(da, c)
        cs_t = cs.T
        dt_t = dt.T
        cs_end = cs[c - 1:c, :]
        dsk = vec("d")
        bgs = [bm[:, g * SSM_STATE:(g + 1) * SSM_STATE] for g in range(SSM_GROUPS)]
        cgs = [cm[:, g * SSM_STATE:(g + 1) * SSM_STATE] for g in range(SSM_GROUPS)]
        yield
        gmats = [_dot_nt(cgs[g], bgs[g]) for g in range(SSM_GROUPS)]
        cs_cols = [cs[:, h:h + 1] for h in heads]
        yield
        lmats = [gmats[h // per_group] * jnp.exp(jnp.where(causal, cs_cols[h] - cs_t[h:h + 1, :], -jnp.inf))
                 * dt_t[h:h + 1, :] for h in heads]
        xhs = [xs[:, sl] for sl in sls]
        hprevs = [hn[h] for h in heads]
        yield
        y_in = [_dot(lmats[h], xhs[h]) for h in heads]
        y_st = [_dot_nt(cgs[h // per_group], hprevs[h]) for h in heads]
        yield
        for h in heads:
            y_acc[:, sls[h]] = y_in[h] + y_st[h] * jnp.exp(cs_cols[h]) + xhs[h] * dsk[:, h:h + 1]
        ces = [cs_end[:, h:h + 1] for h in heads]
        upd = [_dot_tn(xhs[h] * (jnp.exp(ces[h] - cs_cols[h]) * dt[:, h:h + 1]), bgs[h // per_group],
                       transpose_on_mxu=True) for h in heads]
        yield
        for h in heads:
            hn[h] = hprevs[h] * jnp.exp(ces[h]) + upd[h]
        gy = y_acc[...] * _silu(proj[:, :WIDTH].astype(f32))
        nw = vec("norm_w")
        gw = WIDTH // SSM_GROUPS
        for g in range(SSM_GROUPS):
            part = gy[:, g * gw:(g + 1) * gw]
            ms = jnp.mean(part * part, axis=-1, keepdims=True)
            out[:, g * gw:(g + 1) * gw] = part * lax.rsqrt(ms + NORM_EPS) * nw[:, g * gw:(g + 1) * gw]

    _interleave(elem(bi) for bi in range(nb))


def _state_spec(shape, sidx, nb):
    zeros = (0,) * (len(shape) - 2)
    return pl.BlockSpec((1, nb) + tuple(shape[2:]), lambda i, j: (sidx, i) + zeros)


def _batch_spec(nb, *dims):
    zeros = (0,) * len(dims)
    return pl.BlockSpec((nb,) + dims, lambda i, j: (i,) + zeros)


def _ssm(proj, conv0, h0, sidx, P, layer, c, nb):
    b, l, wp = proj.shape
    return pl.pallas_call(
        functools.partial(_ssm_kernel, c=c, nb=nb),
        grid=(b // nb, l // c),
        in_specs=[pl.BlockSpec((nb, c, wp), lambda i, j: (i, j, 0)),
                  _state_spec(conv0.shape, sidx, nb), _state_spec(h0.shape, sidx, nb),
                  _layer_spec(P["ssm_conv_w"].shape, layer, 2), _layer_spec(P["ssm_vec"].shape, layer, 2)],
        out_specs=[pl.BlockSpec((nb, c, WIDTH), lambda i, j: (i, j, 0)),
                   _batch_spec(nb, N_HEADS, HEAD_DIM, SSM_STATE), _batch_spec(nb, CONV_W - 1, SSM_CONV_CH)],
        out_shape=[jax.ShapeDtypeStruct((b, l, WIDTH), f32),
                   jax.ShapeDtypeStruct((b, N_HEADS, HEAD_DIM, SSM_STATE), f32),
                   jax.ShapeDtypeStruct((b, CONV_W - 1, SSM_CONV_CH), f32)],
        scratch_shapes=[pltpu.VMEM((nb, c + 8, SSM_CONV_CH), f32), pltpu.VMEM((nb, c, WIDTH), f32)],
        compiler_params=_params("parallel", "arbitrary"),
        name="ssd_branch",
    )(proj, conv0, h0, P["ssm_conv_w"], P["ssm_vec"])


def _rwkv_chunk_kernel(proj_ref, shift0_ref, s0_ref, vec_ref, w2_ref, a2_ref,
                       out_ref, sn_ref, shiftn_ref, ext_scr, *, c, rows, nb):
    vec = _vec_reader(vec_ref, RWKV_VEC)

    @pl.when(pl.program_id(1) == 0)
    def _():
        for bi in range(nb):
            ext_scr[bi, 7:8, :] = shift0_ref[0, bi]
            sn_ref[bi] = s0_ref[0, bi]

    row2 = _iota2((c, 2 * c), 0)
    col2 = _iota2((c, 2 * c), 1)
    col2 = jnp.where(col2 >= c, col2 - c, col2)
    strict2 = row2 > col2
    incl2 = row2 >= col2
    n_double = (c - 1).bit_length()
    heads = range(N_HEADS)
    sls = [slice(h * HEAD_DIM, (h + 1) * HEAD_DIM) for h in heads]

    def elem(bi):
        proj, out, sn, ext = proj_ref.at[bi], out_ref.at[bi], sn_ref.at[bi], ext_scr.at[bi]
        u = proj[:, WIDTH:].astype(f32)
        ext[8:8 + rows, :] = u
        sh = u + (ext[7:7 + rows, :] - u) * vec("mu")
        last = ext[rows + 7:rows + 8, :]
        ext[7:8, :] = last
        shiftn_ref[bi] = last
        r_all = sh[:, :WIDTH]
        k = sh[:, WIDTH:2 * WIDTH]
        v_all = sh[:, 2 * WIDTH:3 * WIDTH]
        w_lo = sh[:, 3 * WIDTH:3 * WIDTH + RWKV_RANK]
        a_lo = sh[:, 3 * WIDTH + RWKV_RANK:]
        yield
        w_log = -_softplus(-(vec("w0") + _dot(jnp.tanh(w_lo), w2_ref[0]))) - 0.5
        lw_all = -jnp.exp(w_log)
        a_in_all = _sigmoid(vec("a0") + _dot(a_lo, a2_ref[0]))
        kk_all = k * vec("k_k")
        k2_all = k * (1.0 + (a_in_all - 1.0) * vec("k_a"))
        rk, lnw, lnb = vec("r_k"), vec("ln_w"), vec("ln_b")
        yield
        for c0 in range(0, rows, c):
            cr = slice(c0, c0 + c)
            r, v, lw, a_in, kk, k2 = r_all[cr], v_all[cr], lw_all[cr], a_in_all[cr], kk_all[cr], k2_all[cr]
            cl = _running_sum(lw, c)
            cl_end = cl[c - 1:c, :]
            g_fwd = jnp.exp(cl)
            g_inv = jnp.exp(-cl)
            g_rem = jnp.exp(cl_end - cl)
            g_prev = jnp.exp(cl - lw)
            g_end = jnp.exp(cl_end)
            yield
            s0s = [sn[h] for h in heads]
            bhs, ars, bks = [], [], []
            for sl in sls:
                kkh = kk[:, sl]
                kkn = kkh * (1.0 / jnp.maximum(jnp.sqrt(jnp.sum(kkh * kkh, axis=-1, keepdims=True)), 1e-12))
                bhs.append(kkn * a_in[:, sl])
                ars.append(jnp.concatenate([-kkn * g_prev[:, sl], r[:, sl] * g_fwd[:, sl]], axis=0))
                bks.append(jnp.concatenate([bhs[-1] * g_inv[:, sl], k2[:, sl] * g_inv[:, sl]], axis=0))
            yield
            m4s = [_dot_nt(ars[h], bks[h]) for h in heads]
            ahs = [_dot_nt(ars[h], s0s[h]) for h in heads]
            yield
            tops = [jnp.where(strict2, m4[:c, :], 0.0) for m4 in m4s]
            bots = [jnp.where(incl2, m4[c:, :], 0.0) for m4 in m4s]
            amats = [top[:, :c].astype(bf16) for top in tops]
            xs = [ahs[h][:c] + _dot(tops[h][:, c:], v[:, sls[h]]) for h in heads]
            yield
            for i in range(n_double):
                xs = [xs[h] + _dot(amats[h], xs[h]) for h in heads]
                if i + 1 < n_double:
                    amats = [_dot(amats[h], amats[h]).astype(bf16) for h in heads]
                yield
            pvs = [jnp.concatenate([xs[h], v[:, sls[h]]], axis=0) for h in heads]
            ys = [ahs[h][c:] + _dot(bots[h], pvs[h]) for h in heads]
            upd = [_dot_tn(pvs[h], jnp.concatenate([bhs[h] * g_rem[:, sls[h]], k2[:, sls[h]] * g_rem[:, sls[h]]],
                                                   axis=0)) for h in heads]
            yield
            for h in heads:
                sl = sls[h]
                sn[h] = s0s[h] * g_end[:, sl] + upd[h]
                y = ys[h]
                mu = jnp.mean(y, axis=-1, keepdims=True)
                var = jnp.mean(jnp.square(y - mu), axis=-1, keepdims=True)
                ln = (y - mu) * lax.rsqrt(var + RWKV_LN_EPS) * lnw[:, sl] + lnb[:, sl]
                bonus = jnp.sum(r[:, sl] * k2[:, sl] * rk[:, sl], axis=-1, keepdims=True) * v[:, sl]
                out[cr, sl] = (ln + bonus) * _silu(proj[cr, sl].astype(f32))
            yield

    _interleave(elem(bi) for bi in range(nb))


def _rwkv(proj, shift0, s0, sidx, P, layer, c, rows, nb):
    b, l, wp = proj.shape
    return pl.pallas_call(
        functools.partial(_rwkv_chunk_kernel, c=c, rows=rows, nb=nb),
        grid=(b // nb, l // rows),
        in_specs=[pl.BlockSpec((nb, rows, wp), lambda i, j: (i, j, 0)),
                  _state_spec(shift0.shape, sidx, nb), _state_spec(s0.shape, sidx, nb),
                  _layer_spec(P["rwkv_vec"].shape, layer, 2),
                  _layer_spec(P["rwkv_w2"].shape, layer, 2), _layer_spec(P["rwkv_a2"].shape, layer, 2)],
        out_specs=[pl.BlockSpec((nb, rows, WIDTH), lambda i, j: (i, j, 0)),
                   _batch_spec(nb, N_HEADS, HEAD_DIM, HEAD_DIM), _batch_spec(nb, 1, RWKV_SHIFT_CH)],
        out_shape=[jax.ShapeDtypeStruct((b, l, WIDTH), f32),
                   jax.ShapeDtypeStruct((b, N_HEADS, HEAD_DIM, HEAD_DIM), f32),
                   jax.ShapeDtypeStruct((b, 1, RWKV_SHIFT_CH), f32)],
        scratch_shapes=[pltpu.VMEM((nb, rows + 8, RWKV_SHIFT_CH), f32)],
        compiler_params=_params("parallel", "arbitrary"),
        name="rwkv7_chunked",
    )(proj, shift0, s0, P["rwkv_vec"], P["rwkv_w2"], P["rwkv_a2"])


def _mlstm_kernel(proj_ref, conv0_ref, c0_ref, n0_ref, m0_ref, cw_ref, vec_ref,
                  out_ref, cn_ref, nn_ref, mn_ref, convn_ref, ext_scr, h_scr, *, c, nb):
    vec = _vec_reader(vec_ref, MLSTM_VEC)

    @pl.when(pl.program_id(1) == 0)
    def _():
        for bi in range(nb):
            ext_scr[bi, 5:8, :] = conv0_ref[0, bi]
            cn_ref[bi] = c0_ref[0, bi]
            nn_ref[bi] = n0_ref[0, bi]
            mn_ref[bi] = m0_ref[0, bi]

    causal = _iota2((c, c), 0) >= _iota2((c, c), 1)
    lane = _iota2((1, LANES), 1)
    heads = range(N_HEADS)
    sls = [slice(h * HEAD_DIM, (h + 1) * HEAD_DIM) for h in heads]

    def elem(bi):
        proj, out, cn, nn, h_acc = proj_ref.at[bi], out_ref.at[bi], cn_ref.at[bi], nn_ref.at[bi], h_scr.at[bi]
        u = proj[:, WIDTH:3 * WIDTH].astype(f32)
        acc, tail = _causal_conv(ext_scr.at[bi], u, c, cw_ref, vec("conv_b"))
        convn_ref[bi] = tail
        qk = _silu(acc)
        gates = proj[:, 5 * WIDTH:].astype(f32) + vec("gate_b")
        lf = -_softplus(-gates)
        yield
        bc = _running_sum(lf, c)
        bc_t = bc.T
        ig_t = gates.T
        m_prev = mn_ref[bi]
        yield
        m_out = jnp.zeros((1, LANES), f32)
        for grp in range(0, N_HEADS, MLSTM_HEAD_GROUP):
            heads = range(grp, grp + MLSTM_HEAD_GROUP)
            b_cols = {h: bc[:, N_HEADS + h:N_HEADS + h + 1] for h in heads}
            fold = c <= 8
            emats = {h: jnp.where(causal, (b_cols[h] if fold else 0.0) + ig_t[h:h + 1, :]
                                  - bc_t[N_HEADS + h:N_HEADS + h + 1, :], -jnp.inf) for h in heads}
            ecols = {h: 0.0 if fold else b_cols[h] for h in heads}
            m0s = {h: m_prev[:, h:h + 1] for h in heads}
            gcols = {h: b_cols[h] + m0s[h] for h in heads}
            mts = {h: jnp.maximum(gcols[h], ecols[h] + jnp.max(emats[h], axis=-1, keepdims=True)) for h in heads}
            yield
            qhs = {h: qk[:, sls[h]] for h in heads}
            khs = {h: qk[:, WIDTH + h * HEAD_DIM:WIDTH + (h + 1) * HEAD_DIM] * HEAD_DIM ** -0.5 for h in heads}
            vhs = {h: proj[:, 3 * WIDTH + h * HEAD_DIM:3 * WIDTH + (h + 1) * HEAD_DIM].astype(f32) for h in heads}
            cprevs = {h: cn[h] for h in heads}
            nprevs = {h: nn[h:h + 1, :] for h in heads}
            yield
            sws = {h: jnp.exp(emats[h] + (ecols[h] - mts[h])) * _dot_nt(qhs[h], khs[h]) for h in heads}
            wis = {h: jnp.exp(gcols[h] - mts[h]) for h in heads}
            qcs = {h: _dot(qhs[h], cprevs[h]) for h in heads}
            yield
            nums = {h: _dot(sws[h], vhs[h]) + wis[h] * qcs[h] for h in heads}
            yield
            for h in heads:
                den = (jnp.sum(sws[h], axis=-1, keepdims=True)
                       + wis[h] * jnp.sum(qhs[h] * nprevs[h], axis=-1, keepdims=True))
                h_acc[:, sls[h]] = nums[h] * (1.0 / jnp.maximum(jnp.abs(den), jnp.exp(-mts[h])))
            m_news = {h: mts[h][c - 1:c, :] for h in heads}
            b_ends = {h: b_cols[h][c - 1:c, :] for h in heads}
            kws = {h: khs[h] * jnp.exp(b_ends[h] - b_cols[h] + gates[:, h:h + 1] - m_news[h]) for h in heads}
            yield
            kvs = {h: _dot_tn(kws[h], vhs[h]) for h in heads}
            yield
            for h in heads:
                dec = jnp.exp(b_ends[h] + m0s[h] - m_news[h])
                cn[h] = dec * cprevs[h] + kvs[h]
                nn[h:h + 1, :] = dec * nprevs[h] + jnp.sum(kws[h], axis=0, keepdims=True)
                m_out = jnp.where(lane == h, m_news[h], m_out)
        mn_ref[bi] = m_out
        nw = vec("norm_w")
        for h in range(N_HEADS):
            sl = sls[h]
            o_gate = proj[:, 4 * WIDTH + h * HEAD_DIM:4 * WIDTH + (h + 1) * HEAD_DIM].astype(f32)
            hm = _sigmoid(o_gate) * h_acc[:, sl]
            ms = jnp.mean(hm * hm, axis=-1, keepdims=True)
            out[:, sl] = hm * lax.rsqrt(ms + NORM_EPS) * nw[:, sl] * _silu(proj[:, sl].astype(f32))

    _interleave(elem(bi) for bi in range(nb))


def _mlstm(proj, conv0, c0, n0, m0, sidx, P, layer, c, nb):
    b, l, wp = proj.shape
    outs = pl.pallas_call(
        functools.partial(_mlstm_kernel, c=c, nb=nb),
        grid=(b // nb, l // c),
        in_specs=[pl.BlockSpec((nb, c, wp), lambda i, j: (i, j, 0)),
                  _state_spec(conv0.shape, sidx, nb), _state_spec(c0.shape, sidx, nb),
                  _state_spec(n0.shape, sidx, nb), _state_spec(m0.shape, sidx, nb),
                  _layer_spec(P["mlstm_conv_w"].shape, layer, 2), _layer_spec(P["mlstm_vec"].shape, layer, 2)],
        out_specs=[pl.BlockSpec((nb, c, WIDTH), lambda i, j: (i, j, 0)),
                   _batch_spec(nb, N_HEADS, HEAD_DIM, HEAD_DIM), _batch_spec(nb, N_HEADS, HEAD_DIM),
                   _batch_spec(nb, 1, LANES), _batch_spec(nb, CONV_W - 1, 2 * WIDTH)],
        out_shape=[jax.ShapeDtypeStruct((b, l, WIDTH), f32),
                   jax.ShapeDtypeStruct((b, N_HEADS, HEAD_DIM, HEAD_DIM), f32),
                   jax.ShapeDtypeStruct((b, N_HEADS, HEAD_DIM), f32),
                   jax.ShapeDtypeStruct((b, 1, LANES), f32),
                   jax.ShapeDtypeStruct((b, CONV_W - 1, 2 * WIDTH), f32)],
        scratch_shapes=[pltpu.VMEM((nb, c + 8, 2 * WIDTH), f32), pltpu.VMEM((nb, c, WIDTH), f32)],
        compiler_params=_params("parallel", "arbitrary"),
        name="mlstm_branch",
    )(proj, conv0, c0, n0, m0, P["mlstm_conv_w"], P["mlstm_vec"])
    out, cn, nn, mn, convn = outs
    return out, cn, nn, mn[:, 0, :N_HEADS], convn


def _head_mean_matrix():
    hid = jnp.arange(ATTN_W) // HEAD_DIM
    return ((hid[:, None] == hid[None, :]).astype(f32) / HEAD_DIM).astype(bf16)


def _qk_norm(x, hm_ref, w):
    sq = x * x
    hi = sq.astype(bf16)
    lo = (sq - hi.astype(f32)).astype(bf16)
    ms = _dot(hi, hm_ref[...]) + _dot(lo, hm_ref[...])
    return x * lax.rsqrt(ms + NORM_EPS) * w


def _attn_prompt_kernel(q_ref, k_ref, v_ref, z_ref, hm_ref, vec_ref, out_ref, kv0_ref, kv1_ref, kv2_ref,
                        qn_scr, kn_scr, vv_scr, acc_scr, m_scr, d_scr, to_scr, tl_scr, *, l):
    vec = _vec_reader(vec_ref, ATTN_VEC)
    g = pl.program_id(1)
    rb = 256
    n_pairs = ATTN_W // LANES

    def norm_body(i, carry):
        rows = pl.ds(pl.multiple_of(i * rb, rb), rb)
        qn = _qk_norm(q_ref[0, rows, :].astype(f32), hm_ref, vec("q_norm"))
        kn = _qk_norm(k_ref[0, rows, :].astype(f32), hm_ref, vec("k_norm"))
        v = v_ref[0, rows, :].astype(f32)
        for p in range(n_pairs):
            qn_scr[p, rows, :] = qn[:, p * LANES:(p + 1) * LANES]
            kn_scr[p, rows, :] = kn[:, p * LANES:(p + 1) * LANES]
            vv_scr[p, rows, :] = v[:, p * LANES:(p + 1) * LANES]
        return carry

    lax.fori_loop(0, l // rb, norm_body, 0)

    qb = ATTN_J
    qi = _iota2((qb, qb), 0)
    ki = _iota2((qb, qb), 1)
    scale = HEAD_DIM ** -0.5

    def run_group(dil, first):
        n_iter = l // qb

        def block(i, slot):
            r = i % dil
            blk = i // dil
            u0 = blk * qb
            rows_c = pl.ds(u0 * dil + r, qb, stride=dil)
            rows_p = pl.ds(jnp.maximum(u0 - qb, 0) * dil + r, qb, stride=dil)
            mask_c = ki <= qi
            mask_p = (ki >= qi) & (blk > 0)
            hsl = [slice(hh * HEAD_DIM, (hh + 1) * HEAD_DIM) for hh in range(LANES // HEAD_DIM)]
            heads = [(p, sl) for p in range(n_pairs) for sl in hsl]
            qv = [qn_scr[p, rows_c, :] for p in range(n_pairs)]
            kc = [kn_scr[p, rows_c, :] for p in range(n_pairs)]
            vc = [vv_scr[p, rows_c, :] for p in range(n_pairs)]
            kp = [kn_scr[p, rows_p, :] for p in range(n_pairs)]
            vp = [vv_scr[p, rows_p, :] for p in range(n_pairs)]
            yield
            scs = [jnp.where(mask_c, _dot_nt(qv[p][:, sl], kc[p][:, sl]) * scale, -jnp.inf) for p, sl in heads]
            sps = [jnp.where(mask_p, _dot_nt(qv[p][:, sl], kp[p][:, sl]) * scale, -jnp.inf) for p, sl in heads]
            yield
            ms = [jnp.max(jnp.maximum(sc, sp), axis=-1, keepdims=True) for sc, sp in zip(scs, sps)]
            pcs = [jnp.exp(sc - m) for sc, m in zip(scs, ms)]
            pps = [jnp.exp(sp - m) for sp, m in zip(sps, ms)]
            dens = [jnp.sum(pc + pp, axis=-1, keepdims=True) for pc, pp in zip(pcs, pps)]
            yield
            for (p, sl), pc, pp, m, den in zip(heads, pcs, pps, ms, dens):
                to_scr[slot, p, :, sl] = (_dot(pc, vc[p][:, sl]) + _dot(pp, vp[p][:, sl])) * (1.0 / den)
                tl_scr[slot, p, :, sl] = jnp.broadcast_to(m + jnp.log(den), (qb, HEAD_DIM))
            yield
            for p in range(n_pairs):
                o = to_scr[slot, p]
                lse = tl_scr[slot, p]
                if first:
                    acc_scr[p, rows_c, :] = o
                    m_scr[p, rows_c, :] = lse
                    d_scr[p, rows_c, :] = jnp.ones_like(lse)
                else:
                    m_old = m_scr[p, rows_c, :]
                    m_new = jnp.maximum(m_old, lse)
                    a_old = jnp.exp(m_old - m_new)
                    a_new = jnp.exp(lse - m_new)
                    acc_scr[p, rows_c, :] = acc_scr[p, rows_c, :] * a_old + o * a_new
                    d_scr[p, rows_c, :] = d_scr[p, rows_c, :] * a_old + a_new
                    m_scr[p, rows_c, :] = m_new

        def body(i, carry):
            _interleave(block(i * ATTN_UNROLL + s, s) for s in range(ATTN_UNROLL))
            return carry

        lax.fori_loop(0, n_iter // ATTN_UNROLL, body, 0)

    def save_window(kv_ref):
        w = kv_ref.shape[1]
        wb = min(w, rb)
        for r0 in range(0, w, wb):
            src = slice(l - w + r0, l - w + r0 + wb)
            for p in range(n_pairs):
                kv_ref[0, r0:r0 + wb, p * LANES:(p + 1) * LANES] = kn_scr[p, src, :]
                kv_ref[0, r0:r0 + wb, ATTN_W + p * LANES:ATTN_W + (p + 1) * LANES] = vv_scr[p, src, :]

    def group_step(gi, dil):
        save_window((kv0_ref, kv1_ref, kv2_ref)[gi])
        run_group(dil, gi == 0)

    for gi, (_, dil) in enumerate(ATTN_GROUPS):
        pl.when(g == gi)(functools.partial(group_step, gi, dil))

    @pl.when(g == len(ATTN_GROUPS) - 1)
    def _():
        def fin_body(i, carry):
            rows = pl.ds(pl.multiple_of(i * rb, rb), rb)
            for p in range(n_pairs):
                lanes = slice(p * LANES, (p + 1) * LANES)
                z = z_ref[0, rows, lanes].astype(f32)
                out_ref[0, rows, lanes] = acc_scr[p, rows, :] / d_scr[p, rows, :] * _silu(z)
            return carry

        lax.fori_loop(0, l // rb, fin_body, 0)


def _attn_prompt(proj, P, layer):
    b, l, _ = proj.shape
    ng = len(ATTN_GROUPS)
    col = lambda base: pl.BlockSpec((1, l, ATTN_W), lambda i, g: (i, 0, base + g))
    pair_scr = pltpu.VMEM((ATTN_W // LANES, l, LANES), f32)
    windows = [min(w, l) for w, _ in ATTN_GROUPS]
    return pl.pallas_call(
        functools.partial(_attn_prompt_kernel, l=l),
        grid=(b, ng),
        in_specs=[col(1), col(1 + ng), col(1 + 2 * ng),
                  pl.BlockSpec((1, l, ATTN_W), lambda i, g: (i, 0, 0)),
                  pl.BlockSpec((ATTN_W, ATTN_W), lambda i, g: (0, 0)),
                  _layer_spec(P["attn_vec"].shape, layer, 2)],
        out_specs=[pl.BlockSpec((1, l, ATTN_W), lambda i, g: (i, 0, 0))]
                  + [pl.BlockSpec((1, w, 2 * ATTN_W), lambda i, g: (i, 0, 0)) for w in windows],
        out_shape=[jax.ShapeDtypeStruct((b, l, ATTN_W), f32)]
                  + [jax.ShapeDtypeStruct((b, w, 2 * ATTN_W), f32) for w in windows],
        scratch_shapes=[pair_scr] * 6 + [pltpu.VMEM((ATTN_UNROLL, ATTN_W // LANES, ATTN_J, LANES), f32)] * 2,
        compiler_params=_params("parallel", "arbitrary"),
        name="dilated_attn_prompt",
    )(proj, proj, proj, proj, P["head_mean"], P["attn_vec"])


def _attn_step_kernel(x_ref, c0_ref, c1_ref, c2_ref, hm_ref, vec_ref, out_ref, kv_ref, *, l, nb):
    vec = _vec_reader(vec_ref, ATTN_VEC)
    ng = len(ATTN_GROUPS)
    scale = HEAD_DIM ** -0.5
    c_refs = (c0_ref, c1_ref, c2_ref)
    hsl = [slice(h * HEAD_DIM, (h + 1) * HEAD_DIM) for h in range(ATTN_HEADS)]
    masks_p, masks_n = [], []
    for gi, (window, dil) in enumerate(ATTN_GROUPS):
        w = c_refs[gi].shape[-1]
        jmax = window // dil
        shift = dil.bit_length() - 1
        dist_p = w + _iota2((l, w), 0) - _iota2((l, w), 1)
        masks_p.append(((dist_p & (dil - 1)) == 0) & ((dist_p >> shift) <= jmax))
        dist_n = _iota2((l, l), 0) - _iota2((l, l), 1)
        masks_n.append((dist_n >= 0) & ((dist_n & (dil - 1)) == 0) & ((dist_n >> shift) <= jmax))
    gh = [(gi, h) for gi in range(ng) for h in range(ATTN_HEADS)]

    def elem(bi):
        x = x_ref[bi]
        qns, kns, vs = [], [], []
        for gi in range(ng):
            qns.append(_qk_norm(x[:, (1 + gi) * ATTN_W:(2 + gi) * ATTN_W], hm_ref, vec("q_norm")))
            kns.append(_qk_norm(x[:, (1 + ng + gi) * ATTN_W:(2 + ng + gi) * ATTN_W], hm_ref, vec("k_norm")))
            vs.append(x[:, (1 + 2 * ng + gi) * ATTN_W:(2 + 2 * ng + gi) * ATTN_W])
            kv_ref[bi, gi, :, 0:ATTN_W] = kns[gi]
            kv_ref[bi, gi, :, ATTN_W:2 * ATTN_W] = vs[gi]
        yield
        sps = [jnp.where(masks_p[gi], _dot(qns[gi][:, hsl[h]], c_refs[gi][0, bi, 0, h]) * scale, -jnp.inf)
               for gi, h in gh]
        sns = [jnp.where(masks_n[gi], _dot_nt(qns[gi][:, hsl[h]], kns[gi][:, hsl[h]]) * scale, -jnp.inf)
               for gi, h in gh]
        yield
        ms = [jnp.maximum(jnp.max(sp, axis=-1, keepdims=True), jnp.max(sn, axis=-1, keepdims=True))
              for sp, sn in zip(sps, sns)]
        pps = [jnp.exp(sp - m) for sp, m in zip(sps, ms)]
        pns = [jnp.exp(sn - m) for sn, m in zip(sns, ms)]
        dens = [jnp.sum(pp, axis=-1, keepdims=True) + jnp.sum(pn, axis=-1, keepdims=True)
                for pp, pn in zip(pps, pns)]
        yield
        og = [(_dot_nt(pp, c_refs[gi][0, bi, 1, h]) + _dot(pn, vs[gi][:, hsl[h]])) * (1.0 / den)
              for (gi, h), pp, pn, den in zip(gh, pps, pns, dens)]
        lg = [m + jnp.log(den) for m, den in zip(ms, dens)]
        yield
        outs = [og[gi * ATTN_HEADS:(gi + 1) * ATTN_HEADS] for gi in range(ng)]
        lses = [lg[gi * ATTN_HEADS:(gi + 1) * ATTN_HEADS] for gi in range(ng)]
        for h in range(ATTN_HEADS):
            sl = hsl[h]
            mx = functools.reduce(jnp.maximum, [lses[gi][h] for gi in range(ng)])
            ws = [jnp.exp(lses[gi][h] - mx) for gi in range(ng)]
            tot = functools.reduce(lambda a, b: a + b, ws)
            o = functools.reduce(lambda a, b: a + b, [outs[gi][h] * (ws[gi] / tot) for gi in range(ng)])
            out_ref[bi, :, sl] = o * _silu(x[:, sl])

    _interleave(elem(bi) for bi in range(nb))


def _attn_step(proj, caches, P, layer, nb):
    b, l, wp = proj.shape
    ng = len(ATTN_GROUPS)
    cache_specs = [pl.BlockSpec((1, nb) + c.shape[2:], lambda i: (layer, i, 0, 0, 0, 0)) for c in caches]
    return pl.pallas_call(
        functools.partial(_attn_step_kernel, l=l, nb=nb),
        grid=(b // nb,),
        in_specs=[pl.BlockSpec((nb, l, wp), lambda i: (i, 0, 0))] + cache_specs
                 + [pl.BlockSpec((ATTN_W, ATTN_W), lambda i: (0, 0)), _layer_spec(P["attn_vec"].shape, layer, 1)],
        out_specs=[pl.BlockSpec((nb, l, ATTN_W), lambda i: (i, 0, 0)),
                   pl.BlockSpec((nb, ng, l, 2 * ATTN_W), lambda i: (i, 0, 0, 0))],
        out_shape=[jax.ShapeDtypeStruct((b, l, ATTN_W), f32),
                   jax.ShapeDtypeStruct((b, ng, l, 2 * ATTN_W), f32)],
        compiler_params=_params("parallel"),
        name="dilated_attn_step",
    )(proj, *caches, P["head_mean"], P["attn_vec"])


def _merge_kernel(oa_ref, ob_ref, oc_ref, od_ref, mg_ref, x_ref, gate_ref, wb_ref, wo_ref, o_ref):
    merged = None
    start = 0
    for bi, ref in enumerate((oa_ref, ob_ref, oc_ref, od_ref)):
        width = ref.shape[-1]
        t = jnp.dot(ref[0].astype(bf16), wb_ref[0, start:start + width, :], preferred_element_type=f32)
        term = _sigmoid(mg_ref[0, :, bi * D_MODEL:(bi + 1) * D_MODEL].astype(f32)) * t
        merged = term if merged is None else merged + term
        start += width
    y = jnp.dot(merged.astype(bf16), wo_ref[0], preferred_element_type=f32)
    o_ref[0] = x_ref[0] + gate_ref[0] * y


def _merge(branches, mg3, x3, gate3, P, layer, tl):
    b, l, d = x3.shape
    per_row = gate3.shape[1] != 1
    rows = lambda w: pl.BlockSpec((1, tl, w), lambda i, j: (i, j, 0))
    gate_spec = pl.BlockSpec((1, tl if per_row else 1, d), (lambda i, j: (i, j, 0)) if per_row else (lambda i, j: (i, 0, 0)))
    return pl.pallas_call(
        _merge_kernel,
        grid=(b, l // tl),
        in_specs=[rows(o.shape[-1]) for o in branches] + [rows(N_BRANCH * d), rows(d), gate_spec,
                                                          _layer_spec(P["w_branch"].shape, layer, 2),
                                                          _layer_spec(P["w_out"].shape, layer, 2)],
        out_specs=rows(d),
        out_shape=jax.ShapeDtypeStruct((b, l, d), f32),
        compiler_params=_params("parallel", "parallel"),
        name="merge_out",
    )(*branches, mg3, x3, gate3, P["w_branch"], P["w_out"])


def _proj_windows():
    start, first = {}, 0
    for name, size in IN_SEGMENTS:
        start[name] = first
        first += size
    spans = {"ssm": ("ssm_z", "rwkv_z", 1408), "rwkv": ("rwkv_z", "mlstm_z", 2176),
             "mlstm": ("mlstm_z", "attn_z", 896), "attn": ("attn_z", "merge", 1280),
             "merge": ("merge", None, 1024)}
    out = {}
    for key, (lo, hi, tn) in spans.items():
        end = first if hi is None else start[hi]
        width = -(-(end - start[lo]) // LANES) * LANES
        assert start[lo] % 8 == 0 and width % tn == 0 and start[lo] + width <= first
        out[key] = (start[lo], width, tn)
    return out


_PROJ = _proj_windows()


def _layer(x, mod, st, sidx, caches, P, layer, w16=None):
    b, l, d = x.shape
    prompt = caches is None
    shift, scale, gate = mod[:, :d], mod[:, d:2 * d], mod[:, 2 * d:]
    if prompt:
        x3, sc3, sh3, g3 = x, scale[:, None], shift[:, None], gate[:, None]
        tl, chunk, tl_merge, tm, proj_dtype = 512, 128, 512, 1024, bf16
        rwkv_rows, nb_ssd, nb_rwkv, nb_mlstm, nb_attn = 2 * RWKV_CHUNK, math.gcd(b, 4), math.gcd(b, 2), 1, 1
    else:
        rep = lambda t: jnp.repeat(t, l, axis=0)[None]
        x3, sc3, sh3, g3 = x.reshape(1, b * l, d), rep(scale), rep(shift), rep(gate)
        tl = min(256, b * l)
        chunk, tl_merge, tm, proj_dtype = l, tl, min(1024, b * l), f32
        rwkv_rows, nb_ssd, nb_rwkv, nb_mlstm, nb_attn = l, math.gcd(b, 8), math.gcd(b, 4), 1, math.gcd(b, 2)
    h2 = _norm(x3, P["norm_w"], layer, sc3, sh3, tl).reshape(b * l, d)
    if w16 is None:
        both = {k: _matmul_t(h2, P["w_in_t"], layer, col0, width, tm, tn, proj_dtype)
                for k, (col0, width, tn) in _PROJ.items()}
        proj = {k: v[0].reshape(b, l, -1) for k, v in both.items()}
        w16 = {k: v[1] for k, v in both.items()}
    else:
        proj = {k: _matmul(h2, w16[k], tm, tn, proj_dtype).reshape(b, l, -1) for k, (_, _, tn) in _PROJ.items()}

    out_a, ssm_new, ssm_conv_new = _ssm(proj["ssm"], st["ssm_conv"], st["ssm"], sidx, P, layer, chunk, nb_ssd)
    out_b, rwkv_new, shift_new = _rwkv(proj["rwkv"], st["rwkv_shift"], st["rwkv"], sidx, P, layer,
                                       min(RWKV_CHUNK, l), rwkv_rows, nb_rwkv)
    out_c, c_new, n_new, m_new, mconv_new = _mlstm(proj["mlstm"], st["mlstm_conv"], st["mlstm_c"], st["mlstm_n"],
                                                   st["mlstm_m"], sidx, P, layer, chunk, nb_mlstm)
    if prompt:
        out_d, *kv = _attn_prompt(proj["attn"], P, layer)
        kv_new = [t.reshape(b, t.shape[1], 2, ATTN_HEADS, HEAD_DIM) for t in kv]
    else:
        out_d, kv = _attn_step(proj["attn"], caches, P, layer, nb_attn)
        kv_new = [kv[:, gi].reshape(b, l, 2, ATTN_HEADS, HEAD_DIM) for gi in range(len(ATTN_GROUPS))]

    branches = [o.reshape(x3.shape[0], x3.shape[1], -1) for o in (out_a, out_b, out_c, out_d)]
    mg3 = proj["merge"].reshape(x3.shape[0], x3.shape[1], -1)
    x_new = _merge(branches, mg3, x3, g3, P, layer, tl_merge).reshape(b, l, d)
    new_state = {"ssm": ssm_new, "ssm_conv": ssm_conv_new, "rwkv": rwkv_new, "rwkv_shift": shift_new[:, 0],
                 "mlstm_c": c_new, "mlstm_n": n_new, "mlstm_m": m_new, "mlstm_conv": mconv_new,
                 "kv_0": kv_new[0], "kv_1": kv_new[1], "kv_2": kv_new[2]}
    return x_new, new_state, w16


_STATE_NAMES = ("ssm", "ssm_conv", "rwkv", "rwkv_shift", "mlstm_c", "mlstm_n", "mlstm_m", "mlstm_conv",
                "kv_0", "kv_1", "kv_2")


def _prepare_params(w):
    P = {"norm_w": w["norm_w"][:, None, :], "ada_w": w["ada_w"], "ada_b": w["ada_b"][:, None, :],
         "w_branch": w["w_branch"].astype(bf16), "w_out": w["w_out"].astype(bf16),
         "ssm_conv_w": w["ssm_conv_w"], "mlstm_conv_w": w["mlstm_conv_w"],
         "rwkv_w2": w["rwkv_w2"], "rwkv_a2": w["rwkv_a2"], "head_mean": _head_mean_matrix(),
         "ssm_vec": _pack_vecs(SSM_VEC, SSM_VEC_N, {"conv_b": w["ssm_conv_b"], "dt_bias": w["ssm_dt_bias"],
                                                    "a_log": w["ssm_a_log"], "d": w["ssm_d"],
                                                    "norm_w": w["ssm_norm_w"]}),
         "rwkv_vec": _pack_vecs(RWKV_VEC, RWKV_VEC_N, {"mu": w["rwkv_mu"], "w0": w["rwkv_w0"], "a0": w["rwkv_a0"],
                                                       "k_k": w["rwkv_k_k"], "k_a": w["rwkv_k_a"],
                                                       "r_k": w["rwkv_r_k"], "ln_w": w["rwkv_ln_w"],
                                                       "ln_b": w["rwkv_ln_b"]}),
         "mlstm_vec": _pack_vecs(MLSTM_VEC, MLSTM_VEC_N, {"conv_b": w["mlstm_conv_b"], "gate_b": w["mlstm_gate_b"],
                                                          "norm_w": w["mlstm_norm_w"]}),
         "attn_vec": _pack_vecs(ATTN_VEC, ATTN_VEC_N, {"q_norm": jnp.tile(w["attn_q_norm"], (1, ATTN_HEADS)),
                                                       "k_norm": jnp.tile(w["attn_k_norm"], (1, ATTN_HEADS))})}
    P["w_in_t"] = jnp.transpose(w["w_in"], (0, 2, 1))
    return P


def _pad_heads(m):
    return jnp.pad(m, [(0, 0)] * (m.ndim - 1) + [(0, LANES - N_HEADS)])[..., None, :]


def kernel(x_prompt, x_sample, c_prompt, c_sample, state_ssm, state_ssm_conv, state_rwkv, state_rwkv_shift, state_mlstm_c, state_mlstm_n, state_mlstm_m, state_mlstm_conv, cache_kv_w128, cache_kv_w512, cache_kv_w2048, norm_w, ada_w, ada_b, w_in, w_branch, w_out, ssm_conv_w, ssm_conv_b, ssm_dt_bias, ssm_a_log, ssm_d, ssm_norm_w, rwkv_mu, rwkv_w0, rwkv_w2, rwkv_a0, rwkv_a2, rwkv_k_k, rwkv_k_a, rwkv_r_k, rwkv_ln_w, rwkv_ln_b, mlstm_conv_w, mlstm_conv_b, mlstm_gate_b, mlstm_norm_w, attn_q_norm, attn_k_norm):
    P = _prepare_params(dict(
        norm_w=norm_w, ada_w=ada_w, ada_b=ada_b, w_in=w_in, w_branch=w_branch, w_out=w_out, ssm_conv_w=ssm_conv_w,
        ssm_conv_b=ssm_conv_b, ssm_dt_bias=ssm_dt_bias, ssm_a_log=ssm_a_log, ssm_d=ssm_d, ssm_norm_w=ssm_norm_w,
        rwkv_mu=rwkv_mu, rwkv_w0=rwkv_w0, rwkv_w2=rwkv_w2, rwkv_a0=rwkv_a0, rwkv_a2=rwkv_a2, rwkv_k_k=rwkv_k_k,
        rwkv_k_a=rwkv_k_a, rwkv_r_k=rwkv_r_k, rwkv_ln_w=rwkv_ln_w, rwkv_ln_b=rwkv_ln_b, mlstm_conv_w=mlstm_conv_w,
        mlstm_conv_b=mlstm_conv_b, mlstm_gate_b=mlstm_gate_b, mlstm_norm_w=mlstm_norm_w, attn_q_norm=attn_q_norm,
        attn_k_norm=attn_k_norm))
    bp = x_prompt.shape[0]
    fresh = {"ssm": jnp.zeros((1, bp, N_HEADS, HEAD_DIM, SSM_STATE), f32),
             "ssm_conv": jnp.zeros((1, bp, CONV_W - 1, SSM_CONV_CH), f32),
             "rwkv": jnp.zeros((1, bp, N_HEADS, HEAD_DIM, HEAD_DIM), f32),
             "rwkv_shift": jnp.zeros((1, bp, 1, RWKV_SHIFT_CH), f32),
             "mlstm_c": jnp.zeros((1, bp, N_HEADS, HEAD_DIM, HEAD_DIM), f32),
             "mlstm_n": jnp.zeros((1, bp, N_HEADS, HEAD_DIM), f32),
             "mlstm_m": jnp.zeros((1, bp, 1, LANES), f32),
             "mlstm_conv": jnp.zeros((1, bp, CONV_W - 1, 2 * WIDTH), f32)}
    carried = {"ssm": state_ssm, "ssm_conv": state_ssm_conv, "rwkv": state_rwkv,
               "rwkv_shift": state_rwkv_shift[:, :, None, :], "mlstm_c": state_mlstm_c, "mlstm_n": state_mlstm_n,
               "mlstm_m": _pad_heads(state_mlstm_m), "mlstm_conv": state_mlstm_conv}
    c_all = jnp.concatenate([c_prompt, c_sample], axis=0)
    caches_t = [jnp.transpose(c, (0, 1, 3, 4, 5, 2)) for c in (cache_kv_w128, cache_kv_w512, cache_kv_w2048)]
    y_prompt, y_sample = x_prompt, x_sample
    prompt_states, sample_states = [], []
    for layer in range(DEPTH):
        mod = _ada(c_all, P["ada_w"], P["ada_b"], layer)
        y_prompt, sp, w16 = _layer(y_prompt, mod[:bp], fresh, 0, None, P, layer)
        y_sample, ss, _ = _layer(y_sample, mod[bp:], carried, layer, caches_t, P, layer, w16)
        prompt_states.append(sp)
        sample_states.append(ss)
    stack = lambda states, name: jnp.stack([s[name] for s in states])
    return ((y_prompt, y_sample)
            + tuple(stack(prompt_states, n) for n in _STATE_NAMES)
            + tuple(stack(sample_states, n) for n in _STATE_NAMES))
```
